```python
import jax
import jax.numpy as jnp
from jax import lax
import numpy as np

D_MODEL = 1024
BATCH = 8
SEQ = 4096
DEPTH = 2

F32 = jnp.float32

MIX_W = D_MODEL
N_GROUPS = 4
GROUP_W = MIX_W // N_GROUPS
HEAD_DIM = 64
N_HEADS_G = GROUP_W // HEAD_DIM
NORM_EPS = 1e-5
Q_BLOCK = 128

RET_CHUNK = 128
RET_THETA = 10000.0
RWKV_W_LORA = 32
RWKV_A_LORA = 32
RWKV_G_LORA = 64
RWKV_LN_EPS = 64e-5
DSA_Q_LORA = 128
DSA_KV_DIM = HEAD_DIM
IDX_HEADS = 8
IDX_DIM = 32
DSA_TOPK_MAX = 256
ROPE_THETA = 500000.0
ROPE_FRAC = 4
N_EXPERTS = 32
TOP_K = 4
D_FF = D_MODEL
SWIGLU_ALPHA = 1.702
SWIGLU_LIMIT = 7.0
MOE_BLOCK = 512

RET_COLS = 4 * GROUP_W
RWKV_COLS = 3 * GROUP_W + RWKV_W_LORA + RWKV_A_LORA + RWKV_G_LORA
DSA_COLS = DSA_Q_LORA + 2 * DSA_KV_DIM + IDX_DIM + IDX_HEADS
SB_COLS = 3 * GROUP_W
IN_COLS = RET_COLS + RWKV_COLS + DSA_COLS + SB_COLS
GROUP_SPLITS = (RET_COLS, RET_COLS + RWKV_COLS, RET_COLS + RWKV_COLS + DSA_COLS)
RWKV_SPLITS = (GROUP_W, 2 * GROUP_W, 3 * GROUP_W, 3 * GROUP_W + RWKV_W_LORA,
               3 * GROUP_W + RWKV_W_LORA + RWKV_A_LORA)
DSA_SPLITS = (DSA_Q_LORA, DSA_Q_LORA + DSA_KV_DIM, DSA_Q_LORA + 2 * DSA_KV_DIM,
              DSA_Q_LORA + 2 * DSA_KV_DIM + IDX_DIM)

kernel_name = 'hybrid_parallel_heads_moe_decoder'


def rms_norm(x, g, eps=NORM_EPS):
    xf = x.astype(F32)
    y = xf * lax.rsqrt(jnp.mean(xf * xf, axis=-1, keepdims=True) + eps)
    return (y * g.astype(F32)).astype(x.dtype)


def head_norm(x, g, eps):
    xf = x.astype(F32)
    xf = xf - jnp.mean(xf, axis=-1, keepdims=True)
    y = xf * lax.rsqrt(jnp.mean(xf * xf, axis=-1, keepdims=True) + eps)
    return (y.reshape(x.shape[:-2] + (-1,)) * g.astype(F32)).astype(x.dtype)


def rope(x, pos, rot_dim, theta):
    half = rot_dim // 2
    inv = theta ** (-jnp.arange(half, dtype=F32) / half)
    ang = pos.astype(F32)[..., None] * inv
    ang = ang.reshape(ang.shape[:2] + (1,) * (x.ndim - 3) + (half,))
    cos, sin = jnp.cos(ang), jnp.sin(ang)
    xr = x[..., :rot_dim].astype(F32)
    x1, x2 = xr[..., :half], xr[..., half:]
    rot = jnp.concatenate([x1 * cos - x2 * sin, x1 * sin + x2 * cos], axis=-1).astype(x.dtype)
    return jnp.concatenate([rot, x[..., rot_dim:]], axis=-1)


def retention(q, k, v, gate, pos, gn):
    B, S, H, d = q.shape
    q = rope(q, pos, d, RET_THETA)
    k = rope(k, pos, d, RET_THETA) * d ** -0.5
    lg = jnp.log(1.0 - 2.0 ** (-5.0 - jnp.arange(H, dtype=F32)))
    C = RET_CHUNK
    nc = S // C
    idx = jnp.arange(C, dtype=F32)
    diff = idx[:, None] - idx[None, :]
    decay_in = jnp.where(diff >= 0, jnp.exp(lg[:, None, None] * jnp.maximum(diff, 0.0)), 0.0)
    q_dec = jnp.exp(lg[:, None] * (idx + 1.0))
    k_dec = jnp.exp(lg[:, None] * (C - 1.0 - idx))
    chunk_dec = jnp.exp(lg * C)

    def to_chunks(t):
        return t.astype(F32).reshape(B, nc, C, H, d).transpose(1, 0, 3, 2, 4)

    def step(state, qkv):
        qc, kc, vc = qkv
        inner = jnp.einsum('bhnm,bhmv->bhnv', jnp.einsum('bhnd,bhmd->bhnm', qc, kc) * decay_in, vc)
        cross = jnp.einsum('bhnd,bhdv->bhnv', qc * q_dec[..., None], state)
        state = state * chunk_dec[:, None, None] + jnp.einsum('bhmd,bhmv->bhdv', kc * k_dec[..., None], vc)
        return state, inner + cross

    state0 = jnp.zeros((B, H, d, d), F32)
    _, out = lax.scan(step, state0, (to_chunks(q), to_chunks(k), to_chunks(v)))
    out = out.transpose(1, 0, 3, 2, 4).reshape(B, S, H, d).astype(v.dtype)
    return jax.nn.silu(gate) * head_norm(out, gn, NORM_EPS)


def token_shift(z, mu):
    prev = jnp.pad(z, ((0, 0), (1, 0), (0, 0)))[:, :-1]
    return z + (prev - z) * mu


def rwkv7(feats, w0, w2, a0, a2, g2, k_k, k_a, r_k, ln_g):
    B, S, _ = feats.shape
    H, d = N_HEADS_G, HEAD_DIM
    r, k, v, wd, ad, gd = jnp.split(feats, RWKV_SPLITS, axis=-1)
    w_log = -jax.nn.softplus(-(w0 + jnp.tanh(wd) @ w2)) - 0.5
    decay = jnp.exp(-jnp.exp(w_log.astype(F32)))
    a = jax.nn.sigmoid(a0 + ad @ a2)
    gate = jax.nn.sigmoid(gd) @ g2

    def heads(t):
        return t.reshape(B, S, H, d)

    kk = heads((k * k_k).astype(F32))
    kk = kk / jnp.maximum(jnp.sqrt(jnp.sum(kk * kk, axis=-1, keepdims=True)), 1e-12)
    k = k * (1.0 + (a - 1.0) * k_a)
    r_h, k_h, v_h, a_h = heads(r), heads(k), heads(v), heads(a)
    b_h = kk * a_h.astype(F32)

    def step(state, inp):
        rt, wt, kt, vt, kkt, bt = inp
        sa = jnp.einsum('bhvk,bhk->bhv', state, -kkt)
        state = (state * wt[:, :, None, :] + sa[..., None] * bt[:, :, None, :]
                 + vt[..., None] * kt[:, :, None, :])
        return state, jnp.einsum('bhvk,bhk->bhv', state, rt)

    def seq_first(t):
        return t.astype(F32).transpose(1, 0, 2, 3)

    state0 = jnp.zeros((B, H, d, d), F32)
    _, y = lax.scan(step, state0, (seq_first(r_h), seq_first(heads(decay)), seq_first(k_h),
                                   seq_first(v_h), seq_first(kk), seq_first(b_h)))
    y = y.transpose(1, 0, 2, 3).astype(feats.dtype)
    y = head_norm(y, ln_g, RWKV_LN_EPS)
    bonus = (jnp.sum(r_h * k_h * r_k, axis=-1, keepdims=True) * v_h).reshape(B, S, H * d)
    return (y + bonus) * gate


def dsa_attention(cq, k, v, k_idx, w_idx, pos, q_norm, wq_up, wqi_up, o_norm):
    B, S, _ = cq.shape
    H, d = N_HEADS_G, HEAD_DIM
    cq = rms_norm(cq, q_norm)
    q = rope((cq @ wq_up).reshape(B, S, H, d), pos, d // ROPE_FRAC, ROPE_THETA)
    qi = rope((cq @ wqi_up).reshape(B, S, IDX_HEADS, IDX_DIM), pos, IDX_DIM // ROPE_FRAC, ROPE_THETA)
    k = rope(k, pos, d // ROPE_FRAC, ROPE_THETA)
    k_idx = rope(k_idx, pos, IDX_DIM // ROPE_FRAC, ROPE_THETA)
    w_idx = w_idx * (IDX_HEADS ** -0.5 * IDX_DIM ** -0.5)
    topk = min(DSA_TOPK_MAX, S // 4)
    nb = S // Q_BLOCK
    key_pos = jnp.arange(S)

    def blocks(t):
        return t.reshape((B, nb, Q_BLOCK) + t.shape[2:]).swapaxes(0, 1)

    def gather_rows(table, sel):
        return jax.vmap(lambda tb, ib: tb[ib])(table, sel)

    def one_block(args):
        qb, qib, wb, start = args
        q_pos = start + jnp.arange(Q_BLOCK)
        rel = jax.nn.relu(jnp.einsum('bthe,bse->bths', qib, k_idx))
        score = jnp.einsum('bth,bths->bts', wb, rel).astype(F32)
        causal = key_pos[None, :] <= q_pos[:, None]
        score = jnp.where(causal[None], score, -jnp.inf)
        vals, sel = lax.top_k(score, topk)
        valid = jnp.isfinite(vals)
        kg = gather_rows(k, sel)
        vg = gather_rows(v, sel)
        logits = jnp.einsum('bthd,btkd->bthk', qb, kg).astype(F32) * d ** -0.5
        logits = jnp.where(valid[:, :, None, :], logits, -jnp.inf)
        p = jax.nn.softmax(logits, axis=-1).astype(vg.dtype)
        return jnp.einsum('bthk,btkd->bthd', p, vg)

    starts = jnp.arange(nb) * Q_BLOCK
    out = lax.map(one_block, (blocks(q), blocks(qi), blocks(w_idx), starts))
    out = out.swapaxes(0, 1).reshape(B, S, H * d)
    return rms_norm(out, o_norm)


def stick_breaking(q, k, v, o_norm):
    B, S, H, d = q.shape
    nb = S // Q_BLOCK
    key_pos = jnp.arange(S)

    def one_block(args):
        qb, start = args
        q_pos = start + jnp.arange(Q_BLOCK)
        z = jnp.einsum('bthd,bshd->bhts', qb, k).astype(F32) * d ** -0.5
        mask = (key_pos[None, :] < q_pos[:, None])[None, None]
        log_1m = jnp.where(mask, jax.nn.log_sigmoid(-z), 0.0)
        after = lax.cumsum(log_1m, axis=3, reverse=True) - log_1m
        A = jnp.where(mask, jnp.exp(jax.nn.log_sigmoid(z) + after), 0.0)
        return jnp.einsum('bhts,bshd->bthd', A.astype(v.dtype), v)

    q_blocks = q.reshape(B, nb, Q_BLOCK, H, d).swapaxes(0, 1)
    out = lax.map(one_block, (q_blocks, jnp.arange(nb) * Q_BLOCK))
    out = out.swapaxes(0, 1).reshape(B, S, H * d)
    return rms_norm(out, o_norm)


def clamped_swiglu(h):
    glu, lin = jnp.split(h, 2, axis=-1)
    glu = jnp.minimum(glu, SWIGLU_LIMIT)
    lin = jnp.clip(lin, -SWIGLU_LIMIT, SWIGLU_LIMIT)
    return glu * jax.nn.sigmoid(SWIGLU_ALPHA * glu) * (lin + 1.0)


def moe(h, router_w, router_b, w1, b1, w2, b2):
    B, S, D = h.shape
    n = B * S
    xt = h.reshape(n, D)
    logits = (xt @ router_w + router_b).astype(F32)
    top_val, top_idx = lax.top_k(logits, TOP_K)
    gate = jax.nn.softmax(top_val, axis=-1)
    n_assign = n * TOP_K
    expert = top_idx.reshape(-1).astype(jnp.int32)
    token = jnp.arange(n_assign, dtype=jnp.int32) // TOP_K
    weight = gate.reshape(-1)
    order = jnp.argsort(expert)
    expert_s, token_s, weight_s = expert[order], token[order], weight[order]
    counts = jnp.zeros((N_EXPERTS,), jnp.int32).at[expert].add(1)
    starts = jnp.cumsum(counts) - counts
    padded = (counts + MOE_BLOCK - 1) // MOE_BLOCK * MOE_BLOCK
    pad_end = jnp.cumsum(padded)
    pad_start = pad_end - padded
    dest = pad_start[expert_s] + jnp.arange(n_assign, dtype=jnp.int32) - starts[expert_s]
    n_blocks = -(-n_assign // MOE_BLOCK) + N_EXPERTS
    rows = n_blocks * MOE_BLOCK
    row_token = jnp.full((rows,), n, jnp.int32).at[dest].set(token_s)
    row_weight = jnp.zeros((rows,), F32).at[dest].set(weight_s)
    block_start = jnp.arange(n_blocks, dtype=jnp.int32) * MOE_BLOCK
    block_expert = jnp.minimum(jnp.sum(pad_end[None, :] <= block_start[:, None], axis=1), N_EXPERTS - 1)
    x_pad = jnp.concatenate([xt, jnp.zeros((1, D), xt.dtype)], axis=0)

    def one_block(args):
        tok, e = args
        hb = clamped_swiglu(x_pad[tok] @ w1[e] + b1[e])
        return hb @ w2[e] + b2[e]

    y = lax.map(one_block, (row_token.reshape(n_blocks, MOE_BLOCK), block_expert))
    y = y.reshape(rows, D) * row_weight[:, None].astype(y.dtype)
    out = jax.ops.segment_sum(y, row_token, num_segments=n + 1)[:n]
    return out.reshape(B, S, D)


def setup_inputs(seed: int = 0) -> dict:
    key = jax.random.key(seed)
    ks = list(jax.random.split(key, 40))
    kit = iter(ks)

    def nrm(shape, scale):
        return jax.random.normal(next(kit), shape, F32) * scale

    def gain(shape):
        return 1.0 + nrm(shape, 0.02)

    L = DEPTH
    x = nrm((BATCH, SEQ, D_MODEL), 1.0)
    c = nrm((BATCH, D_MODEL), 1.0)
    offsets = jax.random.randint(next(kit), (BATCH, 1), 0, 2048, dtype=jnp.int32)
    positions = offsets + jnp.arange(SEQ, dtype=jnp.int32)[None, :]
    return {
        'x': x,
        'c': c,
        'positions': positions,
        'ada_w': nrm((L, D_MODEL, 6 * D_MODEL), 0.5 * D_MODEL ** -0.5),
        'ada_b': nrm((L, 6 * D_MODEL), 0.02),
        'norm_mix': gain((L, D_MODEL)),
        'norm_ffn': gain((L, D_MODEL)),
        'w_in': nrm((L, D_MODEL, IN_COLS), D_MODEL ** -0.5),
        'ret_gn': gain((L, GROUP_W)),
        'rwkv_mu': jax.random.uniform(next(kit), (L, RWKV_COLS), F32, 0.0, 1.0),
        'rwkv_w0': jax.random.uniform(next(kit), (L, GROUP_W), F32, -6.5, -1.5),
        'rwkv_w2': nrm((L, RWKV_W_LORA, GROUP_W), 0.1),
        'rwkv_a0': nrm((L, GROUP_W), 0.5),
        'rwkv_a2': nrm((L, RWKV_A_LORA, GROUP_W), 0.1),
        'rwkv_g2': nrm((L, RWKV_G_LORA, GROUP_W), RWKV_G_LORA ** -0.5),
        'rwkv_kk': 0.85 + nrm((L, GROUP_W), 0.05),
        'rwkv_ka': 1.0 + nrm((L, GROUP_W), 0.05),
        'rwkv_rk': nrm((L, N_HEADS_G, HEAD_DIM), 0.1),
        'rwkv_ln': gain((L, GROUP_W)),
        'dsa_qnorm': gain((L, DSA_Q_LORA)),
        'dsa_wq_up': nrm((L, DSA_Q_LORA, GROUP_W), DSA_Q_LORA ** -0.5),
        'dsa_wqi_up': nrm((L, DSA_Q_LORA, IDX_HEADS * IDX_DIM), DSA_Q_LORA ** -0.5),
        'dsa_onorm': gain((L, GROUP_W)),
        'sb_onorm': gain((L, GROUP_W)),
        'w_out': nrm((L, MIX_W, D_MODEL), MIX_W ** -0.5),
        'router_w': nrm((L, D_MODEL, N_EXPERTS), D_MODEL ** -0.5),
        'router_b': nrm((L, N_EXPERTS), 0.01),
        'moe_w1': nrm((L, N_EXPERTS, D_MODEL, 2 * D_FF), D_MODEL ** -0.5),
        'moe_b1': nrm((L, N_EXPERTS, 2 * D_FF), 0.02),
        'moe_w2': nrm((L, N_EXPERTS, D_FF, D_MODEL), D_FF ** -0.5),
        'moe_b2': nrm((L, N_EXPERTS, D_MODEL), 0.02),
        'norm_final': gain((D_MODEL,)),
    }


def reference(x, c, positions, ada_w, ada_b, norm_mix, norm_ffn, w_in, ret_gn,
              rwkv_mu, rwkv_w0, rwkv_w2, rwkv_a0, rwkv_a2, rwkv_g2, rwkv_kk, rwkv_ka, rwkv_rk, rwkv_ln,
              dsa_qnorm, dsa_wq_up, dsa_wqi_up, dsa_onorm, sb_onorm, w_out,
              router_w, router_b, moe_w1, moe_b1, moe_w2, moe_b2, norm_final):
    B, S, D = x.shape
    H, d = N_HEADS_G, HEAD_DIM

    def heads(t):
        return t.reshape(B, S, H, d)

    cond = jax.nn.silu(c)
    for l in range(DEPTH):
        mod = cond @ ada_w[l] + ada_b[l]
        sh1, sc1, g1, sh2, sc2, g2 = jnp.split(mod[:, None, :], 6, axis=-1)
        h = rms_norm(x, norm_mix[l]) * (1.0 + sc1) + sh1
        proj = h @ w_in[l]
        ret_f, rwkv_f, dsa_f, sb_f = jnp.split(proj, GROUP_SPLITS, axis=-1)
        rq, rk, rv, rg = jnp.split(ret_f, 4, axis=-1)
        y_ret = retention(heads(rq), heads(rk), heads(rv), rg, positions, ret_gn[l])
        y_rwkv = rwkv7(token_shift(rwkv_f, rwkv_mu[l]), rwkv_w0[l], rwkv_w2[l], rwkv_a0[l], rwkv_a2[l],
                       rwkv_g2[l], rwkv_kk[l], rwkv_ka[l], rwkv_rk[l], rwkv_ln[l])
        cq, dk, dv, dki, dwi = jnp.split(dsa_f, DSA_SPLITS, axis=-1)
        y_dsa = dsa_attention(cq, dk, dv, dki, dwi, positions, dsa_qnorm[l], dsa_wq_up[l],
                              dsa_wqi_up[l], dsa_onorm[l])
        sq, sk, sv = jnp.split(sb_f, 3, axis=-1)
        y_sb = stick_breaking(heads(sq), heads(sk), heads(sv), sb_onorm[l])
        mixed = jnp.concatenate([y_ret, y_rwkv, y_dsa, y_sb], axis=-1) @ w_out[l]
        x = x + g1 * mixed
        h = rms_norm(x, norm_ffn[l]) * (1.0 + sc2) + sh2
        x = x + g2 * moe(h, router_w[l], router_b[l], moe_w1[l], moe_b1[l], moe_w2[l], moe_b2[l])
    return rms_norm(x, norm_final)
```

```python
import functools
import math

import jax
import jax.numpy as jnp
from jax import lax
from jax.experimental import pallas as pl
from jax.experimental.pallas import tpu as pltpu

F32 = jnp.float32
BF16 = jnp.bfloat16
I32 = jnp.int32

D_MODEL = 1024
GROUP_W = 256
HEAD_DIM = 64
N_HEADS_G = 4
NORM_EPS = 1e-5
Q_BLOCK = 128
RET_THETA = 10000.0
RWKV_LN_EPS = 64e-5
DSA_Q_LORA = 128
IDX_HEADS = 8
IDX_DIM = 32
DSA_TOPK_MAX = 256
ROPE_THETA = 500000.0
N_EXPERTS = 32
TOP_K = 4
D_FF = D_MODEL
SWIGLU_ALPHA = 1.702
SWIGLU_LIMIT = 7.0
MOE_BLOCK = 512

RET_COLS = 4 * GROUP_W
RWKV_COLS = 3 * GROUP_W + 128
DSA_COLS = 296
DSA_PAD = 384
SB_COLS = 3 * GROUP_W

LANES = 128
ROW_TILE = 512
VMEM_LIMIT = 48 * 1024 * 1024
INT_MIN = -2 ** 31
SB_UNDERFLOW = -110.0

HIGHEST = lax.Precision.HIGHEST
NT_DIMS = (((1,), (1,)), ((), ()))
TN_DIMS = (((0,), (0,)), ((), ()))


def _params(sem):
    return pltpu.CompilerParams(dimension_semantics=sem, vmem_limit_bytes=VMEM_LIMIT)


def _bdot(a, b):
    return jnp.dot(a.astype(BF16), b.astype(BF16), preferred_element_type=F32)


def _split_dot(x, m):
    hi = x.astype(BF16)
    lo = (x - hi.astype(F32)).astype(BF16)
    return (jnp.dot(hi, m, preferred_element_type=F32) + jnp.dot(lo, m, preferred_element_type=F32))


def _sigmoid(x):
    return 1.0 / (1.0 + jnp.exp(-x))


def _softplus(x):
    return jnp.maximum(x, 0.0) + jnp.log(1.0 + jnp.exp(-jnp.abs(x)))


def _rms(x, g, eps=NORM_EPS):
    return x * lax.rsqrt(jnp.mean(x * x, axis=-1, keepdims=True) + eps) * g


def _rope(x, cos, sin_a, sin_b, half):
    w = x.shape[-1]
    return x * cos + pltpu.roll(x, w - half, 1) * sin_a + pltpu.roll(x, half, 1) * sin_b


def _ada_kernel(c_ref, w_ref, b_ref, o_ref):
    c = c_ref[...]
    cond = c * _sigmoid(c)
    o_ref[0] = jnp.dot(cond, w_ref[0], preferred_element_type=F32, precision=HIGHEST) + b_ref[0]


def _ada(c, ada_w, ada_b):
    L, D, W = ada_w.shape
    B = c.shape[0]
    tn = 1024
    return pl.pallas_call(
        _ada_kernel,
        grid=(L, W // tn),
        in_specs=[pl.BlockSpec((B, D), lambda l, j: (0, 0)),
                  pl.BlockSpec((1, D, tn), lambda l, j: (l, 0, j)),
                  pl.BlockSpec((1, 1, tn), lambda l, j: (l, 0, j))],
        out_specs=pl.BlockSpec((1, B, tn), lambda l, j: (l, 0, j)),
        out_shape=jax.ShapeDtypeStruct((L, B, W), F32),
        compiler_params=_params(("parallel", "parallel")),
        name="ada_mod",
    )(c, ada_w, ada_b.reshape(L, 1, W))


def _inproj_kernel(x_ref, g_ref, sc_ref, sh_ref, w_ref, ret_ref, rwkv_ref, dsa_ref, sb_ref):
    h = _rms(x_ref[...], g_ref[...]) * (1.0 + sc_ref[0]) + sh_ref[0]
    hb = h.astype(BF16)
    o0, o1, o2 = RET_COLS, RET_COLS + RWKV_COLS, RET_COLS + RWKV_COLS + DSA_PAD
    ret_ref[...] = jnp.dot(hb, w_ref[:, 0:o0], preferred_element_type=F32)
    rwkv_ref[...] = jnp.dot(hb, w_ref[:, o0:o1], preferred_element_type=F32)
    dsa_ref[...] = jnp.dot(hb, w_ref[:, o1:o2], preferred_element_type=F32)
    sb_ref[...] = jnp.dot(hb, w_ref[:, o2:], preferred_element_type=F32).astype(BF16)


def _inproj(x2, gain, mod6, w_cat, S):
    N, D = x2.shape
    tm = ROW_TILE
    per_b = S // tm
    wt = w_cat.shape[1]
    row = lambda i: (i, 0)
    return pl.pallas_call(
        _inproj_kernel,
        grid=(N // tm,),
        in_specs=[pl.BlockSpec((tm, D), row),
                  pl.BlockSpec((1, D), lambda i: (0, 0)),
                  pl.BlockSpec((1, 1, D), lambda i: ((i // per_b) * 6 + 1, 0, 0)),
                  pl.BlockSpec((1, 1, D), lambda i: ((i // per_b) * 6 + 0, 0, 0)),
                  pl.BlockSpec((D, wt), lambda i: (0, 0))],
        out_specs=[pl.BlockSpec((tm, RET_COLS), row), pl.BlockSpec((tm, RWKV_COLS), row),
                   pl.BlockSpec((tm, DSA_PAD), row), pl.BlockSpec((tm, SB_COLS), row)],
        out_shape=[jax.ShapeDtypeStruct((N, RET_COLS), F32), jax.ShapeDtypeStruct((N, RWKV_COLS), F32),
                   jax.ShapeDtypeStruct((N, DSA_PAD), F32), jax.ShapeDtypeStruct((N, SB_COLS), BF16)],
        compiler_params=_params(("parallel",)),
        name="in_proj",
    )(x2, gain, mod6, mod6, w_cat)


def _ret_kernel(q_ref, k_ref, v_ref, g_ref, cos_ref, sa_ref, sb_ref, din_ref, qd_ref, kd_ref, cd_ref, gn_ref,
                o_ref, state_ref, y_ref):
    @pl.when(pl.program_id(1) == 0)
    def _():
        state_ref[...] = jnp.zeros_like(state_ref)

    cos, sin_a, sin_b = cos_ref[0], sa_ref[0], sb_ref[0]
    q = _rope(q_ref[0], cos, sin_a, sin_b, HEAD_DIM // 2)
    k = _rope(k_ref[0], cos, sin_a, sin_b, HEAD_DIM // 2) * HEAD_DIM ** -0.5
    v = v_ref[0]
    qd = q * qd_ref[...]
    kd = k * kd_ref[...]
    for h in range(N_HEADS_G):
        sl = slice(h * HEAD_DIM, (h + 1) * HEAD_DIM)
        qh, kh, vh = q[:, sl].astype(BF16), k[:, sl].astype(BF16), v[:, sl].astype(BF16)
        s = lax.dot_general(qh, kh, NT_DIMS, preferred_element_type=F32) * din_ref[h]
        inner = jnp.dot(s.astype(BF16), vh, preferred_element_type=F32)
        st = state_ref[h]
        cross = _bdot(qd[:, sl], st)
        state_ref[h] = st * cd_ref[:, sl] + lax.dot_general(
            kd[:, sl].astype(BF16), vh, TN_DIMS, preferred_element_type=F32)
        o = inner + cross
        oc = o - jnp.mean(o, axis=-1, keepdims=True)
        y_ref[:, sl] = oc * lax.rsqrt(jnp.mean(oc * oc, axis=-1, keepdims=True) + NORM_EPS)
    g = g_ref[0]
    o_ref[0] = g * _sigmoid(g) * (y_ref[...] * gn_ref[...])


def _retention(ret3, tabs, consts, gn):
    B, S, _ = ret3.shape
    C = consts["din"].shape[1]
    blk = lambda j: pl.BlockSpec((1, C, GROUP_W), lambda b, c, j=j: (b, c, j))
    tab = pl.BlockSpec((1, C, GROUP_W), lambda b, c: (b, c, 0))
    const2 = lambda shape: pl.BlockSpec(shape, lambda b, c: (0,) * len(shape))
    return pl.pallas_call(
        _ret_kernel,
        grid=(B, S // C),
        in_specs=[blk(0), blk(1), blk(2), blk(3), tab, tab, tab,
                  const2((N_HEADS_G, C, C)), const2((C, GROUP_W)), const2((C, GROUP_W)),
                  const2((1, GROUP_W)), const2((1, GROUP_W))],
        out_specs=pl.BlockSpec((1, C, GROUP_W), lambda b, c: (b, c, 0)),
        out_shape=jax.ShapeDtypeStruct((B, S, GROUP_W), F32),
        scratch_shapes=[pltpu.VMEM((N_HEADS_G, HEAD_DIM, HEAD_DIM), F32), pltpu.VMEM((C, GROUP_W), F32)],
        compiler_params=_params(("parallel", "arbitrary")),
        name="retention",
    )(ret3, ret3, ret3, ret3, tabs[0], tabs[1], tabs[2],
      consts["din"], consts["qd"], consts["kd"], consts["cd"], gn)


def _ret_consts(C):
    H = N_HEADS_G
    lg = jnp.log(1.0 - 2.0 ** (-5.0 - jnp.arange(H, dtype=F32)))
    idx = jnp.arange(C, dtype=F32)
    diff = idx[:, None] - idx[None, :]
    din = jnp.where(diff >= 0, jnp.exp(lg[:, None, None] * jnp.maximum(diff, 0.0)), 0.0)
    q_dec = jnp.exp(lg[:, None] * (idx + 1.0))
    k_dec = jnp.exp(lg[:, None] * (C - 1.0 - idx))
    chunk_dec = jnp.exp(lg * C)
    wide = lambda t: jnp.repeat(t.T, HEAD_DIM, axis=1)
    return {"din": din, "qd": wide(q_dec), "kd": wide(k_dec),
            "cd": jnp.repeat(chunk_dec, HEAD_DIM)[None, :]}


def _rwkv_prep_kernel(per_b, z_ref, zp_ref, mu_ref, w0_ref, a0_ref, w2_ref, a2_ref, g2_ref, kk_ref, ka_ref,
                      rk_ref, hm_ref, r_o, w_o, k_o, v_o, nkk_o, b_o, gate_o, bonus_o):
    z = z_ref[...]
    first = (pl.program_id(0) % per_b) == 0
    prow = jnp.where(first, 0.0, zp_ref[7:8, :])
    rid = lax.broadcasted_iota(I32, z.shape, 0)
    prev = jnp.where(rid == 0, prow, pltpu.roll(z, 1, 0))
    f = z + (prev - z) * mu_ref[...]
    G = GROUP_W
    r, k, v, lo = f[:, 0:G], f[:, G:2 * G], f[:, 2 * G:3 * G], f[:, 3 * G:3 * G + 128]
    w_log = -_softplus(-(w0_ref[...] + _bdot(jnp.tanh(lo), w2_ref[...]))) - 0.5
    decay = jnp.exp(-jnp.exp(w_log))
    a = _sigmoid(a0_ref[...] + _bdot(lo, a2_ref[...]))
    gate = _bdot(_sigmoid(lo), g2_ref[...])
    hm = hm_ref[...]
    kk = k * kk_ref[...]
    kk = kk / jnp.maximum(jnp.sqrt(_split_dot(kk * kk, hm)), 1e-12)
    k2 = k * (1.0 + (a - 1.0) * ka_ref[...])
    r_o[...] = r
    w_o[...] = decay
    k_o[...] = k2
    v_o[...] = v
    nkk_o[...] = -kk
    b_o[...] = kk * a
    gate_o[...] = gate
    bonus_o[...] = _split_dot(r * k2 * rk_ref[...], hm) * v


def _rwkv_prep(z2, S, mu, w0, a0, w2p, a2p, g2p, kk, ka, rk, hm):
    N, W = z2.shape
    tm = ROW_TILE
    per_b = S // tm
    row = lambda i: (i, 0)
    c = lambda shape: pl.BlockSpec(shape, lambda i: (0, 0))
    G = GROUP_W
    return pl.pallas_call(
        functools.partial(_rwkv_prep_kernel, per_b),
        grid=(N // tm,),
        in_specs=[pl.BlockSpec((tm, W), row),
                  pl.BlockSpec((8, W), lambda i: (jnp.maximum(i * (tm // 8) - 1, 0), 0)),
                  c((1, W)), c((1, G)), c((1, G)), c((128, G)), c((128, G)), c((128, G)),
                  c((1, G)), c((1, G)), c((1, G)), c((G, G))],
        out_specs=[pl.BlockSpec((tm, G), row)] * 8,
        out_shape=[jax.ShapeDtypeStruct((N, G), F32)] * 8,
        compiler_params=_params(("parallel",)),
        name="rwkv_prep",
    )(z2, z2, mu, w0, a0, w2p, a2p, g2p, kk, ka, rk, hm)


def _rwkv_scan_kernel(T, nkk_ref, w_ref, b_ref, k_ref, r_ref, v_ref, y_ref, s_ref):
    @pl.when(pl.program_id(0) == 0)
    def _():
        s_ref[...] = jnp.zeros_like(s_ref)

    nk = s_ref.shape[0]

    def tree(parts):
        while len(parts) > 1:
            parts = [parts[i] + parts[i + 1] for i in range(0, len(parts), 2)]
        return parts[0]

    def step(t, carry):
        vt = v_ref[t]
        sa = tree([s_ref[k] * nkk_ref[t, k:k + 1, :] for k in range(nk)])
        ys = []
        for k in range(nk):
            s_new = (s_ref[k] * w_ref[t, k:k + 1, :] + sa * b_ref[t, k:k + 1, :]
                     + vt * k_ref[t, k:k + 1, :])
            s_ref[k] = s_new
            ys.append(s_new * r_ref[t, k:k + 1, :])
        y_ref[t] = tree(ys)
        return carry

    lax.fori_loop(0, T, step, 0)


def _rwkv_scan(nkk, w, b, k, r, v):
    S, nk, _ = nkk.shape
    vh = v.shape[1]
    T = 32
    kspec = pl.BlockSpec((T, nk, LANES), lambda i: (i, 0, 0))
    vspec = pl.BlockSpec((T, vh, LANES), lambda i: (i, 0, 0))
    return pl.pallas_call(
        functools.partial(_rwkv_scan_kernel, T),
        grid=(S // T,),
        in_specs=[kspec] * 5 + [vspec],
        out_specs=vspec,
        out_shape=jax.ShapeDtypeStruct((S, vh, LANES), F32),
        scratch_shapes=[pltpu.VMEM((nk, vh, LANES), F32)],
        compiler_params=_params(("arbitrary",)),
        name="rwkv_scan",
    )(nkk, w, b, k, r, v)


def _rwkv_post_kernel(y_ref, bonus_ref, gate_ref, ln_ref, hm_ref, o_ref):
    y = y_ref[...]
    hm = hm_ref[...]
    yc = y - _split_dot(y, hm) * (1.0 / HEAD_DIM)
    var = _split_dot(yc * yc, hm) * (1.0 / HEAD_DIM)
    o_ref[...] = (yc * lax.rsqrt(var + RWKV_LN_EPS) * ln_ref[...] + bonus_ref[...]) * gate_ref[...]


def _rwkv_post(y2, bonus, gate, ln, hm):
    N, G = y2.shape
    tm = ROW_TILE
    row = pl.BlockSpec((tm, G), lambda i: (i, 0))
    return pl.pallas_call(
        _rwkv_post_kernel,
        grid=(N // tm,),
        in_specs=[row, row, row, pl.BlockSpec((1, G), lambda i: (0, 0)), pl.BlockSpec((G, G), lambda i: (0, 0))],
        out_specs=row,
        out_shape=jax.ShapeDtypeStruct((N, G), F32),
        compiler_params=_params(("parallel",)),
        name="rwkv_post",
    )(y2, bonus, gate, ln, hm)


def _to_k_layout(x, B, S):
    P = B * N_HEADS_G
    rep = LANES // P
    t = x.reshape(B, S, N_HEADS_G, HEAD_DIM).transpose(1, 3, 0, 2).reshape(S, HEAD_DIM, 1, P)
    return jnp.broadcast_to(t, (S, HEAD_DIM, rep, P)).reshape(S, HEAD_DIM, LANES)


def _to_v_layout(x, B, S):
    P = B * N_HEADS_G
    rep = LANES // P
    t = x.reshape(B, S, N_HEADS_G, HEAD_DIM // rep, rep).transpose(1, 3, 4, 0, 2)
    return t.reshape(S, HEAD_DIM // rep, LANES)


def _from_v_layout(y, B, S):
    P = B * N_HEADS_G
    rep = LANES // P
    t = y.reshape(S, HEAD_DIM // rep, rep, B, N_HEADS_G).transpose(3, 0, 4, 1, 2)
    return t.reshape(B * S, GROUP_W)


def _dsa_prep_kernel(f_ref, qn_ref, wq_ref, wqi_ref, cq_ref, sqa_ref, sqb_ref, ci_ref, sia_ref, sib_ref,
                     q_o, qi_o, k_o, v_o, ki_o, w_o):
    f = f_ref[...]
    cq = _rms(f[:, 0:DSA_Q_LORA], qn_ref[...]).astype(BF16)
    cos_q, sqa, sqb = cq_ref[...], sqa_ref[...], sqb_ref[...]
    cos_i, sia, sib = ci_ref[...], sia_ref[...], sib_ref[...]
    q = _rope(jnp.dot(cq, wq_ref[...], preferred_element_type=F32), cos_q, sqa, sqb, HEAD_DIM // 8)
    q_o[...] = (q * HEAD_DIM ** -0.5).astype(BF16)
    qi = _rope(jnp.dot(cq, wqi_ref[...], preferred_element_type=F32), cos_i, sia, sib, IDX_DIM // 8)
    qi_o[...] = qi.astype(BF16)
    kv = f[:, 128:256]
    kv_r = _rope(kv, cos_q[:, 0:128], sqa[:, 0:128], sqb[:, 0:128], HEAD_DIM // 8)
    k_o[...] = kv_r[:, 0:HEAD_DIM].astype(BF16)
    v_o[...] = kv[:, HEAD_DIM:128].astype(BF16)
    tail = f[:, 256:384]
    tail_r = _rope(tail, cos_i[:, 0:128], sia[:, 0:128], sib[:, 0:128], IDX_DIM // 8)
    ki_o[...] = tail_r[:, 0:IDX_DIM].astype(BF16)
    w_o[...] = tail[:, IDX_DIM:IDX_DIM + IDX_HEADS] * (IDX_HEADS ** -0.5 * IDX_DIM ** -0.5)


def _dsa_prep(f2, qn, wq, wqi, tq, ti):
    N, W = f2.shape
    tm = ROW_TILE
    G = GROUP_W
    row = lambda w: pl.BlockSpec((tm, w), lambda i: (i, 0))
    c = lambda shape: pl.BlockSpec(shape, lambda i: (0, 0))
    return pl.pallas_call(
        _dsa_prep_kernel,
        grid=(N // tm,),
        in_specs=[row(W), c((1, DSA_Q_LORA)), c((DSA_Q_LORA, G)), c((DSA_Q_LORA, G))] + [row(G)] * 6,
        out_specs=[row(G), row(G), row(HEAD_DIM), row(HEAD_DIM), row(IDX_DIM), row(IDX_HEADS)],
        out_shape=[jax.ShapeDtypeStruct((N, G), BF16), jax.ShapeDtypeStruct((N, G), BF16),
                   jax.ShapeDtypeStruct((N, HEAD_DIM), BF16), jax.ShapeDtypeStruct((N, HEAD_DIM), BF16),
                   jax.ShapeDtypeStruct((N, IDX_DIM), BF16), jax.ShapeDtypeStruct((N, IDX_HEADS), F32)],
        compiler_params=_params(("parallel",)),
        name="dsa_prep",
    )(f2, qn, wq, wqi, *tq, *ti)


def _dsa_kernel(TK, topk, q_ref, qi_ref, w_ref, k_ref, v_ref, ki_ref, on_ref, o_ref, skey_ref, out_ref):
    i = pl.program_id(1)
    TQ = Q_BLOCK
    n_kt = (i * TQ + TQ + TK - 1) // TK
    qi = qi_ref[0]
    w = w_ref[0]
    rpos = i * TQ + lax.broadcasted_iota(I32, (TQ, TK), 0)
    cpos = lax.broadcasted_iota(I32, (TQ, TK), 1)

    def score_tile(kt, carry):
        kid = ki_ref[0, pl.ds(pl.multiple_of(kt * TK, TK), TK), :]
        sc = jnp.zeros((TQ, TK), F32)
        for h in range(IDX_HEADS):
            rel = lax.dot_general(qi[:, h * IDX_DIM:(h + 1) * IDX_DIM], kid, NT_DIMS,
                                  preferred_element_type=F32)
            sc = sc + jnp.maximum(rel, 0.0) * w[:, h:h + 1]
        bits = pltpu.bitcast(sc, I32)
        key = bits ^ ((bits >> 31) & 0x7FFFFFFF)
        skey_ref[kt] = jnp.where(kt * TK + cpos <= rpos, key, INT_MIN)
        return carry

    lax.fori_loop(0, n_kt, score_tile, 0)

    def bit_step(j, thr):
        cand = thr + lax.shift_left(jnp.int32(1), 31 - j)

        def count_tile(kt, acc):
            ge = jnp.where(skey_ref[kt] >= cand, 1.0, 0.0)
            for c0 in range(0, TK, LANES):
                acc = acc + ge[:, c0:c0 + LANES]
            return acc

        acc = lax.fori_loop(0, n_kt, count_tile, jnp.zeros((TQ, LANES), F32))
        cnt = jnp.sum(acc, axis=-1, keepdims=True)
        return jnp.where(cnt >= topk, cand, thr)

    thr = lax.fori_loop(0, 32, bit_step, jnp.full((TQ, 1), INT_MIN, I32))
    thr = jnp.maximum(thr, INT_MIN + 1)

    q = q_ref[0]

    def attn_tile(kt, carry):
        sel = skey_ref[kt] >= thr
        off = pl.multiple_of(kt * TK, TK)
        kb = k_ref[0, pl.ds(off, TK), :]
        vb = v_ref[0, pl.ds(off, TK), :]
        new = []
        for h in range(N_HEADS_G):
            m, l, acc = carry[h]
            lg = lax.dot_general(q[:, h * HEAD_DIM:(h + 1) * HEAD_DIM], kb, NT_DIMS,
                                 preferred_element_type=F32)
            lg = jnp.where(sel, lg, -1e30)
            m_new = jnp.maximum(m, jnp.max(lg, axis=-1, keepdims=True))
            p = jnp.where(sel, jnp.exp(lg - m_new), 0.0)
            alpha = jnp.exp(m - m_new)
            l = alpha * l + jnp.sum(p, axis=-1, keepdims=True)
            acc = alpha * acc + jnp.dot(p.astype(BF16), vb, preferred_element_type=F32)
            new.append((m_new, l, acc))
        return tuple(new)

    init = tuple((jnp.full((TQ, 1), -1e30, F32), jnp.zeros((TQ, 1), F32), jnp.zeros((TQ, HEAD_DIM), F32))
                 for _ in range(N_HEADS_G))
    fin = lax.fori_loop(0, n_kt, attn_tile, init)
    for h in range(N_HEADS_G):
        m, l, acc = fin[h]
        out_ref[:, h * HEAD_DIM:(h + 1) * HEAD_DIM] = acc / l
    o_ref[0] = _rms(out_ref[...], on_ref[...])


def _dsa_attention(q, qi, w, k, v, ki, on):
    B, S, G = q.shape
    TK = min(512, S)
    topk = min(DSA_TOPK_MAX, S // 4)
    qb = lambda wd: pl.BlockSpec((1, Q_BLOCK, wd), lambda b, i: (b, i, 0))
    full = lambda wd: pl.BlockSpec((1, S, wd), lambda b, i: (b, 0, 0))
    return pl.pallas_call(
        functools.partial(_dsa_kernel, TK, topk),
        grid=(B, S // Q_BLOCK),
        in_specs=[qb(G), qb(G), qb(IDX_HEADS), full(HEAD_DIM), full(HEAD_DIM), full(IDX_DIM),
                  pl.BlockSpec((1, G), lambda b, i: (0, 0))],
        out_specs=qb(G),
        out_shape=jax.ShapeDtypeStruct((B, S, G), F32),
        scratch_shapes=[pltpu.VMEM((S // TK, Q_BLOCK, TK), I32), pltpu.VMEM((Q_BLOCK, G), F32)],
        compiler_params=_params(("parallel", "arbitrary")),
        name="dsa_attention",
    )(q, qi, w, k, v, ki, on)


def _sb_kernel(q_ref, k_ref, v_ref, u_ref, on_ref, o_ref, out_ref):
    i = pl.program_id(1)
    T = Q_BLOCK
    u = u_ref[...]
    rid = lax.broadcasted_iota(I32, (T, T), 0)
    cid = lax.broadcasted_iota(I32, (T, T), 1)
    strict = cid < rid
    for h in range(N_HEADS_G):
        sl = slice(h * HEAD_DIM, (h + 1) * HEAD_DIM)
        qh = q_ref[0, :, sl]

        def cond(carry):
            kt, c, _ = carry
            return jnp.logical_and(kt >= 0, jnp.max(c) > SB_UNDERFLOW)

        def body(carry):
            kt, c, acc = carry
            off = pl.multiple_of(kt * T, T)
            kb = k_ref[0, pl.ds(off, T), sl]
            vb = v_ref[0, pl.ds(off, T), sl]
            z = lax.dot_general(qh, kb, NT_DIMS, preferred_element_type=F32) * HEAD_DIM ** -0.5
            mask = jnp.logical_or(strict, kt < i)
            lm = jnp.where(mask, -_softplus(z), 0.0)
            cs = _split_dot(lm, u)
            a = jnp.where(mask, jnp.exp(z + lm + (c + cs)), 0.0)
            acc = acc + jnp.dot(a.astype(BF16), vb, preferred_element_type=F32)
            c = c + cs[:, 0:1] + lm[:, 0:1]
            return kt - 1, c, acc

        _, _, acc = lax.while_loop(cond, body, (i, jnp.zeros((T, 1), F32), jnp.zeros((T, HEAD_DIM), F32)))
        out_ref[:, sl] = acc
    o_ref[0] = _rms(out_ref[...], on_ref[...])


def _stick_breaking(sb3, u, on):
    B, S, _ = sb3.shape
    G = GROUP_W
    return pl.pallas_call(
        _sb_kernel,
        grid=(B, S // Q_BLOCK),
        in_specs=[pl.BlockSpec((1, Q_BLOCK, G), lambda b, i: (b, i, 0)),
                  pl.BlockSpec((1, S, G), lambda b, i: (b, 0, 1)),
                  pl.BlockSpec((1, S, G), lambda b, i: (b, 0, 2)),
                  pl.BlockSpec((Q_BLOCK, Q_BLOCK), lambda b, i: (0, 0)),
                  pl.BlockSpec((1, G), lambda b, i: (0, 0))],
        out_specs=pl.BlockSpec((1, Q_BLOCK, G), lambda b, i: (b, i, 0)),
        out_shape=jax.ShapeDtypeStruct((B, S, G), F32),
        scratch_shapes=[pltpu.VMEM((Q_BLOCK, G), F32)],
        compiler_params=_params(("parallel", "arbitrary")),
        name="stick_breaking",
    )(sb3, sb3, sb3, u, on)


def _outproj_kernel(yr_ref, yw_ref, yd_ref, ys_ref, x_ref, g1_ref, sc_ref, sh_ref, gn_ref, wo_ref, rw_ref, rb_ref,
                    x_o, h_o, lg_o):
    G = GROUP_W
    mixed = (_bdot(yr_ref[...], wo_ref[0:G, :]) + _bdot(yw_ref[...], wo_ref[G:2 * G, :])
             + _bdot(yd_ref[...], wo_ref[2 * G:3 * G, :]) + _bdot(ys_ref[...], wo_ref[3 * G:4 * G, :]))
    x1 = x_ref[...] + g1_ref[0] * mixed
    x_o[...] = x1
    h = _rms(x1, gn_ref[...]) * (1.0 + sc_ref[0]) + sh_ref[0]
    h_o[...] = h.astype(BF16)
    lg_o[...] = jnp.dot(h, rw_ref[...], preferred_element_type=F32, precision=HIGHEST) + rb_ref[...]


def _outproj(ys, x2, mod6, gain, w_out, router_w, router_b, S):
    N, D = x2.shape
    tm = ROW_TILE
    per_b = S // tm
    G = GROUP_W
    row = lambda w: pl.BlockSpec((tm, w), lambda i: (i, 0))
    c = lambda shape: pl.BlockSpec(shape, lambda i: (0, 0))
    modrow = lambda j: pl.BlockSpec((1, 1, D), lambda i, j=j: ((i // per_b) * 6 + j, 0, 0))
    return pl.pallas_call(
        _outproj_kernel,
        grid=(N // tm,),
        in_specs=[row(G)] * 4 + [row(D), modrow(2), modrow(4), modrow(3), c((1, D)), c((D, D)),
                                 c((D, N_EXPERTS)), c((1, N_EXPERTS))],
        out_specs=[row(D), row(D), row(N_EXPERTS)],
        out_shape=[jax.ShapeDtypeStruct((N, D), F32), jax.ShapeDtypeStruct((N, D), BF16),
                   jax.ShapeDtypeStruct((N, N_EXPERTS), F32)],
        compiler_params=_params(("parallel",)),
        name="out_proj_router",
    )(*ys, x2, mod6, mod6, mod6, gain, w_out, router_w, router_b)


def _moe_kernel(be_ref, nu_ref, x_ref, w1_ref, b1_ref, w2_ref, b2_ref, rw_ref, o_ref):
    i = pl.program_id(0)

    @pl.when(i < nu_ref[0])
    def _():
        h = jnp.dot(x_ref[...], w1_ref[0], preferred_element_type=F32) + b1_ref[0]
        glu = jnp.minimum(h[:, 0:D_FF], SWIGLU_LIMIT)
        lin = jnp.clip(h[:, D_FF:], -SWIGLU_LIMIT, SWIGLU_LIMIT)
        act = glu * _sigmoid(SWIGLU_ALPHA * glu) * (lin + 1.0)
        y = jnp.dot(act.astype(BF16), w2_ref[0], preferred_element_type=F32) + b2_ref[0]
        o_ref[...] = y * rw_ref[...]

    @pl.when(i >= nu_ref[0])
    def _():
        o_ref[...] = jnp.zeros_like(o_ref)


def _moe_ffn(block_expert, n_used, xs, w1, b1, w2, b2, row_w):
    rows, D = xs.shape
    nb = rows // MOE_BLOCK
    E = w1.shape[0]
    live = lambda i, nu: jnp.minimum(i, nu[0] - 1)
    grid_spec = pltpu.PrefetchScalarGridSpec(
        num_scalar_prefetch=2,
        grid=(nb,),
        in_specs=[pl.BlockSpec((MOE_BLOCK, D), lambda i, be, nu: (live(i, nu), 0)),
                  pl.BlockSpec((1, D, 2 * D_FF), lambda i, be, nu: (be[i], 0, 0)),
                  pl.BlockSpec((1, 1, 2 * D_FF), lambda i, be, nu: (be[i], 0, 0)),
                  pl.BlockSpec((1, D_FF, D), lambda i, be, nu: (be[i], 0, 0)),
                  pl.BlockSpec((1, 1, D), lambda i, be, nu: (be[i], 0, 0)),
                  pl.BlockSpec((MOE_BLOCK, 1), lambda i, be, nu: (live(i, nu), 0))],
        out_specs=pl.BlockSpec((MOE_BLOCK, D), lambda i, be, nu: (i, 0)),
    )
    return pl.pallas_call(
        _moe_kernel,
        grid_spec=grid_spec,
        out_shape=jax.ShapeDtypeStruct((rows, D), F32),
        compiler_params=_params(("arbitrary",)),
        name="moe_ffn",
    )(block_expert, n_used, xs, w1, b1.reshape(E, 1, -1), w2, b2.reshape(E, 1, -1), row_w)


def _route(logits):
    n = logits.shape[0]
    top_val, top_idx = lax.top_k(logits, TOP_K)
    gate = jax.nn.softmax(top_val, axis=-1)
    expert = top_idx.reshape(-1).astype(I32)
    onehot = (expert[:, None] == jnp.arange(N_EXPERTS, dtype=I32)[None, :]).astype(I32)
    csum = jnp.cumsum(onehot, axis=0)
    counts = csum[-1]
    rank = jnp.take_along_axis(csum, expert[:, None], axis=1)[:, 0] - 1
    padded = (counts + MOE_BLOCK - 1) // MOE_BLOCK * MOE_BLOCK
    pad_end = jnp.cumsum(padded)
    pad_start = pad_end - padded
    dest = pad_start[expert] + rank
    n_assign = n * TOP_K
    n_blocks = -(-n_assign // MOE_BLOCK) + N_EXPERTS
    rows = n_blocks * MOE_BLOCK
    token = jnp.arange(n_assign, dtype=I32) // TOP_K
    row_token = jnp.zeros((rows,), I32).at[dest].set(token)
    row_weight = jnp.zeros((rows,), F32).at[dest].set(gate.reshape(-1))
    block_start = jnp.arange(n_blocks, dtype=I32) * MOE_BLOCK
    block_expert = jnp.minimum(jnp.sum(pad_end[None, :] <= block_start[:, None], axis=1), N_EXPERTS - 1)
    n_used = (pad_end[-1] // MOE_BLOCK).astype(I32).reshape(1)
    return dest, row_token, row_weight, block_expert.astype(I32), n_used


def _final_kernel(x_ref, g_ref, o_ref):
    o_ref[...] = _rms(x_ref[...], g_ref[...])


def _final_norm(x2, g):
    N, D = x2.shape
    tm = ROW_TILE
    row = pl.BlockSpec((tm, D), lambda i: (i, 0))
    return pl.pallas_call(
        _final_kernel, grid=(N // tm,),
        in_specs=[row, pl.BlockSpec((1, D), lambda i: (0, 0))], out_specs=row,
        out_shape=jax.ShapeDtypeStruct((N, D), F32),
        compiler_params=_params(("parallel",)), name="final_norm",
    )(x2, g)


def _rope_tables(pos, group, rot_dim, theta, width):
    half = rot_dim // 2
    inv = theta ** (-jnp.arange(half, dtype=F32) / half)
    ang = pos.astype(F32)[..., None] * inv
    cos, sin = jnp.cos(ang), jnp.sin(ang)
    rest = group - rot_dim
    ones = jnp.ones(ang.shape[:2] + (rest,), F32)
    zeros = jnp.zeros(ang.shape[:2] + (rest,), F32)
    zh = jnp.zeros_like(sin)
    rep = width // group
    tile = lambda t: jnp.tile(t, (1, 1, rep)).reshape(-1, width)
    return (tile(jnp.concatenate([cos, cos, ones], -1)),
            tile(jnp.concatenate([-sin, zh, zeros], -1)),
            tile(jnp.concatenate([zh, sin, zeros], -1)))


def _pad_rows(w, start, total=128):
    return jnp.zeros((total, w.shape[1]), w.dtype).at[start:start + w.shape[0]].set(w)


def kernel(x, c, positions, ada_w, ada_b, norm_mix, norm_ffn, w_in, ret_gn, rwkv_mu, rwkv_w0, rwkv_w2, rwkv_a0,
           rwkv_a2, rwkv_g2, rwkv_kk, rwkv_ka, rwkv_rk, rwkv_ln, dsa_qnorm, dsa_wq_up, dsa_wqi_up, dsa_onorm,
           sb_onorm, w_out, router_w, router_b, moe_w1, moe_b1, moe_w2, moe_b2, norm_final):
    B, S, D = x.shape
    N = B * S
    L = ada_w.shape[0]
    G = GROUP_W
    row2 = lambda t: t.reshape(1, -1)

    mod = _ada(c, ada_w, ada_b)
    ret_tabs = tuple(t.reshape(B, S, G) for t in _rope_tables(positions, HEAD_DIM, HEAD_DIM, RET_THETA, G))
    dq_tabs = _rope_tables(positions, HEAD_DIM, HEAD_DIM // 4, ROPE_THETA, G)
    di_tabs = _rope_tables(positions, IDX_DIM, IDX_DIM // 4, ROPE_THETA, G)
    ret_c = _ret_consts(min(128, S))
    hm = (jnp.arange(G)[:, None] // HEAD_DIM == jnp.arange(G)[None, :] // HEAD_DIM).astype(BF16)
    u_tri = (jnp.arange(Q_BLOCK)[:, None] > jnp.arange(Q_BLOCK)[None, :]).astype(BF16)
    o0, o1, o2 = RET_COLS, RET_COLS + RWKV_COLS, RET_COLS + RWKV_COLS + DSA_COLS

    x2 = x.reshape(N, D)
    for l in range(L):
        mod6 = mod[l].reshape(B * 6, 1, D)
        w_l = w_in[l]
        w_cat = jnp.concatenate(
            [w_l[:, :o1], w_l[:, o1:o2], jnp.zeros((D, DSA_PAD - DSA_COLS), F32), w_l[:, o2:]], axis=1).astype(BF16)
        ret, rwkv, dsa, sb = _inproj(x2, row2(norm_mix[l]), mod6, w_cat, S)

        y_ret = _retention(ret.reshape(B, S, RET_COLS), ret_tabs, ret_c, row2(ret_gn[l])).reshape(N, G)

        prep = _rwkv_prep(rwkv, S, row2(rwkv_mu[l]), row2(rwkv_w0[l]), row2(rwkv_a0[l]),
                          _pad_rows(rwkv_w2[l], 0).astype(BF16), _pad_rows(rwkv_a2[l], 32).astype(BF16),
                          _pad_rows(rwkv_g2[l], 64).astype(BF16), row2(rwkv_kk[l]), row2(rwkv_ka[l]),
                          row2(rwkv_rk[l]), hm)
        r_, w_, k_, v_, nkk_, b_, gate_, bonus_ = prep
        y_scan = _rwkv_scan(_to_k_layout(nkk_, B, S), _to_k_layout(w_, B, S), _to_k_layout(b_, B, S),
                            _to_k_layout(k_, B, S), _to_k_layout(r_, B, S), _to_v_layout(v_, B, S))
        y_rwkv = _rwkv_post(_from_v_layout(y_scan, B, S), bonus_, gate_, row2(rwkv_ln[l]), hm)

        dq, dqi, dk, dv, dki, dw = _dsa_prep(dsa, row2(dsa_qnorm[l]), dsa_wq_up[l].astype(BF16),
                                             dsa_wqi_up[l].astype(BF16), dq_tabs, di_tabs)
        y_dsa = _dsa_attention(dq.reshape(B, S, G), dqi.reshape(B, S, G), dw.reshape(B, S, IDX_HEADS),
                               dk.reshape(B, S, HEAD_DIM), dv.reshape(B, S, HEAD_DIM),
                               dki.reshape(B, S, IDX_DIM), row2(dsa_onorm[l])).reshape(N, G)

        y_sb = _stick_breaking(sb.reshape(B, S, SB_COLS), u_tri, row2(sb_onorm[l])).reshape(N, G)

        x1, h2, logits = _outproj((y_ret, y_rwkv, y_dsa, y_sb), x2, mod6, row2(norm_ffn[l]),
                                  w_out[l].astype(BF16), router_w[l], row2(router_b[l]), S)

        dest, row_token, row_weight, block_expert, n_used = _route(logits)
        xs = jnp.take(h2, row_token, axis=0)
        y = _moe_ffn(block_expert, n_used, xs, moe_w1[l].astype(BF16), moe_b1[l], moe_w2[l].astype(BF16),
                     moe_b2[l], row_weight[:, None])
        moe_out = jnp.take(y, dest, axis=0).reshape(N, TOP_K, D).sum(axis=1)
        g2 = jnp.broadcast_to(mod[l][:, None, 5 * D:6 * D], (B, S, D)).reshape(N, D)
        x2 = x1 + g2 * moe_out
    return _final_norm(x2, row2(norm_final)).reshape(B, S, D)
```

```python
import functools

import jax
import jax.numpy as jnp
from jax import lax
from jax.experimental import pallas as pl
from jax.experimental.pallas import tpu as pltpu

F32 = jnp.float32
BF16 = jnp.bfloat16
I32 = jnp.int32

D_MODEL = 1024
GROUP_W = 256
HEAD_DIM = 64
N_HEADS_G = 4
NORM_EPS = 1e-5
Q_BLOCK = 128
RET_THETA = 10000.0
RWKV_LN_EPS = 64e-5
DSA_Q_LORA = 128
IDX_HEADS = 8
IDX_DIM = 32
DSA_TOPK_MAX = 256
ROPE_THETA = 500000.0
N_EXPERTS = 32
TOP_K = 4
D_FF = D_MODEL
SWIGLU_ALPHA = 1.702
SWIGLU_LIMIT = 7.0
MOE_BLOCK = 512

RET_COLS = 4 * GROUP_W
RWKV_COLS = 3 * GROUP_W + 128
DSA_COLS = 296
DSA_PAD = 384
SB_COLS = 3 * GROUP_W

LANES = 128
SUBLANES = 8
ROW_TILE = 512
VMEM_LIMIT = 48 * 1024 * 1024
INT_MIN = -2 ** 31
SB_UNDERFLOW = -104.0

HIGHEST = lax.Precision.HIGHEST
NT_DIMS = (((1,), (1,)), ((), ()))
TN_DIMS = (((0,), (0,)), ((), ()))


def _params(sem, vmem=VMEM_LIMIT):
    return pltpu.CompilerParams(dimension_semantics=sem, vmem_limit_bytes=vmem)


def _bdot(a, b):
    return jnp.dot(a.astype(BF16), b.astype(BF16), preferred_element_type=F32)


def _split(x):
    hi = x.astype(BF16)
    return hi, (x - hi.astype(F32)).astype(BF16)


def _split_dot(x, m):
    hi, lo = _split(x)
    return jnp.dot(hi, m, preferred_element_type=F32) + jnp.dot(lo, m, preferred_element_type=F32)


def _tree(parts, op):
    while len(parts) > 1:
        parts = [op(parts[i], parts[i + 1]) for i in range(0, len(parts), 2)]
    return parts[0]


def _sigmoid(x):
    return 1.0 / (1.0 + jnp.exp(-x))


def _softplus(x):
    return jnp.maximum(x, 0.0) + jnp.log(1.0 + jnp.exp(-jnp.abs(x)))


def _rms(x, g, eps=NORM_EPS):
    return x * lax.rsqrt(jnp.mean(x * x, axis=-1, keepdims=True) + eps) * g


def _rope(x, cos, sin_a, sin_b, half):
    w = x.shape[-1]
    return x * cos + pltpu.roll(x, w - half, 1) * sin_a + pltpu.roll(x, half, 1) * sin_b


def _ada_kernel(c_ref, w_ref, b_ref, o_ref):
    c = c_ref[...]
    cond = c * _sigmoid(c)
    o_ref[0] = jnp.dot(cond, w_ref[0], preferred_element_type=F32, precision=HIGHEST) + b_ref[0]


def _ada(c, ada_w, ada_b):
    L, D, W = ada_w.shape
    B = c.shape[0]
    tn = 1024
    return pl.pallas_call(
        _ada_kernel,
        grid=(L, W // tn),
        in_specs=[pl.BlockSpec((B, D), lambda l, j: (0, 0)),
                  pl.BlockSpec((1, D, tn), lambda l, j: (l, 0, j)),
                  pl.BlockSpec((1, 1, tn), lambda l, j: (l, 0, j))],
        out_specs=pl.BlockSpec((1, B, tn), lambda l, j: (l, 0, j)),
        out_shape=jax.ShapeDtypeStruct((L, B, W), F32),
        compiler_params=_params(("parallel", "parallel")),
        name="ada_mod",
    )(c, ada_w, ada_b.reshape(L, 1, W))


def _combine(x_ref, g2_ref, y_refs):
    moe = y_refs[0][...].astype(F32)
    for y_ref in y_refs[1:]:
        moe = moe + y_ref[...].astype(F32)
    return x_ref[...] + g2_ref[0] * moe


def _inproj_kernel(n_comb, x_ref, *refs):
    if n_comb:
        g2_ref, y_refs, refs = refs[0], refs[1:1 + n_comb], refs[1 + n_comb:]
        g_ref, sc_ref, sh_ref, w_ref, x_o, ret_ref, rwkv_ref, dsa_ref, sb_ref = refs
        x = _combine(x_ref, g2_ref, y_refs)
        x_o[...] = x
    else:
        g_ref, sc_ref, sh_ref, w_ref, ret_ref, rwkv_ref, dsa_ref, sb_ref = refs
        x = x_ref[...]
    h = _rms(x, g_ref[...]) * (1.0 + sc_ref[0]) + sh_ref[0]
    hb = h.astype(BF16)
    o0, o1, o2 = RET_COLS, RET_COLS + RWKV_COLS, RET_COLS + RWKV_COLS + DSA_PAD
    ret_ref[...] = jnp.dot(hb, w_ref[:, 0:o0], preferred_element_type=F32)
    rwkv_ref[...] = jnp.dot(hb, w_ref[:, o0:o1], preferred_element_type=F32)
    dsa_ref[...] = jnp.dot(hb, w_ref[:, o1:o2], preferred_element_type=F32)
    sb_ref[...] = jnp.dot(hb, w_ref[:, o2:], preferred_element_type=F32).astype(BF16)


def _mod_row(per_b, j):
    return pl.BlockSpec((1, 1, D_MODEL), lambda i: ((i // per_b) * 6 + j, 0, 0))


def _inproj(x2, gain, mod6, w_cat, S, comb=None):
    N, D = x2.shape
    tm = ROW_TILE
    per_b = S // tm
    wt = w_cat.shape[1]
    row = lambda i: (i, 0)
    in_specs = [pl.BlockSpec((tm, D), row)]
    args = [x2]
    out_specs, out_shape = [], []
    if comb is not None:
        in_specs += [_mod_row(per_b, 5)] + [pl.BlockSpec((tm, D), row)] * len(comb[1])
        args += [comb[0]] + list(comb[1])
        out_specs.append(pl.BlockSpec((tm, D), row))
        out_shape.append(jax.ShapeDtypeStruct((N, D), F32))
    in_specs += [pl.BlockSpec((1, D), lambda i: (0, 0)), _mod_row(per_b, 1), _mod_row(per_b, 0),
                 pl.BlockSpec((D, wt), lambda i: (0, 0))]
    args += [gain, mod6, mod6, w_cat]
    out_specs += [pl.BlockSpec((tm, RET_COLS), row), pl.BlockSpec((tm, RWKV_COLS), row),
                  pl.BlockSpec((tm, DSA_PAD), row), pl.BlockSpec((tm, SB_COLS), row)]
    out_shape += [jax.ShapeDtypeStruct((N, RET_COLS), F32), jax.ShapeDtypeStruct((N, RWKV_COLS), F32),
                  jax.ShapeDtypeStruct((N, DSA_PAD), F32), jax.ShapeDtypeStruct((N, SB_COLS), BF16)]
    return pl.pallas_call(
        functools.partial(_inproj_kernel, 0 if comb is None else len(comb[1])),
        grid=(N // tm,),
        in_specs=in_specs, out_specs=out_specs, out_shape=out_shape,
        compiler_params=_params(("parallel",)),
        name="in_proj",
    )(*args)


def _ret_kernel(q_ref, k_ref, v_ref, g_ref, cos_ref, sa_ref, sb_ref, din_ref, qd_ref, kd_ref, cd_ref, gn_ref,
                o_ref, state_ref, y_ref):
    @pl.when(pl.program_id(1) == 0)
    def _():
        state_ref[...] = jnp.zeros_like(state_ref)

    cos, sin_a, sin_b = cos_ref[0], sa_ref[0], sb_ref[0]
    q = _rope(q_ref[0], cos, sin_a, sin_b, HEAD_DIM // 2)
    k = _rope(k_ref[0], cos, sin_a, sin_b, HEAD_DIM // 2) * HEAD_DIM ** -0.5
    v = v_ref[0]
    qd = q * qd_ref[...]
    kd = k * kd_ref[...]
    for h in range(N_HEADS_G):
        sl = slice(h * HEAD_DIM, (h + 1) * HEAD_DIM)
        qh, kh, vh = q[:, sl].astype(BF16), k[:, sl].astype(BF16), v[:, sl].astype(BF16)
        s = lax.dot_general(qh, kh, NT_DIMS, preferred_element_type=F32) * din_ref[h]
        inner = jnp.dot(s.astype(BF16), vh, preferred_element_type=F32)
        st = state_ref[h]
        cross = _bdot(qd[:, sl], st)
        state_ref[h] = st * cd_ref[:, sl] + lax.dot_general(
            kd[:, sl].astype(BF16), vh, TN_DIMS, preferred_element_type=F32)
        o = inner + cross
        oc = o - jnp.mean(o, axis=-1, keepdims=True)
        y_ref[:, sl] = oc * lax.rsqrt(jnp.mean(oc * oc, axis=-1, keepdims=True) + NORM_EPS)
    g = g_ref[0]
    o_ref[0] = (g * _sigmoid(g) * (y_ref[...] * gn_ref[...])).astype(BF16)


def _retention(ret3, tabs, consts, gn):
    B, S, _ = ret3.shape
    C = consts["din"].shape[1]
    blk = lambda j: pl.BlockSpec((1, C, GROUP_W), lambda b, c, j=j: (b, c, j))
    tab = pl.BlockSpec((1, C, GROUP_W), lambda b, c: (b, c, 0))
    const2 = lambda shape: pl.BlockSpec(shape, lambda b, c: (0,) * len(shape))
    return pl.pallas_call(
        _ret_kernel,
        grid=(B, S // C),
        in_specs=[blk(0), blk(1), blk(2), blk(3), tab, tab, tab,
                  const2((N_HEADS_G, C, C)), const2((C, GROUP_W)), const2((C, GROUP_W)),
                  const2((1, GROUP_W)), const2((1, GROUP_W))],
        out_specs=pl.BlockSpec((1, C, GROUP_W), lambda b, c: (b, c, 0)),
        out_shape=jax.ShapeDtypeStruct((B, S, GROUP_W), BF16),
        scratch_shapes=[pltpu.VMEM((N_HEADS_G, HEAD_DIM, HEAD_DIM), F32), pltpu.VMEM((C, GROUP_W), F32)],
        compiler_params=_params(("parallel", "arbitrary")),
        name="retention",
    )(ret3, ret3, ret3, ret3, tabs[0], tabs[1], tabs[2],
      consts["din"], consts["qd"], consts["kd"], consts["cd"], gn)


def _ret_consts(C):
    H = N_HEADS_G
    lg = jnp.log(1.0 - 2.0 ** (-5.0 - jnp.arange(H, dtype=F32)))
    idx = jnp.arange(C, dtype=F32)
    diff = idx[:, None] - idx[None, :]
    din = jnp.where(diff >= 0, jnp.exp(lg[:, None, None] * jnp.maximum(diff, 0.0)), 0.0)
    q_dec = jnp.exp(lg[:, None] * (idx + 1.0))
    k_dec = jnp.exp(lg[:, None] * (C - 1.0 - idx))
    chunk_dec = jnp.exp(lg * C)
    wide = lambda t: jnp.repeat(t.T, HEAD_DIM, axis=1)
    return {"din": din, "qd": wide(q_dec), "kd": wide(k_dec),
            "cd": jnp.repeat(chunk_dec, HEAD_DIM)[None, :]}


def _rwkv_prep_kernel(per_b, z_ref, zp_ref, mu_ref, w0_ref, a0_ref, w2_ref, a2_ref, g2_ref, kk_ref, ka_ref,
                      rk_ref, hm_ref, r_o, w_o, k_o, v_o, nkk_o, b_o, gate_o, bonus_o):
    z = z_ref[...]
    first = (pl.program_id(0) % per_b) == 0
    prow = jnp.where(first, 0.0, zp_ref[7:8, :])
    rid = lax.broadcasted_iota(I32, z.shape, 0)
    prev = jnp.where(rid == 0, prow, pltpu.roll(z, 1, 0))
    f = z + (prev - z) * mu_ref[...]
    G = GROUP_W
    r, k, v, lo = f[:, 0:G], f[:, G:2 * G], f[:, 2 * G:3 * G], f[:, 3 * G:3 * G + 128]
    w_log = -_softplus(-(w0_ref[...] + _bdot(jnp.tanh(lo), w2_ref[...]))) - 0.5
    decay = jnp.exp(-jnp.exp(w_log))
    a = _sigmoid(a0_ref[...] + _bdot(lo, a2_ref[...]))
    gate = _bdot(_sigmoid(lo), g2_ref[...])
    hm = hm_ref[...]
    kk = k * kk_ref[...]
    kk = kk / jnp.maximum(jnp.sqrt(_split_dot(kk * kk, hm)), 1e-12)
    k2 = k * (1.0 + (a - 1.0) * ka_ref[...])
    r_o[...] = r
    w_o[...] = decay
    k_o[...] = k2
    v_o[...] = v
    nkk_o[...] = -kk
    b_o[...] = kk * a
    gate_o[...] = gate
    bonus_o[...] = _split_dot(r * k2 * rk_ref[...], hm) * v


def _rwkv_prep(z2, S, mu, w0, a0, w2p, a2p, g2p, kk, ka, rk, hm):
    N, W = z2.shape
    tm = ROW_TILE
    per_b = S // tm
    row = lambda i: (i, 0)
    c = lambda shape: pl.BlockSpec(shape, lambda i: (0, 0))
    G = GROUP_W
    return pl.pallas_call(
        functools.partial(_rwkv_prep_kernel, per_b),
        grid=(N // tm,),
        in_specs=[pl.BlockSpec((tm, W), row),
                  pl.BlockSpec((SUBLANES, W), lambda i: (jnp.maximum(i * (tm // SUBLANES) - 1, 0), 0)),
                  c((1, W)), c((1, G)), c((1, G)), c((128, G)), c((128, G)), c((128, G)),
                  c((1, G)), c((1, G)), c((1, G)), c((G, G))],
        out_specs=[pl.BlockSpec((tm, G), row)] * 8,
        out_shape=[jax.ShapeDtypeStruct((N, G), F32)] * 8,
        compiler_params=_params(("parallel",)),
        name="rwkv_prep",
    )(z2, z2, mu, w0, a0, w2p, a2p, g2p, kk, ka, rk, hm)


def _rwkv_scan_kernel(T, nkk_ref, w_ref, b_ref, k_ref, r_ref, v_ref, y_ref, s_ref):
    @pl.when(pl.program_id(0) == 0)
    def _():
        s_ref[...] = jnp.zeros_like(s_ref)

    nk = s_ref.shape[0]
    tree = lambda parts: _tree(parts, jnp.add)

    def step(t, carry):
        vt = v_ref[t]
        sa = tree([s_ref[k] * nkk_ref[t, k:k + 1, :] for k in range(nk)])
        ys = []
        for k in range(nk):
            s_new = (s_ref[k] * w_ref[t, k:k + 1, :] + sa * b_ref[t, k:k + 1, :]
                     + vt * k_ref[t, k:k + 1, :])
            s_ref[k] = s_new
            ys.append(s_new * r_ref[t, k:k + 1, :])
        y_ref[t] = tree(ys)
        return carry

    lax.fori_loop(0, T, step, 0)


def _rwkv_scan(nkk, w, b, k, r, v):
    S, nk, _ = nkk.shape
    vh = v.shape[1]
    T = 32
    kspec = pl.BlockSpec((T, nk, LANES), lambda i: (i, 0, 0))
    vspec = pl.BlockSpec((T, vh, LANES), lambda i: (i, 0, 0))
    return pl.pallas_call(
        functools.partial(_rwkv_scan_kernel, T),
        grid=(S // T,),
        in_specs=[kspec] * 5 + [vspec],
        out_specs=vspec,
        out_shape=jax.ShapeDtypeStruct((S, vh, LANES), F32),
        scratch_shapes=[pltpu.VMEM((nk, vh, LANES), F32)],
        compiler_params=_params(("arbitrary",)),
        name="rwkv_scan",
    )(nkk, w, b, k, r, v)


def _rwkv_post_kernel(y_ref, bonus_ref, gate_ref, ln_ref, hm_ref, o_ref):
    y = y_ref[...]
    hm = hm_ref[...]
    yc = y - _split_dot(y, hm) * (1.0 / HEAD_DIM)
    var = _split_dot(yc * yc, hm) * (1.0 / HEAD_DIM)
    o_ref[...] = ((yc * lax.rsqrt(var + RWKV_LN_EPS) * ln_ref[...] + bonus_ref[...]) * gate_ref[...]).astype(BF16)


def _rwkv_post(y2, bonus, gate, ln, hm):
    N, G = y2.shape
    tm = ROW_TILE
    row = pl.BlockSpec((tm, G), lambda i: (i, 0))
    return pl.pallas_call(
        _rwkv_post_kernel,
        grid=(N // tm,),
        in_specs=[row, row, row, pl.BlockSpec((1, G), lambda i: (0, 0)), pl.BlockSpec((G, G), lambda i: (0, 0))],
        out_specs=row,
        out_shape=jax.ShapeDtypeStruct((N, G), BF16),
        compiler_params=_params(("parallel",)),
        name="rwkv_post",
    )(y2, bonus, gate, ln, hm)


def _to_k_layout(x, B, S):
    P = B * N_HEADS_G
    rep = LANES // P
    t = x.reshape(B, S, N_HEADS_G, HEAD_DIM).transpose(1, 3, 0, 2).reshape(S, HEAD_DIM, P)
    return jnp.concatenate([t] * rep, axis=-1)


def _to_v_layout(x, B, S):
    P = B * N_HEADS_G
    rep = LANES // P
    t = x.reshape(B, S, N_HEADS_G, HEAD_DIM // rep, rep).transpose(1, 3, 4, 0, 2)
    return t.reshape(S, HEAD_DIM // rep, LANES)


def _from_v_layout(y, B, S):
    P = B * N_HEADS_G
    rep = LANES // P
    t = y.reshape(S, HEAD_DIM // rep, rep, B, N_HEADS_G).transpose(3, 0, 4, 1, 2)
    return t.reshape(B * S, GROUP_W)


def _dsa_prep_kernel(f_ref, qn_ref, wq_ref, wqi_ref, cq_ref, sqa_ref, sqb_ref, ci_ref, sia_ref, sib_ref,
                     qt_o, qit_o, wt_o, k_o, v_o, ki_o):
    f = f_ref[...]
    cq = _rms(f[:, 0:DSA_Q_LORA], qn_ref[...]).astype(BF16)
    cos_q, sqa, sqb = cq_ref[...], sqa_ref[...], sqb_ref[...]
    cos_i, sia, sib = ci_ref[...], sia_ref[...], sib_ref[...]
    q = _rope(jnp.dot(cq, wq_ref[...], preferred_element_type=F32), cos_q, sqa, sqb, HEAD_DIM // 8)
    qt_o[...] = (q * HEAD_DIM ** -0.5).T.astype(BF16)
    qi = _rope(jnp.dot(cq, wqi_ref[...], preferred_element_type=F32), cos_i, sia, sib, IDX_DIM // 8)
    qit_o[...] = qi.T.astype(BF16)
    kv = f[:, 128:256]
    kv_r = _rope(kv, cos_q[:, 0:128], sqa[:, 0:128], sqb[:, 0:128], HEAD_DIM // 8)
    k_o[...] = kv_r[:, 0:HEAD_DIM].astype(BF16)
    v_o[...] = kv[:, HEAD_DIM:128].astype(BF16)
    tail = f[:, 256:384]
    tail_r = _rope(tail, cos_i[:, 0:128], sia[:, 0:128], sib[:, 0:128], IDX_DIM // 8)
    ki_o[...] = tail_r[:, 0:IDX_DIM].astype(BF16)
    wt_o[...] = tail.T[IDX_DIM:IDX_DIM + IDX_HEADS, :] * (IDX_HEADS ** -0.5 * IDX_DIM ** -0.5)


def _dsa_prep(f2, qn, wq, wqi, tq, ti):
    N, W = f2.shape
    tm = ROW_TILE
    G = GROUP_W
    row = lambda w: pl.BlockSpec((tm, w), lambda i: (i, 0))
    col = lambda h: pl.BlockSpec((h, tm), lambda i: (0, i))
    c = lambda shape: pl.BlockSpec(shape, lambda i: (0, 0))
    return pl.pallas_call(
        _dsa_prep_kernel,
        grid=(N // tm,),
        in_specs=[row(W), c((1, DSA_Q_LORA)), c((DSA_Q_LORA, G)), c((DSA_Q_LORA, G))] + [row(G)] * 6,
        out_specs=[col(G), col(G), col(IDX_HEADS), row(HEAD_DIM), row(HEAD_DIM), row(IDX_DIM)],
        out_shape=[jax.ShapeDtypeStruct((G, N), BF16), jax.ShapeDtypeStruct((G, N), BF16),
                   jax.ShapeDtypeStruct((IDX_HEADS, N), F32),
                   jax.ShapeDtypeStruct((N, HEAD_DIM), BF16), jax.ShapeDtypeStruct((N, HEAD_DIM), BF16),
                   jax.ShapeDtypeStruct((N, IDX_DIM), BF16)],
        compiler_params=_params(("parallel",)),
        name="dsa_prep",
    )(f2, qn, wq, wqi, *tq, *ti)


def _dsa_kernel(TK, topk, qt_ref, qit_ref, wt_ref, k_ref, v_ref, ki_ref, on_ref, o_ref, skey_ref, out_ref):
    i = pl.program_id(1)
    TQ = Q_BLOCK
    n_kt = (i * TQ + TQ + TK - 1) // TK
    qit = qit_ref[...]
    wt = wt_ref[...]
    kpos = lax.broadcasted_iota(I32, (TK, TQ), 0)
    qpos = i * TQ + lax.broadcasted_iota(I32, (TK, TQ), 1)

    def score_tile(kt, carry):
        kid = ki_ref[0, pl.ds(pl.multiple_of(kt * TK, TK), TK), :]
        sc = jnp.zeros((TK, TQ), F32)
        for h in range(IDX_HEADS):
            rel = jnp.dot(kid, qit[h * IDX_DIM:(h + 1) * IDX_DIM, :], preferred_element_type=F32)
            sc = sc + jnp.maximum(rel, 0.0) * wt[h:h + 1, :]
        bits = pltpu.bitcast(sc, I32)
        key = bits ^ ((bits >> 31) & 0x7FFFFFFF)
        skey_ref[kt] = jnp.where(kt * TK + kpos <= qpos, key, INT_MIN)
        return carry

    lax.fori_loop(0, n_kt, score_tile, 0)

    def fold(x, op):
        return _tree([x[r * SUBLANES:(r + 1) * SUBLANES] for r in range(TK // SUBLANES)], op)

    def bit_step(j, thr):
        cand = thr + lax.shift_left(jnp.int32(1), 31 - j)

        def count_tile(kt, acc):
            return acc + fold(jnp.where(skey_ref[kt] >= cand, 1.0, 0.0), jnp.add)

        acc = lax.fori_loop(0, n_kt, count_tile, jnp.zeros((SUBLANES, TQ), F32))
        cnt = jnp.sum(acc, axis=0, keepdims=True)
        return jnp.where(cnt >= topk, cand, thr)

    thr = lax.fori_loop(0, 32, bit_step, jnp.full((1, TQ), INT_MIN, I32))
    thr = jnp.maximum(thr, INT_MIN + 1)

    qt = qt_ref[...]
    heads = [qt[h * HEAD_DIM:(h + 1) * HEAD_DIM, :] for h in range(N_HEADS_G)]

    def key_tile(kt):
        off = pl.multiple_of(kt * TK, TK)
        return skey_ref[kt] >= thr, k_ref[0, pl.ds(off, TK), :], off

    def max_tile(kt, ms):
        sel, kb, _ = key_tile(kt)
        return tuple(
            jnp.maximum(ms[h], fold(jnp.where(sel, jnp.dot(kb, heads[h], preferred_element_type=F32), -1e30),
                                    jnp.maximum))
            for h in range(N_HEADS_G))

    ms = lax.fori_loop(0, n_kt, max_tile, tuple(jnp.full((SUBLANES, TQ), -1e30, F32) for _ in range(N_HEADS_G)))
    ms = [jnp.max(m, axis=0, keepdims=True) for m in ms]

    def acc_tile(kt, carry):
        sel, kb, off = key_tile(kt)
        vb = v_ref[0, pl.ds(off, TK), :]
        new = []
        for h in range(N_HEADS_G):
            l, acc = carry[h]
            lg = jnp.dot(kb, heads[h], preferred_element_type=F32)
            p = jnp.where(sel, jnp.exp(lg - ms[h]), 0.0)
            new.append((l + fold(p, jnp.add),
                        acc + lax.dot_general(vb, p.astype(BF16), TN_DIMS, preferred_element_type=F32)))
        return tuple(new)

    init = tuple((jnp.zeros((SUBLANES, TQ), F32), jnp.zeros((HEAD_DIM, TQ), F32)) for _ in range(N_HEADS_G))
    fin = lax.fori_loop(0, n_kt, acc_tile, init)
    for h in range(N_HEADS_G):
        l, acc = fin[h]
        out_ref[h * HEAD_DIM:(h + 1) * HEAD_DIM, :] = acc / jnp.sum(l, axis=0, keepdims=True)
    o_ref[0] = _rms(out_ref[...].T, on_ref[...]).astype(BF16)


def _dsa_attention(qt, qit, wt, k, v, ki, on):
    B, S, _ = k.shape
    G = GROUP_W
    nq = S // Q_BLOCK
    TK = min(512, S)
    topk = min(DSA_TOPK_MAX, S // 4)
    qcol = lambda h: pl.BlockSpec((h, Q_BLOCK), lambda b, i: (0, b * nq + i))
    full = lambda wd: pl.BlockSpec((1, S, wd), lambda b, i: (b, 0, 0))
    return pl.pallas_call(
        functools.partial(_dsa_kernel, TK, topk),
        grid=(B, nq),
        in_specs=[qcol(G), qcol(G), qcol(IDX_HEADS), full(HEAD_DIM), full(HEAD_DIM), full(IDX_DIM),
                  pl.BlockSpec((1, G), lambda b, i: (0, 0))],
        out_specs=pl.BlockSpec((1, Q_BLOCK, G), lambda b, i: (b, i, 0)),
        out_shape=jax.ShapeDtypeStruct((B, S, G), BF16),
        scratch_shapes=[pltpu.VMEM((S // TK, TK, Q_BLOCK), I32), pltpu.VMEM((G, Q_BLOCK), F32)],
        compiler_params=_params(("parallel", "arbitrary")),
        name="dsa_attention",
    )(qt, qit, wt, k, v, ki, on)


def _sb_kernel(q_ref, k_ref, v_ref, u_ref, on_ref, o_ref, out_ref):
    i = pl.program_id(1)
    T = Q_BLOCK
    u = u_ref[...]
    qt = q_ref[0].astype(F32).T.astype(BF16)
    kid = lax.broadcasted_iota(I32, (T, T), 0)
    qid = lax.broadcasted_iota(I32, (T, T), 1)
    strict = kid < qid

    def cond(carry):
        kt, cs, _ = carry
        live = jnp.max(jnp.maximum(jnp.maximum(cs[0], cs[1]), jnp.maximum(cs[2], cs[3])))
        return jnp.logical_and(kt >= 0, live > SB_UNDERFLOW)

    def body(carry):
        kt, cs, accs = carry
        off = pl.multiple_of(kt * T, T)
        mask = jnp.logical_or(strict, kt < i)
        new_c, new_acc = [], []
        for h in range(N_HEADS_G):
            sl = slice(h * HEAD_DIM, (h + 1) * HEAD_DIM)
            kb = k_ref[0, pl.ds(off, T), sl]
            vb = v_ref[0, pl.ds(off, T), sl]
            z = jnp.dot(kb, qt[sl, :], preferred_element_type=F32) * HEAD_DIM ** -0.5
            lm = jnp.where(mask, -_softplus(z), 0.0)
            hi, lo = _split(lm)
            later = (jnp.dot(u, hi, preferred_element_type=F32)
                     + jnp.dot(u, lo, preferred_element_type=F32))
            a = jnp.where(mask, jnp.exp(z + lm + (cs[h] + later)), 0.0)
            new_acc.append(accs[h] + lax.dot_general(vb, a.astype(BF16), TN_DIMS, preferred_element_type=F32))
            new_c.append(cs[h] + jnp.sum(lm, axis=0, keepdims=True))
        return kt - 1, tuple(new_c), tuple(new_acc)

    init = (i, tuple(jnp.zeros((1, T), F32) for _ in range(N_HEADS_G)),
            tuple(jnp.zeros((HEAD_DIM, T), F32) for _ in range(N_HEADS_G)))
    _, _, accs = lax.while_loop(cond, body, init)
    for h in range(N_HEADS_G):
        out_ref[h * HEAD_DIM:(h + 1) * HEAD_DIM, :] = accs[h]
    o_ref[0] = _rms(out_ref[...].T, on_ref[...]).astype(BF16)


def _stick_breaking(sb3, u, on):
    B, S, _ = sb3.shape
    G = GROUP_W
    return pl.pallas_call(
        _sb_kernel,
        grid=(B, S // Q_BLOCK),
        in_specs=[pl.BlockSpec((1, Q_BLOCK, G), lambda b, i: (b, i, 0)),
                  pl.BlockSpec((1, S, G), lambda b, i: (b, 0, 1)),
                  pl.BlockSpec((1, S, G), lambda b, i: (b, 0, 2)),
                  pl.BlockSpec((Q_BLOCK, Q_BLOCK), lambda b, i: (0, 0)),
                  pl.BlockSpec((1, G), lambda b, i: (0, 0))],
        out_specs=pl.BlockSpec((1, Q_BLOCK, G), lambda b, i: (b, i, 0)),
        out_shape=jax.ShapeDtypeStruct((B, S, G), BF16),
        scratch_shapes=[pltpu.VMEM((G, Q_BLOCK), F32)],
        compiler_params=_params(("parallel", "arbitrary")),
        name="stick_breaking",
    )(sb3, sb3, sb3, u, on)


def _outproj_kernel(yr_ref, yw_ref, yd_ref, ys_ref, x_ref, g1_ref, sc_ref, sh_ref, gn_ref, wo_ref, rw_ref, rb_ref,
                    x_o, h_o, lg_o):
    G = GROUP_W
    dot = lambda y_ref, g: jnp.dot(y_ref[...], wo_ref[g * G:(g + 1) * G, :], preferred_element_type=F32)
    mixed = dot(yr_ref, 0) + dot(yw_ref, 1) + dot(yd_ref, 2) + dot(ys_ref, 3)
    x1 = x_ref[...] + g1_ref[0] * mixed
    x_o[...] = x1
    h = _rms(x1, gn_ref[...]) * (1.0 + sc_ref[0]) + sh_ref[0]
    h_o[...] = h
    lg_o[...] = jnp.dot(h, rw_ref[...], preferred_element_type=F32, precision=HIGHEST) + rb_ref[...]


def _outproj(ys, x2, mod6, gain, w_out, router_w, router_b, S):
    N, D = x2.shape
    tm = ROW_TILE
    per_b = S // tm
    G = GROUP_W
    row = lambda w: pl.BlockSpec((tm, w), lambda i: (i, 0))
    c = lambda shape: pl.BlockSpec(shape, lambda i: (0, 0))
    return pl.pallas_call(
        _outproj_kernel,
        grid=(N // tm,),
        in_specs=[row(G)] * 4 + [row(D), _mod_row(per_b, 2), _mod_row(per_b, 4), _mod_row(per_b, 3),
                                 c((1, D)), c((D, D)), c((D, N_EXPERTS)), c((1, N_EXPERTS))],
        out_specs=[row(D), row(D), row(N_EXPERTS)],
        out_shape=[jax.ShapeDtypeStruct((N, D), F32), jax.ShapeDtypeStruct((N, D), F32),
                   jax.ShapeDtypeStruct((N, N_EXPERTS), F32)],
        compiler_params=_params(("parallel",)),
        name="out_proj_router",
    )(*ys, x2, mod6, mod6, mod6, gain, w_out, router_w, router_b)


def _moe_kernel(be_ref, nu_ref, x_ref, w1_ref, b1_ref, w2_ref, b2_ref, rw_ref, o_ref):
    i = pl.program_id(0)

    @pl.when(i < nu_ref[0])
    def _():
        xb = x_ref[...].astype(BF16)
        glu = jnp.dot(xb, w1_ref[0, :, 0:D_FF].astype(BF16), preferred_element_type=F32) + b1_ref[0, :, 0:D_FF]
        lin = jnp.dot(xb, w1_ref[0, :, D_FF:].astype(BF16), preferred_element_type=F32) + b1_ref[0, :, D_FF:]
        glu = jnp.minimum(glu, SWIGLU_LIMIT)
        lin = jnp.clip(lin, -SWIGLU_LIMIT, SWIGLU_LIMIT)
        act = glu * _sigmoid(SWIGLU_ALPHA * glu) * (lin + 1.0)
        y = jnp.dot(act.astype(BF16), w2_ref[0].astype(BF16), preferred_element_type=F32) + b2_ref[0]
        o_ref[...] = (y * rw_ref[...]).astype(BF16)

    @pl.when(i >= nu_ref[0])
    def _():
        o_ref[...] = jnp.zeros_like(o_ref)


def _moe_ffn(block_expert, n_used, xs, w1, b1, w2, b2, row_w):
    rows, D = xs.shape
    nb = rows // MOE_BLOCK
    E = w1.shape[0]
    live = lambda i, nu: jnp.minimum(i, nu[0] - 1)
    grid_spec = pltpu.PrefetchScalarGridSpec(
        num_scalar_prefetch=2,
        grid=(nb,),
        in_specs=[pl.BlockSpec((MOE_BLOCK, D), lambda i, be, nu: (live(i, nu), 0)),
                  pl.BlockSpec((1, D, 2 * D_FF), lambda i, be, nu: (be[i], 0, 0)),
                  pl.BlockSpec((1, 1, 2 * D_FF), lambda i, be, nu: (be[i], 0, 0)),
                  pl.BlockSpec((1, D_FF, D), lambda i, be, nu: (be[i], 0, 0)),
                  pl.BlockSpec((1, 1, D), lambda i, be, nu: (be[i], 0, 0)),
                  pl.BlockSpec((MOE_BLOCK, 1), lambda i, be, nu: (live(i, nu), 0))],
        out_specs=pl.BlockSpec((MOE_BLOCK, D), lambda i, be, nu: (i, 0)),
    )
    return pl.pallas_call(
        _moe_kernel,
        grid_spec=grid_spec,
        out_shape=jax.ShapeDtypeStruct((rows, D), BF16),
        compiler_params=_params(("arbitrary",), 56 * 1024 * 1024),
        name="moe_ffn",
    )(block_expert, n_used, xs, w1, b1.reshape(E, 1, -1), w2, b2.reshape(E, 1, -1), row_w)


def _route(logits):
    n = logits.shape[0]
    top_val, top_idx = lax.top_k(logits, TOP_K)
    gate = jax.nn.softmax(top_val, axis=-1).reshape(-1)
    expert = top_idx.reshape(-1).astype(I32)
    n_assign = n * TOP_K
    ids = jnp.arange(n_assign, dtype=I32)
    _, order = lax.sort((expert, ids), num_keys=1, is_stable=True)
    _, inv = lax.sort((order, ids), num_keys=1)
    counts = jnp.sum((expert[:, None] == jnp.arange(N_EXPERTS, dtype=I32)[None, :]).astype(I32), axis=0)
    starts = jnp.cumsum(counts) - counts
    padded = (counts + MOE_BLOCK - 1) // MOE_BLOCK * MOE_BLOCK
    pad_end = jnp.cumsum(padded)
    pad_start = pad_end - padded
    dest = pad_start[expert] + inv - starts[expert]
    n_blocks = -(-n_assign // MOE_BLOCK) + N_EXPERTS
    block_start = jnp.arange(n_blocks, dtype=I32) * MOE_BLOCK
    block_expert = jnp.minimum(jnp.sum(pad_end[None, :] <= block_start[:, None], axis=1), N_EXPERTS - 1)
    block_expert = block_expert.astype(I32)
    per_row = lambda t: jnp.repeat(t[block_expert], MOE_BLOCK)
    off = jnp.arange(n_blocks * MOE_BLOCK, dtype=I32) - per_row(pad_start)
    valid = off < per_row(counts)
    src = order[jnp.clip(per_row(starts) + off, 0, n_assign - 1)]
    row_token = jnp.where(valid, src // TOP_K, 0)
    row_weight = jnp.where(valid, gate[src], 0.0)
    n_used = (pad_end[-1] // MOE_BLOCK).astype(I32).reshape(1)
    return dest.reshape(n, TOP_K), row_token, row_weight, block_expert, n_used


def _rows(t, idx):
    return t.at[idx].get(mode="promise_in_bounds")


def _final_kernel(x_ref, g2_ref, y0, y1, y2, y3, g_ref, o_ref):
    o_ref[...] = _rms(_combine(x_ref, g2_ref, (y0, y1, y2, y3)), g_ref[...])


def _final_norm(x2, mod6, ys, g, S):
    N, D = x2.shape
    tm = ROW_TILE
    row = pl.BlockSpec((tm, D), lambda i: (i, 0))
    return pl.pallas_call(
        _final_kernel, grid=(N // tm,),
        in_specs=[row, _mod_row(S // tm, 5)] + [row] * 4 + [pl.BlockSpec((1, D), lambda i: (0, 0))],
        out_specs=row,
        out_shape=jax.ShapeDtypeStruct((N, D), F32),
        compiler_params=_params(("parallel",)), name="final_norm",
    )(x2, mod6, *ys, g)


def _rope_tables(pos, group, rot_dim, theta, width):
    half = rot_dim // 2
    inv = theta ** (-jnp.arange(half, dtype=F32) / half)
    ang = pos.astype(F32)[..., None] * inv
    cos, sin = jnp.cos(ang), jnp.sin(ang)
    rest = group - rot_dim
    ones = jnp.ones(ang.shape[:2] + (rest,), F32)
    zeros = jnp.zeros(ang.shape[:2] + (rest,), F32)
    zh = jnp.zeros_like(sin)
    rep = width // group
    tile = lambda t: jnp.tile(t, (1, 1, rep)).reshape(-1, width)
    return (tile(jnp.concatenate([cos, cos, ones], -1)),
            tile(jnp.concatenate([-sin, zh, zeros], -1)),
            tile(jnp.concatenate([zh, sin, zeros], -1)))


def _pad_rows(w, start, total=128):
    return jnp.zeros((total, w.shape[1]), w.dtype).at[start:start + w.shape[0]].set(w)


def kernel(x, c, positions, ada_w, ada_b, norm_mix, norm_ffn, w_in, ret_gn, rwkv_mu, rwkv_w0, rwkv_w2, rwkv_a0,
           rwkv_a2, rwkv_g2, rwkv_kk, rwkv_ka, rwkv_rk, rwkv_ln, dsa_qnorm, dsa_wq_up, dsa_wqi_up, dsa_onorm,
           sb_onorm, w_out, router_w, router_b, moe_w1, moe_b1, moe_w2, moe_b2, norm_final):
    B, S, D = x.shape
    N = B * S
    L = ada_w.shape[0]
    G = GROUP_W
    row2 = lambda t: t.reshape(1, -1)

    mod = _ada(c, ada_w, ada_b)
    ret_tabs = tuple(t.reshape(B, S, G) for t in _rope_tables(positions, HEAD_DIM, HEAD_DIM, RET_THETA, G))
    dq_tabs = _rope_tables(positions, HEAD_DIM, HEAD_DIM // 4, ROPE_THETA, G)
    di_tabs = _rope_tables(positions, IDX_DIM, IDX_DIM // 4, ROPE_THETA, G)
    ret_c = _ret_consts(min(128, S))
    hm = (jnp.arange(G)[:, None] // HEAD_DIM == jnp.arange(G)[None, :] // HEAD_DIM).astype(BF16)
    u_later = (jnp.arange(Q_BLOCK)[None, :] > jnp.arange(Q_BLOCK)[:, None]).astype(BF16)
    o1, o2 = RET_COLS + RWKV_COLS, RET_COLS + RWKV_COLS + DSA_COLS

    x2 = x.reshape(N, D)
    comb = None
    for l in range(L):
        mod6 = mod[l].reshape(B * 6, 1, D)
        w_l = w_in[l]
        w_cat = jnp.concatenate(
            [w_l[:, :o1], w_l[:, o1:o2], jnp.zeros((D, DSA_PAD - DSA_COLS), F32), w_l[:, o2:]], axis=1).astype(BF16)
        outs = _inproj(x2, row2(norm_mix[l]), mod6, w_cat, S, comb)
        if comb is not None:
            x2, outs = outs[0], outs[1:]
        ret, rwkv, dsa, sb = outs

        y_ret = _retention(ret.reshape(B, S, RET_COLS), ret_tabs, ret_c, row2(ret_gn[l])).reshape(N, G)

        prep = _rwkv_prep(rwkv, S, row2(rwkv_mu[l]), row2(rwkv_w0[l]), row2(rwkv_a0[l]),
                          _pad_rows(rwkv_w2[l], 0).astype(BF16), _pad_rows(rwkv_a2[l], 32).astype(BF16),
                          _pad_rows(rwkv_g2[l], 64).astype(BF16), row2(rwkv_kk[l]), row2(rwkv_ka[l]),
                          row2(rwkv_rk[l]), hm)
        r_, w_, k_, v_, nkk_, b_, gate_, bonus_ = prep
        y_scan = _rwkv_scan(_to_k_layout(nkk_, B, S), _to_k_layout(w_, B, S), _to_k_layout(b_, B, S),
                            _to_k_layout(k_, B, S), _to_k_layout(r_, B, S), _to_v_layout(v_, B, S))
        y_rwkv = _rwkv_post(_from_v_layout(y_scan, B, S), bonus_, gate_, row2(rwkv_ln[l]), hm)

        dqt, dqit, dwt, dk, dv, dki = _dsa_prep(dsa, row2(dsa_qnorm[l]), dsa_wq_up[l].astype(BF16),
                                                dsa_wqi_up[l].astype(BF16), dq_tabs, di_tabs)
        y_dsa = _dsa_attention(dqt, dqit, dwt, dk.reshape(B, S, HEAD_DIM), dv.reshape(B, S, HEAD_DIM),
                               dki.reshape(B, S, IDX_DIM), row2(dsa_onorm[l])).reshape(N, G)

        y_sb = _stick_breaking(sb.reshape(B, S, SB_COLS), u_later, row2(sb_onorm[l])).reshape(N, G)

        x1, h2, logits = _outproj((y_ret, y_rwkv, y_dsa, y_sb), x2, mod6, row2(norm_ffn[l]),
                                  w_out[l].astype(BF16), router_w[l], row2(router_b[l]), S)

        dest, row_token, row_weight, block_expert, n_used = _route(logits)
        y = _moe_ffn(block_expert, n_used, _rows(h2, row_token), moe_w1[l], moe_b1[l], moe_w2[l], moe_b2[l],
                     row_weight[:, None])
        x2 = x1
        comb = (mod6, tuple(_rows(y, dest[:, j]) for j in range(TOP_K)))
    return _final_norm(x2, comb[0], comb[1], row2(norm_final), S).reshape(B, S, D)
```

```python
import functools

import jax
import jax.numpy as jnp
from jax import lax
from jax.experimental import pallas as pl
from jax.experimental.pallas import tpu as pltpu

F32 = jnp.float32
BF16 = jnp.bfloat16
I32 = jnp.int32

D_MODEL = 1024
GROUP_W = 256
HEAD_DIM = 64
N_HEADS_G = 4
NORM_EPS = 1e-5
Q_BLOCK = 128
RET_THETA = 10000.0
RWKV_LN_EPS = 64e-5
DSA_Q_LORA = 128
IDX_HEADS = 8
IDX_DIM = 32
DSA_TOPK_MAX = 256
ROPE_THETA = 500000.0
N_EXPERTS = 32
TOP_K = 4
D_FF = D_MODEL
SWIGLU_ALPHA = 1.702
SWIGLU_LIMIT = 7.0
MOE_BLOCK = 512

RET_COLS = 4 * GROUP_W
RWKV_COLS = 3 * GROUP_W + 128
DSA_COLS = 296
DSA_PAD = 384
SB_COLS = 3 * GROUP_W

LANES = 128
SUBLANES = 8
ROW_TILE = 512
VMEM_LIMIT = 48 * 1024 * 1024
INT_MIN = -2 ** 31
SB_UNDERFLOW = -104.0

HIGHEST = lax.Precision.HIGHEST
NT_DIMS = (((1,), (1,)), ((), ()))
TN_DIMS = (((0,), (0,)), ((), ()))


def _params(sem, vmem=VMEM_LIMIT):
    return pltpu.CompilerParams(dimension_semantics=sem, vmem_limit_bytes=vmem)


def _bdot(a, b):
    return jnp.dot(a.astype(BF16), b.astype(BF16), preferred_element_type=F32)


def _split(x):
    hi = x.astype(BF16)
    return hi, (x - hi.astype(F32)).astype(BF16)


def _split_dot(x, m):
    hi, lo = _split(x)
    return jnp.dot(hi, m, preferred_element_type=F32) + jnp.dot(lo, m, preferred_element_type=F32)


def _tree(parts, op):
    while len(parts) > 1:
        parts = [op(parts[i], parts[i + 1]) for i in range(0, len(parts), 2)]
    return parts[0]


def _sigmoid(x):
    return 1.0 / (1.0 + jnp.exp(-x))


def _softplus(x):
    return jnp.maximum(x, 0.0) + jnp.log(1.0 + jnp.exp(-jnp.abs(x)))


def _rms(x, g, eps=NORM_EPS):
    return x * lax.rsqrt(jnp.mean(x * x, axis=-1, keepdims=True) + eps) * g


def _rope(x, cos, sin_a, sin_b, half):
    w = x.shape[-1]
    return x * cos + pltpu.roll(x, w - half, 1) * sin_a + pltpu.roll(x, half, 1) * sin_b


def _ada_kernel(c_ref, w_ref, b_ref, o_ref):
    c = c_ref[...]
    cond = c * _sigmoid(c)
    o_ref[0] = jnp.dot(cond, w_ref[0], preferred_element_type=F32, precision=HIGHEST) + b_ref[0]


def _ada(c, ada_w, ada_b):
    L, D, W = ada_w.shape
    B = c.shape[0]
    tn = 1024
    return pl.pallas_call(
        _ada_kernel,
        grid=(L, W // tn),
        in_specs=[pl.BlockSpec((B, D), lambda l, j: (0, 0)),
                  pl.BlockSpec((1, D, tn), lambda l, j: (l, 0, j)),
                  pl.BlockSpec((1, 1, tn), lambda l, j: (l, 0, j))],
        out_specs=pl.BlockSpec((1, B, tn), lambda l, j: (l, 0, j)),
        out_shape=jax.ShapeDtypeStruct((L, B, W), F32),
        compiler_params=_params(("parallel", "parallel")),
        name="ada_mod",
    )(c, ada_w, ada_b.reshape(L, 1, W))


def _combine(x_ref, g2_ref, y_refs):
    moe = y_refs[0][...].astype(F32)
    for y_ref in y_refs[1:]:
        moe = moe + y_ref[...].astype(F32)
    return x_ref[...] + g2_ref[0] * moe


def _inproj_kernel(n_comb, x_ref, *refs):
    if n_comb:
        g2_ref, y_refs, refs = refs[0], refs[1:1 + n_comb], refs[1 + n_comb:]
        g_ref, sc_ref, sh_ref, w_ref, x_o, ret_ref, rwkv_ref, dsa_ref, sb_ref = refs
        x = _combine(x_ref, g2_ref, y_refs)
        x_o[...] = x
    else:
        g_ref, sc_ref, sh_ref, w_ref, ret_ref, rwkv_ref, dsa_ref, sb_ref = refs
        x = x_ref[...]
    h = _rms(x, g_ref[...]) * (1.0 + sc_ref[0]) + sh_ref[0]
    hb = h.astype(BF16)
    o0, o1, o2 = RET_COLS, RET_COLS + RWKV_COLS, RET_COLS + RWKV_COLS + DSA_PAD
    ret_ref[...] = jnp.dot(hb, w_ref[:, 0:o0], preferred_element_type=F32)
    rwkv_ref[...] = jnp.dot(hb, w_ref[:, o0:o1], preferred_element_type=F32)
    dsa_ref[...] = jnp.dot(hb, w_ref[:, o1:o2], preferred_element_type=F32)
    sb_ref[...] = jnp.dot(hb, w_ref[:, o2:], preferred_element_type=F32).astype(BF16)


def _mod_row(per_b, j):
    return pl.BlockSpec((1, 1, D_MODEL), lambda i: ((i // per_b) * 6 + j, 0, 0))


def _inproj(x2, gain, mod6, w_cat, S, comb=None):
    N, D = x2.shape
    tm = ROW_TILE
    per_b = S // tm
    wt = w_cat.shape[1]
    row = lambda i: (i, 0)
    in_specs = [pl.BlockSpec((tm, D), row)]
    args = [x2]
    out_specs, out_shape = [], []
    if comb is not None:
        in_specs += [_mod_row(per_b, 5)] + [pl.BlockSpec((tm, D), row)] * len(comb[1])
        args += [comb[0]] + list(comb[1])
        out_specs.append(pl.BlockSpec((tm, D), row))
        out_shape.append(jax.ShapeDtypeStruct((N, D), F32))
    in_specs += [pl.BlockSpec((1, D), lambda i: (0, 0)), _mod_row(per_b, 1), _mod_row(per_b, 0),
                 pl.BlockSpec((D, wt), lambda i: (0, 0))]
    args += [gain, mod6, mod6, w_cat]
    out_specs += [pl.BlockSpec((tm, RET_COLS), row), pl.BlockSpec((tm, RWKV_COLS), row),
                  pl.BlockSpec((tm, DSA_PAD), row), pl.BlockSpec((tm, SB_COLS), row)]
    out_shape += [jax.ShapeDtypeStruct((N, RET_COLS), F32), jax.ShapeDtypeStruct((N, RWKV_COLS), F32),
                  jax.ShapeDtypeStruct((N, DSA_PAD), F32), jax.ShapeDtypeStruct((N, SB_COLS), BF16)]
    return pl.pallas_call(
        functools.partial(_inproj_kernel, 0 if comb is None else len(comb[1])),
        grid=(N // tm,),
        in_specs=in_specs, out_specs=out_specs, out_shape=out_shape,
        compiler_params=_params(("parallel",)),
        name="in_proj",
    )(*args)


def _ret_kernel(q_ref, k_ref, v_ref, g_ref, cos_ref, sa_ref, sb_ref, din_ref, qd_ref, kd_ref, cd_ref, gn_ref,
                o_ref, state_ref, y_ref):
    @pl.when(pl.program_id(1) == 0)
    def _():
        state_ref[...] = jnp.zeros_like(state_ref)

    cos, sin_a, sin_b = cos_ref[0], sa_ref[0], sb_ref[0]
    q = _rope(q_ref[0], cos, sin_a, sin_b, HEAD_DIM // 2)
    k = _rope(k_ref[0], cos, sin_a, sin_b, HEAD_DIM // 2) * HEAD_DIM ** -0.5
    v = v_ref[0]
    qd = q * qd_ref[...]
    kd = k * kd_ref[...]
    for h in range(N_HEADS_G):
        sl = slice(h * HEAD_DIM, (h + 1) * HEAD_DIM)
        qh, kh, vh = q[:, sl].astype(BF16), k[:, sl].astype(BF16), v[:, sl].astype(BF16)
        s = lax.dot_general(qh, kh, NT_DIMS, preferred_element_type=F32) * din_ref[h]
        inner = jnp.dot(s.astype(BF16), vh, preferred_element_type=F32)
        st = state_ref[h]
        cross = _bdot(qd[:, sl], st)
        state_ref[h] = st * cd_ref[:, sl] + lax.dot_general(
            kd[:, sl].astype(BF16), vh, TN_DIMS, preferred_element_type=F32)
        o = inner + cross
        oc = o - jnp.mean(o, axis=-1, keepdims=True)
        y_ref[:, sl] = oc * lax.rsqrt(jnp.mean(oc * oc, axis=-1, keepdims=True) + NORM_EPS)
    g = g_ref[0]
    o_ref[0] = (g * _sigmoid(g) * (y_ref[...] * gn_ref[...])).astype(BF16)


def _retention(ret3, tabs, consts, gn):
    B, S, _ = ret3.shape
    C = consts["din"].shape[1]
    blk = lambda j: pl.BlockSpec((1, C, GROUP_W), lambda b, c, j=j: (b, c, j))
    tab = pl.BlockSpec((1, C, GROUP_W), lambda b, c: (b, c, 0))
    const2 = lambda shape: pl.BlockSpec(shape, lambda b, c: (0,) * len(shape))
    return pl.pallas_call(
        _ret_kernel,
        grid=(B, S // C),
        in_specs=[blk(0), blk(1), blk(2), blk(3), tab, tab, tab,
                  const2((N_HEADS_G, C, C)), const2((C, GROUP_W)), const2((C, GROUP_W)),
                  const2((1, GROUP_W)), const2((1, GROUP_W))],
        out_specs=pl.BlockSpec((1, C, GROUP_W), lambda b, c: (b, c, 0)),
        out_shape=jax.ShapeDtypeStruct((B, S, GROUP_W), BF16),
        scratch_shapes=[pltpu.VMEM((N_HEADS_G, HEAD_DIM, HEAD_DIM), F32), pltpu.VMEM((C, GROUP_W), F32)],
        compiler_params=_params(("parallel", "arbitrary")),
        name="retention",
    )(ret3, ret3, ret3, ret3, tabs[0], tabs[1], tabs[2],
      consts["din"], consts["qd"], consts["kd"], consts["cd"], gn)


def _ret_consts(C):
    H = N_HEADS_G
    lg = jnp.log(1.0 - 2.0 ** (-5.0 - jnp.arange(H, dtype=F32)))
    idx = jnp.arange(C, dtype=F32)
    diff = idx[:, None] - idx[None, :]
    din = jnp.where(diff >= 0, jnp.exp(lg[:, None, None] * jnp.maximum(diff, 0.0)), 0.0)
    q_dec = jnp.exp(lg[:, None] * (idx + 1.0))
    k_dec = jnp.exp(lg[:, None] * (C - 1.0 - idx))
    chunk_dec = jnp.exp(lg * C)
    wide = lambda t: jnp.repeat(t.T, HEAD_DIM, axis=1)
    return {"din": din, "qd": wide(q_dec), "kd": wide(k_dec),
            "cd": jnp.repeat(chunk_dec, HEAD_DIM)[None, :]}


def _rwkv_prep_kernel(per_b, z_ref, zp_ref, mu_ref, w0_ref, a0_ref, w2_ref, a2_ref, g2_ref, kk_ref, ka_ref,
                      rk_ref, hm_ref, r_o, w_o, k_o, v_o, nkk_o, b_o, gate_o, bonus_o):
    z = z_ref[...]
    first = (pl.program_id(0) % per_b) == 0
    prow = jnp.where(first, 0.0, zp_ref[7:8, :])
    rid = lax.broadcasted_iota(I32, z.shape, 0)
    prev = jnp.where(rid == 0, prow, pltpu.roll(z, 1, 0))
    f = z + (prev - z) * mu_ref[...]
    G = GROUP_W
    r, k, v, lo = f[:, 0:G], f[:, G:2 * G], f[:, 2 * G:3 * G], f[:, 3 * G:3 * G + 128]
    w_log = -_softplus(-(w0_ref[...] + _bdot(jnp.tanh(lo), w2_ref[...]))) - 0.5
    decay = jnp.exp(-jnp.exp(w_log))
    a = _sigmoid(a0_ref[...] + _bdot(lo, a2_ref[...]))
    gate = _bdot(_sigmoid(lo), g2_ref[...])
    hm = hm_ref[...]
    kk = k * kk_ref[...]
    kk = kk / jnp.maximum(jnp.sqrt(_split_dot(kk * kk, hm)), 1e-12)
    k2 = k * (1.0 + (a - 1.0) * ka_ref[...])
    r_o[...] = r
    w_o[...] = decay
    k_o[...] = k2
    v_o[...] = v
    nkk_o[...] = -kk
    b_o[...] = kk * a
    gate_o[...] = gate
    bonus_o[...] = _split_dot(r * k2 * rk_ref[...], hm) * v


def _rwkv_prep(z2, S, mu, w0, a0, w2p, a2p, g2p, kk, ka, rk, hm):
    N, W = z2.shape
    tm = ROW_TILE
    per_b = S // tm
    row = lambda i: (i, 0)
    c = lambda shape: pl.BlockSpec(shape, lambda i: (0, 0))
    G = GROUP_W
    return pl.pallas_call(
        functools.partial(_rwkv_prep_kernel, per_b),
        grid=(N // tm,),
        in_specs=[pl.BlockSpec((tm, W), row),
                  pl.BlockSpec((SUBLANES, W), lambda i: (jnp.maximum(i * (tm // SUBLANES) - 1, 0), 0)),
                  c((1, W)), c((1, G)), c((1, G)), c((128, G)), c((128, G)), c((128, G)),
                  c((1, G)), c((1, G)), c((1, G)), c((G, G))],
        out_specs=[pl.BlockSpec((tm, G), row)] * 8,
        out_shape=[jax.ShapeDtypeStruct((N, G), F32)] * 8,
        compiler_params=_params(("parallel",)),
        name="rwkv_prep",
    )(z2, z2, mu, w0, a0, w2p, a2p, g2p, kk, ka, rk, hm)


def _rwkv_scan_kernel(T, nkk_ref, w_ref, b_ref, k_ref, r_ref, v_ref, y_ref, s_ref, vrep_ref):
    @pl.when(pl.program_id(0) == 0)
    def _():
        s_ref[...] = jnp.zeros_like(s_ref)

    ch = v_ref.shape[1]
    groups = HEAD_DIM // ch
    width = LANES // groups
    tree = lambda parts: _tree(parts, jnp.add)

    def over_groups(x):
        shift = width
        while shift < LANES:
            x = x + pltpu.roll(x, shift, 1)
            shift *= 2
        return x

    grp = lax.broadcasted_iota(I32, (T * ch, LANES), 1) // width
    x = v_ref[...].reshape(T * ch, LANES)
    for j in range(groups):
        vrep_ref[:, j] = over_groups(jnp.where(grp == j, x, 0.0)).reshape(T, ch, LANES)

    def step(t, carry):
        vt = vrep_ref[t].reshape(HEAD_DIM, LANES)
        sa = over_groups(tree([s_ref[kh] * nkk_ref[t, kh:kh + 1, :] for kh in range(ch)]))
        ys = []
        for kh in range(ch):
            s_new = (s_ref[kh] * w_ref[t, kh:kh + 1, :] + sa * b_ref[t, kh:kh + 1, :]
                     + vt * k_ref[t, kh:kh + 1, :])
            s_ref[kh] = s_new
            ys.append(s_new * r_ref[t, kh:kh + 1, :])
        y_ref[t] = over_groups(tree(ys))
        return carry

    lax.fori_loop(0, T, step, 0)


def _rwkv_scan(nkk, w, b, k, r, v):
    S, ch, _ = v.shape
    T = 32
    spec = pl.BlockSpec((T, ch, LANES), lambda i: (i, 0, 0))
    return pl.pallas_call(
        functools.partial(_rwkv_scan_kernel, T),
        grid=(S // T,),
        in_specs=[spec] * 6,
        out_specs=pl.BlockSpec((T, HEAD_DIM, LANES), lambda i: (i, 0, 0)),
        out_shape=jax.ShapeDtypeStruct((S, HEAD_DIM, LANES), F32),
        scratch_shapes=[pltpu.VMEM((ch, HEAD_DIM, LANES), F32),
                        pltpu.VMEM((T, HEAD_DIM // ch, ch, LANES), F32)],
        compiler_params=_params(("arbitrary",)),
        name="rwkv_scan",
    )(nkk, w, b, k, r, v)


def _rwkv_post_kernel(y_ref, bonus_ref, gate_ref, ln_ref, hm_ref, o_ref):
    y = y_ref[...]
    hm = hm_ref[...]
    yc = y - _split_dot(y, hm) * (1.0 / HEAD_DIM)
    var = _split_dot(yc * yc, hm) * (1.0 / HEAD_DIM)
    o_ref[...] = ((yc * lax.rsqrt(var + RWKV_LN_EPS) * ln_ref[...] + bonus_ref[...]) * gate_ref[...]).astype(BF16)


def _rwkv_post(y2, bonus, gate, ln, hm):
    N, G = y2.shape
    tm = ROW_TILE
    row = pl.BlockSpec((tm, G), lambda i: (i, 0))
    return pl.pallas_call(
        _rwkv_post_kernel,
        grid=(N // tm,),
        in_specs=[row, row, row, pl.BlockSpec((1, G), lambda i: (0, 0)), pl.BlockSpec((G, G), lambda i: (0, 0))],
        out_specs=row,
        out_shape=jax.ShapeDtypeStruct((N, G), BF16),
        compiler_params=_params(("parallel",)),
        name="rwkv_post",
    )(y2, bonus, gate, ln, hm)


def _to_v_layout(x, B, S):
    P = B * N_HEADS_G
    rep = LANES // P
    t = x.reshape(B, S, N_HEADS_G, HEAD_DIM // rep, rep).transpose(1, 3, 4, 0, 2)
    return t.reshape(S, HEAD_DIM // rep, LANES)


def _from_scan_layout(y, B, S):
    P = B * N_HEADS_G
    rep = LANES // P
    t = y[:, :, :P].reshape(S, rep, HEAD_DIM // rep, B, N_HEADS_G).transpose(3, 0, 4, 2, 1)
    return t.reshape(B * S, GROUP_W)


def _dsa_prep_kernel(f_ref, qn_ref, wq_ref, wqi_ref, cq_ref, sqa_ref, sqb_ref, ci_ref, sia_ref, sib_ref,
                     qt_o, qit_o, wt_o, k_o, v_o, ki_o):
    f = f_ref[...]
    cq = _rms(f[:, 0:DSA_Q_LORA], qn_ref[...]).astype(BF16)
    cos_q, sqa, sqb = cq_ref[...], sqa_ref[...], sqb_ref[...]
    cos_i, sia, sib = ci_ref[...], sia_ref[...], sib_ref[...]
    q = _rope(jnp.dot(cq, wq_ref[...], preferred_element_type=F32), cos_q, sqa, sqb, HEAD_DIM // 8)
    tm = f.shape[0]
    qt_o[...] = (q * HEAD_DIM ** -0.5).T.reshape(N_HEADS_G, HEAD_DIM, tm).astype(BF16)
    qi = _rope(jnp.dot(cq, wqi_ref[...], preferred_element_type=F32), cos_i, sia, sib, IDX_DIM // 8)
    qit_o[...] = qi.T.reshape(IDX_HEADS, IDX_DIM, tm).astype(BF16)
    kv = f[:, 128:256]
    kv_r = _rope(kv, cos_q[:, 0:128], sqa[:, 0:128], sqb[:, 0:128], HEAD_DIM // 8)
    k_o[...] = kv_r[:, 0:HEAD_DIM].astype(BF16)
    v_o[...] = kv[:, HEAD_DIM:128].astype(BF16)
    tail = f[:, 256:384]
    tail_r = _rope(tail, cos_i[:, 0:128], sia[:, 0:128], sib[:, 0:128], IDX_DIM // 8)
    ki_o[...] = tail_r[:, 0:IDX_DIM].astype(BF16)
    wt_o[...] = tail.T[IDX_DIM:IDX_DIM + IDX_HEADS, :] * (IDX_HEADS ** -0.5 * IDX_DIM ** -0.5)


def _dsa_prep(f2, qn, wq, wqi, tq, ti):
    N, W = f2.shape
    tm = ROW_TILE
    G = GROUP_W
    row = lambda w: pl.BlockSpec((tm, w), lambda i: (i, 0))
    col = lambda h: pl.BlockSpec((h, tm), lambda i: (0, i))
    slab = lambda h, d: pl.BlockSpec((h, d, tm), lambda i: (0, 0, i))
    c = lambda shape: pl.BlockSpec(shape, lambda i: (0, 0))
    return pl.pallas_call(
        _dsa_prep_kernel,
        grid=(N // tm,),
        in_specs=[row(W), c((1, DSA_Q_LORA)), c((DSA_Q_LORA, G)), c((DSA_Q_LORA, G))] + [row(G)] * 6,
        out_specs=[slab(N_HEADS_G, HEAD_DIM), slab(IDX_HEADS, IDX_DIM), col(IDX_HEADS),
                   row(HEAD_DIM), row(HEAD_DIM), row(IDX_DIM)],
        out_shape=[jax.ShapeDtypeStruct((N_HEADS_G, HEAD_DIM, N), BF16),
                   jax.ShapeDtypeStruct((IDX_HEADS, IDX_DIM, N), BF16),
                   jax.ShapeDtypeStruct((IDX_HEADS, N), F32),
                   jax.ShapeDtypeStruct((N, HEAD_DIM), BF16), jax.ShapeDtypeStruct((N, HEAD_DIM), BF16),
                   jax.ShapeDtypeStruct((N, IDX_DIM), BF16)],
        compiler_params=_params(("parallel",)),
        name="dsa_prep",
    )(f2, qn, wq, wqi, *tq, *ti)


def _float_key(bits):
    return bits ^ ((bits >> 31) & 0x7FFFFFFF)


def _dsa_kernel(TK, topk, qt_ref, qit_ref, wt_ref, k_ref, v_ref, ki_ref, on_ref, o_ref,
                skey_ref, lg_ref, out_ref):
    i = pl.program_id(1)
    TQ = Q_BLOCK
    n_kt = (i * TQ + TQ + TK - 1) // TK
    q_all = jnp.concatenate([qt_ref[h] for h in range(N_HEADS_G)], axis=1)
    qi_all = jnp.concatenate([qit_ref[h] for h in range(IDX_HEADS)], axis=1)
    wt = wt_ref[...]
    kpos = lax.broadcasted_iota(I32, (TK, TQ), 0)
    qpos = i * TQ + lax.broadcasted_iota(I32, (TK, TQ), 1)
    head = lambda x, h: x[:, h * TQ:(h + 1) * TQ]

    def fold(x, op):
        return _tree([x[r * SUBLANES:(r + 1) * SUBLANES] for r in range(TK // SUBLANES)], op)

    def score_tile(kt, carry):
        off = pl.multiple_of(kt * TK, TK)
        rel = jnp.dot(ki_ref[0, pl.ds(off, TK), :], qi_all, preferred_element_type=F32)
        sc = jnp.zeros((TK, TQ), F32)
        for h in range(IDX_HEADS):
            sc = sc + jnp.maximum(head(rel, h), 0.0) * wt[h:h + 1, :]
        key = _float_key(pltpu.bitcast(sc, I32))
        skey_ref[kt] = jnp.where(kt * TK + kpos <= qpos, key, INT_MIN)
        lg_ref[kt] = jnp.dot(k_ref[0, pl.ds(off, TK), :], q_all, preferred_element_type=F32)
        return carry

    lax.fori_loop(0, n_kt, score_tile, 0)

    def bit_step(j, thr):
        cand = thr + lax.shift_left(jnp.int32(1), 31 - j)

        def count_tile(kt, acc):
            return acc + fold(jnp.where(skey_ref[kt] >= cand, 1.0, 0.0), jnp.add)

        acc = lax.fori_loop(0, n_kt, count_tile, jnp.zeros((SUBLANES, TQ), F32))
        return jnp.where(jnp.sum(acc, axis=0, keepdims=True) >= topk, cand, thr)

    thr = lax.fori_loop(0, 32, bit_step, jnp.full((1, TQ), INT_MIN, I32))
    thr = jnp.maximum(thr, INT_MIN + 1)

    def max_tile(kt, ms):
        sel = skey_ref[kt] >= thr
        return tuple(jnp.maximum(ms[h], fold(jnp.where(sel, lg_ref[kt, :, h * TQ:(h + 1) * TQ], -1e30),
                                             jnp.maximum))
                     for h in range(N_HEADS_G))

    ms = lax.fori_loop(0, n_kt, max_tile, tuple(jnp.full((SUBLANES, TQ), -1e30, F32) for _ in range(N_HEADS_G)))
    ms = [jnp.max(m, axis=0, keepdims=True) for m in ms]

    def acc_tile(kt, carry):
        ls, acc = carry
        sel = skey_ref[kt] >= thr
        ps = [jnp.where(sel, jnp.exp(lg_ref[kt, :, h * TQ:(h + 1) * TQ] - ms[h]), 0.0)
              for h in range(N_HEADS_G)]
        p_all = jnp.concatenate([p.astype(BF16) for p in ps], axis=1)
        vb = v_ref[0, pl.ds(pl.multiple_of(kt * TK, TK), TK), :]
        return (tuple(ls[h] + fold(ps[h], jnp.add) for h in range(N_HEADS_G)),
                acc + lax.dot_general(vb, p_all, TN_DIMS, preferred_element_type=F32))

    ls, acc = lax.fori_loop(0, n_kt, acc_tile,
                            (tuple(jnp.zeros((SUBLANES, TQ), F32) for _ in range(N_HEADS_G)),
                             jnp.zeros((HEAD_DIM, N_HEADS_G * TQ), F32)))
    for h in range(N_HEADS_G):
        out_ref[h * HEAD_DIM:(h + 1) * HEAD_DIM, :] = head(acc, h) / jnp.sum(ls[h], axis=0, keepdims=True)
    o_ref[0] = _rms(out_ref[...].T, on_ref[...]).astype(BF16)


def _dsa_attention(qt, qit, wt, k, v, ki, on):
    B, S, _ = k.shape
    G = GROUP_W
    nq = S // Q_BLOCK
    TK = min(512, S)
    topk = min(DSA_TOPK_MAX, S // 4)
    qcol = lambda h: pl.BlockSpec((h, Q_BLOCK), lambda b, i: (0, b * nq + i))
    qslab = lambda h, d: pl.BlockSpec((h, d, Q_BLOCK), lambda b, i: (0, 0, b * nq + i))
    full = lambda wd: pl.BlockSpec((1, S, wd), lambda b, i: (b, 0, 0))
    return pl.pallas_call(
        functools.partial(_dsa_kernel, TK, topk),
        grid=(B, nq),
        in_specs=[qslab(N_HEADS_G, HEAD_DIM), qslab(IDX_HEADS, IDX_DIM), qcol(IDX_HEADS),
                  full(HEAD_DIM), full(HEAD_DIM), full(IDX_DIM), pl.BlockSpec((1, G), lambda b, i: (0, 0))],
        out_specs=pl.BlockSpec((1, Q_BLOCK, G), lambda b, i: (b, i, 0)),
        out_shape=jax.ShapeDtypeStruct((B, S, G), BF16),
        scratch_shapes=[pltpu.VMEM((S // TK, TK, Q_BLOCK), I32),
                        pltpu.VMEM((S // TK, TK, N_HEADS_G * Q_BLOCK), F32),
                        pltpu.VMEM((G, Q_BLOCK), F32)],
        compiler_params=_params(("parallel", "arbitrary")),
        name="dsa_attention",
    )(qt, qit, wt, k, v, ki, on)


def _sb_kernel(q_ref, k_ref, v_ref, u_ref, on_ref, o_ref):
    i = pl.program_id(1)
    T = Q_BLOCK
    H, G = N_HEADS_G, GROUP_W
    u = u_ref[...]
    qt = q_ref[0].astype(F32).T.astype(BF16)
    q_rows = lax.broadcasted_iota(I32, (G, H * T), 0) // HEAD_DIM
    q_cols = lax.broadcasted_iota(I32, (G, H * T), 1) // T
    qbd = jnp.where(q_rows == q_cols, jnp.concatenate([qt] * H, axis=1), jnp.zeros((), BF16))
    kid = lax.broadcasted_iota(I32, (T, H * T), 0)
    qid = lax.broadcasted_iota(I32, (T, H * T), 1) & (T - 1)
    strict = kid < qid

    def cond(carry):
        kt, c, _ = carry
        return jnp.logical_and(kt >= 0, jnp.max(c) > SB_UNDERFLOW)

    def body(carry):
        kt, c, acc = carry
        off = pl.multiple_of(kt * T, T)
        mask = jnp.logical_or(strict, kt < i)
        z = jnp.dot(k_ref[0, pl.ds(off, T), :], qbd, preferred_element_type=F32) * HEAD_DIM ** -0.5
        lm = jnp.where(mask, -_softplus(z), 0.0)
        hi, lo = _split(lm)
        later = jnp.dot(u, jnp.concatenate([hi, lo], axis=1), preferred_element_type=F32)
        later = later[:, :H * T] + later[:, H * T:]
        a = jnp.where(mask, jnp.exp(z + lm + (c + later)), 0.0)
        av = lax.dot_general(v_ref[0, pl.ds(off, T), :], a.astype(BF16), TN_DIMS, preferred_element_type=F32)
        acc = acc + jnp.concatenate(
            [av[h * HEAD_DIM:(h + 1) * HEAD_DIM, h * T:(h + 1) * T] for h in range(H)], axis=0)
        return kt - 1, c + jnp.sum(lm, axis=0, keepdims=True), acc

    _, _, acc = lax.while_loop(cond, body, (i, jnp.zeros((1, H * T), F32), jnp.zeros((G, T), F32)))
    o_ref[0] = _rms(acc.T, on_ref[...]).astype(BF16)


def _stick_breaking(sb3, u, on):
    B, S, _ = sb3.shape
    G = GROUP_W
    return pl.pallas_call(
        _sb_kernel,
        grid=(B, S // Q_BLOCK),
        in_specs=[pl.BlockSpec((1, Q_BLOCK, G), lambda b, i: (b, i, 0)),
                  pl.BlockSpec((1, S, G), lambda b, i: (b, 0, 1)),
                  pl.BlockSpec((1, S, G), lambda b, i: (b, 0, 2)),
                  pl.BlockSpec((Q_BLOCK, Q_BLOCK), lambda b, i: (0, 0)),
                  pl.BlockSpec((1, G), lambda b, i: (0, 0))],
        out_specs=pl.BlockSpec((1, Q_BLOCK, G), lambda b, i: (b, i, 0)),
        out_shape=jax.ShapeDtypeStruct((B, S, G), BF16),
        compiler_params=_params(("parallel", "arbitrary")),
        name="stick_breaking",
    )(sb3, sb3, sb3, u, on)


def _outproj_kernel(yr_ref, yw_ref, yd_ref, ys_ref, x_ref, g1_ref, sc_ref, sh_ref, gn_ref, wo_ref, rw_ref, rb_ref,
                    x_o, h_o, lg_o):
    G = GROUP_W
    dot = lambda y_ref, g: jnp.dot(y_ref[...], wo_ref[g * G:(g + 1) * G, :], preferred_element_type=F32)
    mixed = dot(yr_ref, 0) + dot(yw_ref, 1) + dot(yd_ref, 2) + dot(ys_ref, 3)
    x1 = x_ref[...] + g1_ref[0] * mixed
    x_o[...] = x1
    h = _rms(x1, gn_ref[...]) * (1.0 + sc_ref[0]) + sh_ref[0]
    h_o[...] = h
    hi, lo = _split(h)
    lg_o[...] = (jnp.dot(hi, rw_ref[0], preferred_element_type=F32) + jnp.dot(lo, rw_ref[0], preferred_element_type=F32)
                 + jnp.dot(hi, rw_ref[1], preferred_element_type=F32) + rb_ref[...])


def _outproj(ys, x2, mod6, gain, w_out, router_w, router_b, S):
    N, D = x2.shape
    tm = ROW_TILE
    per_b = S // tm
    G = GROUP_W
    row = lambda w: pl.BlockSpec((tm, w), lambda i: (i, 0))
    c = lambda shape: pl.BlockSpec(shape, lambda i: (0, 0))
    return pl.pallas_call(
        _outproj_kernel,
        grid=(N // tm,),
        in_specs=[row(G)] * 4 + [row(D), _mod_row(per_b, 2), _mod_row(per_b, 4), _mod_row(per_b, 3),
                                 c((1, D)), c((D, D)), pl.BlockSpec((2, D, N_EXPERTS), lambda i: (0, 0, 0)),
                                 c((1, N_EXPERTS))],
        out_specs=[row(D), row(D), row(N_EXPERTS)],
        out_shape=[jax.ShapeDtypeStruct((N, D), F32), jax.ShapeDtypeStruct((N, D), F32),
                   jax.ShapeDtypeStruct((N, N_EXPERTS), F32)],
        compiler_params=_params(("parallel",)),
        name="out_proj_router",
    )(*ys, x2, mod6, mod6, mod6, gain, w_out, jnp.stack(_split(router_w)), router_b)


def _moe_kernel(be_ref, nu_ref, x_ref, w1_ref, b1_ref, w2_ref, b2_ref, rw_ref, o_ref):
    i = pl.program_id(0)

    @pl.when(i < nu_ref[0])
    def _():
        xb = x_ref[...].astype(BF16)
        w1 = w1_ref.at[0, 0]
        glu = jnp.dot(xb, w1[:, 0:D_FF].astype(BF16), preferred_element_type=F32) + b1_ref[0, :, 0:D_FF]
        lin = jnp.dot(xb, w1[:, D_FF:].astype(BF16), preferred_element_type=F32) + b1_ref[0, :, D_FF:]
        glu = jnp.minimum(glu, SWIGLU_LIMIT)
        lin = jnp.clip(lin, -SWIGLU_LIMIT, SWIGLU_LIMIT)
        act = glu * _sigmoid(SWIGLU_ALPHA * glu) * (lin + 1.0)
        y = jnp.dot(act.astype(BF16), w2_ref[0, 0].astype(BF16), preferred_element_type=F32) + b2_ref[0]
        o_ref[...] = (y * rw_ref[...]).astype(BF16)

    @pl.when(i >= nu_ref[0])
    def _():
        o_ref[...] = jnp.zeros_like(o_ref)


def _moe_ffn(block_expert, n_used, xs, layer, w1, b1, w2, b2, row_w):
    rows, D = xs.shape
    nb = rows // MOE_BLOCK
    L, E = w1.shape[:2]
    live = lambda i, nu: jnp.minimum(i, nu[0] - 1)
    grid_spec = pltpu.PrefetchScalarGridSpec(
        num_scalar_prefetch=2,
        grid=(nb,),
        in_specs=[pl.BlockSpec((MOE_BLOCK, D), lambda i, be, nu: (live(i, nu), 0)),
                  pl.BlockSpec((1, 1, D, 2 * D_FF), lambda i, be, nu: (layer, be[i], 0, 0)),
                  pl.BlockSpec((1, 1, 2 * D_FF), lambda i, be, nu: (layer * E + be[i], 0, 0)),
                  pl.BlockSpec((1, 1, D_FF, D), lambda i, be, nu: (layer, be[i], 0, 0)),
                  pl.BlockSpec((1, 1, D), lambda i, be, nu: (layer * E + be[i], 0, 0)),
                  pl.BlockSpec((MOE_BLOCK, 1), lambda i, be, nu: (live(i, nu), 0))],
        out_specs=pl.BlockSpec((MOE_BLOCK, D), lambda i, be, nu: (i, 0)),
    )
    return pl.pallas_call(
        _moe_kernel,
        grid_spec=grid_spec,
        out_shape=jax.ShapeDtypeStruct((rows, D), BF16),
        compiler_params=_params(("arbitrary",), 56 * 1024 * 1024),
        name="moe_ffn",
    )(block_expert, n_used, xs, w1, b1.reshape(L * E, 1, -1), w2, b2.reshape(L * E, 1, -1), row_w)


def _route(logits):
    n = logits.shape[0]
    top_val, top_idx = lax.top_k(logits, TOP_K)
    gate = jax.nn.softmax(top_val, axis=-1).reshape(-1)
    expert = top_idx.reshape(-1).astype(I32)
    n_assign = n * TOP_K
    ids = jnp.arange(n_assign, dtype=I32)
    _, order = lax.sort((expert, ids), num_keys=1, is_stable=True)
    _, inv = lax.sort((order, ids), num_keys=1)
    counts = jnp.sum((expert[:, None] == jnp.arange(N_EXPERTS, dtype=I32)[None, :]).astype(I32), axis=0)
    starts = jnp.cumsum(counts) - counts
    padded = (counts + MOE_BLOCK - 1) // MOE_BLOCK * MOE_BLOCK
    pad_end = jnp.cumsum(padded)
    pad_start = pad_end - padded
    dest = pad_start[expert] + inv - starts[expert]
    n_blocks = -(-n_assign // MOE_BLOCK) + N_EXPERTS
    block_start = jnp.arange(n_blocks, dtype=I32) * MOE_BLOCK
    block_expert = jnp.minimum(jnp.sum(pad_end[None, :] <= block_start[:, None], axis=1), N_EXPERTS - 1)
    block_expert = block_expert.astype(I32)
    per_row = lambda t: jnp.repeat(t[block_expert], MOE_BLOCK)
    off = jnp.arange(n_blocks * MOE_BLOCK, dtype=I32) - per_row(pad_start)
    valid = off < per_row(counts)
    src = order[jnp.clip(per_row(starts) + off, 0, n_assign - 1)]
    row_token = jnp.where(valid, src // TOP_K, 0)
    row_weight = jnp.where(valid, gate[src], 0.0)
    n_used = (pad_end[-1] // MOE_BLOCK).astype(I32).reshape(1)
    return dest.reshape(n, TOP_K), row_token, row_weight, block_expert, n_used


def _rows(t, idx):
    return t.at[idx].get(mode="promise_in_bounds")


def _final_kernel(x_ref, g2_ref, y0, y1, y2, y3, g_ref, o_ref):
    o_ref[...] = _rms(_combine(x_ref, g2_ref, (y0, y1, y2, y3)), g_ref[...])


def _final_norm(x2, mod6, ys, g, S):
    N, D = x2.shape
    tm = ROW_TILE
    row = pl.BlockSpec((tm, D), lambda i: (i, 0))
    return pl.pallas_call(
        _final_kernel, grid=(N // tm,),
        in_specs=[row, _mod_row(S // tm, 5)] + [row] * 4 + [pl.BlockSpec((1, D), lambda i: (0, 0))],
        out_specs=row,
        out_shape=jax.ShapeDtypeStruct((N, D), F32),
        compiler_params=_params(("parallel",)), name="final_norm",
    )(x2, mod6, *ys, g)


def _rope_tables(pos, group, rot_dim, theta, width):
    half = rot_dim // 2
    inv = theta ** (-jnp.arange(half, dtype=F32) / half)
    ang = pos.astype(F32)[..., None] * inv
    cos, sin = jnp.cos(ang), jnp.sin(ang)
    rest = group - rot_dim
    ones = jnp.ones(ang.shape[:2] + (rest,), F32)
    zeros = jnp.zeros(ang.shape[:2] + (rest,), F32)
    zh = jnp.zeros_like(sin)
    rep = width // group
    tile = lambda t: jnp.tile(t, (1, 1, rep)).reshape(-1, width)
    return (tile(jnp.concatenate([cos, cos, ones], -1)),
            tile(jnp.concatenate([-sin, zh, zeros], -1)),
            tile(jnp.concatenate([zh, sin, zeros], -1)))


def _pad_rows(w, start, total=128):
    return jnp.zeros((total, w.shape[1]), w.dtype).at[start:start + w.shape[0]].set(w)


def kernel(x, c, positions, ada_w, ada_b, norm_mix, norm_ffn, w_in, ret_gn, rwkv_mu, rwkv_w0, rwkv_w2, rwkv_a0,
           rwkv_a2, rwkv_g2, rwkv_kk, rwkv_ka, rwkv_rk, rwkv_ln, dsa_qnorm, dsa_wq_up, dsa_wqi_up, dsa_onorm,
           sb_onorm, w_out, router_w, router_b, moe_w1, moe_b1, moe_w2, moe_b2, norm_final):
    B, S, D = x.shape
    N = B * S
    L = ada_w.shape[0]
    G = GROUP_W
    row2 = lambda t: t.reshape(1, -1)

    mod = _ada(c, ada_w, ada_b)
    ret_tabs = tuple(t.reshape(B, S, G) for t in _rope_tables(positions, HEAD_DIM, HEAD_DIM, RET_THETA, G))
    dq_tabs = _rope_tables(positions, HEAD_DIM, HEAD_DIM // 4, ROPE_THETA, G)
    di_tabs = _rope_tables(positions, IDX_DIM, IDX_DIM // 4, ROPE_THETA, G)
    ret_c = _ret_consts(min(128, S))
    hm = (jnp.arange(G)[:, None] // HEAD_DIM == jnp.arange(G)[None, :] // HEAD_DIM).astype(BF16)
    u_later = (jnp.arange(Q_BLOCK)[None, :] > jnp.arange(Q_BLOCK)[:, None]).astype(BF16)
    o1, o2 = RET_COLS + RWKV_COLS, RET_COLS + RWKV_COLS + DSA_COLS

    x2 = x.reshape(N, D)
    comb = None
    for l in range(L):
        mod6 = mod[l].reshape(B * 6, 1, D)
        w_l = w_in[l]
        w_cat = jnp.concatenate(
            [w_l[:, :o1], w_l[:, o1:o2], jnp.zeros((D, DSA_PAD - DSA_COLS), F32), w_l[:, o2:]], axis=1).astype(BF16)
        outs = _inproj(x2, row2(norm_mix[l]), mod6, w_cat, S, comb)
        if comb is not None:
            x2, outs = outs[0], outs[1:]
        ret, rwkv, dsa, sb = outs

        y_ret = _retention(ret.reshape(B, S, RET_COLS), ret_tabs, ret_c, row2(ret_gn[l])).reshape(N, G)

        prep = _rwkv_prep(rwkv, S, row2(rwkv_mu[l]), row2(rwkv_w0[l]), row2(rwkv_a0[l]),
                          _pad_rows(rwkv_w2[l], 0).astype(BF16), _pad_rows(rwkv_a2[l], 32).astype(BF16),
                          _pad_rows(rwkv_g2[l], 64).astype(BF16), row2(rwkv_kk[l]), row2(rwkv_ka[l]),
                          row2(rwkv_rk[l]), hm)
        r_, w_, k_, v_, nkk_, b_, gate_, bonus_ = prep
        y_scan = _rwkv_scan(*(_to_v_layout(t, B, S) for t in (nkk_, w_, b_, k_, r_, v_)))
        y_rwkv = _rwkv_post(_from_scan_layout(y_scan, B, S), bonus_, gate_, row2(rwkv_ln[l]), hm)

        dqt, dqit, dwt, dk, dv, dki = _dsa_prep(dsa, row2(dsa_qnorm[l]), dsa_wq_up[l].astype(BF16),
                                                dsa_wqi_up[l].astype(BF16), dq_tabs, di_tabs)
        y_dsa = _dsa_attention(dqt, dqit, dwt, dk.reshape(B, S, HEAD_DIM), dv.reshape(B, S, HEAD_DIM),
                               dki.reshape(B, S, IDX_DIM), row2(dsa_onorm[l])).reshape(N, G)

        y_sb = _stick_breaking(sb.reshape(B, S, SB_COLS), u_later, row2(sb_onorm[l])).reshape(N, G)

        x1, h2, logits = _outproj((y_ret, y_rwkv, y_dsa, y_sb), x2, mod6, row2(norm_ffn[l]),
                                  w_out[l].astype(BF16), router_w[l], row2(router_b[l]), S)

        dest, row_token, row_weight, block_expert, n_used = _route(logits)
        y = _moe_ffn(block_expert, n_used, _rows(h2, row_token), l, moe_w1, moe_b1, moe_w2, moe_b2,
                     row_weight[:, None])
        x2 = x1
        comb = (mod6, tuple(_rows(y, dest[:, j]) for j in range(TOP_K)))
    return _final_norm(x2, comb[0], comb[1], row2(norm_final), S).reshape(B, S, D)
```

```python
import functools

import jax
import jax.numpy as jnp
from jax import lax
from jax.experimental import pallas as pl
from jax.experimental.pallas import tpu as pltpu

F32 = jnp.float32
BF16 = jnp.bfloat16
I32 = jnp.int32

D_MODEL = 1024
GROUP_W = 256
HEAD_DIM = 64
N_HEADS_G = 4
NORM_EPS = 1e-5
Q_BLOCK = 128
RET_THETA = 10000.0
RWKV_LN_EPS = 64e-5
DSA_Q_LORA = 128
IDX_HEADS = 8
IDX_DIM = 32
DSA_TOPK_MAX = 256
ROPE_THETA = 500000.0
N_EXPERTS = 32
TOP_K = 4
D_FF = D_MODEL
SWIGLU_ALPHA = 1.702
SWIGLU_LIMIT = 7.0
MOE_BLOCK = 512

RET_COLS = 4 * GROUP_W
RWKV_COLS = 3 * GROUP_W + 128
DSA_COLS = 296
DSA_PAD = 384
SB_COLS = 3 * GROUP_W

LANES = 128
SUBLANES = 8
ROW_TILE = 512
VMEM_LIMIT = 48 * 1024 * 1024
INT_MIN = -2 ** 31
SB_UNDERFLOW = -104.0

HIGHEST = lax.Precision.HIGHEST
NT_DIMS = (((1,), (1,)), ((), ()))
TN_DIMS = (((0,), (0,)), ((), ()))


def _params(sem, vmem=VMEM_LIMIT):
    return pltpu.CompilerParams(dimension_semantics=sem, vmem_limit_bytes=vmem)


def _bdot(a, b):
    return jnp.dot(a.astype(BF16), b.astype(BF16), preferred_element_type=F32)


def _split(x):
    hi = x.astype(BF16)
    return hi, (x - hi.astype(F32)).astype(BF16)


def _split_dot(x, m):
    hi, lo = _split(x)
    return jnp.dot(hi, m, preferred_element_type=F32) + jnp.dot(lo, m, preferred_element_type=F32)


def _tree(parts, op):
    while len(parts) > 1:
        parts = [op(parts[i], parts[i + 1]) for i in range(0, len(parts), 2)]
    return parts[0]


def _sigmoid(x):
    return 1.0 / (1.0 + jnp.exp(-x))


def _softplus(x):
    return jnp.maximum(x, 0.0) + jnp.log(1.0 + jnp.exp(-jnp.abs(x)))


def _rms(x, g, eps=NORM_EPS):
    return x * lax.rsqrt(jnp.mean(x * x, axis=-1, keepdims=True) + eps) * g


def _rope(x, cos, sin_a, sin_b, half):
    w = x.shape[-1]
    return x * cos + pltpu.roll(x, w - half, 1) * sin_a + pltpu.roll(x, half, 1) * sin_b


def _ada_kernel(c_ref, w_ref, b_ref, o_ref):
    c = c_ref[...]
    cond = c * _sigmoid(c)
    o_ref[0] = jnp.dot(cond, w_ref[0], preferred_element_type=F32, precision=HIGHEST) + b_ref[0]


def _ada(c, ada_w, ada_b):
    L, D, W = ada_w.shape
    B = c.shape[0]
    tn = 1024
    return pl.pallas_call(
        _ada_kernel,
        grid=(L, W // tn),
        in_specs=[pl.BlockSpec((B, D), lambda l, j: (0, 0)),
                  pl.BlockSpec((1, D, tn), lambda l, j: (l, 0, j)),
                  pl.BlockSpec((1, 1, tn), lambda l, j: (l, 0, j))],
        out_specs=pl.BlockSpec((1, B, tn), lambda l, j: (l, 0, j)),
        out_shape=jax.ShapeDtypeStruct((L, B, W), F32),
        compiler_params=_params(("parallel", "parallel")),
        name="ada_mod",
    )(c, ada_w, ada_b.reshape(L, 1, W))


def _combine(x_ref, g2_ref, y_refs):
    moe = y_refs[0][...].astype(F32)
    for y_ref in y_refs[1:]:
        moe = moe + y_ref[...].astype(F32)
    return x_ref[...] + g2_ref[0] * moe


def _inproj_kernel(n_comb, x_ref, *refs):
    if n_comb:
        g2_ref, y_refs, refs = refs[0], refs[1:1 + n_comb], refs[1 + n_comb:]
        g_ref, sc_ref, sh_ref, w_ref, x_o, ret_ref, rwkv_ref, dsa_ref, sb_ref = refs
        x = _combine(x_ref, g2_ref, y_refs)
        x_o[...] = x
    else:
        g_ref, sc_ref, sh_ref, w_ref, ret_ref, rwkv_ref, dsa_ref, sb_ref = refs
        x = x_ref[...]
    h = _rms(x, g_ref[...]) * (1.0 + sc_ref[0]) + sh_ref[0]
    hb = h.astype(BF16)
    o0, o1, o2 = RET_COLS, RET_COLS + RWKV_COLS, RET_COLS + RWKV_COLS + DSA_PAD
    ret_ref[...] = jnp.dot(hb, w_ref[:, 0:o0], preferred_element_type=F32)
    rwkv_ref[...] = jnp.dot(hb, w_ref[:, o0:o1], preferred_element_type=F32)
    dsa_ref[...] = jnp.dot(hb, w_ref[:, o1:o2], preferred_element_type=F32)
    sb_ref[...] = jnp.dot(hb, w_ref[:, o2:], preferred_element_type=F32).astype(BF16)


def _mod_row(per_b, j):
    return pl.BlockSpec((1, 1, D_MODEL), lambda i: ((i // per_b) * 6 + j, 0, 0))


def _inproj(x2, gain, mod6, w_cat, S, comb=None):
    N, D = x2.shape
    tm = ROW_TILE
    per_b = S // tm
    wt = w_cat.shape[1]
    row = lambda i: (i, 0)
    in_specs = [pl.BlockSpec((tm, D), row)]
    args = [x2]
    out_specs, out_shape = [], []
    if comb is not None:
        in_specs += [_mod_row(per_b, 5)] + [pl.BlockSpec((tm, D), row)] * len(comb[1])
        args += [comb[0]] + list(comb[1])
        out_specs.append(pl.BlockSpec((tm, D), row))
        out_shape.append(jax.ShapeDtypeStruct((N, D), F32))
    in_specs += [pl.BlockSpec((1, D), lambda i: (0, 0)), _mod_row(per_b, 1), _mod_row(per_b, 0),
                 pl.BlockSpec((D, wt), lambda i: (0, 0))]
    args += [gain, mod6, mod6, w_cat]
    out_specs += [pl.BlockSpec((tm, RET_COLS), row), pl.BlockSpec((tm, RWKV_COLS), row),
                  pl.BlockSpec((tm, DSA_PAD), row), pl.BlockSpec((tm, SB_COLS), row)]
    out_shape += [jax.ShapeDtypeStruct((N, RET_COLS), F32), jax.ShapeDtypeStruct((N, RWKV_COLS), F32),
                  jax.ShapeDtypeStruct((N, DSA_PAD), F32), jax.ShapeDtypeStruct((N, SB_COLS), BF16)]
    return pl.pallas_call(
        functools.partial(_inproj_kernel, 0 if comb is None else len(comb[1])),
        grid=(N // tm,),
        in_specs=in_specs, out_specs=out_specs, out_shape=out_shape,
        compiler_params=_params(("parallel",)),
        name="in_proj",
    )(*args)


def _ret_kernel(q_ref, k_ref, v_ref, g_ref, cos_ref, sa_ref, sb_ref, din_ref, qd_ref, kd_ref, cd_ref, gn_ref,
                o_ref, state_ref, y_ref):
    @pl.when(pl.program_id(1) == 0)
    def _():
        state_ref[...] = jnp.zeros_like(state_ref)

    cos, sin_a, sin_b = cos_ref[0], sa_ref[0], sb_ref[0]
    q = _rope(q_ref[0], cos, sin_a, sin_b, HEAD_DIM // 2)
    k = _rope(k_ref[0], cos, sin_a, sin_b, HEAD_DIM // 2) * HEAD_DIM ** -0.5
    v = v_ref[0]
    qd = q * qd_ref[...]
    kd = k * kd_ref[...]
    for h in range(N_HEADS_G):
        sl = slice(h * HEAD_DIM, (h + 1) * HEAD_DIM)
        qh, kh, vh = q[:, sl].astype(BF16), k[:, sl].astype(BF16), v[:, sl].astype(BF16)
        s = lax.dot_general(qh, kh, NT_DIMS, preferred_element_type=F32) * din_ref[h]
        inner = jnp.dot(s.astype(BF16), vh, preferred_element_type=F32)
        st = state_ref[h]
        cross = _bdot(qd[:, sl], st)
        state_ref[h] = st * cd_ref[:, sl] + lax.dot_general(
            kd[:, sl].astype(BF16), vh, TN_DIMS, preferred_element_type=F32)
        o = inner + cross
        oc = o - jnp.mean(o, axis=-1, keepdims=True)
        y_ref[:, sl] = oc * lax.rsqrt(jnp.mean(oc * oc, axis=-1, keepdims=True) + NORM_EPS)
    g = g_ref[0]
    o_ref[0] = (g * _sigmoid(g) * (y_ref[...] * gn_ref[...])).astype(BF16)


def _retention(ret3, tabs, consts, gn):
    B, S, _ = ret3.shape
    C = consts["din"].shape[1]
    blk = lambda j: pl.BlockSpec((1, C, GROUP_W), lambda b, c, j=j: (b, c, j))
    tab = pl.BlockSpec((1, C, GROUP_W), lambda b, c: (b, c, 0))
    const2 = lambda shape: pl.BlockSpec(shape, lambda b, c: (0,) * len(shape))
    return pl.pallas_call(
        _ret_kernel,
        grid=(B, S // C),
        in_specs=[blk(0), blk(1), blk(2), blk(3), tab, tab, tab,
                  const2((N_HEADS_G, C, C)), const2((C, GROUP_W)), const2((C, GROUP_W)),
                  const2((1, GROUP_W)), const2((1, GROUP_W))],
        out_specs=pl.BlockSpec((1, C, GROUP_W), lambda b, c: (b, c, 0)),
        out_shape=jax.ShapeDtypeStruct((B, S, GROUP_W), BF16),
        scratch_shapes=[pltpu.VMEM((N_HEADS_G, HEAD_DIM, HEAD_DIM), F32), pltpu.VMEM((C, GROUP_W), F32)],
        compiler_params=_params(("parallel", "arbitrary")),
        name="retention",
    )(ret3, ret3, ret3, ret3, tabs[0], tabs[1], tabs[2],
      consts["din"], consts["qd"], consts["kd"], consts["cd"], gn)


def _ret_consts(C):
    H = N_HEADS_G
    lg = jnp.log(1.0 - 2.0 ** (-5.0 - jnp.arange(H, dtype=F32)))
    idx = jnp.arange(C, dtype=F32)
    diff = idx[:, None] - idx[None, :]
    din = jnp.where(diff >= 0, jnp.exp(lg[:, None, None] * jnp.maximum(diff, 0.0)), 0.0)
    q_dec = jnp.exp(lg[:, None] * (idx + 1.0))
    k_dec = jnp.exp(lg[:, None] * (C - 1.0 - idx))
    chunk_dec = jnp.exp(lg * C)
    wide = lambda t: jnp.repeat(t.T, HEAD_DIM, axis=1)
    return {"din": din, "qd": wide(q_dec), "kd": wide(k_dec),
            "cd": jnp.repeat(chunk_dec, HEAD_DIM)[None, :]}


def _rwkv_prep_kernel(per_b, z_ref, zp_ref, mu_ref, w0_ref, a0_ref, w2_ref, a2_ref, g2_ref, kk_ref, ka_ref,
                      rk_ref, hm_ref, r_o, w_o, k_o, v_o, nkk_o, b_o, gate_o, bonus_o):
    z = z_ref[...]
    first = (pl.program_id(0) % per_b) == 0
    prow = jnp.where(first, 0.0, zp_ref[7:8, :])
    rid = lax.broadcasted_iota(I32, z.shape, 0)
    prev = jnp.where(rid == 0, prow, pltpu.roll(z, 1, 0))
    f = z + (prev - z) * mu_ref[...]
    G = GROUP_W
    r, k, v, lo = f[:, 0:G], f[:, G:2 * G], f[:, 2 * G:3 * G], f[:, 3 * G:3 * G + 128]
    w_log = -_softplus(-(w0_ref[...] + _bdot(jnp.tanh(lo), w2_ref[...]))) - 0.5
    decay = jnp.exp(-jnp.exp(w_log))
    a = _sigmoid(a0_ref[...] + _bdot(lo, a2_ref[...]))
    gate = _bdot(_sigmoid(lo), g2_ref[...])
    hm = hm_ref[...]
    kk = k * kk_ref[...]
    kk = kk / jnp.maximum(jnp.sqrt(_split_dot(kk * kk, hm)), 1e-12)
    k2 = k * (1.0 + (a - 1.0) * ka_ref[...])
    r_o[...] = r
    w_o[...] = decay
    k_o[...] = k2
    v_o[...] = v
    nkk_o[...] = -kk
    b_o[...] = kk * a
    gate_o[...] = gate
    bonus_o[...] = _split_dot(r * k2 * rk_ref[...], hm) * v


def _rwkv_prep(z2, S, mu, w0, a0, w2p, a2p, g2p, kk, ka, rk, hm):
    N, W = z2.shape
    tm = ROW_TILE
    per_b = S // tm
    row = lambda i: (i, 0)
    c = lambda shape: pl.BlockSpec(shape, lambda i: (0, 0))
    G = GROUP_W
    return pl.pallas_call(
        functools.partial(_rwkv_prep_kernel, per_b),
        grid=(N // tm,),
        in_specs=[pl.BlockSpec((tm, W), row),
                  pl.BlockSpec((SUBLANES, W), lambda i: (jnp.maximum(i * (tm // SUBLANES) - 1, 0), 0)),
                  c((1, W)), c((1, G)), c((1, G)), c((128, G)), c((128, G)), c((128, G)),
                  c((1, G)), c((1, G)), c((1, G)), c((G, G))],
        out_specs=[pl.BlockSpec((tm, G), row)] * 8,
        out_shape=[jax.ShapeDtypeStruct((N, G), F32)] * 8,
        compiler_params=_params(("parallel",)),
        name="rwkv_prep",
    )(z2, z2, mu, w0, a0, w2p, a2p, g2p, kk, ka, rk, hm)


def _rwkv_scan_kernel(T, nkk_ref, w_ref, b_ref, k_ref, r_ref, v_ref, y_ref, s_ref, rep_ref):
    @pl.when(pl.program_id(0) == 0)
    def _():
        s_ref[...] = jnp.zeros_like(s_ref)

    ch = v_ref.shape[1]
    groups = HEAD_DIM // ch
    width = LANES // groups
    tree = lambda parts: _tree(parts, jnp.add)

    grp = lax.broadcasted_iota(I32, (T * ch, LANES), 1) // width
    for o, ref in enumerate((nkk_ref, w_ref, b_ref, k_ref, r_ref)):
        x = ref[...].reshape(T * ch, LANES)
        turned = [x] + [pltpu.roll(x, m * width, 1) for m in range(1, groups)]
        for j in range(groups):
            y = turned[0]
            for m in range(1, groups):
                y = jnp.where(grp == (j + m) % groups, turned[m], y)
            rep_ref[o, :, j] = y.reshape(T, ch, LANES)

    keys = [(j, kh) for j in range(groups) for kh in range(ch)]
    row = lambda o, t, j, kh: rep_ref[o, t, j, kh:kh + 1, :]

    def step(t, carry):
        vt = v_ref[t]
        sa = tree([s_ref[j * ch + kh] * row(0, t, j, kh) for j, kh in keys])
        ys = []
        for j, kh in keys:
            s_new = (s_ref[j * ch + kh] * row(1, t, j, kh) + sa * row(2, t, j, kh) + vt * row(3, t, j, kh))
            s_ref[j * ch + kh] = s_new
            ys.append(s_new * row(4, t, j, kh))
        y_ref[t] = tree(ys)
        return carry

    lax.fori_loop(0, T, step, 0)


def _rwkv_scan(nkk, w, b, k, r, v):
    S, ch, _ = v.shape
    T = 32
    spec = pl.BlockSpec((T, ch, LANES), lambda i: (i, 0, 0))
    return pl.pallas_call(
        functools.partial(_rwkv_scan_kernel, T),
        grid=(S // T,),
        in_specs=[spec] * 6,
        out_specs=spec,
        out_shape=jax.ShapeDtypeStruct((S, ch, LANES), F32),
        scratch_shapes=[pltpu.VMEM((HEAD_DIM, ch, LANES), F32),
                        pltpu.VMEM((5, T, HEAD_DIM // ch, ch, LANES), F32)],
        compiler_params=_params(("arbitrary",)),
        name="rwkv_scan",
    )(nkk, w, b, k, r, v)


def _rwkv_post_kernel(y_ref, bonus_ref, gate_ref, ln_ref, hm_ref, o_ref):
    y = y_ref[...]
    hm = hm_ref[...]
    yc = y - _split_dot(y, hm) * (1.0 / HEAD_DIM)
    var = _split_dot(yc * yc, hm) * (1.0 / HEAD_DIM)
    o_ref[...] = ((yc * lax.rsqrt(var + RWKV_LN_EPS) * ln_ref[...] + bonus_ref[...]) * gate_ref[...]).astype(BF16)


def _rwkv_post(y2, bonus, gate, ln, hm):
    N, G = y2.shape
    tm = ROW_TILE
    row = pl.BlockSpec((tm, G), lambda i: (i, 0))
    return pl.pallas_call(
        _rwkv_post_kernel,
        grid=(N // tm,),
        in_specs=[row, row, row, pl.BlockSpec((1, G), lambda i: (0, 0)), pl.BlockSpec((G, G), lambda i: (0, 0))],
        out_specs=row,
        out_shape=jax.ShapeDtypeStruct((N, G), BF16),
        compiler_params=_params(("parallel",)),
        name="rwkv_post",
    )(y2, bonus, gate, ln, hm)


def _to_v_layout(x, B, S):
    P = B * N_HEADS_G
    rep = LANES // P
    t = x.reshape(B, S, N_HEADS_G, HEAD_DIM // rep, rep).transpose(1, 3, 4, 0, 2)
    return t.reshape(S, HEAD_DIM // rep, LANES)


def _from_v_layout(y, B, S):
    P = B * N_HEADS_G
    rep = LANES // P
    t = y.reshape(S, HEAD_DIM // rep, rep, B, N_HEADS_G).transpose(3, 0, 4, 1, 2)
    return t.reshape(B * S, GROUP_W)


def _dsa_prep_kernel(f_ref, qn_ref, wq_ref, wqi_ref, cq_ref, sqa_ref, sqb_ref, ci_ref, sia_ref, sib_ref,
                     qt_o, qit_o, wt_o, k_o, v_o, ki_o):
    f = f_ref[...]
    cq = _rms(f[:, 0:DSA_Q_LORA], qn_ref[...]).astype(BF16)
    cos_q, sqa, sqb = cq_ref[...], sqa_ref[...], sqb_ref[...]
    cos_i, sia, sib = ci_ref[...], sia_ref[...], sib_ref[...]
    q = _rope(jnp.dot(cq, wq_ref[...], preferred_element_type=F32), cos_q, sqa, sqb, HEAD_DIM // 8)
    tm = f.shape[0]
    qt_o[...] = (q * HEAD_DIM ** -0.5).T.reshape(N_HEADS_G, HEAD_DIM, tm).astype(BF16)
    qi = _rope(jnp.dot(cq, wqi_ref[...], preferred_element_type=F32), cos_i, sia, sib, IDX_DIM // 8)
    qit_o[...] = qi.T.reshape(IDX_HEADS, IDX_DIM, tm).astype(BF16)
    kv = f[:, 128:256]
    kv_r = _rope(kv, cos_q[:, 0:128], sqa[:, 0:128], sqb[:, 0:128], HEAD_DIM // 8)
    k_o[...] = kv_r[:, 0:HEAD_DIM].astype(BF16)
    v_o[...] = kv[:, HEAD_DIM:128].astype(BF16)
    tail = f[:, 256:384]
    tail_r = _rope(tail, cos_i[:, 0:128], sia[:, 0:128], sib[:, 0:128], IDX_DIM // 8)
    ki_o[...] = tail_r[:, 0:IDX_DIM].astype(BF16)
    wt_o[...] = tail.T[IDX_DIM:IDX_DIM + IDX_HEADS, :] * (IDX_HEADS ** -0.5 * IDX_DIM ** -0.5)


def _dsa_prep(f2, qn, wq, wqi, tq, ti):
    N, W = f2.shape
    tm = ROW_TILE
    G = GROUP_W
    row = lambda w: pl.BlockSpec((tm, w), lambda i: (i, 0))
    col = lambda h: pl.BlockSpec((h, tm), lambda i: (0, i))
    slab = lambda h, d: pl.BlockSpec((h, d, tm), lambda i: (0, 0, i))
    c = lambda shape: pl.BlockSpec(shape, lambda i: (0, 0))
    return pl.pallas_call(
        _dsa_prep_kernel,
        grid=(N // tm,),
        in_specs=[row(W), c((1, DSA_Q_LORA)), c((DSA_Q_LORA, G)), c((DSA_Q_LORA, G))] + [row(G)] * 6,
        out_specs=[slab(N_HEADS_G, HEAD_DIM), slab(IDX_HEADS, IDX_DIM), col(IDX_HEADS),
                   row(HEAD_DIM), row(HEAD_DIM), row(IDX_DIM)],
        out_shape=[jax.ShapeDtypeStruct((N_HEADS_G, HEAD_DIM, N), BF16),
                   jax.ShapeDtypeStruct((IDX_HEADS, IDX_DIM, N), BF16),
                   jax.ShapeDtypeStruct((IDX_HEADS, N), F32),
                   jax.ShapeDtypeStruct((N, HEAD_DIM), BF16), jax.ShapeDtypeStruct((N, HEAD_DIM), BF16),
                   jax.ShapeDtypeStruct((N, IDX_DIM), BF16)],
        compiler_params=_params(("parallel",)),
        name="dsa_prep",
    )(f2, qn, wq, wqi, *tq, *ti)


def _float_key(bits):
    return bits ^ ((bits >> 31) & 0x7FFFFFFF)


def _dsa_kernel(TK, topk, qt_ref, qit_ref, wt_ref, k_ref, v_ref, ki_ref, on_ref, o_ref,
                skey_ref, lg_ref, out_ref):
    i = pl.program_id(1)
    TQ = Q_BLOCK
    n_kt = (i * TQ + TQ + TK - 1) // TK
    q_all = jnp.concatenate([qt_ref[h] for h in range(N_HEADS_G)], axis=1)
    qi_all = jnp.concatenate([qit_ref[h] for h in range(IDX_HEADS)], axis=1)
    wt = wt_ref[...]
    kpos = lax.broadcasted_iota(I32, (TK, TQ), 0)
    qpos = i * TQ + lax.broadcasted_iota(I32, (TK, TQ), 1)
    head = lambda x, h: x[:, h * TQ:(h + 1) * TQ]

    def fold(x, op):
        return _tree([x[r * SUBLANES:(r + 1) * SUBLANES] for r in range(TK // SUBLANES)], op)

    def score_tile(kt, carry):
        off = pl.multiple_of(kt * TK, TK)
        rel = jnp.dot(ki_ref[0, pl.ds(off, TK), :], qi_all, preferred_element_type=F32)
        sc = jnp.zeros((TK, TQ), F32)
        for h in range(IDX_HEADS):
            sc = sc + jnp.maximum(head(rel, h), 0.0) * wt[h:h + 1, :]
        key = _float_key(pltpu.bitcast(sc, I32))
        skey_ref[kt] = jnp.where(kt * TK + kpos <= qpos, key, INT_MIN)
        lg_ref[kt] = jnp.dot(k_ref[0, pl.ds(off, TK), :], q_all, preferred_element_type=F32)
        return carry

    lax.fori_loop(0, n_kt, score_tile, 0)

    def bit_step(j, thr):
        cand = thr + lax.shift_left(jnp.int32(1), 31 - j)

        def count_tile(kt, acc):
            return acc + fold(jnp.where(skey_ref[kt] >= cand, 1.0, 0.0), jnp.add)

        acc = lax.fori_loop(0, n_kt, count_tile, jnp.zeros((SUBLANES, TQ), F32))
        return jnp.where(jnp.sum(acc, axis=0, keepdims=True) >= topk, cand, thr)

    thr = lax.fori_loop(0, 32, bit_step, jnp.full((1, TQ), INT_MIN, I32))
    thr = jnp.maximum(thr, INT_MIN + 1)

    def max_tile(kt, ms):
        sel = skey_ref[kt] >= thr
        return tuple(jnp.maximum(ms[h], fold(jnp.where(sel, lg_ref[kt, :, h * TQ:(h + 1) * TQ], -1e30),
                                             jnp.maximum))
                     for h in range(N_HEADS_G))

    ms = lax.fori_loop(0, n_kt, max_tile, tuple(jnp.full((SUBLANES, TQ), -1e30, F32) for _ in range(N_HEADS_G)))
    ms = [jnp.max(m, axis=0, keepdims=True) for m in ms]

    def acc_tile(kt, carry):
        ls, acc = carry
        sel = skey_ref[kt] >= thr
        ps = [jnp.where(sel, jnp.exp(lg_ref[kt, :, h * TQ:(h + 1) * TQ] - ms[h]), 0.0)
              for h in range(N_HEADS_G)]
        p_all = jnp.concatenate([p.astype(BF16) for p in ps], axis=1)
        vb = v_ref[0, pl.ds(pl.multiple_of(kt * TK, TK), TK), :]
        return (tuple(ls[h] + fold(ps[h], jnp.add) for h in range(N_HEADS_G)),
                acc + lax.dot_general(vb, p_all, TN_DIMS, preferred_element_type=F32))

    ls, acc = lax.fori_loop(0, n_kt, acc_tile,
                            (tuple(jnp.zeros((SUBLANES, TQ), F32) for _ in range(N_HEADS_G)),
                             jnp.zeros((HEAD_DIM, N_HEADS_G * TQ), F32)))
    for h in range(N_HEADS_G):
        out_ref[h * HEAD_DIM:(h + 1) * HEAD_DIM, :] = head(acc, h) / jnp.sum(ls[h], axis=0, keepdims=True)
    o_ref[0] = _rms(out_ref[...].T, on_ref[...]).astype(BF16)


def _dsa_attention(qt, qit, wt, k, v, ki, on):
    B, S, _ = k.shape
    G = GROUP_W
    nq = S // Q_BLOCK
    TK = min(512, S)
    topk = min(DSA_TOPK_MAX, S // 4)
    qcol = lambda h: pl.BlockSpec((h, Q_BLOCK), lambda b, i: (0, b * nq + i))
    qslab = lambda h, d: pl.BlockSpec((h, d, Q_BLOCK), lambda b, i: (0, 0, b * nq + i))
    full = lambda wd: pl.BlockSpec((1, S, wd), lambda b, i: (b, 0, 0))
    return pl.pallas_call(
        functools.partial(_dsa_kernel, TK, topk),
        grid=(B, nq),
        in_specs=[qslab(N_HEADS_G, HEAD_DIM), qslab(IDX_HEADS, IDX_DIM), qcol(IDX_HEADS),
                  full(HEAD_DIM), full(HEAD_DIM), full(IDX_DIM), pl.BlockSpec((1, G), lambda b, i: (0, 0))],
        out_specs=pl.BlockSpec((1, Q_BLOCK, G), lambda b, i: (b, i, 0)),
        out_shape=jax.ShapeDtypeStruct((B, S, G), BF16),
        scratch_shapes=[pltpu.VMEM((S // TK, TK, Q_BLOCK), I32),
                        pltpu.VMEM((S // TK, TK, N_HEADS_G * Q_BLOCK), F32),
                        pltpu.VMEM((G, Q_BLOCK), F32)],
        compiler_params=_params(("parallel", "arbitrary")),
        name="dsa_attention",
    )(qt, qit, wt, k, v, ki, on)


def _sb_kernel(q_ref, k_ref, v_ref, u_ref, on_ref, o_ref):
    i = pl.program_id(1)
    T = Q_BLOCK
    H, G = N_HEADS_G, GROUP_W
    u = u_ref[...]
    qt = q_ref[0].astype(F32).T.astype(BF16)
    q_rows = lax.broadcasted_iota(I32, (G, H * T), 0) // HEAD_DIM
    q_cols = lax.broadcasted_iota(I32, (G, H * T), 1) // T
    qbd = jnp.where(q_rows == q_cols, jnp.concatenate([qt] * H, axis=1), jnp.zeros((), BF16))
    kid = lax.broadcasted_iota(I32, (T, H * T), 0)
    qid = lax.broadcasted_iota(I32, (T, H * T), 1) & (T - 1)
    strict = kid < qid

    def cond(carry):
        kt, c, _ = carry
        return jnp.logical_and(kt >= 0, jnp.max(c) > SB_UNDERFLOW)

    def body(carry):
        kt, c, acc = carry
        off = pl.multiple_of(kt * T, T)
        mask = jnp.logical_or(strict, kt < i)
        z = jnp.dot(k_ref[0, pl.ds(off, T), :], qbd, preferred_element_type=F32) * HEAD_DIM ** -0.5
        lm = jnp.where(mask, -_softplus(z), 0.0)
        hi, lo = _split(lm)
        later = jnp.dot(u, jnp.concatenate([hi, lo], axis=1), preferred_element_type=F32)
        later = later[:, :H * T] + later[:, H * T:]
        a = jnp.where(mask, jnp.exp(z + lm + (c + later)), 0.0)
        av = lax.dot_general(v_ref[0, pl.ds(off, T), :], a.astype(BF16), TN_DIMS, preferred_element_type=F32)
        acc = acc + jnp.concatenate(
            [av[h * HEAD_DIM:(h + 1) * HEAD_DIM, h * T:(h + 1) * T] for h in range(H)], axis=0)
        return kt - 1, c + jnp.sum(lm, axis=0, keepdims=True), acc

    _, _, acc = lax.while_loop(cond, body, (i, jnp.zeros((1, H * T), F32), jnp.zeros((G, T), F32)))
    o_ref[0] = _rms(acc.T, on_ref[...]).astype(BF16)


def _stick_breaking(sb3, u, on):
    B, S, _ = sb3.shape
    G = GROUP_W
    return pl.pallas_call(
        _sb_kernel,
        grid=(B, S // Q_BLOCK),
        in_specs=[pl.BlockSpec((1, Q_BLOCK, G), lambda b, i: (b, i, 0)),
                  pl.BlockSpec((1, S, G), lambda b, i: (b, 0, 1)),
                  pl.BlockSpec((1, S, G), lambda b, i: (b, 0, 2)),
                  pl.BlockSpec((Q_BLOCK, Q_BLOCK), lambda b, i: (0, 0)),
                  pl.BlockSpec((1, G), lambda b, i: (0, 0))],
        out_specs=pl.BlockSpec((1, Q_BLOCK, G), lambda b, i: (b, i, 0)),
        out_shape=jax.ShapeDtypeStruct((B, S, G), BF16),
        compiler_params=_params(("parallel", "arbitrary")),
        name="stick_breaking",
    )(sb3, sb3, sb3, u, on)


def _outproj_kernel(yr_ref, yw_ref, yd_ref, ys_ref, x_ref, g1_ref, sc_ref, sh_ref, gn_ref, wo_ref, rw_ref, rb_ref,
                    x_o, h_o, lg_o):
    G = GROUP_W
    dot = lambda y_ref, g: jnp.dot(y_ref[...], wo_ref[g * G:(g + 1) * G, :], preferred_element_type=F32)
    mixed = dot(yr_ref, 0) + dot(yw_ref, 1) + dot(yd_ref, 2) + dot(ys_ref, 3)
    x1 = x_ref[...] + g1_ref[0] * mixed
    x_o[...] = x1
    h = _rms(x1, gn_ref[...]) * (1.0 + sc_ref[0]) + sh_ref[0]
    h_o[...] = h
    hi, lo = _split(h)
    work = (jnp.dot(hi, rw_ref[0], preferred_element_type=F32) + jnp.dot(lo, rw_ref[0], preferred_element_type=F32)
            + jnp.dot(hi, rw_ref[1], preferred_element_type=F32) + rb_ref[...])
    col = lax.broadcasted_iota(I32, work.shape, 1).astype(F32)
    out = jnp.zeros_like(work)
    vals = []
    for j in range(TOP_K):
        m = jnp.max(work, axis=-1, keepdims=True)
        idx = jnp.min(jnp.where(work == m, col, float(LANES)), axis=-1, keepdims=True)
        out = jnp.where(col == float(j), idx, out)
        vals.append(m)
        work = jnp.where(col == idx, -jnp.inf, work)
    es = [jnp.exp(v - vals[0]) for v in vals]
    den = es[0] + es[1] + es[2] + es[3]
    for j in range(TOP_K):
        out = jnp.where(col == float(TOP_K + j), es[j] / den, out)
    lg_o[...] = out


def _outproj(ys, x2, mod6, gain, w_out, router_w, router_b, S):
    N, D = x2.shape
    tm = ROW_TILE
    per_b = S // tm
    G = GROUP_W
    row = lambda w: pl.BlockSpec((tm, w), lambda i: (i, 0))
    c = lambda shape: pl.BlockSpec(shape, lambda i: (0, 0))
    return pl.pallas_call(
        _outproj_kernel,
        grid=(N // tm,),
        in_specs=[row(G)] * 4 + [row(D), _mod_row(per_b, 2), _mod_row(per_b, 4), _mod_row(per_b, 3),
                                 c((1, D)), c((D, D)), pl.BlockSpec((2, D, LANES), lambda i: (0, 0, 0)),
                                 c((1, LANES))],
        out_specs=[row(D), row(D), row(LANES)],
        out_shape=[jax.ShapeDtypeStruct((N, D), F32), jax.ShapeDtypeStruct((N, D), F32),
                   jax.ShapeDtypeStruct((N, LANES), F32)],
        compiler_params=_params(("parallel",)),
        name="out_proj_router",
    )(*ys, x2, mod6, mod6, mod6, gain, w_out,
      jnp.pad(jnp.stack(_split(router_w)), ((0, 0), (0, 0), (0, LANES - N_EXPERTS))),
      jnp.pad(router_b, ((0, 0), (0, LANES - N_EXPERTS)), constant_values=-1e30))


def _moe_kernel(be_ref, nu_ref, x_ref, w1_ref, b1_ref, w2_ref, b2_ref, rw_ref, o_ref):
    i = pl.program_id(0)

    @pl.when(i < nu_ref[0])
    def _():
        xb = x_ref[...].astype(BF16)
        w1 = w1_ref.at[0, 0]
        glu = jnp.dot(xb, w1[:, 0:D_FF].astype(BF16), preferred_element_type=F32) + b1_ref[0, :, 0:D_FF]
        lin = jnp.dot(xb, w1[:, D_FF:].astype(BF16), preferred_element_type=F32) + b1_ref[0, :, D_FF:]
        glu = jnp.minimum(glu, SWIGLU_LIMIT)
        lin = jnp.clip(lin, -SWIGLU_LIMIT, SWIGLU_LIMIT)
        act = glu * _sigmoid(SWIGLU_ALPHA * glu) * (lin + 1.0)
        y = jnp.dot(act.astype(BF16), w2_ref[0, 0].astype(BF16), preferred_element_type=F32) + b2_ref[0]
        o_ref[...] = (y * rw_ref[...]).astype(BF16)

    @pl.when(i >= nu_ref[0])
    def _():
        o_ref[...] = jnp.zeros_like(o_ref)


def _moe_ffn(block_expert, n_used, xs, layer, w1, b1, w2, b2, row_w):
    rows, D = xs.shape
    nb = rows // MOE_BLOCK
    L, E = w1.shape[:2]
    live = lambda i, nu: jnp.minimum(i, nu[0] - 1)
    grid_spec = pltpu.PrefetchScalarGridSpec(
        num_scalar_prefetch=2,
        grid=(nb,),
        in_specs=[pl.BlockSpec((MOE_BLOCK, D), lambda i, be, nu: (live(i, nu), 0)),
                  pl.BlockSpec((1, 1, D, 2 * D_FF), lambda i, be, nu: (layer, be[i], 0, 0)),
                  pl.BlockSpec((1, 1, 2 * D_FF), lambda i, be, nu: (layer * E + be[i], 0, 0)),
                  pl.BlockSpec((1, 1, D_FF, D), lambda i, be, nu: (layer, be[i], 0, 0)),
                  pl.BlockSpec((1, 1, D), lambda i, be, nu: (layer * E + be[i], 0, 0)),
                  pl.BlockSpec((MOE_BLOCK, 1), lambda i, be, nu: (live(i, nu), 0))],
        out_specs=pl.BlockSpec((MOE_BLOCK, D), lambda i, be, nu: (i, 0)),
    )
    return pl.pallas_call(
        _moe_kernel,
        grid_spec=grid_spec,
        out_shape=jax.ShapeDtypeStruct((rows, D), BF16),
        compiler_params=_params(("arbitrary",), 56 * 1024 * 1024),
        name="moe_ffn",
    )(block_expert, n_used, xs, w1, b1.reshape(L * E, 1, -1), w2, b2.reshape(L * E, 1, -1), row_w)


def _route(routed):
    n = routed.shape[0]
    gate = routed[:, TOP_K:2 * TOP_K].reshape(-1)
    expert = routed[:, 0:TOP_K].reshape(-1).astype(I32)
    n_assign = n * TOP_K
    ids = jnp.arange(n_assign, dtype=I32)
    _, order = lax.sort((expert, ids), num_keys=1, is_stable=True)
    _, inv = lax.sort((order, ids), num_keys=1)
    counts = jnp.sum((expert[:, None] == jnp.arange(N_EXPERTS, dtype=I32)[None, :]).astype(I32), axis=0)
    starts = jnp.cumsum(counts) - counts
    padded = (counts + MOE_BLOCK - 1) // MOE_BLOCK * MOE_BLOCK
    pad_end = jnp.cumsum(padded)
    pad_start = pad_end - padded
    dest = pad_start[expert] + inv - starts[expert]
    n_blocks = -(-n_assign // MOE_BLOCK) + N_EXPERTS
    block_start = jnp.arange(n_blocks, dtype=I32) * MOE_BLOCK
    block_expert = jnp.minimum(jnp.sum(pad_end[None, :] <= block_start[:, None], axis=1), N_EXPERTS - 1)
    block_expert = block_expert.astype(I32)
    per_row = lambda t: jnp.repeat(t[block_expert], MOE_BLOCK)
    off = jnp.arange(n_blocks * MOE_BLOCK, dtype=I32) - per_row(pad_start)
    valid = off < per_row(counts)
    src = order[jnp.clip(per_row(starts) + off, 0, n_assign - 1)]
    row_token = jnp.where(valid, src // TOP_K, 0)
    row_weight = jnp.where(valid, gate[src], 0.0)
    n_used = (pad_end[-1] // MOE_BLOCK).astype(I32).reshape(1)
    return dest.reshape(n, TOP_K), row_token, row_weight, block_expert, n_used


def _rows(t, idx):
    return t.at[idx].get(mode="promise_in_bounds")


def _final_kernel(x_ref, g2_ref, y0, y1, y2, y3, g_ref, o_ref):
    o_ref[...] = _rms(_combine(x_ref, g2_ref, (y0, y1, y2, y3)), g_ref[...])


def _final_norm(x2, mod6, ys, g, S):
    N, D = x2.shape
    tm = ROW_TILE
    row = pl.BlockSpec((tm, D), lambda i: (i, 0))
    return pl.pallas_call(
        _final_kernel, grid=(N // tm,),
        in_specs=[row, _mod_row(S // tm, 5)] + [row] * 4 + [pl.BlockSpec((1, D), lambda i: (0, 0))],
        out_specs=row,
        out_shape=jax.ShapeDtypeStruct((N, D), F32),
        compiler_params=_params(("parallel",)), name="final_norm",
    )(x2, mod6, *ys, g)


def _rope_tables(pos, group, rot_dim, theta, width):
    half = rot_dim // 2
    inv = theta ** (-jnp.arange(half, dtype=F32) / half)
    ang = pos.astype(F32)[..., None] * inv
    cos, sin = jnp.cos(ang), jnp.sin(ang)
    rest = group - rot_dim
    ones = jnp.ones(ang.shape[:2] + (rest,), F32)
    zeros = jnp.zeros(ang.shape[:2] + (rest,), F32)
    zh = jnp.zeros_like(sin)
    rep = width // group
    tile = lambda t: jnp.tile(t, (1, 1, rep)).reshape(-1, width)
    return (tile(jnp.concatenate([cos, cos, ones], -1)),
            tile(jnp.concatenate([-sin, zh, zeros], -1)),
            tile(jnp.concatenate([zh, sin, zeros], -1)))


def _pad_rows(w, start, total=128):
    return jnp.zeros((total, w.shape[1]), w.dtype).at[start:start + w.shape[0]].set(w)


def kernel(x, c, positions, ada_w, ada_b, norm_mix, norm_ffn, w_in, ret_gn, rwkv_mu, rwkv_w0, rwkv_w2, rwkv_a0,
           rwkv_a2, rwkv_g2, rwkv_kk, rwkv_ka, rwkv_rk, rwkv_ln, dsa_qnorm, dsa_wq_up, dsa_wqi_up, dsa_onorm,
           sb_onorm, w_out, router_w, router_b, moe_w1, moe_b1, moe_w2, moe_b2, norm_final):
    B, S, D = x.shape
    N = B * S
    L = ada_w.shape[0]
    G = GROUP_W
    row2 = lambda t: t.reshape(1, -1)

    mod = _ada(c, ada_w, ada_b)
    ret_tabs = tuple(t.reshape(B, S, G) for t in _rope_tables(positions, HEAD_DIM, HEAD_DIM, RET_THETA, G))
    dq_tabs = _rope_tables(positions, HEAD_DIM, HEAD_DIM // 4, ROPE_THETA, G)
    di_tabs = _rope_tables(positions, IDX_DIM, IDX_DIM // 4, ROPE_THETA, G)
    ret_c = _ret_consts(min(128, S))
    hm = (jnp.arange(G)[:, None] // HEAD_DIM == jnp.arange(G)[None, :] // HEAD_DIM).astype(BF16)
    u_later = (jnp.arange(Q_BLOCK)[None, :] > jnp.arange(Q_BLOCK)[:, None]).astype(BF16)
    o1, o2 = RET_COLS + RWKV_COLS, RET_COLS + RWKV_COLS + DSA_COLS

    x2 = x.reshape(N, D)
    comb = None
    for l in range(L):
        mod6 = mod[l].reshape(B * 6, 1, D)
        w_l = w_in[l]
        w_cat = jnp.concatenate(
            [w_l[:, :o1], w_l[:, o1:o2], jnp.zeros((D, DSA_PAD - DSA_COLS), F32), w_l[:, o2:]], axis=1).astype(BF16)
        outs = _inproj(x2, row2(norm_mix[l]), mod6, w_cat, S, comb)
        if comb is not None:
            x2, outs = outs[0], outs[1:]
        ret, rwkv, dsa, sb = outs

        prep = _rwkv_prep(rwkv, S, row2(rwkv_mu[l]), row2(rwkv_w0[l]), row2(rwkv_a0[l]),
                          _pad_rows(rwkv_w2[l], 0).astype(BF16), _pad_rows(rwkv_a2[l], 32).astype(BF16),
                          _pad_rows(rwkv_g2[l], 64).astype(BF16), row2(rwkv_kk[l]), row2(rwkv_ka[l]),
                          row2(rwkv_rk[l]), hm)
        r_, w_, k_, v_, nkk_, b_, gate_, bonus_ = prep
        scan_in = tuple(_to_v_layout(t, B, S) for t in (nkk_, w_, b_, k_, r_, v_))

        y_ret = _retention(ret.reshape(B, S, RET_COLS), ret_tabs, ret_c, row2(ret_gn[l])).reshape(N, G)

        dqt, dqit, dwt, dk, dv, dki = _dsa_prep(dsa, row2(dsa_qnorm[l]), dsa_wq_up[l].astype(BF16),
                                                dsa_wqi_up[l].astype(BF16), dq_tabs, di_tabs)
        y_dsa = _dsa_attention(dqt, dqit, dwt, dk.reshape(B, S, HEAD_DIM), dv.reshape(B, S, HEAD_DIM),
                               dki.reshape(B, S, IDX_DIM), row2(dsa_onorm[l])).reshape(N, G)

        y_sb = _stick_breaking(sb.reshape(B, S, SB_COLS), u_later, row2(sb_onorm[l])).reshape(N, G)

        y_scan = _rwkv_scan(*scan_in)
        y_rwkv = _rwkv_post(_from_v_layout(y_scan, B, S), bonus_, gate_, row2(rwkv_ln[l]), hm)

        x1, h2, routed = _outproj((y_ret, y_rwkv, y_dsa, y_sb), x2, mod6, row2(norm_ffn[l]),
                                  w_out[l].astype(BF16), router_w[l], row2(router_b[l]), S)

        dest, row_token, row_weight, block_expert, n_used = _route(routed)
        y = _moe_ffn(block_expert, n_used, _rows(h2, row_token), l, moe_w1, moe_b1, moe_w2, moe_b2,
                     row_weight[:, None])
        x2 = x1
        comb = (mod6, tuple(_rows(y, dest[:, j]) for j in range(TOP_K)))
    return _final_norm(x2, comb[0], comb[1], row2(norm_final), S).reshape(B, S, D)
```

```python
import functools

import jax
import jax.numpy as jnp
from jax import lax
from jax.experimental import pallas as pl
from jax.experimental.pallas import tpu as pltpu

F32 = jnp.float32
BF16 = jnp.bfloat16
I32 = jnp.int32

D_MODEL = 1024
GROUP_W = 256
HEAD_DIM = 64
N_HEADS_G = 4
NORM_EPS = 1e-5
Q_BLOCK = 128
RET_THETA = 10000.0
RWKV_LN_EPS = 64e-5
DSA_Q_LORA = 128
IDX_HEADS = 8
IDX_DIM = 32
DSA_TOPK_MAX = 256
ROPE_THETA = 500000.0
N_EXPERTS = 32
TOP_K = 4
D_FF = D_MODEL
SWIGLU_ALPHA = 1.702
SWIGLU_LIMIT = 7.0
MOE_BLOCK = 512

RET_COLS = 4 * GROUP_W
RWKV_COLS = 3 * GROUP_W + 128
DSA_COLS = 296
DSA_PAD = 384
SB_COLS = 3 * GROUP_W

LANES = 128
SUBLANES = 8
ROW_TILE = 512
VMEM_LIMIT = 48 * 1024 * 1024
INT_MIN = -2 ** 31
SB_UNDERFLOW = -104.0

HIGHEST = lax.Precision.HIGHEST
NT_DIMS = (((1,), (1,)), ((), ()))
TN_DIMS = (((0,), (0,)), ((), ()))


def _params(sem, vmem=VMEM_LIMIT):
    return pltpu.CompilerParams(dimension_semantics=sem, vmem_limit_bytes=vmem)


def _bdot(a, b):
    return jnp.dot(a.astype(BF16), b.astype(BF16), preferred_element_type=F32)


def _split(x):
    hi = x.astype(BF16)
    return hi, (x - hi.astype(F32)).astype(BF16)


def _split_dot(x, m):
    hi, lo = _split(x)
    return jnp.dot(hi, m, preferred_element_type=F32) + jnp.dot(lo, m, preferred_element_type=F32)


def _tree(parts, op):
    while len(parts) > 1:
        parts = [op(parts[i], parts[i + 1]) for i in range(0, len(parts), 2)]
    return parts[0]


def _sigmoid(x):
    return 1.0 / (1.0 + jnp.exp(-x))


def _softplus(x):
    return jnp.maximum(x, 0.0) + jnp.log(1.0 + jnp.exp(-jnp.abs(x)))


def _rms(x, g, eps=NORM_EPS):
    return x * lax.rsqrt(jnp.mean(x * x, axis=-1, keepdims=True) + eps) * g


def _rope(x, cos, sin_a, sin_b, half):
    w = x.shape[-1]
    return x * cos + pltpu.roll(x, w - half, 1) * sin_a + pltpu.roll(x, half, 1) * sin_b


def _ada_kernel(c_ref, w_ref, b_ref, o_ref):
    c = c_ref[...]
    cond = c * _sigmoid(c)
    o_ref[0] = jnp.dot(cond, w_ref[0], preferred_element_type=F32, precision=HIGHEST) + b_ref[0]


def _ada(c, ada_w, ada_b):
    L, D, W = ada_w.shape
    B = c.shape[0]
    tn = 1024
    return pl.pallas_call(
        _ada_kernel,
        grid=(L, W // tn),
        in_specs=[pl.BlockSpec((B, D), lambda l, j: (0, 0)),
                  pl.BlockSpec((1, D, tn), lambda l, j: (l, 0, j)),
                  pl.BlockSpec((1, 1, tn), lambda l, j: (l, 0, j))],
        out_specs=pl.BlockSpec((1, B, tn), lambda l, j: (l, 0, j)),
        out_shape=jax.ShapeDtypeStruct((L, B, W), F32),
        compiler_params=_params(("parallel", "parallel")),
        name="ada_mod",
    )(c, ada_w, ada_b.reshape(L, 1, W))


def _combine(x_ref, g2_ref, y_refs):
    moe = y_refs[0][...].astype(F32)
    for y_ref in y_refs[1:]:
        moe = moe + y_ref[...].astype(F32)
    return x_ref[...] + g2_ref[0] * moe


def _inproj_kernel(n_comb, x_ref, *refs):
    if n_comb:
        g2_ref, y_refs, refs = refs[0], refs[1:1 + n_comb], refs[1 + n_comb:]
        g_ref, sc_ref, sh_ref, w_ref, x_o, ret_ref, rwkv_ref, dsa_ref, sb_ref = refs
        x = _combine(x_ref, g2_ref, y_refs)
        x_o[...] = x
    else:
        g_ref, sc_ref, sh_ref, w_ref, ret_ref, rwkv_ref, dsa_ref, sb_ref = refs
        x = x_ref[...]
    h = _rms(x, g_ref[...]) * (1.0 + sc_ref[0]) + sh_ref[0]
    hb = h.astype(BF16)
    o0, o1, o2 = RET_COLS, RET_COLS + RWKV_COLS, RET_COLS + RWKV_COLS + DSA_PAD
    ret_ref[...] = jnp.dot(hb, w_ref[:, 0:o0], preferred_element_type=F32)
    rwkv_ref[...] = jnp.dot(hb, w_ref[:, o0:o1], preferred_element_type=F32)
    dsa_ref[...] = jnp.dot(hb, w_ref[:, o1:o2], preferred_element_type=F32)
    sb_ref[...] = jnp.dot(hb, w_ref[:, o2:], preferred_element_type=F32).astype(BF16)


def _mod_row(per_b, j):
    return pl.BlockSpec((1, 1, D_MODEL), lambda i: ((i // per_b) * 6 + j, 0, 0))


def _inproj(x2, gain, mod6, w_cat, S, comb=None):
    N, D = x2.shape
    tm = ROW_TILE
    per_b = S // tm
    wt = w_cat.shape[1]
    row = lambda i: (i, 0)
    in_specs = [pl.BlockSpec((tm, D), row)]
    args = [x2]
    out_specs, out_shape = [], []
    if comb is not None:
        in_specs += [_mod_row(per_b, 5)] + [pl.BlockSpec((tm, D), row)] * len(comb[1])
        args += [comb[0]] + list(comb[1])
        out_specs.append(pl.BlockSpec((tm, D), row))
        out_shape.append(jax.ShapeDtypeStruct((N, D), F32))
    in_specs += [pl.BlockSpec((1, D), lambda i: (0, 0)), _mod_row(per_b, 1), _mod_row(per_b, 0),
                 pl.BlockSpec((D, wt), lambda i: (0, 0))]
    args += [gain, mod6, mod6, w_cat]
    out_specs += [pl.BlockSpec((tm, RET_COLS), row), pl.BlockSpec((tm, RWKV_COLS), row),
                  pl.BlockSpec((tm, DSA_PAD), row), pl.BlockSpec((tm, SB_COLS), row)]
    out_shape += [jax.ShapeDtypeStruct((N, RET_COLS), F32), jax.ShapeDtypeStruct((N, RWKV_COLS), F32),
                  jax.ShapeDtypeStruct((N, DSA_PAD), F32), jax.ShapeDtypeStruct((N, SB_COLS), BF16)]
    return pl.pallas_call(
        functools.partial(_inproj_kernel, 0 if comb is None else len(comb[1])),
        grid=(N // tm,),
        in_specs=in_specs, out_specs=out_specs, out_shape=out_shape,
        compiler_params=_params(("parallel",)),
        name="in_proj",
    )(*args)


def _ret_kernel(q_ref, k_ref, v_ref, g_ref, cos_ref, sa_ref, sb_ref, din_ref, qd_ref, kd_ref, cd_ref, gn_ref,
                o_ref, state_ref, y_ref):
    @pl.when(pl.program_id(1) == 0)
    def _():
        state_ref[...] = jnp.zeros_like(state_ref)

    cos, sin_a, sin_b = cos_ref[0], sa_ref[0], sb_ref[0]
    q = _rope(q_ref[0], cos, sin_a, sin_b, HEAD_DIM // 2)
    k = _rope(k_ref[0], cos, sin_a, sin_b, HEAD_DIM // 2) * HEAD_DIM ** -0.5
    v = v_ref[0]
    qd = q * qd_ref[...]
    kd = k * kd_ref[...]
    for h in range(N_HEADS_G):
        sl = slice(h * HEAD_DIM, (h + 1) * HEAD_DIM)
        qh, kh, vh = q[:, sl].astype(BF16), k[:, sl].astype(BF16), v[:, sl].astype(BF16)
        s = lax.dot_general(qh, kh, NT_DIMS, preferred_element_type=F32) * din_ref[h]
        inner = jnp.dot(s.astype(BF16), vh, preferred_element_type=F32)
        st = state_ref[h]
        cross = _bdot(qd[:, sl], st)
        state_ref[h] = st * cd_ref[:, sl] + lax.dot_general(
            kd[:, sl].astype(BF16), vh, TN_DIMS, preferred_element_type=F32)
        o = inner + cross
        oc = o - jnp.mean(o, axis=-1, keepdims=True)
        y_ref[:, sl] = oc * lax.rsqrt(jnp.mean(oc * oc, axis=-1, keepdims=True) + NORM_EPS)
    g = g_ref[0]
    o_ref[0] = (g * _sigmoid(g) * (y_ref[...] * gn_ref[...])).astype(BF16)


def _retention(ret3, tabs, consts, gn):
    B, S, _ = ret3.shape
    C = consts["din"].shape[1]
    blk = lambda j: pl.BlockSpec((1, C, GROUP_W), lambda b, c, j=j: (b, c, j))
    tab = pl.BlockSpec((1, C, GROUP_W), lambda b, c: (b, c, 0))
    const2 = lambda shape: pl.BlockSpec(shape, lambda b, c: (0,) * len(shape))
    return pl.pallas_call(
        _ret_kernel,
        grid=(B, S // C),
        in_specs=[blk(0), blk(1), blk(2), blk(3), tab, tab, tab,
                  const2((N_HEADS_G, C, C)), const2((C, GROUP_W)), const2((C, GROUP_W)),
                  const2((1, GROUP_W)), const2((1, GROUP_W))],
        out_specs=pl.BlockSpec((1, C, GROUP_W), lambda b, c: (b, c, 0)),
        out_shape=jax.ShapeDtypeStruct((B, S, GROUP_W), BF16),
        scratch_shapes=[pltpu.VMEM((N_HEADS_G, HEAD_DIM, HEAD_DIM), F32), pltpu.VMEM((C, GROUP_W), F32)],
        compiler_params=_params(("parallel", "arbitrary")),
        name="retention",
    )(ret3, ret3, ret3, ret3, tabs[0], tabs[1], tabs[2],
      consts["din"], consts["qd"], consts["kd"], consts["cd"], gn)


def _ret_consts(C):
    H = N_HEADS_G
    lg = jnp.log(1.0 - 2.0 ** (-5.0 - jnp.arange(H, dtype=F32)))
    idx = jnp.arange(C, dtype=F32)
    diff = idx[:, None] - idx[None, :]
    din = jnp.where(diff >= 0, jnp.exp(lg[:, None, None] * jnp.maximum(diff, 0.0)), 0.0)
    q_dec = jnp.exp(lg[:, None] * (idx + 1.0))
    k_dec = jnp.exp(lg[:, None] * (C - 1.0 - idx))
    chunk_dec = jnp.exp(lg * C)
    wide = lambda t: jnp.repeat(t.T, HEAD_DIM, axis=1)
    return {"din": din, "qd": wide(q_dec), "kd": wide(k_dec),
            "cd": jnp.repeat(chunk_dec, HEAD_DIM)[None, :]}


def _rwkv_prep_kernel(per_b, z_ref, zp_ref, mu_ref, w0_ref, a0_ref, w2_ref, a2_ref, g2_ref, kk_ref, ka_ref,
                      rk_ref, hm_ref, r_o, w_o, k_o, v_o, nkk_o, b_o, gate_o, bonus_o):
    z = z_ref[...]
    first = (pl.program_id(0) % per_b) == 0
    prow = jnp.where(first, 0.0, zp_ref[7:8, :])
    rid = lax.broadcasted_iota(I32, z.shape, 0)
    prev = jnp.where(rid == 0, prow, pltpu.roll(z, 1, 0))
    f = z + (prev - z) * mu_ref[...]
    G = GROUP_W
    r, k, v, lo = f[:, 0:G], f[:, G:2 * G], f[:, 2 * G:3 * G], f[:, 3 * G:3 * G + 128]
    w_log = -_softplus(-(w0_ref[...] + _bdot(jnp.tanh(lo), w2_ref[...]))) - 0.5
    decay = jnp.exp(-jnp.exp(w_log))
    a = _sigmoid(a0_ref[...] + _bdot(lo, a2_ref[...]))
    gate = _bdot(_sigmoid(lo), g2_ref[...])
    hm = hm_ref[...]
    kk = k * kk_ref[...]
    kk = kk / jnp.maximum(jnp.sqrt(_split_dot(kk * kk, hm)), 1e-12)
    k2 = k * (1.0 + (a - 1.0) * ka_ref[...])
    r_o[...] = r
    w_o[...] = decay
    k_o[...] = k2
    v_o[...] = v
    nkk_o[...] = -kk
    b_o[...] = kk * a
    gate_o[...] = gate
    bonus_o[...] = _split_dot(r * k2 * rk_ref[...], hm) * v


def _rwkv_prep(z2, S, mu, w0, a0, w2p, a2p, g2p, kk, ka, rk, hm):
    N, W = z2.shape
    tm = ROW_TILE
    per_b = S // tm
    row = lambda i: (i, 0)
    c = lambda shape: pl.BlockSpec(shape, lambda i: (0, 0))
    G = GROUP_W
    return pl.pallas_call(
        functools.partial(_rwkv_prep_kernel, per_b),
        grid=(N // tm,),
        in_specs=[pl.BlockSpec((tm, W), row),
                  pl.BlockSpec((SUBLANES, W), lambda i: (jnp.maximum(i * (tm // SUBLANES) - 1, 0), 0)),
                  c((1, W)), c((1, G)), c((1, G)), c((128, G)), c((128, G)), c((128, G)),
                  c((1, G)), c((1, G)), c((1, G)), c((G, G))],
        out_specs=[pl.BlockSpec((tm, G), row)] * 8,
        out_shape=[jax.ShapeDtypeStruct((N, G), F32)] * 8,
        compiler_params=_params(("parallel",)),
        name="rwkv_prep",
    )(z2, z2, mu, w0, a0, w2p, a2p, g2p, kk, ka, rk, hm)


def _rwkv_scan_kernel(T, nkk_ref, w_ref, b_ref, k_ref, r_ref, v_ref, y_ref, s_ref, rep_ref):
    @pl.when(pl.program_id(0) == 0)
    def _():
        s_ref[...] = jnp.zeros_like(s_ref)

    ch = v_ref.shape[1]
    groups = HEAD_DIM // ch
    width = LANES // groups
    tree = lambda parts: _tree(parts, jnp.add)

    grp = lax.broadcasted_iota(I32, (T * ch, LANES), 1) // width
    for o, ref in enumerate((nkk_ref, w_ref, b_ref, k_ref, r_ref)):
        x = ref[...].reshape(T * ch, LANES)
        turned = [x] + [pltpu.roll(x, m * width, 1) for m in range(1, groups)]
        for j in range(groups):
            y = turned[0]
            for m in range(1, groups):
                y = jnp.where(grp == (j + m) % groups, turned[m], y)
            rep_ref[o, :, j] = y.reshape(T, ch, LANES)

    keys = [(j, kh) for j in range(groups) for kh in range(ch)]
    row = lambda o, t, j, kh: rep_ref[o, t, j, kh:kh + 1, :]

    def step(t, carry):
        vt = v_ref[t]
        sa = tree([s_ref[j * ch + kh] * row(0, t, j, kh) for j, kh in keys])
        ys = []
        for j, kh in keys:
            s_new = (s_ref[j * ch + kh] * row(1, t, j, kh) + sa * row(2, t, j, kh) + vt * row(3, t, j, kh))
            s_ref[j * ch + kh] = s_new
            ys.append(s_new * row(4, t, j, kh))
        y_ref[t] = tree(ys)
        return carry

    lax.fori_loop(0, T, step, 0)


def _rwkv_scan(nkk, w, b, k, r, v):
    S, ch, _ = v.shape
    T = 32
    spec = pl.BlockSpec((T, ch, LANES), lambda i: (i, 0, 0))
    return pl.pallas_call(
        functools.partial(_rwkv_scan_kernel, T),
        grid=(S // T,),
        in_specs=[spec] * 6,
        out_specs=spec,
        out_shape=jax.ShapeDtypeStruct((S, ch, LANES), F32),
        scratch_shapes=[pltpu.VMEM((HEAD_DIM, ch, LANES), F32),
                        pltpu.VMEM((5, T, HEAD_DIM // ch, ch, LANES), F32)],
        compiler_params=_params(("arbitrary",)),
        name="rwkv_scan",
    )(nkk, w, b, k, r, v)


def _rwkv_post_kernel(y_ref, bonus_ref, gate_ref, ln_ref, hm_ref, o_ref):
    y = y_ref[...]
    hm = hm_ref[...]
    yc = y - _split_dot(y, hm) * (1.0 / HEAD_DIM)
    var = _split_dot(yc * yc, hm) * (1.0 / HEAD_DIM)
    o_ref[...] = ((yc * lax.rsqrt(var + RWKV_LN_EPS) * ln_ref[...] + bonus_ref[...]) * gate_ref[...]).astype(BF16)


def _rwkv_post(y2, bonus, gate, ln, hm):
    N, G = y2.shape
    tm = ROW_TILE
    row = pl.BlockSpec((tm, G), lambda i: (i, 0))
    return pl.pallas_call(
        _rwkv_post_kernel,
        grid=(N // tm,),
        in_specs=[row, row, row, pl.BlockSpec((1, G), lambda i: (0, 0)), pl.BlockSpec((G, G), lambda i: (0, 0))],
        out_specs=row,
        out_shape=jax.ShapeDtypeStruct((N, G), BF16),
        compiler_params=_params(("parallel",)),
        name="rwkv_post",
    )(y2, bonus, gate, ln, hm)


def _to_v_layout(x, B, S):
    P = B * N_HEADS_G
    rep = LANES // P
    t = x.reshape(B, S, N_HEADS_G, HEAD_DIM // rep, rep).transpose(1, 3, 4, 0, 2)
    return t.reshape(S, HEAD_DIM // rep, LANES)


def _from_v_layout(y, B, S):
    P = B * N_HEADS_G
    rep = LANES // P
    t = y.reshape(S, HEAD_DIM // rep, rep, B, N_HEADS_G).transpose(3, 0, 4, 1, 2)
    return t.reshape(B * S, GROUP_W)


def _dsa_prep_kernel(f_ref, qn_ref, wq_ref, wqi_ref, cq_ref, sqa_ref, sqb_ref, ci_ref, sia_ref, sib_ref,
                     qt_o, qit_o, wt_o, k_o, v_o, ki_o):
    f = f_ref[...]
    cq = _rms(f[:, 0:DSA_Q_LORA], qn_ref[...]).astype(BF16)
    cos_q, sqa, sqb = cq_ref[...], sqa_ref[...], sqb_ref[...]
    cos_i, sia, sib = ci_ref[...], sia_ref[...], sib_ref[...]
    q = _rope(jnp.dot(cq, wq_ref[...], preferred_element_type=F32), cos_q, sqa, sqb, HEAD_DIM // 8)
    tm = f.shape[0]
    qt_o[...] = (q * HEAD_DIM ** -0.5).T.reshape(N_HEADS_G, HEAD_DIM, tm).astype(BF16)
    qi = _rope(jnp.dot(cq, wqi_ref[...], preferred_element_type=F32), cos_i, sia, sib, IDX_DIM // 8)
    qit_o[...] = qi.T.reshape(IDX_HEADS, IDX_DIM, tm).astype(BF16)
    kv = f[:, 128:256]
    kv_r = _rope(kv, cos_q[:, 0:128], sqa[:, 0:128], sqb[:, 0:128], HEAD_DIM // 8)
    k_o[...] = kv_r[:, 0:HEAD_DIM].astype(BF16)
    v_o[...] = kv[:, HEAD_DIM:128].astype(BF16)
    tail = f[:, 256:384]
    tail_r = _rope(tail, cos_i[:, 0:128], sia[:, 0:128], sib[:, 0:128], IDX_DIM // 8)
    ki_o[...] = tail_r[:, 0:IDX_DIM].astype(BF16)
    wt_o[...] = tail.T[IDX_DIM:IDX_DIM + IDX_HEADS, :] * (IDX_HEADS ** -0.5 * IDX_DIM ** -0.5)


def _dsa_prep(f2, qn, wq, wqi, tq, ti):
    N, W = f2.shape
    tm = ROW_TILE
    G = GROUP_W
    row = lambda w: pl.BlockSpec((tm, w), lambda i: (i, 0))
    col = lambda h: pl.BlockSpec((h, tm), lambda i: (0, i))
    slab = lambda h, d: pl.BlockSpec((h, d, tm), lambda i: (0, 0, i))
    c = lambda shape: pl.BlockSpec(shape, lambda i: (0, 0))
    return pl.pallas_call(
        _dsa_prep_kernel,
        grid=(N // tm,),
        in_specs=[row(W), c((1, DSA_Q_LORA)), c((DSA_Q_LORA, G)), c((DSA_Q_LORA, G))] + [row(G)] * 6,
        out_specs=[slab(N_HEADS_G, HEAD_DIM), slab(IDX_HEADS, IDX_DIM), col(IDX_HEADS),
                   row(HEAD_DIM), row(HEAD_DIM), row(IDX_DIM)],
        out_shape=[jax.ShapeDtypeStruct((N_HEADS_G, HEAD_DIM, N), BF16),
                   jax.ShapeDtypeStruct((IDX_HEADS, IDX_DIM, N), BF16),
                   jax.ShapeDtypeStruct((IDX_HEADS, N), F32),
                   jax.ShapeDtypeStruct((N, HEAD_DIM), BF16), jax.ShapeDtypeStruct((N, HEAD_DIM), BF16),
                   jax.ShapeDtypeStruct((N, IDX_DIM), BF16)],
        compiler_params=_params(("parallel",)),
        name="dsa_prep",
    )(f2, qn, wq, wqi, *tq, *ti)


def _float_key(bits):
    return bits ^ ((bits >> 31) & 0x7FFFFFFF)


def _dsa_kernel(TK, topk, qt_ref, qit_ref, wt_ref, k_ref, v_ref, ki_ref, on_ref, o_ref,
                skey_ref, lg_ref, out_ref):
    i = pl.program_id(1)
    TQ = Q_BLOCK
    n_kt = (i * TQ + TQ + TK - 1) // TK
    q_all = jnp.concatenate([qt_ref[h] for h in range(N_HEADS_G)], axis=1)
    qi_all = jnp.concatenate([qit_ref[h] for h in range(IDX_HEADS)], axis=1)
    wt = wt_ref[...]
    kpos = lax.broadcasted_iota(I32, (TK, TQ), 0)
    qpos = i * TQ + lax.broadcasted_iota(I32, (TK, TQ), 1)
    head = lambda x, h: x[:, h * TQ:(h + 1) * TQ]

    def fold(x, op):
        return _tree([x[r * SUBLANES:(r + 1) * SUBLANES] for r in range(TK // SUBLANES)], op)

    def score_tile(kt, carry):
        off = pl.multiple_of(kt * TK, TK)
        rel = jnp.dot(ki_ref[0, pl.ds(off, TK), :], qi_all, preferred_element_type=F32)
        sc = jnp.zeros((TK, TQ), F32)
        for h in range(IDX_HEADS):
            sc = sc + jnp.maximum(head(rel, h), 0.0) * wt[h:h + 1, :]
        key = _float_key(pltpu.bitcast(sc, I32))
        skey_ref[kt] = jnp.where(kt * TK + kpos <= qpos, key, INT_MIN)
        lg_ref[kt] = jnp.dot(k_ref[0, pl.ds(off, TK), :], q_all, preferred_element_type=F32)
        return carry

    lax.fori_loop(0, n_kt, score_tile, 0)

    def bit_step(j, thr):
        cand = thr + lax.shift_left(jnp.int32(1), 31 - j)

        def count_tile(kt, acc):
            return acc + fold(jnp.where(skey_ref[kt] >= cand, 1.0, 0.0), jnp.add)

        acc = lax.fori_loop(0, n_kt, count_tile, jnp.zeros((SUBLANES, TQ), F32))
        return jnp.where(jnp.sum(acc, axis=0, keepdims=True) >= topk, cand, thr)

    thr = lax.fori_loop(0, 32, bit_step, jnp.full((1, TQ), INT_MIN, I32))
    thr = jnp.maximum(thr, INT_MIN + 1)

    def max_tile(kt, ms):
        sel = skey_ref[kt] >= thr
        return tuple(jnp.maximum(ms[h], fold(jnp.where(sel, lg_ref[kt, :, h * TQ:(h + 1) * TQ], -1e30),
                                             jnp.maximum))
                     for h in range(N_HEADS_G))

    ms = lax.fori_loop(0, n_kt, max_tile, tuple(jnp.full((SUBLANES, TQ), -1e30, F32) for _ in range(N_HEADS_G)))
    ms = [jnp.max(m, axis=0, keepdims=True) for m in ms]

    def acc_tile(kt, carry):
        ls, acc = carry
        sel = skey_ref[kt] >= thr
        ps = [jnp.where(sel, jnp.exp(lg_ref[kt, :, h * TQ:(h + 1) * TQ] - ms[h]), 0.0)
              for h in range(N_HEADS_G)]
        p_all = jnp.concatenate([p.astype(BF16) for p in ps], axis=1)
        vb = v_ref[0, pl.ds(pl.multiple_of(kt * TK, TK), TK), :]
        return (tuple(ls[h] + fold(ps[h], jnp.add) for h in range(N_HEADS_G)),
                acc + lax.dot_general(vb, p_all, TN_DIMS, preferred_element_type=F32))

    ls, acc = lax.fori_loop(0, n_kt, acc_tile,
                            (tuple(jnp.zeros((SUBLANES, TQ), F32) for _ in range(N_HEADS_G)),
                             jnp.zeros((HEAD_DIM, N_HEADS_G * TQ), F32)))
    for h in range(N_HEADS_G):
        out_ref[h * HEAD_DIM:(h + 1) * HEAD_DIM, :] = head(acc, h) / jnp.sum(ls[h], axis=0, keepdims=True)
    o_ref[0] = _rms(out_ref[...].T, on_ref[...]).astype(BF16)


def _dsa_attention(qt, qit, wt, k, v, ki, on):
    B, S, _ = k.shape
    G = GROUP_W
    nq = S // Q_BLOCK
    TK = min(512, S)
    topk = min(DSA_TOPK_MAX, S // 4)
    qcol = lambda h: pl.BlockSpec((h, Q_BLOCK), lambda b, i: (0, b * nq + i))
    qslab = lambda h, d: pl.BlockSpec((h, d, Q_BLOCK), lambda b, i: (0, 0, b * nq + i))
    full = lambda wd: pl.BlockSpec((1, S, wd), lambda b, i: (b, 0, 0))
    return pl.pallas_call(
        functools.partial(_dsa_kernel, TK, topk),
        grid=(B, nq),
        in_specs=[qslab(N_HEADS_G, HEAD_DIM), qslab(IDX_HEADS, IDX_DIM), qcol(IDX_HEADS),
                  full(HEAD_DIM), full(HEAD_DIM), full(IDX_DIM), pl.BlockSpec((1, G), lambda b, i: (0, 0))],
        out_specs=pl.BlockSpec((1, Q_BLOCK, G), lambda b, i: (b, i, 0)),
        out_shape=jax.ShapeDtypeStruct((B, S, G), BF16),
        scratch_shapes=[pltpu.VMEM((S // TK, TK, Q_BLOCK), I32),
                        pltpu.VMEM((S // TK, TK, N_HEADS_G * Q_BLOCK), F32),
                        pltpu.VMEM((G, Q_BLOCK), F32)],
        compiler_params=_params(("parallel", "arbitrary")),
        name="dsa_attention",
    )(qt, qit, wt, k, v, ki, on)


def _sb_kernel(q_ref, k_ref, v_ref, u_ref, on_ref, o_ref, z_ref, lm_ref):
    i = pl.program_id(1)
    T = Q_BLOCK
    H, G = N_HEADS_G, GROUP_W
    u = u_ref[...]
    qt = q_ref[0].astype(F32).T.astype(BF16)
    q_rows = lax.broadcasted_iota(I32, (G, H * T), 0) // HEAD_DIM
    q_cols = lax.broadcasted_iota(I32, (G, H * T), 1) // T
    qbd = jnp.where(q_rows == q_cols, jnp.concatenate([qt] * H, axis=1), jnp.zeros((), BF16))
    kid = lax.broadcasted_iota(I32, (T, H * T), 0)
    qid = lax.broadcasted_iota(I32, (T, H * T), 1) & (T - 1)
    strict = kid < qid

    def cond(carry):
        kt, c, _ = carry
        return jnp.logical_and(kt >= 0, jnp.max(c) > SB_UNDERFLOW)

    def logits(kt):
        z = jnp.dot(k_ref[0, pl.ds(pl.multiple_of(kt * T, T), T), :], qbd, preferred_element_type=F32)
        z = z * HEAD_DIM ** -0.5
        return z, -_softplus(z)

    z0, lm0 = logits(i)
    z_ref[...] = z0
    lm_ref[...] = jnp.where(strict, lm0, 0.0)

    def body(carry):
        kt, c, acc = carry
        off = pl.multiple_of(kt * T, T)
        mask = jnp.logical_or(strict, kt < i)
        z, lm = z_ref[...], lm_ref[...]
        z_ref[...], lm_ref[...] = logits(jnp.maximum(kt - 1, 0))
        hi, lo = _split(lm)
        later = jnp.dot(u, jnp.concatenate([hi, lo], axis=1), preferred_element_type=F32)
        later = later[:, :H * T] + later[:, H * T:]
        a = jnp.where(mask, jnp.exp(z + lm + (c + later)), 0.0)
        av = lax.dot_general(v_ref[0, pl.ds(off, T), :], a.astype(BF16), TN_DIMS, preferred_element_type=F32)
        acc = acc + jnp.concatenate(
            [av[h * HEAD_DIM:(h + 1) * HEAD_DIM, h * T:(h + 1) * T] for h in range(H)], axis=0)
        return kt - 1, c + jnp.sum(lm, axis=0, keepdims=True), acc

    _, _, acc = lax.while_loop(cond, body, (i, jnp.zeros((1, H * T), F32), jnp.zeros((G, T), F32)))
    o_ref[0] = _rms(acc.T, on_ref[...]).astype(BF16)


def _stick_breaking(sb3, u, on):
    B, S, _ = sb3.shape
    G = GROUP_W
    return pl.pallas_call(
        _sb_kernel,
        grid=(B, S // Q_BLOCK),
        in_specs=[pl.BlockSpec((1, Q_BLOCK, G), lambda b, i: (b, i, 0)),
                  pl.BlockSpec((1, S, G), lambda b, i: (b, 0, 1)),
                  pl.BlockSpec((1, S, G), lambda b, i: (b, 0, 2)),
                  pl.BlockSpec((Q_BLOCK, Q_BLOCK), lambda b, i: (0, 0)),
                  pl.BlockSpec((1, G), lambda b, i: (0, 0))],
        out_specs=pl.BlockSpec((1, Q_BLOCK, G), lambda b, i: (b, i, 0)),
        out_shape=jax.ShapeDtypeStruct((B, S, G), BF16),
        scratch_shapes=[pltpu.VMEM((Q_BLOCK, N_HEADS_G * Q_BLOCK), F32)] * 2,
        compiler_params=_params(("parallel", "arbitrary")),
        name="stick_breaking",
    )(sb3, sb3, sb3, u, on)


def _outproj_kernel(yr_ref, yw_ref, yd_ref, ys_ref, x_ref, g1_ref, sc_ref, sh_ref, gn_ref, wo_ref, rw_ref, rb_ref,
                    x_o, h_o, lg_o):
    G = GROUP_W
    dot = lambda y_ref, g: jnp.dot(y_ref[...], wo_ref[g * G:(g + 1) * G, :], preferred_element_type=F32)
    mixed = dot(yr_ref, 0) + dot(yw_ref, 1) + dot(yd_ref, 2) + dot(ys_ref, 3)
    x1 = x_ref[...] + g1_ref[0] * mixed
    x_o[...] = x1
    h = _rms(x1, gn_ref[...]) * (1.0 + sc_ref[0]) + sh_ref[0]
    h_o[...] = h
    hi, lo = _split(h)
    work = (jnp.dot(hi, rw_ref[0], preferred_element_type=F32) + jnp.dot(lo, rw_ref[0], preferred_element_type=F32)
            + jnp.dot(hi, rw_ref[1], preferred_element_type=F32) + rb_ref[...])
    col = lax.broadcasted_iota(I32, work.shape, 1).astype(F32)
    out = jnp.zeros_like(work)
    vals = []
    for j in range(TOP_K):
        m = jnp.max(work, axis=-1, keepdims=True)
        idx = jnp.min(jnp.where(work == m, col, float(LANES)), axis=-1, keepdims=True)
        out = jnp.where(col == float(j), idx, out)
        vals.append(m)
        work = jnp.where(col == idx, -jnp.inf, work)
    es = [jnp.exp(v - vals[0]) for v in vals]
    den = es[0] + es[1] + es[2] + es[3]
    for j in range(TOP_K):
        out = jnp.where(col == float(TOP_K + j), es[j] / den, out)
    lg_o[...] = out


def _outproj(ys, x2, mod6, gain, w_out, router_w, router_b, S):
    N, D = x2.shape
    tm = ROW_TILE
    per_b = S // tm
    G = GROUP_W
    row = lambda w: pl.BlockSpec((tm, w), lambda i: (i, 0))
    c = lambda shape: pl.BlockSpec(shape, lambda i: (0, 0))
    return pl.pallas_call(
        _outproj_kernel,
        grid=(N // tm,),
        in_specs=[row(G)] * 4 + [row(D), _mod_row(per_b, 2), _mod_row(per_b, 4), _mod_row(per_b, 3),
                                 c((1, D)), c((D, D)), pl.BlockSpec((2, D, LANES), lambda i: (0, 0, 0)),
                                 c((1, LANES))],
        out_specs=[row(D), row(D), row(LANES)],
        out_shape=[jax.ShapeDtypeStruct((N, D), F32), jax.ShapeDtypeStruct((N, D), F32),
                   jax.ShapeDtypeStruct((N, LANES), F32)],
        compiler_params=_params(("parallel",)),
        name="out_proj_router",
    )(*ys, x2, mod6, mod6, mod6, gain, w_out,
      jnp.pad(jnp.stack(_split(router_w)), ((0, 0), (0, 0), (0, LANES - N_EXPERTS))),
      jnp.pad(router_b, ((0, 0), (0, LANES - N_EXPERTS)), constant_values=-1e30))


def _moe_kernel(be_ref, nu_ref, x_ref, w1_ref, b1_ref, w2_ref, b2_ref, rw_ref, o_ref):
    i = pl.program_id(0)

    @pl.when(i < nu_ref[0])
    def _():
        xb = x_ref[...].astype(BF16)
        w1 = w1_ref.at[0, 0]
        glu = jnp.dot(xb, w1[:, 0:D_FF].astype(BF16), preferred_element_type=F32) + b1_ref[0, :, 0:D_FF]
        lin = jnp.dot(xb, w1[:, D_FF:].astype(BF16), preferred_element_type=F32) + b1_ref[0, :, D_FF:]
        glu = jnp.minimum(glu, SWIGLU_LIMIT)
        lin = jnp.clip(lin, -SWIGLU_LIMIT, SWIGLU_LIMIT)
        act = glu * _sigmoid(SWIGLU_ALPHA * glu) * (lin + 1.0)
        y = jnp.dot(act.astype(BF16), w2_ref[0, 0].astype(BF16), preferred_element_type=F32) + b2_ref[0]
        o_ref[...] = (y * rw_ref[...]).astype(BF16)

    @pl.when(i >= nu_ref[0])
    def _():
        o_ref[...] = jnp.zeros_like(o_ref)


def _moe_ffn(block_expert, n_used, xs, layer, w1, b1, w2, b2, row_w):
    rows, D = xs.shape
    nb = rows // MOE_BLOCK
    L, E = w1.shape[:2]
    live = lambda i, nu: jnp.minimum(i, nu[0] - 1)
    grid_spec = pltpu.PrefetchScalarGridSpec(
        num_scalar_prefetch=2,
        grid=(nb,),
        in_specs=[pl.BlockSpec((MOE_BLOCK, D), lambda i, be, nu: (live(i, nu), 0)),
                  pl.BlockSpec((1, 1, D, 2 * D_FF), lambda i, be, nu: (layer, be[i], 0, 0)),
                  pl.BlockSpec((1, 1, 2 * D_FF), lambda i, be, nu: (layer * E + be[i], 0, 0)),
                  pl.BlockSpec((1, 1, D_FF, D), lambda i, be, nu: (layer, be[i], 0, 0)),
                  pl.BlockSpec((1, 1, D), lambda i, be, nu: (layer * E + be[i], 0, 0)),
                  pl.BlockSpec((MOE_BLOCK, 1), lambda i, be, nu: (live(i, nu), 0))],
        out_specs=pl.BlockSpec((MOE_BLOCK, D), lambda i, be, nu: (i, 0)),
    )
    return pl.pallas_call(
        _moe_kernel,
        grid_spec=grid_spec,
        out_shape=jax.ShapeDtypeStruct((rows, D), BF16),
        compiler_params=_params(("arbitrary",), 56 * 1024 * 1024),
        name="moe_ffn",
    )(block_expert, n_used, xs, w1, b1.reshape(L * E, 1, -1), w2, b2.reshape(L * E, 1, -1), row_w)


def _route(routed):
    n = routed.shape[0]
    gate = routed[:, TOP_K:2 * TOP_K].reshape(-1)
    expert = routed[:, 0:TOP_K].reshape(-1).astype(I32)
    n_assign = n * TOP_K
    ids = jnp.arange(n_assign, dtype=I32)
    _, order = lax.sort((expert, ids), num_keys=1, is_stable=True)
    _, inv = lax.sort((order, ids), num_keys=1)
    counts = jnp.sum((expert[:, None] == jnp.arange(N_EXPERTS, dtype=I32)[None, :]).astype(I32), axis=0)
    starts = jnp.cumsum(counts) - counts
    padded = (counts + MOE_BLOCK - 1) // MOE_BLOCK * MOE_BLOCK
    pad_end = jnp.cumsum(padded)
    pad_start = pad_end - padded
    dest = pad_start[expert] + inv - starts[expert]
    n_blocks = -(-n_assign // MOE_BLOCK) + N_EXPERTS
    block_start = jnp.arange(n_blocks, dtype=I32) * MOE_BLOCK
    block_expert = jnp.minimum(jnp.sum(pad_end[None, :] <= block_start[:, None], axis=1), N_EXPERTS - 1)
    block_expert = block_expert.astype(I32)
    per_row = lambda t: jnp.repeat(t[block_expert], MOE_BLOCK)
    off = jnp.arange(n_blocks * MOE_BLOCK, dtype=I32) - per_row(pad_start)
    valid = off < per_row(counts)
    src = order[jnp.clip(per_row(starts) + off, 0, n_assign - 1)]
    row_token = jnp.where(valid, src // TOP_K, 0)
    row_weight = jnp.where(valid, gate[src], 0.0)
    n_used = (pad_end[-1] // MOE_BLOCK).astype(I32).reshape(1)
    return dest.reshape(n, TOP_K), row_token, row_weight, block_expert, n_used


def _rows(t, idx):
    return t.at[idx].get(mode="promise_in_bounds")


def _final_kernel(x_ref, g2_ref, y0, y1, y2, y3, g_ref, o_ref):
    o_ref[...] = _rms(_combine(x_ref, g2_ref, (y0, y1, y2, y3)), g_ref[...])


def _final_norm(x2, mod6, ys, g, S):
    N, D = x2.shape
    tm = ROW_TILE
    row = pl.BlockSpec((tm, D), lambda i: (i, 0))
    return pl.pallas_call(
        _final_kernel, grid=(N // tm,),
        in_specs=[row, _mod_row(S // tm, 5)] + [row] * 4 + [pl.BlockSpec((1, D), lambda i: (0, 0))],
        out_specs=row,
        out_shape=jax.ShapeDtypeStruct((N, D), F32),
        compiler_params=_params(("parallel",)), name="final_norm",
    )(x2, mod6, *ys, g)


def _rope_tables(pos, group, rot_dim, theta, width):
    half = rot_dim // 2
    inv = theta ** (-jnp.arange(half, dtype=F32) / half)
    ang = pos.astype(F32)[..., None] * inv
    cos, sin = jnp.cos(ang), jnp.sin(ang)
    rest = group - rot_dim
    ones = jnp.ones(ang.shape[:2] + (rest,), F32)
    zeros = jnp.zeros(ang.shape[:2] + (rest,), F32)
    zh = jnp.zeros_like(sin)
    rep = width // group
    tile = lambda t: jnp.tile(t, (1, 1, rep)).reshape(-1, width)
    return (tile(jnp.concatenate([cos, cos, ones], -1)),
            tile(jnp.concatenate([-sin, zh, zeros], -1)),
            tile(jnp.concatenate([zh, sin, zeros], -1)))


def _pad_rows(w, start, total=128):
    return jnp.zeros((total, w.shape[1]), w.dtype).at[start:start + w.shape[0]].set(w)


def kernel(x, c, positions, ada_w, ada_b, norm_mix, norm_ffn, w_in, ret_gn, rwkv_mu, rwkv_w0, rwkv_w2, rwkv_a0,
           rwkv_a2, rwkv_g2, rwkv_kk, rwkv_ka, rwkv_rk, rwkv_ln, dsa_qnorm, dsa_wq_up, dsa_wqi_up, dsa_onorm,
           sb_onorm, w_out, router_w, router_b, moe_w1, moe_b1, moe_w2, moe_b2, norm_final):
    B, S, D = x.shape
    N = B * S
    L = ada_w.shape[0]
    G = GROUP_W
    row2 = lambda t: t.reshape(1, -1)

    mod = _ada(c, ada_w, ada_b)
    ret_tabs = tuple(t.reshape(B, S, G) for t in _rope_tables(positions, HEAD_DIM, HEAD_DIM, RET_THETA, G))
    dq_tabs = _rope_tables(positions, HEAD_DIM, HEAD_DIM // 4, ROPE_THETA, G)
    di_tabs = _rope_tables(positions, IDX_DIM, IDX_DIM // 4, ROPE_THETA, G)
    ret_c = _ret_consts(min(256, S))
    hm = (jnp.arange(G)[:, None] // HEAD_DIM == jnp.arange(G)[None, :] // HEAD_DIM).astype(BF16)
    u_later = (jnp.arange(Q_BLOCK)[None, :] > jnp.arange(Q_BLOCK)[:, None]).astype(BF16)
    o1, o2 = RET_COLS + RWKV_COLS, RET_COLS + RWKV_COLS + DSA_COLS

    x2 = x.reshape(N, D)
    comb = None
    for l in range(L):
        mod6 = mod[l].reshape(B * 6, 1, D)
        w_l = w_in[l]
        w_cat = jnp.concatenate(
            [w_l[:, :o1], w_l[:, o1:o2], jnp.zeros((D, DSA_PAD - DSA_COLS), F32), w_l[:, o2:]], axis=1).astype(BF16)
        outs = _inproj(x2, row2(norm_mix[l]), mod6, w_cat, S, comb)
        if comb is not None:
            x2, outs = outs[0], outs[1:]
        ret, rwkv, dsa, sb = outs

        prep = _rwkv_prep(rwkv, S, row2(rwkv_mu[l]), row2(rwkv_w0[l]), row2(rwkv_a0[l]),
                          _pad_rows(rwkv_w2[l], 0).astype(BF16), _pad_rows(rwkv_a2[l], 32).astype(BF16),
                          _pad_rows(rwkv_g2[l], 64).astype(BF16), row2(rwkv_kk[l]), row2(rwkv_ka[l]),
                          row2(rwkv_rk[l]), hm)
        r_, w_, k_, v_, nkk_, b_, gate_, bonus_ = prep
        scan_in = tuple(_to_v_layout(t, B, S) for t in (nkk_, w_, b_, k_, r_, v_))

        y_ret = _retention(ret.reshape(B, S, RET_COLS), ret_tabs, ret_c, row2(ret_gn[l])).reshape(N, G)

        dqt, dqit, dwt, dk, dv, dki = _dsa_prep(dsa, row2(dsa_qnorm[l]), dsa_wq_up[l].astype(BF16),
                                                dsa_wqi_up[l].astype(BF16), dq_tabs, di_tabs)
        y_dsa = _dsa_attention(dqt, dqit, dwt, dk.reshape(B, S, HEAD_DIM), dv.reshape(B, S, HEAD_DIM),
                               dki.reshape(B, S, IDX_DIM), row2(dsa_onorm[l])).reshape(N, G)

        y_sb = _stick_breaking(sb.reshape(B, S, SB_COLS), u_later, row2(sb_onorm[l])).reshape(N, G)

        y_scan = _rwkv_scan(*scan_in)
        y_rwkv = _rwkv_post(_from_v_layout(y_scan, B, S), bonus_, gate_, row2(rwkv_ln[l]), hm)

        x1, h2, routed = _outproj((y_ret, y_rwkv, y_dsa, y_sb), x2, mod6, row2(norm_ffn[l]),
                                  w_out[l].astype(BF16), router_w[l], row2(router_b[l]), S)

        dest, row_token, row_weight, block_expert, n_used = _route(routed)
        y = _moe_ffn(block_expert, n_used, _rows(h2, row_token), l, moe_w1, moe_b1, moe_w2, moe_b2,
                     row_weight[:, None])
        x2 = x1
        comb = (mod6, tuple(_rows(y, dest[:, j]) for j in range(TOP_K)))
    return _final_norm(x2, comb[0], comb[1], row2(norm_final), S).reshape(B, S, D)
```

```python
import functools

import jax
import jax.numpy as jnp
from jax import lax
from jax.experimental import pallas as pl
from jax.experimental.pallas import tpu as pltpu

F32 = jnp.float32
BF16 = jnp.bfloat16
I32 = jnp.int32

D_MODEL = 1024
GROUP_W = 256
HEAD_DIM = 64
N_HEADS_G = 4
NORM_EPS = 1e-5
Q_BLOCK = 128
RET_THETA = 10000.0
RWKV_LN_EPS = 64e-5
DSA_Q_LORA = 128
IDX_HEADS = 8
IDX_DIM = 32
DSA_TOPK_MAX = 256
ROPE_THETA = 500000.0
N_EXPERTS = 32
TOP_K = 4
D_FF = D_MODEL
SWIGLU_ALPHA = 1.702
SWIGLU_LIMIT = 7.0
MOE_BLOCK = 512

RET_COLS = 4 * GROUP_W
RWKV_COLS = 3 * GROUP_W + 128
DSA_COLS = 296
DSA_PAD = 384
SB_COLS = 3 * GROUP_W

LANES = 128
SUBLANES = 8
ROW_TILE = 512
VMEM_LIMIT = 48 * 1024 * 1024
INT_MIN = -2 ** 31
SB_UNDERFLOW = -104.0
SOFTMAX_FLOOR = 1e-20

HIGHEST = lax.Precision.HIGHEST
NT_DIMS = (((1,), (1,)), ((), ()))
TN_DIMS = (((0,), (0,)), ((), ()))


def _params(sem, vmem=VMEM_LIMIT):
    return pltpu.CompilerParams(dimension_semantics=sem, vmem_limit_bytes=vmem)


def _bdot(a, b):
    return jnp.dot(a.astype(BF16), b.astype(BF16), preferred_element_type=F32)


def _split(x):
    hi = x.astype(BF16)
    return hi, (x - hi.astype(F32)).astype(BF16)


def _split_dot(x, m):
    hi, lo = _split(x)
    return jnp.dot(hi, m, preferred_element_type=F32) + jnp.dot(lo, m, preferred_element_type=F32)


def _tree(parts, op):
    while len(parts) > 1:
        parts = [op(parts[i], parts[i + 1]) for i in range(0, len(parts), 2)]
    return parts[0]


def _sigmoid(x):
    return 1.0 / (1.0 + jnp.exp(-x))


def _softplus(x):
    return jnp.maximum(x, 0.0) + jnp.log(1.0 + jnp.exp(-jnp.abs(x)))


def _rms(x, g, eps=NORM_EPS):
    return x * lax.rsqrt(jnp.mean(x * x, axis=-1, keepdims=True) + eps) * g


def _rope(x, cos, sin_a, sin_b, half):
    w = x.shape[-1]
    return x * cos + pltpu.roll(x, w - half, 1) * sin_a + pltpu.roll(x, half, 1) * sin_b


def _ada_kernel(c_ref, w_ref, b_ref, o_ref):
    c = c_ref[...]
    cond = c * _sigmoid(c)
    o_ref[0] = jnp.dot(cond, w_ref[0], preferred_element_type=F32, precision=HIGHEST) + b_ref[0]


def _ada(c, ada_w, ada_b):
    L, D, W = ada_w.shape
    B = c.shape[0]
    tn = 1024
    return pl.pallas_call(
        _ada_kernel,
        grid=(L, W // tn),
        in_specs=[pl.BlockSpec((B, D), lambda l, j: (0, 0)),
                  pl.BlockSpec((1, D, tn), lambda l, j: (l, 0, j)),
                  pl.BlockSpec((1, 1, tn), lambda l, j: (l, 0, j))],
        out_specs=pl.BlockSpec((1, B, tn), lambda l, j: (l, 0, j)),
        out_shape=jax.ShapeDtypeStruct((L, B, W), F32),
        compiler_params=_params(("parallel", "parallel")),
        name="ada_mod",
    )(c, ada_w, ada_b.reshape(L, 1, W))


def _combine(x_ref, g2_ref, y_refs):
    moe = y_refs[0][...].astype(F32)
    for y_ref in y_refs[1:]:
        moe = moe + y_ref[...].astype(F32)
    return x_ref[...] + g2_ref[0] * moe


def _inproj_kernel(n_comb, x_ref, *refs):
    if n_comb:
        g2_ref, y_refs, refs = refs[0], refs[1:1 + n_comb], refs[1 + n_comb:]
        g_ref, sc_ref, sh_ref, w_ref, x_o, ret_ref, rwkv_ref, dsa_ref, sb_ref = refs
        x = _combine(x_ref, g2_ref, y_refs)
        x_o[...] = x
    else:
        g_ref, sc_ref, sh_ref, w_ref, ret_ref, rwkv_ref, dsa_ref, sb_ref = refs
        x = x_ref[...]
    h = _rms(x, g_ref[...]) * (1.0 + sc_ref[0]) + sh_ref[0]
    hb = h.astype(BF16)
    o0, o1, o2 = RET_COLS, RET_COLS + RWKV_COLS, RET_COLS + RWKV_COLS + DSA_PAD
    ret_ref[...] = jnp.dot(hb, w_ref[:, 0:o0], preferred_element_type=F32)
    rwkv_ref[...] = jnp.dot(hb, w_ref[:, o0:o1], preferred_element_type=F32)
    dsa_ref[...] = jnp.dot(hb, w_ref[:, o1:o2], preferred_element_type=F32)
    sb_ref[...] = jnp.dot(hb, w_ref[:, o2:], preferred_element_type=F32).astype(BF16)


def _mod_row(per_b, j):
    return pl.BlockSpec((1, 1, D_MODEL), lambda i: ((i // per_b) * 6 + j, 0, 0))


def _inproj(x2, gain, mod6, w_cat, S, comb=None):
    N, D = x2.shape
    tm = ROW_TILE
    per_b = S // tm
    wt = w_cat.shape[1]
    row = lambda i: (i, 0)
    in_specs = [pl.BlockSpec((tm, D), row)]
    args = [x2]
    out_specs, out_shape = [], []
    if comb is not None:
        in_specs += [_mod_row(per_b, 5)] + [pl.BlockSpec((tm, D), row)] * len(comb[1])
        args += [comb[0]] + list(comb[1])
        out_specs.append(pl.BlockSpec((tm, D), row))
        out_shape.append(jax.ShapeDtypeStruct((N, D), F32))
    in_specs += [pl.BlockSpec((1, D), lambda i: (0, 0)), _mod_row(per_b, 1), _mod_row(per_b, 0),
                 pl.BlockSpec((D, wt), lambda i: (0, 0))]
    args += [gain, mod6, mod6, w_cat]
    out_specs += [pl.BlockSpec((tm, RET_COLS), row), pl.BlockSpec((tm, RWKV_COLS), row),
                  pl.BlockSpec((tm, DSA_PAD), row), pl.BlockSpec((tm, SB_COLS), row)]
    out_shape += [jax.ShapeDtypeStruct((N, RET_COLS), F32), jax.ShapeDtypeStruct((N, RWKV_COLS), F32),
                  jax.ShapeDtypeStruct((N, DSA_PAD), F32), jax.ShapeDtypeStruct((N, SB_COLS), BF16)]
    return pl.pallas_call(
        functools.partial(_inproj_kernel, 0 if comb is None else len(comb[1])),
        grid=(N // tm,),
        in_specs=in_specs, out_specs=out_specs, out_shape=out_shape,
        compiler_params=_params(("parallel",)),
        name="in_proj",
    )(*args)


def _ret_kernel(q_ref, k_ref, v_ref, g_ref, cos_ref, sa_ref, sb_ref, din_ref, qd_ref, kd_ref, cd_ref, gn_ref,
                o_ref, state_ref, y_ref):
    @pl.when(pl.program_id(1) == 0)
    def _():
        state_ref[...] = jnp.zeros_like(state_ref)

    cos, sin_a, sin_b = cos_ref[0], sa_ref[0], sb_ref[0]
    q = _rope(q_ref[0], cos, sin_a, sin_b, HEAD_DIM // 2)
    k = _rope(k_ref[0], cos, sin_a, sin_b, HEAD_DIM // 2) * HEAD_DIM ** -0.5
    v = v_ref[0]
    qd = q * qd_ref[...]
    kd = k * kd_ref[...]
    for h in range(N_HEADS_G):
        sl = slice(h * HEAD_DIM, (h + 1) * HEAD_DIM)
        qh, kh, vh = q[:, sl].astype(BF16), k[:, sl].astype(BF16), v[:, sl].astype(BF16)
        s = lax.dot_general(qh, kh, NT_DIMS, preferred_element_type=F32) * din_ref[h]
        inner = jnp.dot(s.astype(BF16), vh, preferred_element_type=F32)
        st = state_ref[h]
        cross = _bdot(qd[:, sl], st)
        state_ref[h] = st * cd_ref[:, sl] + lax.dot_general(
            kd[:, sl].astype(BF16), vh, TN_DIMS, preferred_element_type=F32)
        o = inner + cross
        oc = o - jnp.mean(o, axis=-1, keepdims=True)
        y_ref[:, sl] = oc * lax.rsqrt(jnp.mean(oc * oc, axis=-1, keepdims=True) + NORM_EPS)
    g = g_ref[0]
    o_ref[0] = (g * _sigmoid(g) * (y_ref[...] * gn_ref[...])).astype(BF16)


def _retention(ret3, tabs, consts, gn):
    B, S, _ = ret3.shape
    C = consts["din"].shape[1]
    blk = lambda j: pl.BlockSpec((1, C, GROUP_W), lambda b, c, j=j: (b, c, j))
    tab = pl.BlockSpec((1, C, GROUP_W), lambda b, c: (b, c, 0))
    const2 = lambda shape: pl.BlockSpec(shape, lambda b, c: (0,) * len(shape))
    return pl.pallas_call(
        _ret_kernel,
        grid=(B, S // C),
        in_specs=[blk(0), blk(1), blk(2), blk(3), tab, tab, tab,
                  const2((N_HEADS_G, C, C)), const2((C, GROUP_W)), const2((C, GROUP_W)),
                  const2((1, GROUP_W)), const2((1, GROUP_W))],
        out_specs=pl.BlockSpec((1, C, GROUP_W), lambda b, c: (b, c, 0)),
        out_shape=jax.ShapeDtypeStruct((B, S, GROUP_W), BF16),
        scratch_shapes=[pltpu.VMEM((N_HEADS_G, HEAD_DIM, HEAD_DIM), F32), pltpu.VMEM((C, GROUP_W), F32)],
        compiler_params=_params(("parallel", "arbitrary")),
        name="retention",
    )(ret3, ret3, ret3, ret3, tabs[0], tabs[1], tabs[2],
      consts["din"], consts["qd"], consts["kd"], consts["cd"], gn)


def _ret_consts(C):
    H = N_HEADS_G
    lg = jnp.log(1.0 - 2.0 ** (-5.0 - jnp.arange(H, dtype=F32)))
    idx = jnp.arange(C, dtype=F32)
    diff = idx[:, None] - idx[None, :]
    din = jnp.where(diff >= 0, jnp.exp(lg[:, None, None] * jnp.maximum(diff, 0.0)), 0.0)
    q_dec = jnp.exp(lg[:, None] * (idx + 1.0))
    k_dec = jnp.exp(lg[:, None] * (C - 1.0 - idx))
    chunk_dec = jnp.exp(lg * C)
    wide = lambda t: jnp.repeat(t.T, HEAD_DIM, axis=1)
    return {"din": din, "qd": wide(q_dec), "kd": wide(k_dec),
            "cd": jnp.repeat(chunk_dec, HEAD_DIM)[None, :]}


def _rwkv_prep_kernel(per_b, z_ref, zp_ref, mu_ref, w0_ref, a0_ref, w2_ref, a2_ref, g2_ref, kk_ref, ka_ref,
                      rk_ref, hm_ref, r_o, w_o, k_o, v_o, nkk_o, b_o, gate_o, bonus_o):
    z = z_ref[...]
    first = (pl.program_id(0) % per_b) == 0
    prow = jnp.where(first, 0.0, zp_ref[7:8, :])
    rid = lax.broadcasted_iota(I32, z.shape, 0)
    prev = jnp.where(rid == 0, prow, pltpu.roll(z, 1, 0))
    f = z + (prev - z) * mu_ref[...]
    G = GROUP_W
    r, k, v, lo = f[:, 0:G], f[:, G:2 * G], f[:, 2 * G:3 * G], f[:, 3 * G:3 * G + 128]
    w_log = -_softplus(-(w0_ref[...] + _bdot(jnp.tanh(lo), w2_ref[...]))) - 0.5
    decay = jnp.exp(-jnp.exp(w_log))
    a = _sigmoid(a0_ref[...] + _bdot(lo, a2_ref[...]))
    gate = _bdot(_sigmoid(lo), g2_ref[...])
    hm = hm_ref[...]
    kk = k * kk_ref[...]
    kk = kk / jnp.maximum(jnp.sqrt(_split_dot(kk * kk, hm)), 1e-12)
    k2 = k * (1.0 + (a - 1.0) * ka_ref[...])
    r_o[...] = r
    w_o[...] = decay
    k_o[...] = k2
    v_o[...] = v
    nkk_o[...] = -kk
    b_o[...] = kk * a
    gate_o[...] = gate
    bonus_o[...] = _split_dot(r * k2 * rk_ref[...], hm) * v


def _rwkv_prep(z2, S, mu, w0, a0, w2p, a2p, g2p, kk, ka, rk, hm):
    N, W = z2.shape
    tm = ROW_TILE
    per_b = S // tm
    row = lambda i: (i, 0)
    c = lambda shape: pl.BlockSpec(shape, lambda i: (0, 0))
    G = GROUP_W
    return pl.pallas_call(
        functools.partial(_rwkv_prep_kernel, per_b),
        grid=(N // tm,),
        in_specs=[pl.BlockSpec((tm, W), row),
                  pl.BlockSpec((SUBLANES, W), lambda i: (jnp.maximum(i * (tm // SUBLANES) - 1, 0), 0)),
                  c((1, W)), c((1, G)), c((1, G)), c((128, G)), c((128, G)), c((128, G)),
                  c((1, G)), c((1, G)), c((1, G)), c((G, G))],
        out_specs=[pl.BlockSpec((tm, G), row)] * 8,
        out_shape=[jax.ShapeDtypeStruct((N, G), F32)] * 8,
        compiler_params=_params(("parallel",)),
        name="rwkv_prep",
    )(z2, z2, mu, w0, a0, w2p, a2p, g2p, kk, ka, rk, hm)


def _rwkv_scan_kernel(T, nkk_ref, w_ref, b_ref, k_ref, r_ref, v_ref, y_ref, s_ref, rep_ref):
    @pl.when(pl.program_id(0) == 0)
    def _():
        s_ref[...] = jnp.zeros_like(s_ref)

    ch = v_ref.shape[1]
    groups = HEAD_DIM // ch
    width = LANES // groups
    tree = lambda parts: _tree(parts, jnp.add)

    grp = lax.broadcasted_iota(I32, (T * ch, LANES), 1) // width
    for o, ref in enumerate((nkk_ref, w_ref, b_ref, k_ref, r_ref)):
        x = ref[...].reshape(T * ch, LANES)
        turned = [x] + [pltpu.roll(x, m * width, 1) for m in range(1, groups)]
        for j in range(groups):
            y = turned[0]
            for m in range(1, groups):
                y = jnp.where(grp == (j + m) % groups, turned[m], y)
            rep_ref[o, :, j] = y.reshape(T, ch, LANES)

    keys = [(j, kh) for j in range(groups) for kh in range(ch)]
    row = lambda o, t, j, kh: rep_ref[o, t, j, kh:kh + 1, :]

    def step(t, carry):
        vt = v_ref[t]
        sa = tree([s_ref[j * ch + kh] * row(0, t, j, kh) for j, kh in keys])
        ys = []
        for j, kh in keys:
            s_new = (s_ref[j * ch + kh] * row(1, t, j, kh) + sa * row(2, t, j, kh) + vt * row(3, t, j, kh))
            s_ref[j * ch + kh] = s_new
            ys.append(s_new * row(4, t, j, kh))
        y_ref[t] = tree(ys)
        return carry

    lax.fori_loop(0, T, step, 0)


def _rwkv_scan(nkk, w, b, k, r, v):
    S, ch, _ = v.shape
    T = 32
    spec = pl.BlockSpec((T, ch, LANES), lambda i: (i, 0, 0))
    return pl.pallas_call(
        functools.partial(_rwkv_scan_kernel, T),
        grid=(S // T,),
        in_specs=[spec] * 6,
        out_specs=spec,
        out_shape=jax.ShapeDtypeStruct((S, ch, LANES), F32),
        scratch_shapes=[pltpu.VMEM((HEAD_DIM, ch, LANES), F32),
                        pltpu.VMEM((5, T, HEAD_DIM // ch, ch, LANES), F32)],
        compiler_params=_params(("arbitrary",)),
        name="rwkv_scan",
    )(nkk, w, b, k, r, v)


def _rwkv_post_kernel(y_ref, bonus_ref, gate_ref, ln_ref, hm_ref, o_ref):
    y = y_ref[...]
    hm = hm_ref[...]
    yc = y - _split_dot(y, hm) * (1.0 / HEAD_DIM)
    var = _split_dot(yc * yc, hm) * (1.0 / HEAD_DIM)
    o_ref[...] = ((yc * lax.rsqrt(var + RWKV_LN_EPS) * ln_ref[...] + bonus_ref[...]) * gate_ref[...]).astype(BF16)


def _rwkv_post(y2, bonus, gate, ln, hm):
    N, G = y2.shape
    tm = ROW_TILE
    row = pl.BlockSpec((tm, G), lambda i: (i, 0))
    return pl.pallas_call(
        _rwkv_post_kernel,
        grid=(N // tm,),
        in_specs=[row, row, row, pl.BlockSpec((1, G), lambda i: (0, 0)), pl.BlockSpec((G, G), lambda i: (0, 0))],
        out_specs=row,
        out_shape=jax.ShapeDtypeStruct((N, G), BF16),
        compiler_params=_params(("parallel",)),
        name="rwkv_post",
    )(y2, bonus, gate, ln, hm)


def _to_v_layout(x, B, S):
    P = B * N_HEADS_G
    rep = LANES // P
    t = x.reshape(B, S, N_HEADS_G, HEAD_DIM // rep, rep).transpose(1, 3, 4, 0, 2)
    return t.reshape(S, HEAD_DIM // rep, LANES)


def _from_v_layout(y, B, S):
    P = B * N_HEADS_G
    rep = LANES // P
    t = y.reshape(S, HEAD_DIM // rep, rep, B, N_HEADS_G).transpose(3, 0, 4, 1, 2)
    return t.reshape(B * S, GROUP_W)


def _dsa_prep_kernel(f_ref, qn_ref, wq_ref, wqi_ref, cq_ref, sqa_ref, sqb_ref, ci_ref, sia_ref, sib_ref,
                     qt_o, qit_o, wt_o, k_o, v_o, ki_o):
    f = f_ref[...]
    cq = _rms(f[:, 0:DSA_Q_LORA], qn_ref[...]).astype(BF16)
    cos_q, sqa, sqb = cq_ref[...], sqa_ref[...], sqb_ref[...]
    cos_i, sia, sib = ci_ref[...], sia_ref[...], sib_ref[...]
    q = _rope(jnp.dot(cq, wq_ref[...], preferred_element_type=F32), cos_q, sqa, sqb, HEAD_DIM // 8)
    tm = f.shape[0]
    qt_o[...] = (q * HEAD_DIM ** -0.5).T.reshape(N_HEADS_G, HEAD_DIM, tm).astype(BF16)
    qi = _rope(jnp.dot(cq, wqi_ref[...], preferred_element_type=F32), cos_i, sia, sib, IDX_DIM // 8)
    qit_o[...] = qi.T.reshape(IDX_HEADS, IDX_DIM, tm).astype(BF16)
    kv = f[:, 128:256]
    kv_r = _rope(kv, cos_q[:, 0:128], sqa[:, 0:128], sqb[:, 0:128], HEAD_DIM // 8)
    k_o[...] = kv_r[:, 0:HEAD_DIM].astype(BF16)
    v_o[...] = kv[:, HEAD_DIM:128].astype(BF16)
    tail = f[:, 256:384]
    tail_r = _rope(tail, cos_i[:, 0:128], sia[:, 0:128], sib[:, 0:128], IDX_DIM // 8)
    ki_o[...] = tail_r[:, 0:IDX_DIM].astype(BF16)
    wt_o[...] = tail.T[IDX_DIM:IDX_DIM + IDX_HEADS, :] * (IDX_HEADS ** -0.5 * IDX_DIM ** -0.5)


def _dsa_prep(f2, qn, wq, wqi, tq, ti):
    N, W = f2.shape
    tm = ROW_TILE
    G = GROUP_W
    row = lambda w: pl.BlockSpec((tm, w), lambda i: (i, 0))
    col = lambda h: pl.BlockSpec((h, tm), lambda i: (0, i))
    slab = lambda h, d: pl.BlockSpec((h, d, tm), lambda i: (0, 0, i))
    c = lambda shape: pl.BlockSpec(shape, lambda i: (0, 0))
    return pl.pallas_call(
        _dsa_prep_kernel,
        grid=(N // tm,),
        in_specs=[row(W), c((1, DSA_Q_LORA)), c((DSA_Q_LORA, G)), c((DSA_Q_LORA, G))] + [row(G)] * 6,
        out_specs=[slab(N_HEADS_G, HEAD_DIM), slab(IDX_HEADS, IDX_DIM), col(IDX_HEADS),
                   row(HEAD_DIM), row(HEAD_DIM), row(IDX_DIM)],
        out_shape=[jax.ShapeDtypeStruct((N_HEADS_G, HEAD_DIM, N), BF16),
                   jax.ShapeDtypeStruct((IDX_HEADS, IDX_DIM, N), BF16),
                   jax.ShapeDtypeStruct((IDX_HEADS, N), F32),
                   jax.ShapeDtypeStruct((N, HEAD_DIM), BF16), jax.ShapeDtypeStruct((N, HEAD_DIM), BF16),
                   jax.ShapeDtypeStruct((N, IDX_DIM), BF16)],
        compiler_params=_params(("parallel",)),
        name="dsa_prep",
    )(f2, qn, wq, wqi, *tq, *ti)


def _float_key(bits):
    return bits ^ ((bits >> 31) & 0x7FFFFFFF)


def _dsa_kernel(TK, topk, qt_ref, qit_ref, wt_ref, k_ref, v_ref, ki_ref, on_ref, o_ref,
                skey_ref, lg_ref, out_ref):
    i = pl.program_id(1)
    TQ = Q_BLOCK
    n_kt = (i * TQ + TQ + TK - 1) // TK
    q_all = jnp.concatenate([qt_ref[h] for h in range(N_HEADS_G)], axis=1)
    qi_all = jnp.concatenate([qit_ref[h] for h in range(IDX_HEADS)], axis=1)
    wt = wt_ref[...]
    kpos = lax.broadcasted_iota(I32, (TK, TQ), 0)
    qpos = i * TQ + lax.broadcasted_iota(I32, (TK, TQ), 1)
    head = lambda x, h: x[:, h * TQ:(h + 1) * TQ]

    def fold(x, op):
        return _tree([x[r * SUBLANES:(r + 1) * SUBLANES] for r in range(TK // SUBLANES)], op)

    def score_tile(kt, tops):
        off = pl.multiple_of(kt * TK, TK)
        rel = jnp.dot(ki_ref[0, pl.ds(off, TK), :], qi_all, preferred_element_type=F32)
        sc = jnp.zeros((TK, TQ), F32)
        for h in range(IDX_HEADS):
            sc = sc + jnp.maximum(head(rel, h), 0.0) * wt[h:h + 1, :]
        key = _float_key(pltpu.bitcast(sc, I32))
        valid = kt * TK + kpos <= qpos
        skey_ref[kt] = jnp.where(valid, key, INT_MIN)
        lg = jnp.dot(k_ref[0, pl.ds(off, TK), :], q_all, preferred_element_type=F32)
        lg_ref[kt] = lg
        return tuple(jnp.maximum(tops[h], fold(jnp.where(valid, head(lg, h), -1e30), jnp.maximum))
                     for h in range(N_HEADS_G))

    lowest = tuple(jnp.full((SUBLANES, TQ), -1e30, F32) for _ in range(N_HEADS_G))
    tops = lax.fori_loop(0, n_kt, score_tile, lowest)

    def bit_step(j, thr):
        cand = thr + lax.shift_left(jnp.int32(1), 31 - j)

        def count_tile(kt, acc):
            return acc + fold(jnp.where(skey_ref[kt] >= cand, 1.0, 0.0), jnp.add)

        acc = lax.fori_loop(0, n_kt, count_tile, jnp.zeros((SUBLANES, TQ), F32))
        return jnp.where(jnp.sum(acc, axis=0, keepdims=True) >= topk, cand, thr)

    thr = lax.fori_loop(0, 32, bit_step, jnp.full((1, TQ), INT_MIN, I32))
    thr = jnp.maximum(thr, INT_MIN + 1)

    def sweep(shifts):
        def acc_tile(kt, carry):
            ls, acc = carry
            sel = skey_ref[kt] >= thr
            ps = [jnp.where(sel, jnp.exp(lg_ref[kt, :, h * TQ:(h + 1) * TQ] - shifts[h]), 0.0)
                  for h in range(N_HEADS_G)]
            p_all = jnp.concatenate([p.astype(BF16) for p in ps], axis=1)
            vb = v_ref[0, pl.ds(pl.multiple_of(kt * TK, TK), TK), :]
            return (tuple(ls[h] + fold(ps[h], jnp.add) for h in range(N_HEADS_G)),
                    acc + lax.dot_general(vb, p_all, TN_DIMS, preferred_element_type=F32))

        ls, acc = lax.fori_loop(0, n_kt, acc_tile,
                                (tuple(jnp.zeros((SUBLANES, TQ), F32) for _ in range(N_HEADS_G)),
                                 jnp.zeros((HEAD_DIM, N_HEADS_G * TQ), F32)))
        ls = [jnp.sum(l, axis=0, keepdims=True) for l in ls]
        for h in range(N_HEADS_G):
            out_ref[h * HEAD_DIM:(h + 1) * HEAD_DIM, :] = head(acc, h) / ls[h]
        return jnp.min(jnp.minimum(jnp.minimum(ls[0], ls[1]), jnp.minimum(ls[2], ls[3])))

    smallest = sweep([jnp.max(t, axis=0, keepdims=True) for t in tops])

    @pl.when(smallest < SOFTMAX_FLOOR)
    def _():
        def max_tile(kt, ms):
            sel = skey_ref[kt] >= thr
            return tuple(jnp.maximum(ms[h], fold(jnp.where(sel, lg_ref[kt, :, h * TQ:(h + 1) * TQ], -1e30),
                                                 jnp.maximum))
                         for h in range(N_HEADS_G))

        ms = lax.fori_loop(0, n_kt, max_tile, lowest)
        sweep([jnp.max(m, axis=0, keepdims=True) for m in ms])

    o_ref[0] = _rms(out_ref[...].T, on_ref[...]).astype(BF16)


def _dsa_attention(qt, qit, wt, k, v, ki, on):
    B, S, _ = k.shape
    G = GROUP_W
    nq = S // Q_BLOCK
    TK = min(512, S)
    topk = min(DSA_TOPK_MAX, S // 4)
    qcol = lambda h: pl.BlockSpec((h, Q_BLOCK), lambda b, i: (0, b * nq + i))
    qslab = lambda h, d: pl.BlockSpec((h, d, Q_BLOCK), lambda b, i: (0, 0, b * nq + i))
    full = lambda wd: pl.BlockSpec((1, S, wd), lambda b, i: (b, 0, 0))
    return pl.pallas_call(
        functools.partial(_dsa_kernel, TK, topk),
        grid=(B, nq),
        in_specs=[qslab(N_HEADS_G, HEAD_DIM), qslab(IDX_HEADS, IDX_DIM), qcol(IDX_HEADS),
                  full(HEAD_DIM), full(HEAD_DIM), full(IDX_DIM), pl.BlockSpec((1, G), lambda b, i: (0, 0))],
        out_specs=pl.BlockSpec((1, Q_BLOCK, G), lambda b, i: (b, i, 0)),
        out_shape=jax.ShapeDtypeStruct((B, S, G), BF16),
        scratch_shapes=[pltpu.VMEM((S // TK, TK, Q_BLOCK), I32),
                        pltpu.VMEM((S // TK, TK, N_HEADS_G * Q_BLOCK), F32),
                        pltpu.VMEM((G, Q_BLOCK), F32)],
        compiler_params=_params(("parallel", "arbitrary")),
        name="dsa_attention",
    )(qt, qit, wt, k, v, ki, on)


def _sb_kernel(q_ref, k_ref, v_ref, u_ref, on_ref, o_ref, z_ref, lm_ref):
    i = pl.program_id(1)
    T = Q_BLOCK
    H, G = N_HEADS_G, GROUP_W
    u = u_ref[...]
    qt = q_ref[0].astype(F32).T.astype(BF16)
    q_rows = lax.broadcasted_iota(I32, (G, H * T), 0) // HEAD_DIM
    q_cols = lax.broadcasted_iota(I32, (G, H * T), 1) // T
    qbd = jnp.where(q_rows == q_cols, jnp.concatenate([qt] * H, axis=1), jnp.zeros((), BF16))
    kid = lax.broadcasted_iota(I32, (T, H * T), 0)
    qid = lax.broadcasted_iota(I32, (T, H * T), 1) & (T - 1)
    strict = kid < qid

    def cond(carry):
        kt, c, _ = carry
        return jnp.logical_and(kt >= 0, jnp.max(c) > SB_UNDERFLOW)

    def logits(kt):
        z = jnp.dot(k_ref[0, pl.ds(pl.multiple_of(kt * T, T), T), :], qbd, preferred_element_type=F32)
        z = z * HEAD_DIM ** -0.5
        return z, -_softplus(z)

    z0, lm0 = logits(i)
    z_ref[...] = z0
    lm_ref[...] = jnp.where(strict, lm0, 0.0)

    def body(carry):
        kt, c, acc = carry
        off = pl.multiple_of(kt * T, T)
        mask = jnp.logical_or(strict, kt < i)
        z, lm = z_ref[...], lm_ref[...]
        z_ref[...], lm_ref[...] = logits(jnp.maximum(kt - 1, 0))
        hi, lo = _split(lm)
        later = jnp.dot(u, jnp.concatenate([hi, lo], axis=1), preferred_element_type=F32)
        later = later[:, :H * T] + later[:, H * T:]
        a = jnp.where(mask, jnp.exp(z + lm + (c + later)), 0.0)
        av = lax.dot_general(v_ref[0, pl.ds(off, T), :], a.astype(BF16), TN_DIMS, preferred_element_type=F32)
        acc = acc + jnp.concatenate(
            [av[h * HEAD_DIM:(h + 1) * HEAD_DIM, h * T:(h + 1) * T] for h in range(H)], axis=0)
        return kt - 1, c + jnp.sum(lm, axis=0, keepdims=True), acc

    _, _, acc = lax.while_loop(cond, body, (i, jnp.zeros((1, H * T), F32), jnp.zeros((G, T), F32)))
    o_ref[0] = _rms(acc.T, on_ref[...]).astype(BF16)


def _stick_breaking(sb3, u, on):
    B, S, _ = sb3.shape
    G = GROUP_W
    return pl.pallas_call(
        _sb_kernel,
        grid=(B, S // Q_BLOCK),
        in_specs=[pl.BlockSpec((1, Q_BLOCK, G), lambda b, i: (b, i, 0)),
                  pl.BlockSpec((1, S, G), lambda b, i: (b, 0, 1)),
                  pl.BlockSpec((1, S, G), lambda b, i: (b, 0, 2)),
                  pl.BlockSpec((Q_BLOCK, Q_BLOCK), lambda b, i: (0, 0)),
                  pl.BlockSpec((1, G), lambda b, i: (0, 0))],
        out_specs=pl.BlockSpec((1, Q_BLOCK, G), lambda b, i: (b, i, 0)),
        out_shape=jax.ShapeDtypeStruct((B, S, G), BF16),
        scratch_shapes=[pltpu.VMEM((Q_BLOCK, N_HEADS_G * Q_BLOCK), F32)] * 2,
        compiler_params=_params(("parallel", "arbitrary")),
        name="stick_breaking",
    )(sb3, sb3, sb3, u, on)


def _outproj_kernel(yr_ref, yw_ref, yd_ref, ys_ref, x_ref, g1_ref, sc_ref, sh_ref, gn_ref, wo_ref, rw_ref, rb_ref,
                    x_o, h_o, lg_o):
    G = GROUP_W
    dot = lambda y_ref, g: jnp.dot(y_ref[...], wo_ref[g * G:(g + 1) * G, :], preferred_element_type=F32)
    mixed = dot(yr_ref, 0) + dot(yw_ref, 1) + dot(yd_ref, 2) + dot(ys_ref, 3)
    x1 = x_ref[...] + g1_ref[0] * mixed
    x_o[...] = x1
    h = _rms(x1, gn_ref[...]) * (1.0 + sc_ref[0]) + sh_ref[0]
    h_o[...] = h
    hi, lo = _split(h)
    work = (jnp.dot(hi, rw_ref[0], preferred_element_type=F32) + jnp.dot(lo, rw_ref[0], preferred_element_type=F32)
            + jnp.dot(hi, rw_ref[1], preferred_element_type=F32) + rb_ref[...])
    col = lax.broadcasted_iota(I32, work.shape, 1).astype(F32)
    out = jnp.zeros_like(work)
    vals = []
    for j in range(TOP_K):
        m = jnp.max(work, axis=-1, keepdims=True)
        idx = jnp.min(jnp.where(work == m, col, float(LANES)), axis=-1, keepdims=True)
        out = jnp.where(col == float(j), idx, out)
        vals.append(m)
        work = jnp.where(col == idx, -jnp.inf, work)
    es = [jnp.exp(v - vals[0]) for v in vals]
    den = es[0] + es[1] + es[2] + es[3]
    for j in range(TOP_K):
        out = jnp.where(col == float(TOP_K + j), es[j] / den, out)
    lg_o[...] = out


def _outproj(ys, x2, mod6, gain, w_out, router_w, router_b, S):
    N, D = x2.shape
    tm = ROW_TILE
    per_b = S // tm
    G = GROUP_W
    row = lambda w: pl.BlockSpec((tm, w), lambda i: (i, 0))
    c = lambda shape: pl.BlockSpec(shape, lambda i: (0, 0))
    return pl.pallas_call(
        _outproj_kernel,
        grid=(N // tm,),
        in_specs=[row(G)] * 4 + [row(D), _mod_row(per_b, 2), _mod_row(per_b, 4), _mod_row(per_b, 3),
                                 c((1, D)), c((D, D)), pl.BlockSpec((2, D, LANES), lambda i: (0, 0, 0)),
                                 c((1, LANES))],
        out_specs=[row(D), row(D), row(LANES)],
        out_shape=[jax.ShapeDtypeStruct((N, D), F32), jax.ShapeDtypeStruct((N, D), F32),
                   jax.ShapeDtypeStruct((N, LANES), F32)],
        compiler_params=_params(("parallel",)),
        name="out_proj_router",
    )(*ys, x2, mod6, mod6, mod6, gain, w_out,
      jnp.pad(jnp.stack(_split(router_w)), ((0, 0), (0, 0), (0, LANES - N_EXPERTS))),
      jnp.pad(router_b, ((0, 0), (0, LANES - N_EXPERTS)), constant_values=-1e30))


def _moe_kernel(be_ref, nu_ref, x_ref, w1_ref, b1_ref, w2_ref, b2_ref, rw_ref, o_ref):
    i = pl.program_id(0)

    @pl.when(i < nu_ref[0])
    def _():
        xb = x_ref[...].astype(BF16)
        w1 = w1_ref.at[0, 0]
        glu = jnp.dot(xb, w1[:, 0:D_FF].astype(BF16), preferred_element_type=F32) + b1_ref[0, :, 0:D_FF]
        lin = jnp.dot(xb, w1[:, D_FF:].astype(BF16), preferred_element_type=F32) + b1_ref[0, :, D_FF:]
        glu = jnp.minimum(glu, SWIGLU_LIMIT)
        lin = jnp.clip(lin, -SWIGLU_LIMIT, SWIGLU_LIMIT)
        act = glu * _sigmoid(SWIGLU_ALPHA * glu) * (lin + 1.0)
        y = jnp.dot(act.astype(BF16), w2_ref[0, 0].astype(BF16), preferred_element_type=F32) + b2_ref[0]
        o_ref[...] = (y * rw_ref[...]).astype(BF16)

    @pl.when(i >= nu_ref[0])
    def _():
        o_ref[...] = jnp.zeros_like(o_ref)


def _moe_ffn(block_expert, n_used, xs, layer, w1, b1, w2, b2, row_w):
    rows, D = xs.shape
    nb = rows // MOE_BLOCK
    L, E = w1.shape[:2]
    live = lambda i, nu: jnp.minimum(i, nu[0] - 1)
    grid_spec = pltpu.PrefetchScalarGridSpec(
        num_scalar_prefetch=2,
        grid=(nb,),
        in_specs=[pl.BlockSpec((MOE_BLOCK, D), lambda i, be, nu: (live(i, nu), 0)),
                  pl.BlockSpec((1, 1, D, 2 * D_FF), lambda i, be, nu: (layer, be[i], 0, 0)),
                  pl.BlockSpec((1, 1, 2 * D_FF), lambda i, be, nu: (layer * E + be[i], 0, 0)),
                  pl.BlockSpec((1, 1, D_FF, D), lambda i, be, nu: (layer, be[i], 0, 0)),
                  pl.BlockSpec((1, 1, D), lambda i, be, nu: (layer * E + be[i], 0, 0)),
                  pl.BlockSpec((MOE_BLOCK, 1), lambda i, be, nu: (live(i, nu), 0))],
        out_specs=pl.BlockSpec((MOE_BLOCK, D), lambda i, be, nu: (i, 0)),
    )
    return pl.pallas_call(
        _moe_kernel,
        grid_spec=grid_spec,
        out_shape=jax.ShapeDtypeStruct((rows, D), BF16),
        compiler_params=_params(("arbitrary",), 56 * 1024 * 1024),
        name="moe_ffn",
    )(block_expert, n_used, xs, w1, b1.reshape(L * E, 1, -1), w2, b2.reshape(L * E, 1, -1), row_w)


def _route(routed):
    n = routed.shape[0]
    gate = routed[:, TOP_K:2 * TOP_K].reshape(-1)
    expert = routed[:, 0:TOP_K].reshape(-1).astype(I32)
    n_assign = n * TOP_K
    ids = jnp.arange(n_assign, dtype=I32)
    _, order = lax.sort((expert, ids), num_keys=1, is_stable=True)
    _, inv = lax.sort((order, ids), num_keys=1)
    counts = jnp.sum((expert[:, None] == jnp.arange(N_EXPERTS, dtype=I32)[None, :]).astype(I32), axis=0)
    starts = jnp.cumsum(counts) - counts
    padded = (counts + MOE_BLOCK - 1) // MOE_BLOCK * MOE_BLOCK
    pad_end = jnp.cumsum(padded)
    pad_start = pad_end - padded
    dest = pad_start[expert] + inv - starts[expert]
    n_blocks = -(-n_assign // MOE_BLOCK) + N_EXPERTS
    block_start = jnp.arange(n_blocks, dtype=I32) * MOE_BLOCK
    block_expert = jnp.minimum(jnp.sum(pad_end[None, :] <= block_start[:, None], axis=1), N_EXPERTS - 1)
    block_expert = block_expert.astype(I32)
    per_row = lambda t: jnp.repeat(t[block_expert], MOE_BLOCK)
    off = jnp.arange(n_blocks * MOE_BLOCK, dtype=I32) - per_row(pad_start)
    valid = off < per_row(counts)
    src = order[jnp.clip(per_row(starts) + off, 0, n_assign - 1)]
    row_token = jnp.where(valid, src // TOP_K, 0)
    row_weight = jnp.where(valid, gate[src], 0.0)
    n_used = (pad_end[-1] // MOE_BLOCK).astype(I32).reshape(1)
    return dest.reshape(n, TOP_K), row_token, row_weight, block_expert, n_used


def _rows(t, idx):
    return t.at[idx].get(mode="promise_in_bounds")


def _final_kernel(x_ref, g2_ref, y0, y1, y2, y3, g_ref, o_ref):
    o_ref[...] = _rms(_combine(x_ref, g2_ref, (y0, y1, y2, y3)), g_ref[...])


def _final_norm(x2, mod6, ys, g, S):
    N, D = x2.shape
    tm = ROW_TILE
    row = pl.BlockSpec((tm, D), lambda i: (i, 0))
    return pl.pallas_call(
        _final_kernel, grid=(N // tm,),
        in_specs=[row, _mod_row(S // tm, 5)] + [row] * 4 + [pl.BlockSpec((1, D), lambda i: (0, 0))],
        out_specs=row,
        out_shape=jax.ShapeDtypeStruct((N, D), F32),
        compiler_params=_params(("parallel",)), name="final_norm",
    )(x2, mod6, *ys, g)


def _rope_tables(pos, group, rot_dim, theta, width):
    half = rot_dim // 2
    inv = theta ** (-jnp.arange(half, dtype=F32) / half)
    ang = pos.astype(F32)[..., None] * inv
    cos, sin = jnp.cos(ang), jnp.sin(ang)
    rest = group - rot_dim
    ones = jnp.ones(ang.shape[:2] + (rest,), F32)
    zeros = jnp.zeros(ang.shape[:2] + (rest,), F32)
    zh = jnp.zeros_like(sin)
    rep = width // group
    tile = lambda t: jnp.tile(t, (1, 1, rep)).reshape(-1, width)
    return (tile(jnp.concatenate([cos, cos, ones], -1)),
            tile(jnp.concatenate([-sin, zh, zeros], -1)),
            tile(jnp.concatenate([zh, sin, zeros], -1)))


def _pad_rows(w, start, total=128):
    return jnp.zeros((total, w.shape[1]), w.dtype).at[start:start + w.shape[0]].set(w)


def kernel(x, c, positions, ada_w, ada_b, norm_mix, norm_ffn, w_in, ret_gn, rwkv_mu, rwkv_w0, rwkv_w2, rwkv_a0,
           rwkv_a2, rwkv_g2, rwkv_kk, rwkv_ka, rwkv_rk, rwkv_ln, dsa_qnorm, dsa_wq_up, dsa_wqi_up, dsa_onorm,
           sb_onorm, w_out, router_w, router_b, moe_w1, moe_b1, moe_w2, moe_b2, norm_final):
    B, S, D = x.shape
    N = B * S
    L = ada_w.shape[0]
    G = GROUP_W
    row2 = lambda t: t.reshape(1, -1)

    mod = _ada(c, ada_w, ada_b)
    ret_tabs = tuple(t.reshape(B, S, G) for t in _rope_tables(positions, HEAD_DIM, HEAD_DIM, RET_THETA, G))
    dq_tabs = _rope_tables(positions, HEAD_DIM, HEAD_DIM // 4, ROPE_THETA, G)
    di_tabs = _rope_tables(positions, IDX_DIM, IDX_DIM // 4, ROPE_THETA, G)
    ret_c = _ret_consts(min(256, S))
    hm = (jnp.arange(G)[:, None] // HEAD_DIM == jnp.arange(G)[None, :] // HEAD_DIM).astype(BF16)
    u_later = (jnp.arange(Q_BLOCK)[None, :] > jnp.arange(Q_BLOCK)[:, None]).astype(BF16)
    o1, o2 = RET_COLS + RWKV_COLS, RET_COLS + RWKV_COLS + DSA_COLS

    x2 = x.reshape(N, D)
    comb = None
    for l in range(L):
        mod6 = mod[l].reshape(B * 6, 1, D)
        w_l = w_in[l]
        w_cat = jnp.concatenate(
            [w_l[:, :o1], w_l[:, o1:o2], jnp.zeros((D, DSA_PAD - DSA_COLS), F32), w_l[:, o2:]], axis=1).astype(BF16)
        outs = _inproj(x2, row2(norm_mix[l]), mod6, w_cat, S, comb)
        if comb is not None:
            x2, outs = outs[0], outs[1:]
        ret, rwkv, dsa, sb = outs

        prep = _rwkv_prep(rwkv, S, row2(rwkv_mu[l]), row2(rwkv_w0[l]), row2(rwkv_a0[l]),
                          _pad_rows(rwkv_w2[l], 0).astype(BF16), _pad_rows(rwkv_a2[l], 32).astype(BF16),
                          _pad_rows(rwkv_g2[l], 64).astype(BF16), row2(rwkv_kk[l]), row2(rwkv_ka[l]),
                          row2(rwkv_rk[l]), hm)
        r_, w_, k_, v_, nkk_, b_, gate_, bonus_ = prep
        scan_in = tuple(_to_v_layout(t, B, S) for t in (nkk_, w_, b_, k_, r_, v_))

        y_ret = _retention(ret.reshape(B, S, RET_COLS), ret_tabs, ret_c, row2(ret_gn[l])).reshape(N, G)

        dqt, dqit, dwt, dk, dv, dki = _dsa_prep(dsa, row2(dsa_qnorm[l]), dsa_wq_up[l].astype(BF16),
                                                dsa_wqi_up[l].astype(BF16), dq_tabs, di_tabs)
        y_dsa = _dsa_attention(dqt, dqit, dwt, dk.reshape(B, S, HEAD_DIM), dv.reshape(B, S, HEAD_DIM),
                               dki.reshape(B, S, IDX_DIM), row2(dsa_onorm[l])).reshape(N, G)

        y_sb = _stick_breaking(sb.reshape(B, S, SB_COLS), u_later, row2(sb_onorm[l])).reshape(N, G)

        y_scan = _rwkv_scan(*scan_in)
        y_rwkv = _rwkv_post(_from_v_layout(y_scan, B, S), bonus_, gate_, row2(rwkv_ln[l]), hm)

        x1, h2, routed = _outproj((y_ret, y_rwkv, y_dsa, y_sb), x2, mod6, row2(norm_ffn[l]),
                                  w_out[l].astype(BF16), router_w[l], row2(router_b[l]), S)

        dest, row_token, row_weight, block_expert, n_used = _route(routed)
        y = _moe_ffn(block_expert, n_used, _rows(h2, row_token), l, moe_w1, moe_b1, moe_w2, moe_b2,
                     row_weight[:, None])
        x2 = x1
        comb = (mod6, tuple(_rows(y, dest[:, j]) for j in range(TOP_K)))
    return _final_norm(x2, comb[0], comb[1], row2(norm_final), S).reshape(B, S, D)
```

```python
import functools

import jax
import jax.numpy as jnp
from jax import lax
from jax.experimental import pallas as pl
from jax.experimental.pallas import tpu as pltpu

F32 = jnp.float32
BF16 = jnp.bfloat16
I32 = jnp.int32

D_MODEL = 1024
GROUP_W = 256
HEAD_DIM = 64
N_HEADS_G = 4
NORM_EPS = 1e-5
Q_BLOCK = 128
RET_THETA = 10000.0
RWKV_LN_EPS = 64e-5
DSA_Q_LORA = 128
IDX_HEADS = 8
IDX_DIM = 32
DSA_TOPK_MAX = 256
ROPE_THETA = 500000.0
N_EXPERTS = 32
TOP_K = 4
D_FF = D_MODEL
SWIGLU_ALPHA = 1.702
SWIGLU_LIMIT = 7.0
MOE_BLOCK = 512

RET_COLS = 4 * GROUP_W
RWKV_COLS = 3 * GROUP_W + 128
DSA_COLS = 296
DSA_PAD = 384
SB_COLS = 3 * GROUP_W

LANES = 128
SUBLANES = 8
ROW_TILE = 512
VMEM_LIMIT = 48 * 1024 * 1024
INT_MIN = -2 ** 31
SB_UNDERFLOW = -104.0
SOFTMAX_FLOOR = 1e-20

HIGHEST = lax.Precision.HIGHEST
NT_DIMS = (((1,), (1,)), ((), ()))
TN_DIMS = (((0,), (0,)), ((), ()))


def _params(sem, vmem=VMEM_LIMIT):
    return pltpu.CompilerParams(dimension_semantics=sem, vmem_limit_bytes=vmem)


def _bdot(a, b):
    return jnp.dot(a.astype(BF16), b.astype(BF16), preferred_element_type=F32)


def _split(x):
    hi = x.astype(BF16)
    return hi, (x - hi.astype(F32)).astype(BF16)


def _split_dot(x, m):
    hi, lo = _split(x)
    return jnp.dot(hi, m, preferred_element_type=F32) + jnp.dot(lo, m, preferred_element_type=F32)


def _tree(parts, op):
    while len(parts) > 1:
        parts = [op(parts[i], parts[i + 1]) for i in range(0, len(parts), 2)]
    return parts[0]


def _sigmoid(x):
    return 1.0 / (1.0 + jnp.exp(-x))


def _softplus(x):
    return jnp.maximum(x, 0.0) + jnp.log(1.0 + jnp.exp(-jnp.abs(x)))


def _rms(x, g, eps=NORM_EPS):
    return x * lax.rsqrt(jnp.mean(x * x, axis=-1, keepdims=True) + eps) * g


def _rope(x, cos, sin_a, sin_b, half):
    w = x.shape[-1]
    return x * cos + pltpu.roll(x, w - half, 1) * sin_a + pltpu.roll(x, half, 1) * sin_b


def _ada_kernel(c_ref, w_ref, b_ref, o_ref):
    c = c_ref[...]
    cond = c * _sigmoid(c)
    o_ref[0] = jnp.dot(cond, w_ref[0], preferred_element_type=F32, precision=HIGHEST) + b_ref[0]


def _ada(c, ada_w, ada_b):
    L, D, W = ada_w.shape
    B = c.shape[0]
    tn = 1024
    return pl.pallas_call(
        _ada_kernel,
        grid=(L, W // tn),
        in_specs=[pl.BlockSpec((B, D), lambda l, j: (0, 0)),
                  pl.BlockSpec((1, D, tn), lambda l, j: (l, 0, j)),
                  pl.BlockSpec((1, 1, tn), lambda l, j: (l, 0, j))],
        out_specs=pl.BlockSpec((1, B, tn), lambda l, j: (l, 0, j)),
        out_shape=jax.ShapeDtypeStruct((L, B, W), F32),
        compiler_params=_params(("parallel", "parallel")),
        name="ada_mod",
    )(c, ada_w, ada_b.reshape(L, 1, W))


def _combine(x_ref, g2_ref, routed_ref, y_refs):
    gates = routed_ref[...]
    moe = gates[:, TOP_K:TOP_K + 1] * y_refs[0][...].astype(F32)
    for j in range(1, len(y_refs)):
        moe = moe + gates[:, TOP_K + j:TOP_K + j + 1] * y_refs[j][...].astype(F32)
    return x_ref[...] + g2_ref[0] * moe


def _inproj_kernel(n_comb, x_ref, *refs):
    if n_comb:
        g2_ref, routed_ref, y_refs, refs = refs[0], refs[1], refs[2:2 + n_comb], refs[2 + n_comb:]
        g_ref, sc_ref, sh_ref, w_ref, x_o, ret_ref, rwkv_ref, dsa_ref, sb_ref = refs
        x = _combine(x_ref, g2_ref, routed_ref, y_refs)
        x_o[...] = x
    else:
        g_ref, sc_ref, sh_ref, w_ref, ret_ref, rwkv_ref, dsa_ref, sb_ref = refs
        x = x_ref[...]
    h = _rms(x, g_ref[...]) * (1.0 + sc_ref[0]) + sh_ref[0]
    hb = h.astype(BF16)
    o0, o1, o2 = RET_COLS, RET_COLS + RWKV_COLS, RET_COLS + RWKV_COLS + DSA_PAD
    ret_ref[...] = jnp.dot(hb, w_ref[:, 0:o0], preferred_element_type=F32)
    rwkv_ref[...] = jnp.dot(hb, w_ref[:, o0:o1], preferred_element_type=F32)
    dsa_ref[...] = jnp.dot(hb, w_ref[:, o1:o2], preferred_element_type=F32)
    sb_ref[...] = jnp.dot(hb, w_ref[:, o2:], preferred_element_type=F32).astype(BF16)


def _mod_row(per_b, j):
    return pl.BlockSpec((1, 1, D_MODEL), lambda i: ((i // per_b) * 6 + j, 0, 0))


def _inproj(x2, gain, mod6, w_cat, S, comb=None):
    N, D = x2.shape
    tm = ROW_TILE
    per_b = S // tm
    wt = w_cat.shape[1]
    row = lambda i: (i, 0)
    in_specs = [pl.BlockSpec((tm, D), row)]
    args = [x2]
    out_specs, out_shape = [], []
    if comb is not None:
        in_specs += ([_mod_row(per_b, 5), pl.BlockSpec((tm, LANES), row)]
                     + [pl.BlockSpec((tm, D), row)] * len(comb[2]))
        args += [comb[0], comb[1]] + list(comb[2])
        out_specs.append(pl.BlockSpec((tm, D), row))
        out_shape.append(jax.ShapeDtypeStruct((N, D), F32))
    in_specs += [pl.BlockSpec((1, D), lambda i: (0, 0)), _mod_row(per_b, 1), _mod_row(per_b, 0),
                 pl.BlockSpec((D, wt), lambda i: (0, 0))]
    args += [gain, mod6, mod6, w_cat]
    out_specs += [pl.BlockSpec((tm, RET_COLS), row), pl.BlockSpec((tm, RWKV_COLS), row),
                  pl.BlockSpec((tm, DSA_PAD), row), pl.BlockSpec((tm, SB_COLS), row)]
    out_shape += [jax.ShapeDtypeStruct((N, RET_COLS), F32), jax.ShapeDtypeStruct((N, RWKV_COLS), F32),
                  jax.ShapeDtypeStruct((N, DSA_PAD), F32), jax.ShapeDtypeStruct((N, SB_COLS), BF16)]
    return pl.pallas_call(
        functools.partial(_inproj_kernel, 0 if comb is None else len(comb[2])),
        grid=(N // tm,),
        in_specs=in_specs, out_specs=out_specs, out_shape=out_shape,
        compiler_params=_params(("parallel",)),
        name="in_proj",
    )(*args)


def _ret_kernel(q_ref, k_ref, v_ref, g_ref, cos_ref, sa_ref, sb_ref, din_ref, qd_ref, kd_ref, cd_ref, gn_ref,
                o_ref, state_ref, y_ref):
    @pl.when(pl.program_id(1) == 0)
    def _():
        state_ref[...] = jnp.zeros_like(state_ref)

    cos, sin_a, sin_b = cos_ref[0], sa_ref[0], sb_ref[0]
    q = _rope(q_ref[0], cos, sin_a, sin_b, HEAD_DIM // 2)
    k = _rope(k_ref[0], cos, sin_a, sin_b, HEAD_DIM // 2) * HEAD_DIM ** -0.5
    v = v_ref[0]
    qd = q * qd_ref[...]
    kd = k * kd_ref[...]
    for h in range(N_HEADS_G):
        sl = slice(h * HEAD_DIM, (h + 1) * HEAD_DIM)
        qh, kh, vh = q[:, sl].astype(BF16), k[:, sl].astype(BF16), v[:, sl].astype(BF16)
        s = lax.dot_general(qh, kh, NT_DIMS, preferred_element_type=F32) * din_ref[h]
        inner = jnp.dot(s.astype(BF16), vh, preferred_element_type=F32)
        st = state_ref[h]
        cross = _bdot(qd[:, sl], st)
        state_ref[h] = st * cd_ref[:, sl] + lax.dot_general(
            kd[:, sl].astype(BF16), vh, TN_DIMS, preferred_element_type=F32)
        o = inner + cross
        oc = o - jnp.mean(o, axis=-1, keepdims=True)
        y_ref[:, sl] = oc * lax.rsqrt(jnp.mean(oc * oc, axis=-1, keepdims=True) + NORM_EPS)
    g = g_ref[0]
    o_ref[0] = (g * _sigmoid(g) * (y_ref[...] * gn_ref[...])).astype(BF16)


def _retention(ret3, tabs, consts, gn):
    B, S, _ = ret3.shape
    C = consts["din"].shape[1]
    blk = lambda j: pl.BlockSpec((1, C, GROUP_W), lambda b, c, j=j: (b, c, j))
    tab = pl.BlockSpec((1, C, GROUP_W), lambda b, c: (b, c, 0))
    const2 = lambda shape: pl.BlockSpec(shape, lambda b, c: (0,) * len(shape))
    return pl.pallas_call(
        _ret_kernel,
        grid=(B, S // C),
        in_specs=[blk(0), blk(1), blk(2), blk(3), tab, tab, tab,
                  const2((N_HEADS_G, C, C)), const2((C, GROUP_W)), const2((C, GROUP_W)),
                  const2((1, GROUP_W)), const2((1, GROUP_W))],
        out_specs=pl.BlockSpec((1, C, GROUP_W), lambda b, c: (b, c, 0)),
        out_shape=jax.ShapeDtypeStruct((B, S, GROUP_W), BF16),
        scratch_shapes=[pltpu.VMEM((N_HEADS_G, HEAD_DIM, HEAD_DIM), F32), pltpu.VMEM((C, GROUP_W), F32)],
        compiler_params=_params(("parallel", "arbitrary")),
        name="retention",
    )(ret3, ret3, ret3, ret3, tabs[0], tabs[1], tabs[2],
      consts["din"], consts["qd"], consts["kd"], consts["cd"], gn)


def _ret_consts(C):
    H = N_HEADS_G
    lg = jnp.log(1.0 - 2.0 ** (-5.0 - jnp.arange(H, dtype=F32)))
    idx = jnp.arange(C, dtype=F32)
    diff = idx[:, None] - idx[None, :]
    din = jnp.where(diff >= 0, jnp.exp(lg[:, None, None] * jnp.maximum(diff, 0.0)), 0.0)
    q_dec = jnp.exp(lg[:, None] * (idx + 1.0))
    k_dec = jnp.exp(lg[:, None] * (C - 1.0 - idx))
    chunk_dec = jnp.exp(lg * C)
    wide = lambda t: jnp.repeat(t.T, HEAD_DIM, axis=1)
    return {"din": din, "qd": wide(q_dec), "kd": wide(k_dec),
            "cd": jnp.repeat(chunk_dec, HEAD_DIM)[None, :]}


def _rwkv_prep_kernel(per_b, z_ref, zp_ref, mu_ref, w0_ref, a0_ref, w2_ref, a2_ref, g2_ref, kk_ref, ka_ref,
                      rk_ref, hm_ref, r_o, w_o, k_o, v_o, nkk_o, b_o, gate_o, bonus_o):
    z = z_ref[...]
    first = (pl.program_id(0) % per_b) == 0
    prow = jnp.where(first, 0.0, zp_ref[7:8, :])
    rid = lax.broadcasted_iota(I32, z.shape, 0)
    prev = jnp.where(rid == 0, prow, pltpu.roll(z, 1, 0))
    f = z + (prev - z) * mu_ref[...]
    G = GROUP_W
    r, k, v, lo = f[:, 0:G], f[:, G:2 * G], f[:, 2 * G:3 * G], f[:, 3 * G:3 * G + 128]
    w_log = -_softplus(-(w0_ref[...] + _bdot(jnp.tanh(lo), w2_ref[...]))) - 0.5
    decay = jnp.exp(-jnp.exp(w_log))
    a = _sigmoid(a0_ref[...] + _bdot(lo, a2_ref[...]))
    gate = _bdot(_sigmoid(lo), g2_ref[...])
    hm = hm_ref[...]
    kk = k * kk_ref[...]
    kk = kk / jnp.maximum(jnp.sqrt(_split_dot(kk * kk, hm)), 1e-12)
    k2 = k * (1.0 + (a - 1.0) * ka_ref[...])
    r_o[...] = r
    w_o[...] = decay
    k_o[...] = k2
    v_o[...] = v
    nkk_o[...] = -kk
    b_o[...] = kk * a
    gate_o[...] = gate
    bonus_o[...] = _split_dot(r * k2 * rk_ref[...], hm) * v


def _rwkv_prep(z2, S, mu, w0, a0, w2p, a2p, g2p, kk, ka, rk, hm):
    N, W = z2.shape
    tm = ROW_TILE
    per_b = S // tm
    row = lambda i: (i, 0)
    c = lambda shape: pl.BlockSpec(shape, lambda i: (0, 0))
    G = GROUP_W
    return pl.pallas_call(
        functools.partial(_rwkv_prep_kernel, per_b),
        grid=(N // tm,),
        in_specs=[pl.BlockSpec((tm, W), row),
                  pl.BlockSpec((SUBLANES, W), lambda i: (jnp.maximum(i * (tm // SUBLANES) - 1, 0), 0)),
                  c((1, W)), c((1, G)), c((1, G)), c((128, G)), c((128, G)), c((128, G)),
                  c((1, G)), c((1, G)), c((1, G)), c((G, G))],
        out_specs=[pl.BlockSpec((tm, G), row)] * 8,
        out_shape=[jax.ShapeDtypeStruct((N, G), F32)] * 8,
        compiler_params=_params(("parallel",)),
        name="rwkv_prep",
    )(z2, z2, mu, w0, a0, w2p, a2p, g2p, kk, ka, rk, hm)


def _rwkv_scan_kernel(T, nkk_ref, w_ref, b_ref, k_ref, r_ref, v_ref, y_ref, s_ref, rep_ref):
    @pl.when(pl.program_id(0) == 0)
    def _():
        s_ref[...] = jnp.zeros_like(s_ref)

    ch = v_ref.shape[1]
    groups = HEAD_DIM // ch
    width = LANES // groups
    tree = lambda parts: _tree(parts, jnp.add)

    grp = lax.broadcasted_iota(I32, (T * ch, LANES), 1) // width
    for o, ref in enumerate((nkk_ref, w_ref, b_ref, k_ref, r_ref)):
        x = ref[...].reshape(T * ch, LANES)
        turned = [x] + [pltpu.roll(x, m * width, 1) for m in range(1, groups)]
        for j in range(groups):
            y = turned[0]
            for m in range(1, groups):
                y = jnp.where(grp == (j + m) % groups, turned[m], y)
            rep_ref[o, :, j] = y.reshape(T, ch, LANES)

    keys = [(j, kh) for j in range(groups) for kh in range(ch)]
    row = lambda o, t, j, kh: rep_ref[o, t, j, kh:kh + 1, :]

    def step(t, carry):
        vt = v_ref[t]
        sa = tree([s_ref[j * ch + kh] * row(0, t, j, kh) for j, kh in keys])
        ys = []
        for j, kh in keys:
            s_new = (s_ref[j * ch + kh] * row(1, t, j, kh) + sa * row(2, t, j, kh) + vt * row(3, t, j, kh))
            s_ref[j * ch + kh] = s_new
            ys.append(s_new * row(4, t, j, kh))
        y_ref[t] = tree(ys)
        return carry

    lax.fori_loop(0, T, step, 0)


def _rwkv_scan(nkk, w, b, k, r, v):
    S, ch, _ = v.shape
    T = 32
    spec = pl.BlockSpec((T, ch, LANES), lambda i: (i, 0, 0))
    return pl.pallas_call(
        functools.partial(_rwkv_scan_kernel, T),
        grid=(S // T,),
        in_specs=[spec] * 6,
        out_specs=spec,
        out_shape=jax.ShapeDtypeStruct((S, ch, LANES), F32),
        scratch_shapes=[pltpu.VMEM((HEAD_DIM, ch, LANES), F32),
                        pltpu.VMEM((5, T, HEAD_DIM // ch, ch, LANES), F32)],
        compiler_params=_params(("arbitrary",)),
        name="rwkv_scan",
    )(nkk, w, b, k, r, v)


def _to_v_layout(x, B, S):
    P = B * N_HEADS_G
    rep = LANES // P
    t = x.reshape(B, S, N_HEADS_G, HEAD_DIM // rep, rep).transpose(1, 3, 4, 0, 2)
    return t.reshape(S, HEAD_DIM // rep, LANES)


def _from_v_layout(y, B, S):
    P = B * N_HEADS_G
    rep = LANES // P
    t = y.reshape(S, HEAD_DIM // rep, rep, B, N_HEADS_G).transpose(3, 0, 4, 1, 2)
    return t.reshape(B * S, GROUP_W)


def _dsa_prep_kernel(f_ref, qn_ref, wq_ref, wqi_ref, cq_ref, sqa_ref, sqb_ref, ci_ref, sia_ref, sib_ref,
                     qt_o, qit_o, wt_o, k_o, v_o, ki_o):
    f = f_ref[...]
    cq = _rms(f[:, 0:DSA_Q_LORA], qn_ref[...]).astype(BF16)
    cos_q, sqa, sqb = cq_ref[...], sqa_ref[...], sqb_ref[...]
    cos_i, sia, sib = ci_ref[...], sia_ref[...], sib_ref[...]
    q = _rope(jnp.dot(cq, wq_ref[...], preferred_element_type=F32), cos_q, sqa, sqb, HEAD_DIM // 8)
    tm = f.shape[0]
    qt_o[...] = (q * HEAD_DIM ** -0.5).T.reshape(N_HEADS_G, HEAD_DIM, tm).astype(BF16)
    qi = _rope(jnp.dot(cq, wqi_ref[...], preferred_element_type=F32), cos_i, sia, sib, IDX_DIM // 8)
    qit_o[...] = qi.T.reshape(IDX_HEADS, IDX_DIM, tm).astype(BF16)
    kv = f[:, 128:256]
    kv_r = _rope(kv, cos_q[:, 0:128], sqa[:, 0:128], sqb[:, 0:128], HEAD_DIM // 8)
    k_o[...] = kv_r[:, 0:HEAD_DIM].astype(BF16)
    v_o[...] = kv[:, HEAD_DIM:128].astype(BF16)
    tail = f[:, 256:384]
    tail_r = _rope(tail, cos_i[:, 0:128], sia[:, 0:128], sib[:, 0:128], IDX_DIM // 8)
    ki_o[...] = tail_r[:, 0:IDX_DIM].astype(BF16)
    wt_o[...] = tail.T[IDX_DIM:IDX_DIM + IDX_HEADS, :] * (IDX_HEADS ** -0.5 * IDX_DIM ** -0.5)


def _dsa_prep(f2, qn, wq, wqi, tq, ti):
    N, W = f2.shape
    tm = ROW_TILE
    G = GROUP_W
    row = lambda w: pl.BlockSpec((tm, w), lambda i: (i, 0))
    col = lambda h: pl.BlockSpec((h, tm), lambda i: (0, i))
    slab = lambda h, d: pl.BlockSpec((h, d, tm), lambda i: (0, 0, i))
    c = lambda shape: pl.BlockSpec(shape, lambda i: (0, 0))
    return pl.pallas_call(
        _dsa_prep_kernel,
        grid=(N // tm,),
        in_specs=[row(W), c((1, DSA_Q_LORA)), c((DSA_Q_LORA, G)), c((DSA_Q_LORA, G))] + [row(G)] * 6,
        out_specs=[slab(N_HEADS_G, HEAD_DIM), slab(IDX_HEADS, IDX_DIM), col(IDX_HEADS),
                   row(HEAD_DIM), row(HEAD_DIM), row(IDX_DIM)],
        out_shape=[jax.ShapeDtypeStruct((N_HEADS_G, HEAD_DIM, N), BF16),
                   jax.ShapeDtypeStruct((IDX_HEADS, IDX_DIM, N), BF16),
                   jax.ShapeDtypeStruct((IDX_HEADS, N), F32),
                   jax.ShapeDtypeStruct((N, HEAD_DIM), BF16), jax.ShapeDtypeStruct((N, HEAD_DIM), BF16),
                   jax.ShapeDtypeStruct((N, IDX_DIM), BF16)],
        compiler_params=_params(("parallel",)),
        name="dsa_prep",
    )(f2, qn, wq, wqi, *tq, *ti)


def _float_key(bits):
    return bits ^ ((bits >> 31) & 0x7FFFFFFF)


def _dsa_kernel(TK, topk, qt_ref, qit_ref, wt_ref, k_ref, v_ref, ki_ref, on_ref, o_ref,
                skey_ref, lg_ref, out_ref):
    i = pl.program_id(1)
    TQ = Q_BLOCK
    n_kt = (i * TQ + TQ + TK - 1) // TK
    q_all = jnp.concatenate([qt_ref[h] for h in range(N_HEADS_G)], axis=1)
    qi_all = jnp.concatenate([qit_ref[h] for h in range(IDX_HEADS)], axis=1)
    wt = wt_ref[...]
    kpos = lax.broadcasted_iota(I32, (TK, TQ), 0)
    qpos = i * TQ + lax.broadcasted_iota(I32, (TK, TQ), 1)
    head = lambda x, h: x[:, h * TQ:(h + 1) * TQ]

    def fold(x, op):
        return _tree([x[r * SUBLANES:(r + 1) * SUBLANES] for r in range(TK // SUBLANES)], op)

    def score_tile(kt, tops):
        off = pl.multiple_of(kt * TK, TK)
        rel = jnp.dot(ki_ref[0, pl.ds(off, TK), :], qi_all, preferred_element_type=F32)
        sc = jnp.zeros((TK, TQ), F32)
        for h in range(IDX_HEADS):
            sc = sc + jnp.maximum(head(rel, h), 0.0) * wt[h:h + 1, :]
        key = _float_key(pltpu.bitcast(sc, I32))
        valid = kt * TK + kpos <= qpos
        skey_ref[kt] = jnp.where(valid, key, INT_MIN)
        lg = jnp.dot(k_ref[0, pl.ds(off, TK), :], q_all, preferred_element_type=F32)
        lg_ref[kt] = lg
        return tuple(jnp.maximum(tops[h], fold(jnp.where(valid, head(lg, h), -1e30), jnp.maximum))
                     for h in range(N_HEADS_G))

    lowest = tuple(jnp.full((SUBLANES, TQ), -1e30, F32) for _ in range(N_HEADS_G))
    tops = lax.fori_loop(0, n_kt, score_tile, lowest)

    def bit_step(j, thr):
        cand = thr + lax.shift_left(jnp.int32(1), 31 - j)

        def count_tile(kt, acc):
            return acc + fold(jnp.where(skey_ref[kt] >= cand, 1.0, 0.0), jnp.add)

        acc = lax.fori_loop(0, n_kt, count_tile, jnp.zeros((SUBLANES, TQ), F32))
        return jnp.where(jnp.sum(acc, axis=0, keepdims=True) >= topk, cand, thr)

    thr = lax.fori_loop(0, 32, bit_step, jnp.full((1, TQ), INT_MIN, I32))
    thr = jnp.maximum(thr, INT_MIN + 1)

    def sweep(shifts):
        def acc_tile(kt, carry):
            ls, acc = carry
            sel = skey_ref[kt] >= thr
            ps = [jnp.where(sel, jnp.exp(lg_ref[kt, :, h * TQ:(h + 1) * TQ] - shifts[h]), 0.0)
                  for h in range(N_HEADS_G)]
            p_all = jnp.concatenate([p.astype(BF16) for p in ps], axis=1)
            vb = v_ref[0, pl.ds(pl.multiple_of(kt * TK, TK), TK), :]
            return (tuple(ls[h] + fold(ps[h], jnp.add) for h in range(N_HEADS_G)),
                    acc + lax.dot_general(vb, p_all, TN_DIMS, preferred_element_type=F32))

        ls, acc = lax.fori_loop(0, n_kt, acc_tile,
                                (tuple(jnp.zeros((SUBLANES, TQ), F32) for _ in range(N_HEADS_G)),
                                 jnp.zeros((HEAD_DIM, N_HEADS_G * TQ), F32)))
        ls = [jnp.sum(l, axis=0, keepdims=True) for l in ls]
        for h in range(N_HEADS_G):
            out_ref[h * HEAD_DIM:(h + 1) * HEAD_DIM, :] = head(acc, h) / ls[h]
        return jnp.min(jnp.minimum(jnp.minimum(ls[0], ls[1]), jnp.minimum(ls[2], ls[3])))

    smallest = sweep([jnp.max(t, axis=0, keepdims=True) for t in tops])

    @pl.when(smallest < SOFTMAX_FLOOR)
    def _():
        def max_tile(kt, ms):
            sel = skey_ref[kt] >= thr
            return tuple(jnp.maximum(ms[h], fold(jnp.where(sel, lg_ref[kt, :, h * TQ:(h + 1) * TQ], -1e30),
                                                 jnp.maximum))
                         for h in range(N_HEADS_G))

        ms = lax.fori_loop(0, n_kt, max_tile, lowest)
        sweep([jnp.max(m, axis=0, keepdims=True) for m in ms])

    o_ref[0] = _rms(out_ref[...].T, on_ref[...]).astype(BF16)


def _dsa_attention(qt, qit, wt, k, v, ki, on):
    B, S, _ = k.shape
    G = GROUP_W
    nq = S // Q_BLOCK
    TK = min(512, S)
    topk = min(DSA_TOPK_MAX, S // 4)
    qcol = lambda h: pl.BlockSpec((h, Q_BLOCK), lambda b, i: (0, b * nq + i))
    qslab = lambda h, d: pl.BlockSpec((h, d, Q_BLOCK), lambda b, i: (0, 0, b * nq + i))
    full = lambda wd: pl.BlockSpec((1, S, wd), lambda b, i: (b, 0, 0))
    return pl.pallas_call(
        functools.partial(_dsa_kernel, TK, topk),
        grid=(B, nq),
        in_specs=[qslab(N_HEADS_G, HEAD_DIM), qslab(IDX_HEADS, IDX_DIM), qcol(IDX_HEADS),
                  full(HEAD_DIM), full(HEAD_DIM), full(IDX_DIM), pl.BlockSpec((1, G), lambda b, i: (0, 0))],
        out_specs=pl.BlockSpec((1, Q_BLOCK, G), lambda b, i: (b, i, 0)),
        out_shape=jax.ShapeDtypeStruct((B, S, G), BF16),
        scratch_shapes=[pltpu.VMEM((S // TK, TK, Q_BLOCK), I32),
                        pltpu.VMEM((S // TK, TK, N_HEADS_G * Q_BLOCK), F32),
                        pltpu.VMEM((G, Q_BLOCK), F32)],
        compiler_params=_params(("parallel", "arbitrary")),
        name="dsa_attention",
    )(qt, qit, wt, k, v, ki, on)


def _sb_kernel(q_ref, k_ref, v_ref, u_ref, on_ref, o_ref, z_ref, lm_ref):
    i = pl.program_id(1)
    T = Q_BLOCK
    H, G = N_HEADS_G, GROUP_W
    u = u_ref[...]
    qt = q_ref[0].astype(F32).T.astype(BF16)
    q_rows = lax.broadcasted_iota(I32, (G, H * T), 0) // HEAD_DIM
    q_cols = lax.broadcasted_iota(I32, (G, H * T), 1) // T
    qbd = jnp.where(q_rows == q_cols, jnp.concatenate([qt] * H, axis=1), jnp.zeros((), BF16))
    kid = lax.broadcasted_iota(I32, (T, H * T), 0)
    qid = lax.broadcasted_iota(I32, (T, H * T), 1) & (T - 1)
    strict = kid < qid

    def cond(carry):
        kt, c, _ = carry
        return jnp.logical_and(kt >= 0, jnp.max(c) > SB_UNDERFLOW)

    def logits(kt):
        z = jnp.dot(k_ref[0, pl.ds(pl.multiple_of(kt * T, T), T), :], qbd, preferred_element_type=F32)
        z = z * HEAD_DIM ** -0.5
        return z, -_softplus(z)

    z0, lm0 = logits(i)
    z_ref[...] = z0
    lm_ref[...] = jnp.where(strict, lm0, 0.0)

    def body(carry):
        kt, c, acc = carry
        off = pl.multiple_of(kt * T, T)
        mask = jnp.logical_or(strict, kt < i)
        z, lm = z_ref[...], lm_ref[...]
        z_ref[...], lm_ref[...] = logits(jnp.maximum(kt - 1, 0))
        hi, lo = _split(lm)
        later = jnp.dot(u, jnp.concatenate([hi, lo], axis=1), preferred_element_type=F32)
        later = later[:, :H * T] + later[:, H * T:]
        a = jnp.where(mask, jnp.exp(z + lm + (c + later)), 0.0)
        av = lax.dot_general(v_ref[0, pl.ds(off, T), :], a.astype(BF16), TN_DIMS, preferred_element_type=F32)
        acc = acc + jnp.concatenate(
            [av[h * HEAD_DIM:(h + 1) * HEAD_DIM, h * T:(h + 1) * T] for h in range(H)], axis=0)
        return kt - 1, c + jnp.sum(lm, axis=0, keepdims=True), acc

    _, _, acc = lax.while_loop(cond, body, (i, jnp.zeros((1, H * T), F32), jnp.zeros((G, T), F32)))
    o_ref[0] = _rms(acc.T, on_ref[...]).astype(BF16)


def _stick_breaking(sb3, u, on):
    B, S, _ = sb3.shape
    G = GROUP_W
    return pl.pallas_call(
        _sb_kernel,
        grid=(B, S // Q_BLOCK),
        in_specs=[pl.BlockSpec((1, Q_BLOCK, G), lambda b, i: (b, i, 0)),
                  pl.BlockSpec((1, S, G), lambda b, i: (b, 0, 1)),
                  pl.BlockSpec((1, S, G), lambda b, i: (b, 0, 2)),
                  pl.BlockSpec((Q_BLOCK, Q_BLOCK), lambda b, i: (0, 0)),
                  pl.BlockSpec((1, G), lambda b, i: (0, 0))],
        out_specs=pl.BlockSpec((1, Q_BLOCK, G), lambda b, i: (b, i, 0)),
        out_shape=jax.ShapeDtypeStruct((B, S, G), BF16),
        scratch_shapes=[pltpu.VMEM((Q_BLOCK, N_HEADS_G * Q_BLOCK), F32)] * 2,
        compiler_params=_params(("parallel", "arbitrary")),
        name="stick_breaking",
    )(sb3, sb3, sb3, u, on)


def _outproj_kernel(yr_ref, yd_ref, ys_ref, yw_ref, bonus_ref, gate_ref, ln_ref, hm_ref, x_ref, g1_ref, sc_ref,
                    sh_ref, gn_ref, wo_ref, rw_ref, rb_ref, x_o, h_o, lg_o):
    G = GROUP_W
    y = yw_ref[...]
    hm = hm_ref[...]
    yc = y - _split_dot(y, hm) * (1.0 / HEAD_DIM)
    var = _split_dot(yc * yc, hm) * (1.0 / HEAD_DIM)
    yw = ((yc * lax.rsqrt(var + RWKV_LN_EPS) * ln_ref[...] + bonus_ref[...]) * gate_ref[...]).astype(BF16)
    dot = lambda yv, g: jnp.dot(yv, wo_ref[g * G:(g + 1) * G, :], preferred_element_type=F32)
    mixed = dot(yr_ref[...], 0) + dot(yw, 1) + dot(yd_ref[...], 2) + dot(ys_ref[...], 3)
    x1 = x_ref[...] + g1_ref[0] * mixed
    x_o[...] = x1
    h = _rms(x1, gn_ref[...]) * (1.0 + sc_ref[0]) + sh_ref[0]
    h_o[...] = h
    hi, lo = _split(h)
    work = (jnp.dot(hi, rw_ref[0], preferred_element_type=F32) + jnp.dot(lo, rw_ref[0], preferred_element_type=F32)
            + jnp.dot(hi, rw_ref[1], preferred_element_type=F32) + rb_ref[...])
    col = lax.broadcasted_iota(I32, work.shape, 1).astype(F32)
    out = jnp.zeros_like(work)
    vals = []
    for j in range(TOP_K):
        m = jnp.max(work, axis=-1, keepdims=True)
        idx = jnp.min(jnp.where(work == m, col, float(LANES)), axis=-1, keepdims=True)
        out = jnp.where(col == float(j), idx, out)
        vals.append(m)
        work = jnp.where(col == idx, -jnp.inf, work)
    es = [jnp.exp(v - vals[0]) for v in vals]
    den = es[0] + es[1] + es[2] + es[3]
    for j in range(TOP_K):
        out = jnp.where(col == float(TOP_K + j), es[j] / den, out)
    lg_o[...] = out


def _outproj(ys, rwkv, x2, mod6, gain, w_out, router_w, router_b, S):
    N, D = x2.shape
    tm = ROW_TILE
    per_b = S // tm
    G = GROUP_W
    row = lambda w: pl.BlockSpec((tm, w), lambda i: (i, 0))
    c = lambda shape: pl.BlockSpec(shape, lambda i: (0, 0))
    return pl.pallas_call(
        _outproj_kernel,
        grid=(N // tm,),
        in_specs=[row(G)] * 6 + [c((1, G)), c((G, G)),
                                 row(D), _mod_row(per_b, 2), _mod_row(per_b, 4), _mod_row(per_b, 3),
                                 c((1, D)), c((D, D)), pl.BlockSpec((2, D, LANES), lambda i: (0, 0, 0)),
                                 c((1, LANES))],
        out_specs=[row(D), row(D), row(LANES)],
        out_shape=[jax.ShapeDtypeStruct((N, D), F32), jax.ShapeDtypeStruct((N, D), F32),
                   jax.ShapeDtypeStruct((N, LANES), F32)],
        compiler_params=_params(("parallel",)),
        name="out_proj_router",
    )(*ys, *rwkv, x2, mod6, mod6, mod6, gain, w_out,
      jnp.pad(jnp.stack(_split(router_w)), ((0, 0), (0, 0), (0, LANES - N_EXPERTS))),
      jnp.pad(router_b, ((0, 0), (0, LANES - N_EXPERTS)), constant_values=-1e30))


def _moe_kernel(be_ref, nu_ref, x_ref, w1_ref, b1_ref, w2_ref, b2_ref, o_ref):
    i = pl.program_id(0)

    @pl.when(i < nu_ref[0])
    def _():
        xb = x_ref[...].astype(BF16)
        w1 = w1_ref.at[0, 0]
        glu = jnp.dot(xb, w1[:, 0:D_FF].astype(BF16), preferred_element_type=F32) + b1_ref[0, :, 0:D_FF]
        lin = jnp.dot(xb, w1[:, D_FF:].astype(BF16), preferred_element_type=F32) + b1_ref[0, :, D_FF:]
        glu = jnp.minimum(glu, SWIGLU_LIMIT)
        lin = jnp.clip(lin, -SWIGLU_LIMIT, SWIGLU_LIMIT)
        act = glu * _sigmoid(SWIGLU_ALPHA * glu) * (lin + 1.0)
        y = jnp.dot(act.astype(BF16), w2_ref[0, 0].astype(BF16), preferred_element_type=F32) + b2_ref[0]
        o_ref[...] = y.astype(BF16)

    @pl.when(i >= nu_ref[0])
    def _():
        o_ref[...] = jnp.zeros_like(o_ref)


def _moe_ffn(block_expert, n_used, xs, layer, w1, b1, w2, b2):
    rows, D = xs.shape
    nb = rows // MOE_BLOCK
    L, E = w1.shape[:2]
    live = lambda i, nu: jnp.minimum(i, nu[0] - 1)
    grid_spec = pltpu.PrefetchScalarGridSpec(
        num_scalar_prefetch=2,
        grid=(nb,),
        in_specs=[pl.BlockSpec((MOE_BLOCK, D), lambda i, be, nu: (live(i, nu), 0)),
                  pl.BlockSpec((1, 1, D, 2 * D_FF), lambda i, be, nu: (layer, be[i], 0, 0)),
                  pl.BlockSpec((1, 1, 2 * D_FF), lambda i, be, nu: (layer * E + be[i], 0, 0)),
                  pl.BlockSpec((1, 1, D_FF, D), lambda i, be, nu: (layer, be[i], 0, 0)),
                  pl.BlockSpec((1, 1, D), lambda i, be, nu: (layer * E + be[i], 0, 0))],
        out_specs=pl.BlockSpec((MOE_BLOCK, D), lambda i, be, nu: (i, 0)),
    )
    return pl.pallas_call(
        _moe_kernel,
        grid_spec=grid_spec,
        out_shape=jax.ShapeDtypeStruct((rows, D), BF16),
        compiler_params=_params(("arbitrary",), 56 * 1024 * 1024),
        name="moe_ffn",
    )(block_expert, n_used, xs, w1, b1.reshape(L * E, 1, -1), w2, b2.reshape(L * E, 1, -1))


def _route(routed):
    n = routed.shape[0]
    expert = routed[:, 0:TOP_K].reshape(-1).astype(I32)
    n_assign = n * TOP_K
    ids = jnp.arange(n_assign, dtype=I32)
    _, order = lax.sort((expert, ids), num_keys=1, is_stable=True)
    _, inv = lax.sort((order, ids), num_keys=1)
    counts = jnp.sum((expert[:, None] == jnp.arange(N_EXPERTS, dtype=I32)[None, :]).astype(I32), axis=0)
    starts = jnp.cumsum(counts) - counts
    padded = (counts + MOE_BLOCK - 1) // MOE_BLOCK * MOE_BLOCK
    pad_end = jnp.cumsum(padded)
    pad_start = pad_end - padded
    dest = pad_start[expert] + inv - starts[expert]
    n_blocks = -(-n_assign // MOE_BLOCK) + N_EXPERTS
    block_start = jnp.arange(n_blocks, dtype=I32) * MOE_BLOCK
    block_expert = jnp.minimum(jnp.sum(pad_end[None, :] <= block_start[:, None], axis=1), N_EXPERTS - 1)
    block_expert = block_expert.astype(I32)
    per_row = lambda t: jnp.repeat(t[block_expert], MOE_BLOCK)
    off = jnp.arange(n_blocks * MOE_BLOCK, dtype=I32) - per_row(pad_start)
    valid = off < per_row(counts)
    src = order[jnp.clip(per_row(starts) + off, 0, n_assign - 1)]
    row_token = jnp.where(valid, src // TOP_K, 0)
    n_used = (pad_end[-1] // MOE_BLOCK).astype(I32).reshape(1)
    return dest.reshape(n, TOP_K), row_token, block_expert, n_used


def _rows(t, idx):
    return t.at[idx].get(mode="promise_in_bounds")


def _final_kernel(x_ref, g2_ref, routed_ref, y0, y1, y2, y3, g_ref, o_ref):
    o_ref[...] = _rms(_combine(x_ref, g2_ref, routed_ref, (y0, y1, y2, y3)), g_ref[...])


def _final_norm(x2, mod6, routed, ys, g, S):
    N, D = x2.shape
    tm = ROW_TILE
    row = pl.BlockSpec((tm, D), lambda i: (i, 0))
    return pl.pallas_call(
        _final_kernel, grid=(N // tm,),
        in_specs=[row, _mod_row(S // tm, 5), pl.BlockSpec((tm, LANES), lambda i: (i, 0))] + [row] * 4
        + [pl.BlockSpec((1, D), lambda i: (0, 0))],
        out_specs=row,
        out_shape=jax.ShapeDtypeStruct((N, D), F32),
        compiler_params=_params(("parallel",)), name="final_norm",
    )(x2, mod6, routed, *ys, g)


def _rope_tables(pos, group, rot_dim, theta, width):
    half = rot_dim // 2
    inv = theta ** (-jnp.arange(half, dtype=F32) / half)
    ang = pos.astype(F32)[..., None] * inv
    cos, sin = jnp.cos(ang), jnp.sin(ang)
    rest = group - rot_dim
    ones = jnp.ones(ang.shape[:2] + (rest,), F32)
    zeros = jnp.zeros(ang.shape[:2] + (rest,), F32)
    zh = jnp.zeros_like(sin)
    rep = width // group
    tile = lambda t: jnp.tile(t, (1, 1, rep)).reshape(-1, width)
    return (tile(jnp.concatenate([cos, cos, ones], -1)),
            tile(jnp.concatenate([-sin, zh, zeros], -1)),
            tile(jnp.concatenate([zh, sin, zeros], -1)))


def _pad_rows(w, start, total=128):
    return jnp.zeros((total, w.shape[1]), w.dtype).at[start:start + w.shape[0]].set(w)


def kernel(x, c, positions, ada_w, ada_b, norm_mix, norm_ffn, w_in, ret_gn, rwkv_mu, rwkv_w0, rwkv_w2, rwkv_a0,
           rwkv_a2, rwkv_g2, rwkv_kk, rwkv_ka, rwkv_rk, rwkv_ln, dsa_qnorm, dsa_wq_up, dsa_wqi_up, dsa_onorm,
           sb_onorm, w_out, router_w, router_b, moe_w1, moe_b1, moe_w2, moe_b2, norm_final):
    B, S, D = x.shape
    N = B * S
    L = ada_w.shape[0]
    G = GROUP_W
    row2 = lambda t: t.reshape(1, -1)

    mod = _ada(c, ada_w, ada_b)
    ret_tabs = tuple(t.reshape(B, S, G) for t in _rope_tables(positions, HEAD_DIM, HEAD_DIM, RET_THETA, G))
    dq_tabs = _rope_tables(positions, HEAD_DIM, HEAD_DIM // 4, ROPE_THETA, G)
    di_tabs = _rope_tables(positions, IDX_DIM, IDX_DIM // 4, ROPE_THETA, G)
    ret_c = _ret_consts(min(256, S))
    hm = (jnp.arange(G)[:, None] // HEAD_DIM == jnp.arange(G)[None, :] // HEAD_DIM).astype(BF16)
    u_later = (jnp.arange(Q_BLOCK)[None, :] > jnp.arange(Q_BLOCK)[:, None]).astype(BF16)
    o1, o2 = RET_COLS + RWKV_COLS, RET_COLS + RWKV_COLS + DSA_COLS

    x2 = x.reshape(N, D)
    comb = None
    for l in range(L):
        mod6 = mod[l].reshape(B * 6, 1, D)
        w_l = w_in[l]
        w_cat = jnp.concatenate(
            [w_l[:, :o1], w_l[:, o1:o2], jnp.zeros((D, DSA_PAD - DSA_COLS), F32), w_l[:, o2:]], axis=1).astype(BF16)
        outs = _inproj(x2, row2(norm_mix[l]), mod6, w_cat, S, comb)
        if comb is not None:
            x2, outs = outs[0], outs[1:]
        ret, rwkv, dsa, sb = outs

        prep = _rwkv_prep(rwkv, S, row2(rwkv_mu[l]), row2(rwkv_w0[l]), row2(rwkv_a0[l]),
                          _pad_rows(rwkv_w2[l], 0).astype(BF16), _pad_rows(rwkv_a2[l], 32).astype(BF16),
                          _pad_rows(rwkv_g2[l], 64).astype(BF16), row2(rwkv_kk[l]), row2(rwkv_ka[l]),
                          row2(rwkv_rk[l]), hm)
        r_, w_, k_, v_, nkk_, b_, gate_, bonus_ = prep
        scan_in = tuple(_to_v_layout(t, B, S) for t in (nkk_, w_, b_, k_, r_, v_))

        y_ret = _retention(ret.reshape(B, S, RET_COLS), ret_tabs, ret_c, row2(ret_gn[l])).reshape(N, G)

        dqt, dqit, dwt, dk, dv, dki = _dsa_prep(dsa, row2(dsa_qnorm[l]), dsa_wq_up[l].astype(BF16),
                                                dsa_wqi_up[l].astype(BF16), dq_tabs, di_tabs)
        y_dsa = _dsa_attention(dqt, dqit, dwt, dk.reshape(B, S, HEAD_DIM), dv.reshape(B, S, HEAD_DIM),
                               dki.reshape(B, S, IDX_DIM), row2(dsa_onorm[l])).reshape(N, G)

        y_sb = _stick_breaking(sb.reshape(B, S, SB_COLS), u_later, row2(sb_onorm[l])).reshape(N, G)

        y_scan = _rwkv_scan(*scan_in)
        rwkv_tail = (_from_v_layout(y_scan, B, S), bonus_, gate_, row2(rwkv_ln[l]), hm)

        x1, h2, routed = _outproj((y_ret, y_dsa, y_sb), rwkv_tail, x2, mod6, row2(norm_ffn[l]),
                                  w_out[l].astype(BF16), router_w[l], row2(router_b[l]), S)

        dest, row_token, block_expert, n_used = _route(routed)
        y = _moe_ffn(block_expert, n_used, _rows(h2, row_token), l, moe_w1, moe_b1, moe_w2, moe_b2)
        x2 = x1
        comb = (mod6, routed, tuple(_rows(y, dest[:, j]) for j in range(TOP_K)))
    return _final_norm(x2, *comb, row2(norm_final), S).reshape(B, S, D)
```

```python
import functools

import jax
import jax.numpy as jnp
from jax import lax
from jax.experimental import pallas as pl
from jax.experimental.pallas import tpu as pltpu

F32 = jnp.float32
BF16 = jnp.bfloat16
I32 = jnp.int32

D_MODEL = 1024
GROUP_W = 256
HEAD_DIM = 64
N_HEADS_G = 4
NORM_EPS = 1e-5
Q_BLOCK = 128
RET_THETA = 10000.0
RWKV_LN_EPS = 64e-5
DSA_Q_LORA = 128
IDX_HEADS = 8
IDX_DIM = 32
DSA_TOPK_MAX = 256
ROPE_THETA = 500000.0
N_EXPERTS = 32
TOP_K = 4
D_FF = D_MODEL
SWIGLU_ALPHA = 1.702
SWIGLU_LIMIT = 7.0
MOE_BLOCK = 512

RET_COLS = 4 * GROUP_W
RWKV_COLS = 3 * GROUP_W + 128
DSA_COLS = 296
DSA_PAD = 384
SB_COLS = 3 * GROUP_W

LANES = 128
SUBLANES = 8
ROW_TILE = 512
VMEM_LIMIT = 48 * 1024 * 1024
INT_MIN = -2 ** 31
SB_UNDERFLOW = -104.0
SOFTMAX_FLOOR = 1e-20

HIGHEST = lax.Precision.HIGHEST
NT_DIMS = (((1,), (1,)), ((), ()))
TN_DIMS = (((0,), (0,)), ((), ()))


def _params(sem, vmem=VMEM_LIMIT):
    return pltpu.CompilerParams(dimension_semantics=sem, vmem_limit_bytes=vmem)


def _bdot(a, b):
    return jnp.dot(a.astype(BF16), b.astype(BF16), preferred_element_type=F32)


def _split(x):
    hi = x.astype(BF16)
    return hi, (x - hi.astype(F32)).astype(BF16)


def _split_dot(x, m):
    hi, lo = _split(x)
    return jnp.dot(hi, m, preferred_element_type=F32) + jnp.dot(lo, m, preferred_element_type=F32)


def _tree(parts, op):
    while len(parts) > 1:
        parts = [op(parts[i], parts[i + 1]) for i in range(0, len(parts), 2)]
    return parts[0]


def _sigmoid(x):
    return 1.0 / (1.0 + jnp.exp(-x))


def _softplus(x):
    return jnp.maximum(x, 0.0) + jnp.log(1.0 + jnp.exp(-jnp.abs(x)))


def _rms(x, g, eps=NORM_EPS):
    return x * lax.rsqrt(jnp.mean(x * x, axis=-1, keepdims=True) + eps) * g


def _rope(x, cos, sin, group, half):
    w = x.shape[-1]
    first = (lax.broadcasted_iota(I32, (1, w), 1) & (group - 1)) < half
    partner = jnp.where(first, pltpu.roll(x, w - half, 1), pltpu.roll(x, half, 1))
    return x * cos + partner * sin


def _ada_kernel(c_ref, w_ref, b_ref, o_ref):
    c = c_ref[...]
    cond = c * _sigmoid(c)
    o_ref[0] = jnp.dot(cond, w_ref[0], preferred_element_type=F32, precision=HIGHEST) + b_ref[0]


def _ada(c, ada_w, ada_b):
    L, D, W = ada_w.shape
    B = c.shape[0]
    tn = 1024
    return pl.pallas_call(
        _ada_kernel,
        grid=(L, W // tn),
        in_specs=[pl.BlockSpec((B, D), lambda l, j: (0, 0)),
                  pl.BlockSpec((1, D, tn), lambda l, j: (l, 0, j)),
                  pl.BlockSpec((1, 1, tn), lambda l, j: (l, 0, j))],
        out_specs=pl.BlockSpec((1, B, tn), lambda l, j: (l, 0, j)),
        out_shape=jax.ShapeDtypeStruct((L, B, W), F32),
        compiler_params=_params(("parallel", "parallel")),
        name="ada_mod",
    )(c, ada_w, ada_b.reshape(L, 1, W))


def _combine(x_ref, g2_ref, routed_ref, y_refs):
    gates = routed_ref[...]
    moe = gates[:, TOP_K:TOP_K + 1] * y_refs[0][...].astype(F32)
    for j in range(1, len(y_refs)):
        moe = moe + gates[:, TOP_K + j:TOP_K + j + 1] * y_refs[j][...].astype(F32)
    return x_ref[...] + g2_ref[0] * moe


def _inproj_kernel(n_comb, x_ref, *refs):
    if n_comb:
        g2_ref, routed_ref, y_refs, refs = refs[0], refs[1], refs[2:2 + n_comb], refs[2 + n_comb:]
        g_ref, sc_ref, sh_ref, w_ref, x_o, ret_ref, rwkv_ref, dsa_ref, sb_ref = refs
        x = _combine(x_ref, g2_ref, routed_ref, y_refs)
        x_o[...] = x
    else:
        g_ref, sc_ref, sh_ref, w_ref, ret_ref, rwkv_ref, dsa_ref, sb_ref = refs
        x = x_ref[...]
    h = _rms(x, g_ref[...]) * (1.0 + sc_ref[0]) + sh_ref[0]
    hb = h.astype(BF16)
    o0, o1, o2 = RET_COLS, RET_COLS + RWKV_COLS, RET_COLS + RWKV_COLS + DSA_PAD
    ret_ref[...] = jnp.dot(hb, w_ref[:, 0:o0], preferred_element_type=F32)
    rwkv_ref[...] = jnp.dot(hb, w_ref[:, o0:o1], preferred_element_type=F32)
    dsa_ref[...] = jnp.dot(hb, w_ref[:, o1:o2], preferred_element_type=F32)
    sb_ref[...] = jnp.dot(hb, w_ref[:, o2:], preferred_element_type=F32).astype(BF16)


def _mod_row(per_b, j):
    return pl.BlockSpec((1, 1, D_MODEL), lambda i: ((i // per_b) * 6 + j, 0, 0))


def _inproj(x2, gain, mod6, w_cat, S, comb=None):
    N, D = x2.shape
    tm = ROW_TILE
    per_b = S // tm
    wt = w_cat.shape[1]
    row = lambda i: (i, 0)
    in_specs = [pl.BlockSpec((tm, D), row)]
    args = [x2]
    out_specs, out_shape = [], []
    if comb is not None:
        in_specs += ([_mod_row(per_b, 5), pl.BlockSpec((tm, LANES), row)]
                     + [pl.BlockSpec((tm, D), row)] * len(comb[2]))
        args += [comb[0], comb[1]] + list(comb[2])
        out_specs.append(pl.BlockSpec((tm, D), row))
        out_shape.append(jax.ShapeDtypeStruct((N, D), F32))
    in_specs += [pl.BlockSpec((1, D), lambda i: (0, 0)), _mod_row(per_b, 1), _mod_row(per_b, 0),
                 pl.BlockSpec((D, wt), lambda i: (0, 0))]
    args += [gain, mod6, mod6, w_cat]
    out_specs += [pl.BlockSpec((tm, RET_COLS), row), pl.BlockSpec((tm, RWKV_COLS), row),
                  pl.BlockSpec((tm, DSA_PAD), row), pl.BlockSpec((tm, SB_COLS), row)]
    out_shape += [jax.ShapeDtypeStruct((N, RET_COLS), F32), jax.ShapeDtypeStruct((N, RWKV_COLS), F32),
                  jax.ShapeDtypeStruct((N, DSA_PAD), F32), jax.ShapeDtypeStruct((N, SB_COLS), BF16)]
    return pl.pallas_call(
        functools.partial(_inproj_kernel, 0 if comb is None else len(comb[2])),
        grid=(N // tm,),
        in_specs=in_specs, out_specs=out_specs, out_shape=out_shape,
        compiler_params=_params(("parallel",)),
        name="in_proj",
    )(*args)


def _ret_kernel(q_ref, k_ref, v_ref, g_ref, cos_ref, sin_ref, din_ref, qd_ref, kd_ref, cd_ref, gn_ref,
                o_ref, state_ref, y_ref):
    @pl.when(pl.program_id(1) == 0)
    def _():
        state_ref[...] = jnp.zeros_like(state_ref)

    cos, sin = cos_ref[0], sin_ref[0]
    q = _rope(q_ref[0], cos, sin, HEAD_DIM, HEAD_DIM // 2)
    k = _rope(k_ref[0], cos, sin, HEAD_DIM, HEAD_DIM // 2) * HEAD_DIM ** -0.5
    v = v_ref[0]
    qd = q * qd_ref[...]
    kd = k * kd_ref[...]
    for h in range(N_HEADS_G):
        sl = slice(h * HEAD_DIM, (h + 1) * HEAD_DIM)
        qh, kh, vh = q[:, sl].astype(BF16), k[:, sl].astype(BF16), v[:, sl].astype(BF16)
        s = lax.dot_general(qh, kh, NT_DIMS, preferred_element_type=F32) * din_ref[h]
        inner = jnp.dot(s.astype(BF16), vh, preferred_element_type=F32)
        st = state_ref[h]
        cross = _bdot(qd[:, sl], st)
        state_ref[h] = st * cd_ref[:, sl] + lax.dot_general(
            kd[:, sl].astype(BF16), vh, TN_DIMS, preferred_element_type=F32)
        o = inner + cross
        oc = o - jnp.mean(o, axis=-1, keepdims=True)
        y_ref[:, sl] = oc * lax.rsqrt(jnp.mean(oc * oc, axis=-1, keepdims=True) + NORM_EPS)
    g = g_ref[0]
    o_ref[0] = (g * _sigmoid(g) * (y_ref[...] * gn_ref[...])).astype(BF16)


def _retention(ret3, tabs, consts, gn):
    B, S, _ = ret3.shape
    C = consts["din"].shape[1]
    blk = lambda j: pl.BlockSpec((1, C, GROUP_W), lambda b, c, j=j: (b, c, j))
    tab = pl.BlockSpec((1, C, GROUP_W), lambda b, c: (b, c, 0))
    const2 = lambda shape: pl.BlockSpec(shape, lambda b, c: (0,) * len(shape))
    return pl.pallas_call(
        _ret_kernel,
        grid=(B, S // C),
        in_specs=[blk(0), blk(1), blk(2), blk(3), tab, tab,
                  const2((N_HEADS_G, C, C)), const2((C, GROUP_W)), const2((C, GROUP_W)),
                  const2((1, GROUP_W)), const2((1, GROUP_W))],
        out_specs=pl.BlockSpec((1, C, GROUP_W), lambda b, c: (b, c, 0)),
        out_shape=jax.ShapeDtypeStruct((B, S, GROUP_W), BF16),
        scratch_shapes=[pltpu.VMEM((N_HEADS_G, HEAD_DIM, HEAD_DIM), F32), pltpu.VMEM((C, GROUP_W), F32)],
        compiler_params=_params(("parallel", "arbitrary")),
        name="retention",
    )(ret3, ret3, ret3, ret3, tabs[0], tabs[1],
      consts["din"], consts["qd"], consts["kd"], consts["cd"], gn)


def _ret_consts(C):
    H = N_HEADS_G
    lg = jnp.log(1.0 - 2.0 ** (-5.0 - jnp.arange(H, dtype=F32)))
    idx = jnp.arange(C, dtype=F32)
    diff = idx[:, None] - idx[None, :]
    din = jnp.where(diff >= 0, jnp.exp(lg[:, None, None] * jnp.maximum(diff, 0.0)), 0.0)
    q_dec = jnp.exp(lg[:, None] * (idx + 1.0))
    k_dec = jnp.exp(lg[:, None] * (C - 1.0 - idx))
    chunk_dec = jnp.exp(lg * C)
    wide = lambda t: jnp.repeat(t.T, HEAD_DIM, axis=1)
    return {"din": din, "qd": wide(q_dec), "kd": wide(k_dec),
            "cd": jnp.repeat(chunk_dec, HEAD_DIM)[None, :]}


def _rwkv_prep_kernel(per_b, z_ref, zp_ref, mu_ref, w0_ref, a0_ref, w2_ref, a2_ref, g2_ref, kk_ref, ka_ref,
                      rk_ref, hm_ref, r_o, w_o, k_o, v_o, nkk_o, b_o, gate_o, bonus_o):
    z = z_ref[...]
    first = (pl.program_id(0) % per_b) == 0
    prow = jnp.where(first, 0.0, zp_ref[7:8, :])
    rid = lax.broadcasted_iota(I32, z.shape, 0)
    prev = jnp.where(rid == 0, prow, pltpu.roll(z, 1, 0))
    f = z + (prev - z) * mu_ref[...]
    G = GROUP_W
    r, k, v, lo = f[:, 0:G], f[:, G:2 * G], f[:, 2 * G:3 * G], f[:, 3 * G:3 * G + 128]
    w_log = -_softplus(-(w0_ref[...] + _bdot(jnp.tanh(lo), w2_ref[...]))) - 0.5
    decay = jnp.exp(-jnp.exp(w_log))
    a = _sigmoid(a0_ref[...] + _bdot(lo, a2_ref[...]))
    gate = _bdot(_sigmoid(lo), g2_ref[...])
    hm = hm_ref[...]
    kk = k * kk_ref[...]
    kk = kk / jnp.maximum(jnp.sqrt(_split_dot(kk * kk, hm)), 1e-12)
    k2 = k * (1.0 + (a - 1.0) * ka_ref[...])
    r_o[...] = r
    w_o[...] = decay
    k_o[...] = k2
    v_o[...] = v
    nkk_o[...] = -kk
    b_o[...] = kk * a
    gate_o[...] = gate
    bonus_o[...] = _split_dot(r * k2 * rk_ref[...], hm) * v


def _rwkv_prep(z2, S, mu, w0, a0, w2p, a2p, g2p, kk, ka, rk, hm):
    N, W = z2.shape
    tm = ROW_TILE
    per_b = S // tm
    row = lambda i: (i, 0)
    c = lambda shape: pl.BlockSpec(shape, lambda i: (0, 0))
    G = GROUP_W
    return pl.pallas_call(
        functools.partial(_rwkv_prep_kernel, per_b),
        grid=(N // tm,),
        in_specs=[pl.BlockSpec((tm, W), row),
                  pl.BlockSpec((SUBLANES, W), lambda i: (jnp.maximum(i * (tm // SUBLANES) - 1, 0), 0)),
                  c((1, W)), c((1, G)), c((1, G)), c((128, G)), c((128, G)), c((128, G)),
                  c((1, G)), c((1, G)), c((1, G)), c((G, G))],
        out_specs=[pl.BlockSpec((tm, G), row)] * 8,
        out_shape=[jax.ShapeDtypeStruct((N, G), F32)] * 8,
        compiler_params=_params(("parallel",)),
        name="rwkv_prep",
    )(z2, z2, mu, w0, a0, w2p, a2p, g2p, kk, ka, rk, hm)


def _rwkv_scan_kernel(T, nkk_ref, w_ref, b_ref, k_ref, r_ref, v_ref, y_ref, s_ref, rep_ref):
    @pl.when(pl.program_id(0) == 0)
    def _():
        s_ref[...] = jnp.zeros_like(s_ref)

    ch = v_ref.shape[1]
    groups = HEAD_DIM // ch
    width = LANES // groups
    tree = lambda parts: _tree(parts, jnp.add)

    grp = lax.broadcasted_iota(I32, (T * ch, LANES), 1) // width
    for o, ref in enumerate((nkk_ref, w_ref, b_ref, k_ref, r_ref)):
        x = ref[...].reshape(T * ch, LANES)
        turned = [x] + [pltpu.roll(x, m * width, 1) for m in range(1, groups)]
        for j in range(groups):
            y = turned[0]
            for m in range(1, groups):
                y = jnp.where(grp == (j + m) % groups, turned[m], y)
            rep_ref[o, :, j] = y.reshape(T, ch, LANES)

    keys = [(j, kh) for j in range(groups) for kh in range(ch)]
    row = lambda o, t, j, kh: rep_ref[o, t, j, kh:kh + 1, :]

    def step(t, carry):
        vt = v_ref[t]
        sa = tree([s_ref[j * ch + kh] * row(0, t, j, kh) for j, kh in keys])
        ys = []
        for j, kh in keys:
            s_new = (s_ref[j * ch + kh] * row(1, t, j, kh) + sa * row(2, t, j, kh) + vt * row(3, t, j, kh))
            s_ref[j * ch + kh] = s_new
            ys.append(s_new * row(4, t, j, kh))
        y_ref[t] = tree(ys)
        return carry

    lax.fori_loop(0, T, step, 0)


def _rwkv_scan(nkk, w, b, k, r, v):
    S, ch, _ = v.shape
    T = 32
    spec = pl.BlockSpec((T, ch, LANES), lambda i: (i, 0, 0))
    return pl.pallas_call(
        functools.partial(_rwkv_scan_kernel, T),
        grid=(S // T,),
        in_specs=[spec] * 6,
        out_specs=spec,
        out_shape=jax.ShapeDtypeStruct((S, ch, LANES), F32),
        scratch_shapes=[pltpu.VMEM((HEAD_DIM, ch, LANES), F32),
                        pltpu.VMEM((5, T, HEAD_DIM // ch, ch, LANES), F32)],
        compiler_params=_params(("arbitrary",)),
        name="rwkv_scan",
    )(nkk, w, b, k, r, v)


def _to_v_layout(x, B, S):
    P = B * N_HEADS_G
    rep = LANES // P
    t = x.reshape(B, S, N_HEADS_G, HEAD_DIM // rep, rep).transpose(1, 3, 4, 0, 2)
    return t.reshape(S, HEAD_DIM // rep, LANES)


def _from_v_layout(y, B, S):
    P = B * N_HEADS_G
    rep = LANES // P
    t = y.reshape(S, HEAD_DIM // rep, rep, B, N_HEADS_G).transpose(3, 0, 4, 1, 2)
    return t.reshape(B * S, GROUP_W)


def _dsa_prep_kernel(f_ref, qn_ref, wq_ref, wqi_ref, cq_ref, sq_ref, ci_ref, si_ref,
                     qt_o, qit_o, wt_o, k_o, v_o, ki_o):
    f = f_ref[...]
    cq = _rms(f[:, 0:DSA_Q_LORA], qn_ref[...]).astype(BF16)
    cos_q, sin_q = cq_ref[...], sq_ref[...]
    cos_i, sin_i = ci_ref[...], si_ref[...]
    q = _rope(jnp.dot(cq, wq_ref[...], preferred_element_type=F32), cos_q, sin_q, HEAD_DIM, HEAD_DIM // 8)
    tm = f.shape[0]
    qt_o[...] = (q * HEAD_DIM ** -0.5).T.reshape(N_HEADS_G, HEAD_DIM, tm).astype(BF16)
    qi = _rope(jnp.dot(cq, wqi_ref[...], preferred_element_type=F32), cos_i, sin_i, IDX_DIM, IDX_DIM // 8)
    qit_o[...] = qi.T.reshape(IDX_HEADS, IDX_DIM, tm).astype(BF16)
    kv = f[:, 128:256]
    kv_r = _rope(kv, cos_q[:, 0:128], sin_q[:, 0:128], HEAD_DIM, HEAD_DIM // 8)
    k_o[...] = kv_r[:, 0:HEAD_DIM].astype(BF16)
    v_o[...] = kv[:, HEAD_DIM:128].astype(BF16)
    tail = f[:, 256:384]
    tail_r = _rope(tail, cos_i[:, 0:128], sin_i[:, 0:128], IDX_DIM, IDX_DIM // 8)
    ki_o[...] = tail_r[:, 0:IDX_DIM].astype(BF16)
    wt_o[...] = tail.T[IDX_DIM:IDX_DIM + IDX_HEADS, :] * (IDX_HEADS ** -0.5 * IDX_DIM ** -0.5)


def _dsa_prep(f2, qn, wq, wqi, tq, ti):
    N, W = f2.shape
    tm = ROW_TILE
    G = GROUP_W
    row = lambda w: pl.BlockSpec((tm, w), lambda i: (i, 0))
    col = lambda h: pl.BlockSpec((h, tm), lambda i: (0, i))
    slab = lambda h, d: pl.BlockSpec((h, d, tm), lambda i: (0, 0, i))
    c = lambda shape: pl.BlockSpec(shape, lambda i: (0, 0))
    return pl.pallas_call(
        _dsa_prep_kernel,
        grid=(N // tm,),
        in_specs=[row(W), c((1, DSA_Q_LORA)), c((DSA_Q_LORA, G)), c((DSA_Q_LORA, G))] + [row(G)] * 4,
        out_specs=[slab(N_HEADS_G, HEAD_DIM), slab(IDX_HEADS, IDX_DIM), col(IDX_HEADS),
                   row(HEAD_DIM), row(HEAD_DIM), row(IDX_DIM)],
        out_shape=[jax.ShapeDtypeStruct((N_HEADS_G, HEAD_DIM, N), BF16),
                   jax.ShapeDtypeStruct((IDX_HEADS, IDX_DIM, N), BF16),
                   jax.ShapeDtypeStruct((IDX_HEADS, N), F32),
                   jax.ShapeDtypeStruct((N, HEAD_DIM), BF16), jax.ShapeDtypeStruct((N, HEAD_DIM), BF16),
                   jax.ShapeDtypeStruct((N, IDX_DIM), BF16)],
        compiler_params=_params(("parallel",)),
        name="dsa_prep",
    )(f2, qn, wq, wqi, *tq, *ti)


def _float_key(bits):
    return bits ^ ((bits >> 31) & 0x7FFFFFFF)


def _dsa_kernel(TK, topk, qt_ref, qit_ref, wt_ref, k_ref, v_ref, ki_ref, on_ref, o_ref,
                skey_ref, lg_ref, out_ref):
    i = pl.program_id(1)
    TQ = Q_BLOCK
    n_kt = (i * TQ + TQ + TK - 1) // TK
    q_all = jnp.concatenate([qt_ref[h] for h in range(N_HEADS_G)], axis=1)
    qi_all = jnp.concatenate([qit_ref[h] for h in range(IDX_HEADS)], axis=1)
    wt = wt_ref[...]
    kpos = lax.broadcasted_iota(I32, (TK, TQ), 0)
    qpos = i * TQ + lax.broadcasted_iota(I32, (TK, TQ), 1)
    head = lambda x, h: x[:, h * TQ:(h + 1) * TQ]

    def fold(x, op):
        return _tree([x[r * SUBLANES:(r + 1) * SUBLANES] for r in range(TK // SUBLANES)], op)

    def score_tile(kt, tops):
        off = pl.multiple_of(kt * TK, TK)
        rel = jnp.dot(ki_ref[0, pl.ds(off, TK), :], qi_all, preferred_element_type=F32)
        sc = jnp.zeros((TK, TQ), F32)
        for h in range(IDX_HEADS):
            sc = sc + jnp.maximum(head(rel, h), 0.0) * wt[h:h + 1, :]
        key = _float_key(pltpu.bitcast(sc, I32))
        valid = kt * TK + kpos <= qpos
        skey_ref[kt] = jnp.where(valid, key, INT_MIN)
        lg = jnp.dot(k_ref[0, pl.ds(off, TK), :], q_all, preferred_element_type=F32)
        lg_ref[kt] = lg
        return tuple(jnp.maximum(tops[h], fold(jnp.where(valid, head(lg, h), -1e30), jnp.maximum))
                     for h in range(N_HEADS_G))

    lowest = tuple(jnp.full((SUBLANES, TQ), -1e30, F32) for _ in range(N_HEADS_G))
    tops = lax.fori_loop(0, n_kt, score_tile, lowest)

    def bit_step(j, thr):
        cand = thr + lax.shift_left(jnp.int32(1), 31 - j)

        def count_tile(kt, acc):
            return acc + fold(jnp.where(skey_ref[kt] >= cand, 1.0, 0.0), jnp.add)

        acc = lax.fori_loop(0, n_kt, count_tile, jnp.zeros((SUBLANES, TQ), F32))
        return jnp.where(jnp.sum(acc, axis=0, keepdims=True) >= topk, cand, thr)

    thr = lax.fori_loop(0, 32, bit_step, jnp.full((1, TQ), INT_MIN, I32))
    thr = jnp.maximum(thr, INT_MIN + 1)

    def sweep(shifts):
        def acc_tile(kt, carry):
            ls, acc = carry
            sel = skey_ref[kt] >= thr
            ps = [jnp.where(sel, jnp.exp(lg_ref[kt, :, h * TQ:(h + 1) * TQ] - shifts[h]), 0.0)
                  for h in range(N_HEADS_G)]
            p_all = jnp.concatenate([p.astype(BF16) for p in ps], axis=1)
            vb = v_ref[0, pl.ds(pl.multiple_of(kt * TK, TK), TK), :]
            return (tuple(ls[h] + fold(ps[h], jnp.add) for h in range(N_HEADS_G)),
                    acc + lax.dot_general(vb, p_all, TN_DIMS, preferred_element_type=F32))

        ls, acc = lax.fori_loop(0, n_kt, acc_tile,
                                (tuple(jnp.zeros((SUBLANES, TQ), F32) for _ in range(N_HEADS_G)),
                                 jnp.zeros((HEAD_DIM, N_HEADS_G * TQ), F32)))
        ls = [jnp.sum(l, axis=0, keepdims=True) for l in ls]
        for h in range(N_HEADS_G):
            out_ref[h * HEAD_DIM:(h + 1) * HEAD_DIM, :] = head(acc, h) / ls[h]
        return jnp.min(jnp.minimum(jnp.minimum(ls[0], ls[1]), jnp.minimum(ls[2], ls[3])))

    smallest = sweep([jnp.max(t, axis=0, keepdims=True) for t in tops])

    @pl.when(smallest < SOFTMAX_FLOOR)
    def _():
        def max_tile(kt, ms):
            sel = skey_ref[kt] >= thr
            return tuple(jnp.maximum(ms[h], fold(jnp.where(sel, lg_ref[kt, :, h * TQ:(h + 1) * TQ], -1e30),
                                                 jnp.maximum))
                         for h in range(N_HEADS_G))

        ms = lax.fori_loop(0, n_kt, max_tile, lowest)
        sweep([jnp.max(m, axis=0, keepdims=True) for m in ms])

    o_ref[0] = _rms(out_ref[...].T, on_ref[...]).astype(BF16)


def _dsa_attention(qt, qit, wt, k, v, ki, on):
    B, S, _ = k.shape
    G = GROUP_W
    nq = S // Q_BLOCK
    TK = min(512, S)
    topk = min(DSA_TOPK_MAX, S // 4)
    qcol = lambda h: pl.BlockSpec((h, Q_BLOCK), lambda b, i: (0, b * nq + i))
    qslab = lambda h, d: pl.BlockSpec((h, d, Q_BLOCK), lambda b, i: (0, 0, b * nq + i))
    full = lambda wd: pl.BlockSpec((1, S, wd), lambda b, i: (b, 0, 0))
    return pl.pallas_call(
        functools.partial(_dsa_kernel, TK, topk),
        grid=(B, nq),
        in_specs=[qslab(N_HEADS_G, HEAD_DIM), qslab(IDX_HEADS, IDX_DIM), qcol(IDX_HEADS),
                  full(HEAD_DIM), full(HEAD_DIM), full(IDX_DIM), pl.BlockSpec((1, G), lambda b, i: (0, 0))],
        out_specs=pl.BlockSpec((1, Q_BLOCK, G), lambda b, i: (b, i, 0)),
        out_shape=jax.ShapeDtypeStruct((B, S, G), BF16),
        scratch_shapes=[pltpu.VMEM((S // TK, TK, Q_BLOCK), I32),
                        pltpu.VMEM((S // TK, TK, N_HEADS_G * Q_BLOCK), F32),
                        pltpu.VMEM((G, Q_BLOCK), F32)],
        compiler_params=_params(("parallel", "arbitrary")),
        name="dsa_attention",
    )(qt, qit, wt, k, v, ki, on)


def _sb_kernel(q_ref, k_ref, v_ref, u_ref, on_ref, o_ref, z_ref, lm_ref):
    i = pl.program_id(1)
    T = Q_BLOCK
    H, G = N_HEADS_G, GROUP_W
    u = u_ref[...]
    qt = q_ref[0].astype(F32).T.astype(BF16)
    q_rows = lax.broadcasted_iota(I32, (G, H * T), 0) // HEAD_DIM
    q_cols = lax.broadcasted_iota(I32, (G, H * T), 1) // T
    qbd = jnp.where(q_rows == q_cols, jnp.concatenate([qt] * H, axis=1), jnp.zeros((), BF16))
    kid = lax.broadcasted_iota(I32, (T, H * T), 0)
    qid = lax.broadcasted_iota(I32, (T, H * T), 1) & (T - 1)
    strict = kid < qid

    def cond(carry):
        kt, c, _ = carry
        return jnp.logical_and(kt >= 0, jnp.max(c) > SB_UNDERFLOW)

    def logits(kt):
        z = jnp.dot(k_ref[0, pl.ds(pl.multiple_of(kt * T, T), T), :], qbd, preferred_element_type=F32)
        z = z * HEAD_DIM ** -0.5
        return z, -_softplus(z)

    z0, lm0 = logits(i)
    z_ref[...] = z0
    lm_ref[...] = jnp.where(strict, lm0, 0.0)

    def body(carry):
        kt, c, acc = carry
        off = pl.multiple_of(kt * T, T)
        mask = jnp.logical_or(strict, kt < i)
        z, lm = z_ref[...], lm_ref[...]
        z_ref[...], lm_ref[...] = logits(jnp.maximum(kt - 1, 0))
        hi, lo = _split(lm)
        later = jnp.dot(u, jnp.concatenate([hi, lo], axis=1), preferred_element_type=F32)
        later = later[:, :H * T] + later[:, H * T:]
        a = jnp.where(mask, jnp.exp(z + lm + (c + later)), 0.0)
        av = lax.dot_general(v_ref[0, pl.ds(off, T), :], a.astype(BF16), TN_DIMS, preferred_element_type=F32)
        acc = acc + jnp.concatenate(
            [av[h * HEAD_DIM:(h + 1) * HEAD_DIM, h * T:(h + 1) * T] for h in range(H)], axis=0)
        return kt - 1, c + jnp.sum(lm, axis=0, keepdims=True), acc

    _, _, acc = lax.while_loop(cond, body, (i, jnp.zeros((1, H * T), F32), jnp.zeros((G, T), F32)))
    o_ref[0] = _rms(acc.T, on_ref[...]).astype(BF16)


def _stick_breaking(sb3, u, on):
    B, S, _ = sb3.shape
    G = GROUP_W
    return pl.pallas_call(
        _sb_kernel,
        grid=(B, S // Q_BLOCK),
        in_specs=[pl.BlockSpec((1, Q_BLOCK, G), lambda b, i: (b, i, 0)),
                  pl.BlockSpec((1, S, G), lambda b, i: (b, 0, 1)),
                  pl.BlockSpec((1, S, G), lambda b, i: (b, 0, 2)),
                  pl.BlockSpec((Q_BLOCK, Q_BLOCK), lambda b, i: (0, 0)),
                  pl.BlockSpec((1, G), lambda b, i: (0, 0))],
        out_specs=pl.BlockSpec((1, Q_BLOCK, G), lambda b, i: (b, i, 0)),
        out_shape=jax.ShapeDtypeStruct((B, S, G), BF16),
        scratch_shapes=[pltpu.VMEM((Q_BLOCK, N_HEADS_G * Q_BLOCK), F32)] * 2,
        compiler_params=_params(("parallel", "arbitrary")),
        name="stick_breaking",
    )(sb3, sb3, sb3, u, on)


def _outproj_kernel(yr_ref, yd_ref, ys_ref, yw_ref, bonus_ref, gate_ref, ln_ref, hm_ref, x_ref, g1_ref, sc_ref,
                    sh_ref, gn_ref, wo_ref, rw_ref, rb_ref, x_o, h_o, lg_o):
    G = GROUP_W
    y = yw_ref[...]
    hm = hm_ref[...]
    yc = y - _split_dot(y, hm) * (1.0 / HEAD_DIM)
    var = _split_dot(yc * yc, hm) * (1.0 / HEAD_DIM)
    yw = ((yc * lax.rsqrt(var + RWKV_LN_EPS) * ln_ref[...] + bonus_ref[...]) * gate_ref[...]).astype(BF16)
    dot = lambda yv, g: jnp.dot(yv, wo_ref[g * G:(g + 1) * G, :], preferred_element_type=F32)
    mixed = dot(yr_ref[...], 0) + dot(yw, 1) + dot(yd_ref[...], 2) + dot(ys_ref[...], 3)
    x1 = x_ref[...] + g1_ref[0] * mixed
    x_o[...] = x1
    h = _rms(x1, gn_ref[...]) * (1.0 + sc_ref[0]) + sh_ref[0]
    h_o[...] = h
    hi, lo = _split(h)
    work = (jnp.dot(hi, rw_ref[0], preferred_element_type=F32) + jnp.dot(lo, rw_ref[0], preferred_element_type=F32)
            + jnp.dot(hi, rw_ref[1], preferred_element_type=F32) + rb_ref[...])
    col = lax.broadcasted_iota(I32, work.shape, 1).astype(F32)
    out = jnp.zeros_like(work)
    vals = []
    for j in range(TOP_K):
        m = jnp.max(work, axis=-1, keepdims=True)
        idx = jnp.min(jnp.where(work == m, col, float(LANES)), axis=-1, keepdims=True)
        out = jnp.where(col == float(j), idx, out)
        vals.append(m)
        work = jnp.where(col == idx, -jnp.inf, work)
    es = [jnp.exp(v - vals[0]) for v in vals]
    den = es[0] + es[1] + es[2] + es[3]
    for j in range(TOP_K):
        out = jnp.where(col == float(TOP_K + j), es[j] / den, out)
    lg_o[...] = out


def _outproj(ys, rwkv, x2, mod6, gain, w_out, router_w, router_b, S):
    N, D = x2.shape
    tm = ROW_TILE
    per_b = S // tm
    G = GROUP_W
    row = lambda w: pl.BlockSpec((tm, w), lambda i: (i, 0))
    c = lambda shape: pl.BlockSpec(shape, lambda i: (0, 0))
    return pl.pallas_call(
        _outproj_kernel,
        grid=(N // tm,),
        in_specs=[row(G)] * 6 + [c((1, G)), c((G, G)),
                                 row(D), _mod_row(per_b, 2), _mod_row(per_b, 4), _mod_row(per_b, 3),
                                 c((1, D)), c((D, D)), pl.BlockSpec((2, D, LANES), lambda i: (0, 0, 0)),
                                 c((1, LANES))],
        out_specs=[row(D), row(D), row(LANES)],
        out_shape=[jax.ShapeDtypeStruct((N, D), F32), jax.ShapeDtypeStruct((N, D), F32),
                   jax.ShapeDtypeStruct((N, LANES), F32)],
        compiler_params=_params(("parallel",)),
        name="out_proj_router",
    )(*ys, *rwkv, x2, mod6, mod6, mod6, gain, w_out,
      jnp.pad(jnp.stack(_split(router_w)), ((0, 0), (0, 0), (0, LANES - N_EXPERTS))),
      jnp.pad(router_b, ((0, 0), (0, LANES - N_EXPERTS)), constant_values=-1e30))


def _moe_kernel(be_ref, nu_ref, x_ref, w1_ref, b1_ref, w2_ref, b2_ref, o_ref):
    i = pl.program_id(0)

    @pl.when(i < nu_ref[0])
    def _():
        xb = x_ref[...].astype(BF16)
        w1 = w1_ref.at[0, 0]
        glu = jnp.dot(xb, w1[:, 0:D_FF].astype(BF16), preferred_element_type=F32) + b1_ref[0, :, 0:D_FF]
        lin = jnp.dot(xb, w1[:, D_FF:].astype(BF16), preferred_element_type=F32) + b1_ref[0, :, D_FF:]
        glu = jnp.minimum(glu, SWIGLU_LIMIT)
        lin = jnp.clip(lin, -SWIGLU_LIMIT, SWIGLU_LIMIT)
        act = glu * _sigmoid(SWIGLU_ALPHA * glu) * (lin + 1.0)
        y = jnp.dot(act.astype(BF16), w2_ref[0, 0].astype(BF16), preferred_element_type=F32) + b2_ref[0]
        o_ref[...] = y.astype(BF16)

    @pl.when(i >= nu_ref[0])
    def _():
        o_ref[...] = jnp.zeros_like(o_ref)


def _moe_ffn(block_expert, n_used, xs, layer, w1, b1, w2, b2):
    rows, D = xs.shape
    nb = rows // MOE_BLOCK
    L, E = w1.shape[:2]
    live = lambda i, nu: jnp.minimum(i, nu[0] - 1)
    grid_spec = pltpu.PrefetchScalarGridSpec(
        num_scalar_prefetch=2,
        grid=(nb,),
        in_specs=[pl.BlockSpec((MOE_BLOCK, D), lambda i, be, nu: (live(i, nu), 0)),
                  pl.BlockSpec((1, 1, D, 2 * D_FF), lambda i, be, nu: (layer, be[i], 0, 0)),
                  pl.BlockSpec((1, 1, 2 * D_FF), lambda i, be, nu: (layer * E + be[i], 0, 0)),
                  pl.BlockSpec((1, 1, D_FF, D), lambda i, be, nu: (layer, be[i], 0, 0)),
                  pl.BlockSpec((1, 1, D), lambda i, be, nu: (layer * E + be[i], 0, 0))],
        out_specs=pl.BlockSpec((MOE_BLOCK, D), lambda i, be, nu: (i, 0)),
    )
    return pl.pallas_call(
        _moe_kernel,
        grid_spec=grid_spec,
        out_shape=jax.ShapeDtypeStruct((rows, D), BF16),
        compiler_params=_params(("arbitrary",), 56 * 1024 * 1024),
        name="moe_ffn",
    )(block_expert, n_used, xs, w1, b1.reshape(L * E, 1, -1), w2, b2.reshape(L * E, 1, -1))


def _route(routed):
    n = routed.shape[0]
    expert = routed[:, 0:TOP_K].reshape(-1).astype(I32)
    n_assign = n * TOP_K
    ids = jnp.arange(n_assign, dtype=I32)
    _, order = lax.sort((expert, ids), num_keys=1, is_stable=True)
    _, inv = lax.sort((order, ids), num_keys=1)
    counts = jnp.sum((expert[:, None] == jnp.arange(N_EXPERTS, dtype=I32)[None, :]).astype(I32), axis=0)
    starts = jnp.cumsum(counts) - counts
    padded = (counts + MOE_BLOCK - 1) // MOE_BLOCK * MOE_BLOCK
    pad_end = jnp.cumsum(padded)
    pad_start = pad_end - padded
    dest = pad_start[expert] + inv - starts[expert]
    n_blocks = -(-n_assign // MOE_BLOCK) + N_EXPERTS
    block_start = jnp.arange(n_blocks, dtype=I32) * MOE_BLOCK
    block_expert = jnp.minimum(jnp.sum(pad_end[None, :] <= block_start[:, None], axis=1), N_EXPERTS - 1)
    block_expert = block_expert.astype(I32)
    per_row = lambda t: jnp.repeat(t[block_expert], MOE_BLOCK)
    off = jnp.arange(n_blocks * MOE_BLOCK, dtype=I32) - per_row(pad_start)
    valid = off < per_row(counts)
    src = order[jnp.clip(per_row(starts) + off, 0, n_assign - 1)]
    row_token = jnp.where(valid, src // TOP_K, 0)
    n_used = (pad_end[-1] // MOE_BLOCK).astype(I32).reshape(1)
    return dest.reshape(n, TOP_K), row_token, block_expert, n_used


def _rows(t, idx):
    return t.at[idx].get(mode="promise_in_bounds")


def _final_kernel(x_ref, g2_ref, routed_ref, y0, y1, y2, y3, g_ref, o_ref):
    o_ref[...] = _rms(_combine(x_ref, g2_ref, routed_ref, (y0, y1, y2, y3)), g_ref[...])


def _final_norm(x2, mod6, routed, ys, g, S):
    N, D = x2.shape
    tm = ROW_TILE
    row = pl.BlockSpec((tm, D), lambda i: (i, 0))
    return pl.pallas_call(
        _final_kernel, grid=(N // tm,),
        in_specs=[row, _mod_row(S // tm, 5), pl.BlockSpec((tm, LANES), lambda i: (i, 0))] + [row] * 4
        + [pl.BlockSpec((1, D), lambda i: (0, 0))],
        out_specs=row,
        out_shape=jax.ShapeDtypeStruct((N, D), F32),
        compiler_params=_params(("parallel",)), name="final_norm",
    )(x2, mod6, routed, *ys, g)


def _rope_tables(pos, group, rot_dim, theta, width):
    half = rot_dim // 2
    inv = theta ** (-jnp.arange(half, dtype=F32) / half)
    ang = pos.astype(F32)[..., None] * inv
    cos, sin = jnp.cos(ang), jnp.sin(ang)
    rest = group - rot_dim
    ones = jnp.ones(ang.shape[:2] + (rest,), F32)
    zeros = jnp.zeros(ang.shape[:2] + (rest,), F32)
    rep = width // group
    tile = lambda t: jnp.tile(t, (1, 1, rep)).reshape(-1, width)
    return (tile(jnp.concatenate([cos, cos, ones], -1)), tile(jnp.concatenate([-sin, sin, zeros], -1)))


def _pad_rows(w, start, total=128):
    return jnp.zeros((total, w.shape[1]), w.dtype).at[start:start + w.shape[0]].set(w)


def kernel(x, c, positions, ada_w, ada_b, norm_mix, norm_ffn, w_in, ret_gn, rwkv_mu, rwkv_w0, rwkv_w2, rwkv_a0,
           rwkv_a2, rwkv_g2, rwkv_kk, rwkv_ka, rwkv_rk, rwkv_ln, dsa_qnorm, dsa_wq_up, dsa_wqi_up, dsa_onorm,
           sb_onorm, w_out, router_w, router_b, moe_w1, moe_b1, moe_w2, moe_b2, norm_final):
    B, S, D = x.shape
    N = B * S
    L = ada_w.shape[0]
    G = GROUP_W
    row2 = lambda t: t.reshape(1, -1)

    mod = _ada(c, ada_w, ada_b)
    ret_tabs = tuple(t.reshape(B, S, G) for t in _rope_tables(positions, HEAD_DIM, HEAD_DIM, RET_THETA, G))
    dq_tabs = _rope_tables(positions, HEAD_DIM, HEAD_DIM // 4, ROPE_THETA, G)
    di_tabs = _rope_tables(positions, IDX_DIM, IDX_DIM // 4, ROPE_THETA, G)
    ret_c = _ret_consts(min(256, S))
    hm = (jnp.arange(G)[:, None] // HEAD_DIM == jnp.arange(G)[None, :] // HEAD_DIM).astype(BF16)
    u_later = (jnp.arange(Q_BLOCK)[None, :] > jnp.arange(Q_BLOCK)[:, None]).astype(BF16)
    o1, o2 = RET_COLS + RWKV_COLS, RET_COLS + RWKV_COLS + DSA_COLS

    x2 = x.reshape(N, D)
    comb = None
    for l in range(L):
        mod6 = mod[l].reshape(B * 6, 1, D)
        w_l = w_in[l]
        w_cat = jnp.concatenate(
            [w_l[:, :o1], w_l[:, o1:o2], jnp.zeros((D, DSA_PAD - DSA_COLS), F32), w_l[:, o2:]], axis=1).astype(BF16)
        outs = _inproj(x2, row2(norm_mix[l]), mod6, w_cat, S, comb)
        if comb is not None:
            x2, outs = outs[0], outs[1:]
        ret, rwkv, dsa, sb = outs

        prep = _rwkv_prep(rwkv, S, row2(rwkv_mu[l]), row2(rwkv_w0[l]), row2(rwkv_a0[l]),
                          _pad_rows(rwkv_w2[l], 0).astype(BF16), _pad_rows(rwkv_a2[l], 32).astype(BF16),
                          _pad_rows(rwkv_g2[l], 64).astype(BF16), row2(rwkv_kk[l]), row2(rwkv_ka[l]),
                          row2(rwkv_rk[l]), hm)
        r_, w_, k_, v_, nkk_, b_, gate_, bonus_ = prep
        scan_in = tuple(_to_v_layout(t, B, S) for t in (nkk_, w_, b_, k_, r_, v_))

        y_ret = _retention(ret.reshape(B, S, RET_COLS), ret_tabs, ret_c, row2(ret_gn[l])).reshape(N, G)

        dqt, dqit, dwt, dk, dv, dki = _dsa_prep(dsa, row2(dsa_qnorm[l]), dsa_wq_up[l].astype(BF16),
                                                dsa_wqi_up[l].astype(BF16), dq_tabs, di_tabs)
        y_dsa = _dsa_attention(dqt, dqit, dwt, dk.reshape(B, S, HEAD_DIM), dv.reshape(B, S, HEAD_DIM),
                               dki.reshape(B, S, IDX_DIM), row2(dsa_onorm[l])).reshape(N, G)

        y_sb = _stick_breaking(sb.reshape(B, S, SB_COLS), u_later, row2(sb_onorm[l])).reshape(N, G)

        y_scan = _rwkv_scan(*scan_in)
        rwkv_tail = (_from_v_layout(y_scan, B, S), bonus_, gate_, row2(rwkv_ln[l]), hm)

        x1, h2, routed = _outproj((y_ret, y_dsa, y_sb), rwkv_tail, x2, mod6, row2(norm_ffn[l]),
                                  w_out[l].astype(BF16), router_w[l], row2(router_b[l]), S)

        dest, row_token, block_expert, n_used = _route(routed)
        y = _moe_ffn(block_expert, n_used, _rows(h2, row_token), l, moe_w1, moe_b1, moe_w2, moe_b2)
        x2 = x1
        comb = (mod6, routed, tuple(_rows(y, dest[:, j]) for j in range(TOP_K)))
    return _final_norm(x2, *comb, row2(norm_final), S).reshape(B, S, D)
```

```python
import functools

import jax
import jax.numpy as jnp
from jax import lax
from jax.experimental import pallas as pl
from jax.experimental.pallas import tpu as pltpu

F32 = jnp.float32
BF16 = jnp.bfloat16
I32 = jnp.int32

D_MODEL = 1024
GROUP_W = 256
HEAD_DIM = 64
N_HEADS_G = 4
NORM_EPS = 1e-5
Q_BLOCK = 128
RET_THETA = 10000.0
RWKV_LN_EPS = 64e-5
DSA_Q_LORA = 128
IDX_HEADS = 8
IDX_DIM = 32
DSA_TOPK_MAX = 256
ROPE_THETA = 500000.0
N_EXPERTS = 32
TOP_K = 4
D_FF = D_MODEL
SWIGLU_ALPHA = 1.702
SWIGLU_LIMIT = 7.0
MOE_BLOCK = 512

RET_COLS = 4 * GROUP_W
RWKV_COLS = 3 * GROUP_W + 128
DSA_COLS = 296
DSA_PAD = 384
SB_COLS = 3 * GROUP_W

LANES = 128
SUBLANES = 8
ROW_TILE = 512
VMEM_LIMIT = 48 * 1024 * 1024
INT_MIN = -2 ** 31
SB_UNDERFLOW = -104.0
SOFTMAX_FLOOR = 1e-20

HIGHEST = lax.Precision.HIGHEST
NT_DIMS = (((1,), (1,)), ((), ()))
TN_DIMS = (((0,), (0,)), ((), ()))


def _params(sem, vmem=VMEM_LIMIT):
    return pltpu.CompilerParams(dimension_semantics=sem, vmem_limit_bytes=vmem)


def _bdot(a, b):
    return jnp.dot(a.astype(BF16), b.astype(BF16), preferred_element_type=F32)


def _split(x):
    hi = x.astype(BF16)
    return hi, (x - hi.astype(F32)).astype(BF16)


def _split_dot(x, m):
    hi, lo = _split(x)
    return jnp.dot(hi, m, preferred_element_type=F32) + jnp.dot(lo, m, preferred_element_type=F32)


def _tree(parts, op):
    while len(parts) > 1:
        parts = [op(parts[i], parts[i + 1]) for i in range(0, len(parts), 2)]
    return parts[0]


def _sigmoid(x):
    return 1.0 / (1.0 + jnp.exp(-x))


def _softplus(x):
    return jnp.maximum(x, 0.0) + jnp.log(1.0 + jnp.exp(-jnp.abs(x)))


def _rms(x, g, eps=NORM_EPS):
    return x * lax.rsqrt(jnp.mean(x * x, axis=-1, keepdims=True) + eps) * g


def _rope(x, cos, sin, group, half):
    w = x.shape[-1]
    first = (lax.broadcasted_iota(I32, (1, w), 1) & (group - 1)) < half
    partner = jnp.where(first, pltpu.roll(x, w - half, 1), pltpu.roll(x, half, 1))
    return x * cos + partner * sin


def _ada_kernel(c_ref, w_ref, b_ref, o_ref):
    c = c_ref[...]
    cond = c * _sigmoid(c)
    o_ref[0] = jnp.dot(cond, w_ref[0], preferred_element_type=F32, precision=HIGHEST) + b_ref[0]


def _ada(c, ada_w, ada_b):
    L, D, W = ada_w.shape
    B = c.shape[0]
    tn = 1024
    return pl.pallas_call(
        _ada_kernel,
        grid=(L, W // tn),
        in_specs=[pl.BlockSpec((B, D), lambda l, j: (0, 0)),
                  pl.BlockSpec((1, D, tn), lambda l, j: (l, 0, j)),
                  pl.BlockSpec((1, 1, tn), lambda l, j: (l, 0, j))],
        out_specs=pl.BlockSpec((1, B, tn), lambda l, j: (l, 0, j)),
        out_shape=jax.ShapeDtypeStruct((L, B, W), F32),
        compiler_params=_params(("parallel", "parallel")),
        name="ada_mod",
    )(c, ada_w, ada_b.reshape(L, 1, W))


def _combine(x_ref, g2_ref, routed_ref, y_refs):
    gates = routed_ref[...]
    moe = gates[:, TOP_K:TOP_K + 1] * y_refs[0][...].astype(F32)
    for j in range(1, len(y_refs)):
        moe = moe + gates[:, TOP_K + j:TOP_K + j + 1] * y_refs[j][...].astype(F32)
    return x_ref[...] + g2_ref[0] * moe


def _inproj_kernel(n_comb, x_ref, *refs):
    if n_comb:
        g2_ref, routed_ref, y_refs, refs = refs[0], refs[1], refs[2:2 + n_comb], refs[2 + n_comb:]
        g_ref, sc_ref, sh_ref, w_ref, x_o, ret_ref, rwkv_ref, dsa_ref, sb_ref = refs
        x = _combine(x_ref, g2_ref, routed_ref, y_refs)
        x_o[...] = x
    else:
        g_ref, sc_ref, sh_ref, w_ref, ret_ref, rwkv_ref, dsa_ref, sb_ref = refs
        x = x_ref[...]
    h = _rms(x, g_ref[...]) * (1.0 + sc_ref[0]) + sh_ref[0]
    hb = h.astype(BF16)
    o0, o1, o2 = RET_COLS, RET_COLS + RWKV_COLS, RET_COLS + RWKV_COLS + DSA_PAD
    ret_ref[...] = jnp.dot(hb, w_ref[:, 0:o0], preferred_element_type=F32)
    rwkv_ref[...] = jnp.dot(hb, w_ref[:, o0:o1], preferred_element_type=F32)
    dsa_ref[...] = jnp.dot(hb, w_ref[:, o1:o2], preferred_element_type=F32)
    sb_ref[...] = jnp.dot(hb, w_ref[:, o2:], preferred_element_type=F32).astype(BF16)


def _mod_row(per_b, j):
    return pl.BlockSpec((1, 1, D_MODEL), lambda i: ((i // per_b) * 6 + j, 0, 0))


def _inproj(x2, gain, mod6, w_cat, S, comb=None):
    N, D = x2.shape
    tm = ROW_TILE
    per_b = S // tm
    wt = w_cat.shape[1]
    row = lambda i: (i, 0)
    in_specs = [pl.BlockSpec((tm, D), row)]
    args = [x2]
    out_specs, out_shape = [], []
    if comb is not None:
        in_specs += ([_mod_row(per_b, 5), pl.BlockSpec((tm, LANES), row)]
                     + [pl.BlockSpec((tm, D), row)] * len(comb[2]))
        args += [comb[0], comb[1]] + list(comb[2])
        out_specs.append(pl.BlockSpec((tm, D), row))
        out_shape.append(jax.ShapeDtypeStruct((N, D), F32))
    in_specs += [pl.BlockSpec((1, D), lambda i: (0, 0)), _mod_row(per_b, 1), _mod_row(per_b, 0),
                 pl.BlockSpec((D, wt), lambda i: (0, 0))]
    args += [gain, mod6, mod6, w_cat]
    out_specs += [pl.BlockSpec((tm, RET_COLS), row), pl.BlockSpec((tm, RWKV_COLS), row),
                  pl.BlockSpec((tm, DSA_PAD), row), pl.BlockSpec((tm, SB_COLS), row)]
    out_shape += [jax.ShapeDtypeStruct((N, RET_COLS), F32), jax.ShapeDtypeStruct((N, RWKV_COLS), F32),
                  jax.ShapeDtypeStruct((N, DSA_PAD), F32), jax.ShapeDtypeStruct((N, SB_COLS), BF16)]
    return pl.pallas_call(
        functools.partial(_inproj_kernel, 0 if comb is None else len(comb[2])),
        grid=(N // tm,),
        in_specs=in_specs, out_specs=out_specs, out_shape=out_shape,
        compiler_params=_params(("parallel",)),
        name="in_proj",
    )(*args)


def _ret_kernel(q_ref, k_ref, v_ref, g_ref, cos_ref, sin_ref, din_ref, qd_ref, kd_ref, cd_ref, gn_ref,
                o_ref, state_ref, y_ref):
    @pl.when(pl.program_id(1) == 0)
    def _():
        state_ref[...] = jnp.zeros_like(state_ref)

    cos, sin = cos_ref[0], sin_ref[0]
    q = _rope(q_ref[0], cos, sin, HEAD_DIM, HEAD_DIM // 2)
    k = _rope(k_ref[0], cos, sin, HEAD_DIM, HEAD_DIM // 2) * HEAD_DIM ** -0.5
    v = v_ref[0]
    qd = q * qd_ref[...]
    kd = k * kd_ref[...]
    for h in range(N_HEADS_G):
        sl = slice(h * HEAD_DIM, (h + 1) * HEAD_DIM)
        qh, kh, vh = q[:, sl].astype(BF16), k[:, sl].astype(BF16), v[:, sl].astype(BF16)
        s = lax.dot_general(qh, kh, NT_DIMS, preferred_element_type=F32) * din_ref[h]
        inner = jnp.dot(s.astype(BF16), vh, preferred_element_type=F32)
        st = state_ref[h]
        cross = _bdot(qd[:, sl], st)
        state_ref[h] = st * cd_ref[:, sl] + lax.dot_general(
            kd[:, sl].astype(BF16), vh, TN_DIMS, preferred_element_type=F32)
        o = inner + cross
        oc = o - jnp.mean(o, axis=-1, keepdims=True)
        y_ref[:, sl] = oc * lax.rsqrt(jnp.mean(oc * oc, axis=-1, keepdims=True) + NORM_EPS)
    g = g_ref[0]
    o_ref[0] = (g * _sigmoid(g) * (y_ref[...] * gn_ref[...])).astype(BF16)


def _retention(ret3, tabs, consts, gn):
    B, S, _ = ret3.shape
    C = consts["din"].shape[1]
    blk = lambda j: pl.BlockSpec((1, C, GROUP_W), lambda b, c, j=j: (b, c, j))
    tab = pl.BlockSpec((1, C, GROUP_W), lambda b, c: (b, c, 0))
    const2 = lambda shape: pl.BlockSpec(shape, lambda b, c: (0,) * len(shape))
    return pl.pallas_call(
        _ret_kernel,
        grid=(B, S // C),
        in_specs=[blk(0), blk(1), blk(2), blk(3), tab, tab,
                  const2((N_HEADS_G, C, C)), const2((C, GROUP_W)), const2((C, GROUP_W)),
                  const2((1, GROUP_W)), const2((1, GROUP_W))],
        out_specs=pl.BlockSpec((1, C, GROUP_W), lambda b, c: (b, c, 0)),
        out_shape=jax.ShapeDtypeStruct((B, S, GROUP_W), BF16),
        scratch_shapes=[pltpu.VMEM((N_HEADS_G, HEAD_DIM, HEAD_DIM), F32), pltpu.VMEM((C, GROUP_W), F32)],
        compiler_params=_params(("parallel", "arbitrary")),
        name="retention",
    )(ret3, ret3, ret3, ret3, tabs[0], tabs[1],
      consts["din"], consts["qd"], consts["kd"], consts["cd"], gn)


def _ret_consts(C):
    H = N_HEADS_G
    lg = jnp.log(1.0 - 2.0 ** (-5.0 - jnp.arange(H, dtype=F32)))
    idx = jnp.arange(C, dtype=F32)
    diff = idx[:, None] - idx[None, :]
    din = jnp.where(diff >= 0, jnp.exp(lg[:, None, None] * jnp.maximum(diff, 0.0)), 0.0)
    q_dec = jnp.exp(lg[:, None] * (idx + 1.0))
    k_dec = jnp.exp(lg[:, None] * (C - 1.0 - idx))
    chunk_dec = jnp.exp(lg * C)
    wide = lambda t: jnp.repeat(t.T, HEAD_DIM, axis=1)
    return {"din": din, "qd": wide(q_dec), "kd": wide(k_dec),
            "cd": jnp.repeat(chunk_dec, HEAD_DIM)[None, :]}


def _rwkv_prep_kernel(per_b, z_ref, zp_ref, mu_ref, w0_ref, a0_ref, w2_ref, a2_ref, g2_ref, kk_ref, ka_ref,
                      rk_ref, hm_ref, r_o, w_o, k_o, v_o, nkk_o, b_o, gate_o, bonus_o):
    z = z_ref[...]
    first = (pl.program_id(0) % per_b) == 0
    prow = jnp.where(first, 0.0, zp_ref[7:8, :])
    rid = lax.broadcasted_iota(I32, z.shape, 0)
    prev = jnp.where(rid == 0, prow, pltpu.roll(z, 1, 0))
    f = z + (prev - z) * mu_ref[...]
    G = GROUP_W
    r, k, v, lo = f[:, 0:G], f[:, G:2 * G], f[:, 2 * G:3 * G], f[:, 3 * G:3 * G + 128]
    w_log = -_softplus(-(w0_ref[...] + _bdot(jnp.tanh(lo), w2_ref[...]))) - 0.5
    decay = jnp.exp(-jnp.exp(w_log))
    a = _sigmoid(a0_ref[...] + _bdot(lo, a2_ref[...]))
    gate = _bdot(_sigmoid(lo), g2_ref[...])
    hm = hm_ref[...]
    kk = k * kk_ref[...]
    kk = kk / jnp.maximum(jnp.sqrt(_split_dot(kk * kk, hm)), 1e-12)
    k2 = k * (1.0 + (a - 1.0) * ka_ref[...])
    r_o[...] = r
    w_o[...] = decay
    k_o[...] = k2
    v_o[...] = v
    nkk_o[...] = -kk
    b_o[...] = kk * a
    gate_o[...] = gate
    bonus_o[...] = _split_dot(r * k2 * rk_ref[...], hm) * v


def _rwkv_prep(z2, S, mu, w0, a0, w2p, a2p, g2p, kk, ka, rk, hm):
    N, W = z2.shape
    tm = ROW_TILE
    per_b = S // tm
    row = lambda i: (i, 0)
    c = lambda shape: pl.BlockSpec(shape, lambda i: (0, 0))
    G = GROUP_W
    return pl.pallas_call(
        functools.partial(_rwkv_prep_kernel, per_b),
        grid=(N // tm,),
        in_specs=[pl.BlockSpec((tm, W), row),
                  pl.BlockSpec((SUBLANES, W), lambda i: (jnp.maximum(i * (tm // SUBLANES) - 1, 0), 0)),
                  c((1, W)), c((1, G)), c((1, G)), c((128, G)), c((128, G)), c((128, G)),
                  c((1, G)), c((1, G)), c((1, G)), c((G, G))],
        out_specs=[pl.BlockSpec((tm, G), row)] * 8,
        out_shape=[jax.ShapeDtypeStruct((N, G), F32)] * 8,
        compiler_params=_params(("parallel",)),
        name="rwkv_prep",
    )(z2, z2, mu, w0, a0, w2p, a2p, g2p, kk, ka, rk, hm)


def _rwkv_scan_kernel(T, nkk_ref, w_ref, b_ref, k_ref, r_ref, v_ref, y_ref, s_ref, rep_ref):
    @pl.when(pl.program_id(0) == 0)
    def _():
        s_ref[...] = jnp.zeros_like(s_ref)

    ch = v_ref.shape[1]
    groups = HEAD_DIM // ch
    width = LANES // groups
    tree = lambda parts: _tree(parts, jnp.add)

    grp = lax.broadcasted_iota(I32, (T * ch, LANES), 1) // width
    for o, ref in enumerate((nkk_ref, w_ref, b_ref, k_ref, r_ref)):
        x = ref[...].reshape(T * ch, LANES)
        turned = [x] + [pltpu.roll(x, m * width, 1) for m in range(1, groups)]
        for j in range(groups):
            y = turned[0]
            for m in range(1, groups):
                y = jnp.where(grp == (j + m) % groups, turned[m], y)
            rep_ref[o, :, j] = y.reshape(T, ch, LANES)

    keys = [(j, kh) for j in range(groups) for kh in range(ch)]
    row = lambda o, t, j, kh: rep_ref[o, t, j, kh:kh + 1, :]

    def step(t, carry):
        vt = v_ref[t]
        sa = tree([s_ref[j * ch + kh] * row(0, t, j, kh) for j, kh in keys])
        ys = []
        for j, kh in keys:
            s_new = (s_ref[j * ch + kh] * row(1, t, j, kh) + sa * row(2, t, j, kh) + vt * row(3, t, j, kh))
            s_ref[j * ch + kh] = s_new
            ys.append(s_new * row(4, t, j, kh))
        y_ref[t] = tree(ys)
        return carry

    lax.fori_loop(0, T, step, 0)


def _rwkv_scan(nkk, w, b, k, r, v):
    S, ch, _ = v.shape
    T = 32
    spec = pl.BlockSpec((T, ch, LANES), lambda i: (i, 0, 0))
    return pl.pallas_call(
        functools.partial(_rwkv_scan_kernel, T),
        grid=(S // T,),
        in_specs=[spec] * 6,
        out_specs=spec,
        out_shape=jax.ShapeDtypeStruct((S, ch, LANES), F32),
        scratch_shapes=[pltpu.VMEM((HEAD_DIM, ch, LANES), F32),
                        pltpu.VMEM((5, T, HEAD_DIM // ch, ch, LANES), F32)],
        compiler_params=_params(("arbitrary",)),
        name="rwkv_scan",
    )(nkk, w, b, k, r, v)


def _to_v_layout(x, B, S):
    P = B * N_HEADS_G
    rep = LANES // P
    t = x.reshape(B, S, N_HEADS_G, HEAD_DIM // rep, rep).transpose(1, 3, 4, 0, 2)
    return t.reshape(S, HEAD_DIM // rep, LANES)


def _from_v_layout(y, B, S):
    P = B * N_HEADS_G
    rep = LANES // P
    t = y.reshape(S, HEAD_DIM // rep, rep, B, N_HEADS_G).transpose(3, 0, 4, 1, 2)
    return t.reshape(B * S, GROUP_W)


def _dsa_prep_kernel(f_ref, qn_ref, wq_ref, wqi_ref, cq_ref, sq_ref, ci_ref, si_ref,
                     qt_o, qit_o, wt_o, k_o, v_o, ki_o):
    f = f_ref[...]
    cq = _rms(f[:, 0:DSA_Q_LORA], qn_ref[...]).astype(BF16)
    cos_q, sin_q = cq_ref[...], sq_ref[...]
    cos_i, sin_i = ci_ref[...], si_ref[...]
    q = _rope(jnp.dot(cq, wq_ref[...], preferred_element_type=F32), cos_q, sin_q, HEAD_DIM, HEAD_DIM // 8)
    tm = f.shape[0]
    qt_o[...] = (q * HEAD_DIM ** -0.5).T.reshape(N_HEADS_G, HEAD_DIM, tm).astype(BF16)
    qi = _rope(jnp.dot(cq, wqi_ref[...], preferred_element_type=F32), cos_i, sin_i, IDX_DIM, IDX_DIM // 8)
    qit_o[...] = qi.T.reshape(IDX_HEADS, IDX_DIM, tm).astype(BF16)
    kv = f[:, 128:256]
    kv_r = _rope(kv, cos_q[:, 0:128], sin_q[:, 0:128], HEAD_DIM, HEAD_DIM // 8)
    k_o[...] = kv_r[:, 0:HEAD_DIM].astype(BF16)
    v_o[...] = kv[:, HEAD_DIM:128].astype(BF16)
    tail = f[:, 256:384]
    tail_r = _rope(tail, cos_i[:, 0:128], sin_i[:, 0:128], IDX_DIM, IDX_DIM // 8)
    ki_o[...] = tail_r[:, 0:IDX_DIM].astype(BF16)
    wt_o[...] = tail.T[IDX_DIM:IDX_DIM + IDX_HEADS, :] * (IDX_HEADS ** -0.5 * IDX_DIM ** -0.5)


def _dsa_prep(f2, qn, wq, wqi, tq, ti):
    N, W = f2.shape
    tm = ROW_TILE
    G = GROUP_W
    row = lambda w: pl.BlockSpec((tm, w), lambda i: (i, 0))
    col = lambda h: pl.BlockSpec((h, tm), lambda i: (0, i))
    slab = lambda h, d: pl.BlockSpec((h, d, tm), lambda i: (0, 0, i))
    c = lambda shape: pl.BlockSpec(shape, lambda i: (0, 0))
    return pl.pallas_call(
        _dsa_prep_kernel,
        grid=(N // tm,),
        in_specs=[row(W), c((1, DSA_Q_LORA)), c((DSA_Q_LORA, G)), c((DSA_Q_LORA, G))] + [row(G)] * 4,
        out_specs=[slab(N_HEADS_G, HEAD_DIM), slab(IDX_HEADS, IDX_DIM), col(IDX_HEADS),
                   row(HEAD_DIM), row(HEAD_DIM), row(IDX_DIM)],
        out_shape=[jax.ShapeDtypeStruct((N_HEADS_G, HEAD_DIM, N), BF16),
                   jax.ShapeDtypeStruct((IDX_HEADS, IDX_DIM, N), BF16),
                   jax.ShapeDtypeStruct((IDX_HEADS, N), F32),
                   jax.ShapeDtypeStruct((N, HEAD_DIM), BF16), jax.ShapeDtypeStruct((N, HEAD_DIM), BF16),
                   jax.ShapeDtypeStruct((N, IDX_DIM), BF16)],
        compiler_params=_params(("parallel",)),
        name="dsa_prep",
    )(f2, qn, wq, wqi, *tq, *ti)


def _float_key(bits):
    return bits ^ ((bits >> 31) & 0x7FFFFFFF)


def _dsa_kernel(TK, topk, qt_ref, qit_ref, wt_ref, k_ref, v_ref, ki_ref, on_ref, o_ref,
                skey_ref, lg_ref, out_ref):
    i = pl.program_id(1)
    TQ = Q_BLOCK
    n_kt = (i * TQ + TQ + TK - 1) // TK
    q_all = jnp.concatenate([qt_ref[h] for h in range(N_HEADS_G)], axis=1)
    qi_all = jnp.concatenate([qit_ref[h] for h in range(IDX_HEADS)], axis=1)
    wt = wt_ref[...]
    kpos = lax.broadcasted_iota(I32, (TK, TQ), 0)
    qpos = i * TQ + lax.broadcasted_iota(I32, (TK, TQ), 1)
    head = lambda x, h: x[:, h * TQ:(h + 1) * TQ]

    def fold(x, op):
        return _tree([x[r * SUBLANES:(r + 1) * SUBLANES] for r in range(TK // SUBLANES)], op)

    def score_tile(kt, tops):
        off = pl.multiple_of(kt * TK, TK)
        rel = jnp.dot(ki_ref[0, pl.ds(off, TK), :], qi_all, preferred_element_type=F32)
        sc = jnp.zeros((TK, TQ), F32)
        for h in range(IDX_HEADS):
            sc = sc + jnp.maximum(head(rel, h), 0.0) * wt[h:h + 1, :]
        key = _float_key(pltpu.bitcast(sc, I32))
        valid = kt * TK + kpos <= qpos
        skey_ref[kt] = jnp.where(valid, key, INT_MIN)
        lg = jnp.dot(k_ref[0, pl.ds(off, TK), :], q_all, preferred_element_type=F32)
        lg_ref[kt] = lg
        return tuple(jnp.maximum(tops[h], fold(jnp.where(valid, head(lg, h), -1e30), jnp.maximum))
                     for h in range(N_HEADS_G))

    lowest = tuple(jnp.full((SUBLANES, TQ), -1e30, F32) for _ in range(N_HEADS_G))
    tops = lax.fori_loop(0, n_kt, score_tile, lowest)

    def count(pred):
        def count_tile(kt, acc):
            return acc + fold(jnp.where(pred(kt, skey_ref[kt]), 1.0, 0.0), jnp.add)

        acc = lax.fori_loop(0, n_kt, count_tile, jnp.zeros((SUBLANES, TQ), F32))
        return jnp.sum(acc, axis=0, keepdims=True)

    def bit_step(j, carry):
        thr, n_thr = carry
        cand = thr + lax.shift_left(jnp.int32(1), 31 - j)
        n = count(lambda kt, keys: keys >= cand)
        return jnp.where(n >= topk, cand, thr), jnp.where(n >= topk, n, n_thr)

    thr, n_thr = lax.fori_loop(0, 32, bit_step,
                               (jnp.full((1, TQ), INT_MIN, I32), jnp.full((1, TQ), float(topk), F32)))
    thr = jnp.maximum(thr, INT_MIN + 1)

    @pl.when(jnp.max(n_thr) > topk)
    def _():
        keep = topk - count(lambda kt, keys: keys > thr)

        pos_bits = (skey_ref.shape[0] * TK - 1).bit_length()

        def pos_step(j, last):
            cand = last + lax.shift_left(jnp.int32(1), pos_bits - 1 - j)
            n = count(lambda kt, keys: jnp.logical_and(keys == thr, kt * TK + kpos < cand))
            return jnp.where(n < keep, cand, last)

        last = lax.fori_loop(0, pos_bits, pos_step, jnp.zeros((1, TQ), I32))

        def retire(kt, carry):
            keys = skey_ref[kt]
            skey_ref[kt] = jnp.where(jnp.logical_and(keys == thr, kt * TK + kpos > last), INT_MIN, keys)
            return carry

        lax.fori_loop(0, n_kt, retire, 0)

    def sweep(shifts):
        def acc_tile(kt, carry):
            ls, acc = carry
            sel = skey_ref[kt] >= thr
            ps = [jnp.where(sel, jnp.exp(lg_ref[kt, :, h * TQ:(h + 1) * TQ] - shifts[h]), 0.0)
                  for h in range(N_HEADS_G)]
            p_all = jnp.concatenate([p.astype(BF16) for p in ps], axis=1)
            vb = v_ref[0, pl.ds(pl.multiple_of(kt * TK, TK), TK), :]
            return (tuple(ls[h] + fold(ps[h], jnp.add) for h in range(N_HEADS_G)),
                    acc + lax.dot_general(vb, p_all, TN_DIMS, preferred_element_type=F32))

        ls, acc = lax.fori_loop(0, n_kt, acc_tile,
                                (tuple(jnp.zeros((SUBLANES, TQ), F32) for _ in range(N_HEADS_G)),
                                 jnp.zeros((HEAD_DIM, N_HEADS_G * TQ), F32)))
        ls = [jnp.sum(l, axis=0, keepdims=True) for l in ls]
        for h in range(N_HEADS_G):
            out_ref[h * HEAD_DIM:(h + 1) * HEAD_DIM, :] = head(acc, h) / ls[h]
        return jnp.min(jnp.minimum(jnp.minimum(ls[0], ls[1]), jnp.minimum(ls[2], ls[3])))

    smallest = sweep([jnp.max(t, axis=0, keepdims=True) for t in tops])

    @pl.when(smallest < SOFTMAX_FLOOR)
    def _():
        def max_tile(kt, ms):
            sel = skey_ref[kt] >= thr
            return tuple(jnp.maximum(ms[h], fold(jnp.where(sel, lg_ref[kt, :, h * TQ:(h + 1) * TQ], -1e30),
                                                 jnp.maximum))
                         for h in range(N_HEADS_G))

        ms = lax.fori_loop(0, n_kt, max_tile, lowest)
        sweep([jnp.max(m, axis=0, keepdims=True) for m in ms])

    o_ref[0] = _rms(out_ref[...].T, on_ref[...]).astype(BF16)


def _dsa_attention(qt, qit, wt, k, v, ki, on):
    B, S, _ = k.shape
    G = GROUP_W
    nq = S // Q_BLOCK
    TK = min(512, S)
    topk = min(DSA_TOPK_MAX, S // 4)
    qcol = lambda h: pl.BlockSpec((h, Q_BLOCK), lambda b, i: (0, b * nq + i))
    qslab = lambda h, d: pl.BlockSpec((h, d, Q_BLOCK), lambda b, i: (0, 0, b * nq + i))
    full = lambda wd: pl.BlockSpec((1, S, wd), lambda b, i: (b, 0, 0))
    return pl.pallas_call(
        functools.partial(_dsa_kernel, TK, topk),
        grid=(B, nq),
        in_specs=[qslab(N_HEADS_G, HEAD_DIM), qslab(IDX_HEADS, IDX_DIM), qcol(IDX_HEADS),
                  full(HEAD_DIM), full(HEAD_DIM), full(IDX_DIM), pl.BlockSpec((1, G), lambda b, i: (0, 0))],
        out_specs=pl.BlockSpec((1, Q_BLOCK, G), lambda b, i: (b, i, 0)),
        out_shape=jax.ShapeDtypeStruct((B, S, G), BF16),
        scratch_shapes=[pltpu.VMEM((S // TK, TK, Q_BLOCK), I32),
                        pltpu.VMEM((S // TK, TK, N_HEADS_G * Q_BLOCK), F32),
                        pltpu.VMEM((G, Q_BLOCK), F32)],
        compiler_params=_params(("parallel", "arbitrary")),
        name="dsa_attention",
    )(qt, qit, wt, k, v, ki, on)


def _sb_kernel(q_ref, k_ref, v_ref, u_ref, on_ref, o_ref, z_ref, lm_ref):
    i = pl.program_id(1)
    T = Q_BLOCK
    H, G = N_HEADS_G, GROUP_W
    u = u_ref[...]
    qt = q_ref[0].astype(F32).T.astype(BF16)
    q_rows = lax.broadcasted_iota(I32, (G, H * T), 0) // HEAD_DIM
    q_cols = lax.broadcasted_iota(I32, (G, H * T), 1) // T
    qbd = jnp.where(q_rows == q_cols, jnp.concatenate([qt] * H, axis=1), jnp.zeros((), BF16))
    kid = lax.broadcasted_iota(I32, (T, H * T), 0)
    qid = lax.broadcasted_iota(I32, (T, H * T), 1) & (T - 1)
    strict = kid < qid

    def cond(carry):
        kt, c, _ = carry
        return jnp.logical_and(kt >= 0, jnp.max(c) > SB_UNDERFLOW)

    def logits(kt):
        z = jnp.dot(k_ref[0, pl.ds(pl.multiple_of(kt * T, T), T), :], qbd, preferred_element_type=F32)
        z = z * HEAD_DIM ** -0.5
        return z, -_softplus(z)

    z0, lm0 = logits(i)
    z_ref[...] = z0
    lm_ref[...] = jnp.where(strict, lm0, 0.0)

    def body(carry):
        kt, c, acc = carry
        off = pl.multiple_of(kt * T, T)
        mask = jnp.logical_or(strict, kt < i)
        z, lm = z_ref[...], lm_ref[...]
        z_ref[...], lm_ref[...] = logits(jnp.maximum(kt - 1, 0))
        hi, lo = _split(lm)
        later = jnp.dot(u, jnp.concatenate([hi, lo], axis=1), preferred_element_type=F32)
        later = later[:, :H * T] + later[:, H * T:]
        a = jnp.where(mask, jnp.exp(z + lm + (c + later)), 0.0)
        av = lax.dot_general(v_ref[0, pl.ds(off, T), :], a.astype(BF16), TN_DIMS, preferred_element_type=F32)
        acc = acc + jnp.concatenate(
            [av[h * HEAD_DIM:(h + 1) * HEAD_DIM, h * T:(h + 1) * T] for h in range(H)], axis=0)
        return kt - 1, c + jnp.sum(lm, axis=0, keepdims=True), acc

    _, _, acc = lax.while_loop(cond, body, (i, jnp.zeros((1, H * T), F32), jnp.zeros((G, T), F32)))
    o_ref[0] = _rms(acc.T, on_ref[...]).astype(BF16)


def _stick_breaking(sb3, u, on):
    B, S, _ = sb3.shape
    G = GROUP_W
    return pl.pallas_call(
        _sb_kernel,
        grid=(B, S // Q_BLOCK),
        in_specs=[pl.BlockSpec((1, Q_BLOCK, G), lambda b, i: (b, i, 0)),
                  pl.BlockSpec((1, S, G), lambda b, i: (b, 0, 1)),
                  pl.BlockSpec((1, S, G), lambda b, i: (b, 0, 2)),
                  pl.BlockSpec((Q_BLOCK, Q_BLOCK), lambda b, i: (0, 0)),
                  pl.BlockSpec((1, G), lambda b, i: (0, 0))],
        out_specs=pl.BlockSpec((1, Q_BLOCK, G), lambda b, i: (b, i, 0)),
        out_shape=jax.ShapeDtypeStruct((B, S, G), BF16),
        scratch_shapes=[pltpu.VMEM((Q_BLOCK, N_HEADS_G * Q_BLOCK), F32)] * 2,
        compiler_params=_params(("parallel", "arbitrary")),
        name="stick_breaking",
    )(sb3, sb3, sb3, u, on)


def _outproj_kernel(yr_ref, yd_ref, ys_ref, yw_ref, bonus_ref, gate_ref, ln_ref, hm_ref, x_ref, g1_ref, sc_ref,
                    sh_ref, gn_ref, wo_ref, rw_ref, rb_ref, x_o, h_o, lg_o):
    G = GROUP_W
    y = yw_ref[...]
    hm = hm_ref[...]
    yc = y - _split_dot(y, hm) * (1.0 / HEAD_DIM)
    var = _split_dot(yc * yc, hm) * (1.0 / HEAD_DIM)
    yw = ((yc * lax.rsqrt(var + RWKV_LN_EPS) * ln_ref[...] + bonus_ref[...]) * gate_ref[...]).astype(BF16)
    dot = lambda yv, g: jnp.dot(yv, wo_ref[g * G:(g + 1) * G, :], preferred_element_type=F32)
    mixed = dot(yr_ref[...], 0) + dot(yw, 1) + dot(yd_ref[...], 2) + dot(ys_ref[...], 3)
    x1 = x_ref[...] + g1_ref[0] * mixed
    x_o[...] = x1
    h = _rms(x1, gn_ref[...]) * (1.0 + sc_ref[0]) + sh_ref[0]
    h_o[...] = h
    hi, lo = _split(h)
    work = (jnp.dot(hi, rw_ref[0], preferred_element_type=F32) + jnp.dot(lo, rw_ref[0], preferred_element_type=F32)
            + jnp.dot(hi, rw_ref[1], preferred_element_type=F32) + rb_ref[...])
    col = lax.broadcasted_iota(I32, work.shape, 1).astype(F32)
    out = jnp.zeros_like(work)
    vals = []
    for j in range(TOP_K):
        m = jnp.max(work, axis=-1, keepdims=True)
        idx = jnp.min(jnp.where(work == m, col, float(LANES)), axis=-1, keepdims=True)
        out = jnp.where(col == float(j), idx, out)
        vals.append(m)
        work = jnp.where(col == idx, -jnp.inf, work)
    es = [jnp.exp(v - vals[0]) for v in vals]
    den = es[0] + es[1] + es[2] + es[3]
    for j in range(TOP_K):
        out = jnp.where(col == float(TOP_K + j), es[j] / den, out)
    lg_o[...] = out


def _outproj(ys, rwkv, x2, mod6, gain, w_out, router_w, router_b, S):
    N, D = x2.shape
    tm = ROW_TILE
    per_b = S // tm
    G = GROUP_W
    row = lambda w: pl.BlockSpec((tm, w), lambda i: (i, 0))
    c = lambda shape: pl.BlockSpec(shape, lambda i: (0, 0))
    return pl.pallas_call(
        _outproj_kernel,
        grid=(N // tm,),
        in_specs=[row(G)] * 6 + [c((1, G)), c((G, G)),
                                 row(D), _mod_row(per_b, 2), _mod_row(per_b, 4), _mod_row(per_b, 3),
                                 c((1, D)), c((D, D)), pl.BlockSpec((2, D, LANES), lambda i: (0, 0, 0)),
                                 c((1, LANES))],
        out_specs=[row(D), row(D), row(LANES)],
        out_shape=[jax.ShapeDtypeStruct((N, D), F32), jax.ShapeDtypeStruct((N, D), F32),
                   jax.ShapeDtypeStruct((N, LANES), F32)],
        compiler_params=_params(("parallel",)),
        name="out_proj_router",
    )(*ys, *rwkv, x2, mod6, mod6, mod6, gain, w_out,
      jnp.pad(jnp.stack(_split(router_w)), ((0, 0), (0, 0), (0, LANES - N_EXPERTS))),
      jnp.pad(router_b, ((0, 0), (0, LANES - N_EXPERTS)), constant_values=-1e30))


def _moe_kernel(be_ref, nu_ref, x_ref, w1_ref, b1_ref, w2_ref, b2_ref, o_ref):
    i = pl.program_id(0)

    @pl.when(i < nu_ref[0])
    def _():
        xb = x_ref[...].astype(BF16)
        w1 = w1_ref.at[0, 0]
        glu = jnp.dot(xb, w1[:, 0:D_FF].astype(BF16), preferred_element_type=F32) + b1_ref[0, :, 0:D_FF]
        lin = jnp.dot(xb, w1[:, D_FF:].astype(BF16), preferred_element_type=F32) + b1_ref[0, :, D_FF:]
        glu = jnp.minimum(glu, SWIGLU_LIMIT)
        lin = jnp.clip(lin, -SWIGLU_LIMIT, SWIGLU_LIMIT)
        act = glu * _sigmoid(SWIGLU_ALPHA * glu) * (lin + 1.0)
        y = jnp.dot(act.astype(BF16), w2_ref[0, 0].astype(BF16), preferred_element_type=F32) + b2_ref[0]
        o_ref[...] = y.astype(BF16)

    @pl.when(i >= nu_ref[0])
    def _():
        o_ref[...] = jnp.zeros_like(o_ref)


def _moe_ffn(block_expert, n_used, xs, layer, w1, b1, w2, b2):
    rows, D = xs.shape
    nb = rows // MOE_BLOCK
    L, E = w1.shape[:2]
    live = lambda i, nu: jnp.minimum(i, nu[0] - 1)
    grid_spec = pltpu.PrefetchScalarGridSpec(
        num_scalar_prefetch=2,
        grid=(nb,),
        in_specs=[pl.BlockSpec((MOE_BLOCK, D), lambda i, be, nu: (live(i, nu), 0)),
                  pl.BlockSpec((1, 1, D, 2 * D_FF), lambda i, be, nu: (layer, be[i], 0, 0)),
                  pl.BlockSpec((1, 1, 2 * D_FF), lambda i, be, nu: (layer * E + be[i], 0, 0)),
                  pl.BlockSpec((1, 1, D_FF, D), lambda i, be, nu: (layer, be[i], 0, 0)),
                  pl.BlockSpec((1, 1, D), lambda i, be, nu: (layer * E + be[i], 0, 0))],
        out_specs=pl.BlockSpec((MOE_BLOCK, D), lambda i, be, nu: (i, 0)),
    )
    return pl.pallas_call(
        _moe_kernel,
        grid_spec=grid_spec,
        out_shape=jax.ShapeDtypeStruct((rows, D), BF16),
        compiler_params=_params(("arbitrary",), 56 * 1024 * 1024),
        name="moe_ffn",
    )(block_expert, n_used, xs, w1, b1.reshape(L * E, 1, -1), w2, b2.reshape(L * E, 1, -1))


def _route(routed):
    n = routed.shape[0]
    expert = routed[:, 0:TOP_K].reshape(-1).astype(I32)
    n_assign = n * TOP_K
    ids = jnp.arange(n_assign, dtype=I32)
    _, order = lax.sort((expert, ids), num_keys=1, is_stable=True)
    _, inv = lax.sort((order, ids), num_keys=1)
    counts = jnp.sum((expert[:, None] == jnp.arange(N_EXPERTS, dtype=I32)[None, :]).astype(I32), axis=0)
    starts = jnp.cumsum(counts) - counts
    padded = (counts + MOE_BLOCK - 1) // MOE_BLOCK * MOE_BLOCK
    pad_end = jnp.cumsum(padded)
    pad_start = pad_end - padded
    dest = pad_start[expert] + inv - starts[expert]
    n_blocks = -(-n_assign // MOE_BLOCK) + N_EXPERTS
    block_start = jnp.arange(n_blocks, dtype=I32) * MOE_BLOCK
    block_expert = jnp.minimum(jnp.sum(pad_end[None, :] <= block_start[:, None], axis=1), N_EXPERTS - 1)
    block_expert = block_expert.astype(I32)
    per_row = lambda t: jnp.repeat(t[block_expert], MOE_BLOCK)
    off = jnp.arange(n_blocks * MOE_BLOCK, dtype=I32) - per_row(pad_start)
    valid = off < per_row(counts)
    src = order[jnp.clip(per_row(starts) + off, 0, n_assign - 1)]
    row_token = jnp.where(valid, src // TOP_K, 0)
    n_used = (pad_end[-1] // MOE_BLOCK).astype(I32).reshape(1)
    return dest.reshape(n, TOP_K), row_token, block_expert, n_used


def _rows(t, idx):
    return t.at[idx].get(mode="promise_in_bounds")


def _final_kernel(x_ref, g2_ref, routed_ref, y0, y1, y2, y3, g_ref, o_ref):
    o_ref[...] = _rms(_combine(x_ref, g2_ref, routed_ref, (y0, y1, y2, y3)), g_ref[...])


def _final_norm(x2, mod6, routed, ys, g, S):
    N, D = x2.shape
    tm = ROW_TILE
    row = pl.BlockSpec((tm, D), lambda i: (i, 0))
    return pl.pallas_call(
        _final_kernel, grid=(N // tm,),
        in_specs=[row, _mod_row(S // tm, 5), pl.BlockSpec((tm, LANES), lambda i: (i, 0))] + [row] * 4
        + [pl.BlockSpec((1, D), lambda i: (0, 0))],
        out_specs=row,
        out_shape=jax.ShapeDtypeStruct((N, D), F32),
        compiler_params=_params(("parallel",)), name="final_norm",
    )(x2, mod6, routed, *ys, g)


def _rope_tables(pos, group, rot_dim, theta, width):
    half = rot_dim // 2
    inv = theta ** (-jnp.arange(half, dtype=F32) / half)
    ang = pos.astype(F32)[..., None] * inv
    cos, sin = jnp.cos(ang), jnp.sin(ang)
    rest = group - rot_dim
    ones = jnp.ones(ang.shape[:2] + (rest,), F32)
    zeros = jnp.zeros(ang.shape[:2] + (rest,), F32)
    rep = width // group
    tile = lambda t: jnp.tile(t, (1, 1, rep)).reshape(-1, width)
    return (tile(jnp.concatenate([cos, cos, ones], -1)), tile(jnp.concatenate([-sin, sin, zeros], -1)))


def _pad_rows(w, start, total=128):
    return jnp.zeros((total, w.shape[1]), w.dtype).at[start:start + w.shape[0]].set(w)


def kernel(x, c, positions, ada_w, ada_b, norm_mix, norm_ffn, w_in, ret_gn, rwkv_mu, rwkv_w0, rwkv_w2, rwkv_a0,
           rwkv_a2, rwkv_g2, rwkv_kk, rwkv_ka, rwkv_rk, rwkv_ln, dsa_qnorm, dsa_wq_up, dsa_wqi_up, dsa_onorm,
           sb_onorm, w_out, router_w, router_b, moe_w1, moe_b1, moe_w2, moe_b2, norm_final):
    B, S, D = x.shape
    N = B * S
    L = ada_w.shape[0]
    G = GROUP_W
    row2 = lambda t: t.reshape(1, -1)

    mod = _ada(c, ada_w, ada_b)
    ret_tabs = tuple(t.reshape(B, S, G) for t in _rope_tables(positions, HEAD_DIM, HEAD_DIM, RET_THETA, G))
    dq_tabs = _rope_tables(positions, HEAD_DIM, HEAD_DIM // 4, ROPE_THETA, G)
    di_tabs = _rope_tables(positions, IDX_DIM, IDX_DIM // 4, ROPE_THETA, G)
    ret_c = _ret_consts(min(256, S))
    hm = (jnp.arange(G)[:, None] // HEAD_DIM == jnp.arange(G)[None, :] // HEAD_DIM).astype(BF16)
    u_later = (jnp.arange(Q_BLOCK)[None, :] > jnp.arange(Q_BLOCK)[:, None]).astype(BF16)
    o1, o2 = RET_COLS + RWKV_COLS, RET_COLS + RWKV_COLS + DSA_COLS

    x2 = x.reshape(N, D)
    comb = None
    for l in range(L):
        mod6 = mod[l].reshape(B * 6, 1, D)
        w_l = w_in[l]
        w_cat = jnp.concatenate(
            [w_l[:, :o1], w_l[:, o1:o2], jnp.zeros((D, DSA_PAD - DSA_COLS), F32), w_l[:, o2:]], axis=1).astype(BF16)
        outs = _inproj(x2, row2(norm_mix[l]), mod6, w_cat, S, comb)
        if comb is not None:
            x2, outs = outs[0], outs[1:]
        ret, rwkv, dsa, sb = outs

        prep = _rwkv_prep(rwkv, S, row2(rwkv_mu[l]), row2(rwkv_w0[l]), row2(rwkv_a0[l]),
                          _pad_rows(rwkv_w2[l], 0).astype(BF16), _pad_rows(rwkv_a2[l], 32).astype(BF16),
                          _pad_rows(rwkv_g2[l], 64).astype(BF16), row2(rwkv_kk[l]), row2(rwkv_ka[l]),
                          row2(rwkv_rk[l]), hm)
        r_, w_, k_, v_, nkk_, b_, gate_, bonus_ = prep
        scan_in = tuple(_to_v_layout(t, B, S) for t in (nkk_, w_, b_, k_, r_, v_))

        y_ret = _retention(ret.reshape(B, S, RET_COLS), ret_tabs, ret_c, row2(ret_gn[l])).reshape(N, G)

        dqt, dqit, dwt, dk, dv, dki = _dsa_prep(dsa, row2(dsa_qnorm[l]), dsa_wq_up[l].astype(BF16),
                                                dsa_wqi_up[l].astype(BF16), dq_tabs, di_tabs)
        y_dsa = _dsa_attention(dqt, dqit, dwt, dk.reshape(B, S, HEAD_DIM), dv.reshape(B, S, HEAD_DIM),
                               dki.reshape(B, S, IDX_DIM), row2(dsa_onorm[l])).reshape(N, G)

        y_sb = _stick_breaking(sb.reshape(B, S, SB_COLS), u_later, row2(sb_onorm[l])).reshape(N, G)

        y_scan = _rwkv_scan(*scan_in)
        rwkv_tail = (_from_v_layout(y_scan, B, S), bonus_, gate_, row2(rwkv_ln[l]), hm)

        x1, h2, routed = _outproj((y_ret, y_dsa, y_sb), rwkv_tail, x2, mod6, row2(norm_ffn[l]),
                                  w_out[l].astype(BF16), router_w[l], row2(router_b[l]), S)

        dest, row_token, block_expert, n_used = _route(routed)
        y = _moe_ffn(block_expert, n_used, _rows(h2, row_token), l, moe_w1, moe_b1, moe_w2, moe_b2)
        x2 = x1
        comb = (mod6, routed, tuple(_rows(y, dest[:, j]) for j in range(TOP_K)))
    return _final_norm(x2, *comb, row2(norm_final), S).reshape(B, S, D)
```

```python
import functools

import jax
import jax.numpy as jnp
from jax import lax
from jax.experimental import pallas as pl
from jax.experimental.pallas import tpu as pltpu

F32 = jnp.float32
BF16 = jnp.bfloat16
I32 = jnp.int32

D_MODEL = 1024
GROUP_W = 256
HEAD_DIM = 64
N_HEADS_G = 4
NORM_EPS = 1e-5
Q_BLOCK = 128
RET_THETA = 10000.0
RWKV_LN_EPS = 64e-5
DSA_Q_LORA = 128
IDX_HEADS = 8
IDX_DIM = 32
DSA_TOPK_MAX = 256
ROPE_THETA = 500000.0
N_EXPERTS = 32
TOP_K = 4
D_FF = D_MODEL
SWIGLU_ALPHA = 1.702
SWIGLU_LIMIT = 7.0
MOE_BLOCK = 512

RET_COLS = 4 * GROUP_W
RWKV_COLS = 3 * GROUP_W + 128
DSA_COLS = 296
DSA_PAD = 384
SB_COLS = 3 * GROUP_W

LANES = 128
SUBLANES = 8
ROW_TILE = 512
VMEM_LIMIT = 48 * 1024 * 1024
INT_MIN = -2 ** 31
SB_UNDERFLOW = -104.0
SOFTMAX_FLOOR = 1e-20

HIGHEST = lax.Precision.HIGHEST
NT_DIMS = (((1,), (1,)), ((), ()))
TN_DIMS = (((0,), (0,)), ((), ()))


def _params(sem, vmem=VMEM_LIMIT):
    return pltpu.CompilerParams(dimension_semantics=sem, vmem_limit_bytes=vmem)


def _bdot(a, b):
    return jnp.dot(a.astype(BF16), b.astype(BF16), preferred_element_type=F32)


def _split(x):
    hi = x.astype(BF16)
    return hi, (x - hi.astype(F32)).astype(BF16)


def _split_dot(x, m):
    hi, lo = _split(x)
    return jnp.dot(hi, m, preferred_element_type=F32) + jnp.dot(lo, m, preferred_element_type=F32)


def _tree(parts, op):
    while len(parts) > 1:
        parts = [op(parts[i], parts[i + 1]) for i in range(0, len(parts), 2)]
    return parts[0]


def _sigmoid(x):
    return 1.0 / (1.0 + jnp.exp(-x))


def _softplus(x):
    return jnp.maximum(x, 0.0) + jnp.log(1.0 + jnp.exp(-jnp.abs(x)))


def _rms(x, g, eps=NORM_EPS):
    return x * lax.rsqrt(jnp.mean(x * x, axis=-1, keepdims=True) + eps) * g


def _rope(x, cos, sin, group, half):
    w = x.shape[-1]
    first = (lax.broadcasted_iota(I32, (1, w), 1) & (group - 1)) < half
    partner = jnp.where(first, pltpu.roll(x, w - half, 1), pltpu.roll(x, half, 1))
    return x * cos + partner * sin


def _ada_kernel(c_ref, w_ref, b_ref, o_ref):
    c = c_ref[...]
    cond = c * _sigmoid(c)
    o_ref[0] = jnp.dot(cond, w_ref[0], preferred_element_type=F32, precision=HIGHEST) + b_ref[0]


def _ada(c, ada_w, ada_b):
    L, D, W = ada_w.shape
    B = c.shape[0]
    tn = 1024
    return pl.pallas_call(
        _ada_kernel,
        grid=(L, W // tn),
        in_specs=[pl.BlockSpec((B, D), lambda l, j: (0, 0)),
                  pl.BlockSpec((1, D, tn), lambda l, j: (l, 0, j)),
                  pl.BlockSpec((1, 1, tn), lambda l, j: (l, 0, j))],
        out_specs=pl.BlockSpec((1, B, tn), lambda l, j: (l, 0, j)),
        out_shape=jax.ShapeDtypeStruct((L, B, W), F32),
        compiler_params=_params(("parallel", "parallel")),
        name="ada_mod",
    )(c, ada_w, ada_b.reshape(L, 1, W))


def _combine(x_ref, g2_ref, routed_ref, y_refs):
    gates = routed_ref[...]
    moe = gates[:, TOP_K:TOP_K + 1] * y_refs[0][...].astype(F32)
    for j in range(1, len(y_refs)):
        moe = moe + gates[:, TOP_K + j:TOP_K + j + 1] * y_refs[j][...].astype(F32)
    return x_ref[...] + g2_ref[0] * moe


def _inproj_kernel(n_comb, x_ref, *refs):
    if n_comb:
        g2_ref, routed_ref, y_refs, refs = refs[0], refs[1], refs[2:2 + n_comb], refs[2 + n_comb:]
        g_ref, sc_ref, sh_ref, w_ref, x_o, ret_ref, rwkv_ref, dsa_ref, sb_ref = refs
        x = _combine(x_ref, g2_ref, routed_ref, y_refs)
        x_o[...] = x
    else:
        g_ref, sc_ref, sh_ref, w_ref, ret_ref, rwkv_ref, dsa_ref, sb_ref = refs
        x = x_ref[...]
    h = _rms(x, g_ref[...]) * (1.0 + sc_ref[0]) + sh_ref[0]
    hb = h.astype(BF16)
    o0, o1, o2 = RET_COLS, RET_COLS + RWKV_COLS, RET_COLS + RWKV_COLS + DSA_PAD
    ret_ref[...] = jnp.dot(hb, w_ref[:, 0:o0], preferred_element_type=F32)
    rwkv_ref[...] = jnp.dot(hb, w_ref[:, o0:o1], preferred_element_type=F32)
    dsa_ref[...] = jnp.dot(hb, w_ref[:, o1:o2], preferred_element_type=F32)
    sb_ref[...] = jnp.dot(hb, w_ref[:, o2:], preferred_element_type=F32).astype(BF16)


def _mod_row(per_b, j):
    return pl.BlockSpec((1, 1, D_MODEL), lambda i: ((i // per_b) * 6 + j, 0, 0))


def _inproj(x2, gain, mod6, w_cat, S, comb=None):
    N, D = x2.shape
    tm = ROW_TILE
    per_b = S // tm
    wt = w_cat.shape[1]
    row = lambda i: (i, 0)
    in_specs = [pl.BlockSpec((tm, D), row)]
    args = [x2]
    out_specs, out_shape = [], []
    if comb is not None:
        in_specs += ([_mod_row(per_b, 5), pl.BlockSpec((tm, LANES), row)]
                     + [pl.BlockSpec((tm, D), row)] * len(comb[2]))
        args += [comb[0], comb[1]] + list(comb[2])
        out_specs.append(pl.BlockSpec((tm, D), row))
        out_shape.append(jax.ShapeDtypeStruct((N, D), F32))
    in_specs += [pl.BlockSpec((1, D), lambda i: (0, 0)), _mod_row(per_b, 1), _mod_row(per_b, 0),
                 pl.BlockSpec((D, wt), lambda i: (0, 0))]
    args += [gain, mod6, mod6, w_cat]
    out_specs += [pl.BlockSpec((tm, RET_COLS), row), pl.BlockSpec((tm, RWKV_COLS), row),
                  pl.BlockSpec((tm, DSA_PAD), row), pl.BlockSpec((tm, SB_COLS), row)]
    out_shape += [jax.ShapeDtypeStruct((N, RET_COLS), F32), jax.ShapeDtypeStruct((N, RWKV_COLS), F32),
                  jax.ShapeDtypeStruct((N, DSA_PAD), F32), jax.ShapeDtypeStruct((N, SB_COLS), BF16)]
    return pl.pallas_call(
        functools.partial(_inproj_kernel, 0 if comb is None else len(comb[2])),
        grid=(N // tm,),
        in_specs=in_specs, out_specs=out_specs, out_shape=out_shape,
        compiler_params=_params(("parallel",)),
        name="in_proj",
    )(*args)


def _ret_kernel(q_ref, k_ref, v_ref, g_ref, cos_ref, sin_ref, din_ref, qd_ref, kd_ref, cd_ref, gn_ref,
                o_ref, state_ref, y_ref):
    @pl.when(pl.program_id(1) == 0)
    def _():
        state_ref[...] = jnp.zeros_like(state_ref)

    cos, sin = cos_ref[0], sin_ref[0]
    q = _rope(q_ref[0], cos, sin, HEAD_DIM, HEAD_DIM // 2)
    k = _rope(k_ref[0], cos, sin, HEAD_DIM, HEAD_DIM // 2) * HEAD_DIM ** -0.5
    v = v_ref[0]
    qd = q * qd_ref[...]
    kd = k * kd_ref[...]
    for h in range(N_HEADS_G):
        sl = slice(h * HEAD_DIM, (h + 1) * HEAD_DIM)
        qh, kh, vh = q[:, sl].astype(BF16), k[:, sl].astype(BF16), v[:, sl].astype(BF16)
        s = lax.dot_general(qh, kh, NT_DIMS, preferred_element_type=F32) * din_ref[h]
        inner = jnp.dot(s.astype(BF16), vh, preferred_element_type=F32)
        st = state_ref[h]
        cross = _bdot(qd[:, sl], st)
        state_ref[h] = st * cd_ref[:, sl] + lax.dot_general(
            kd[:, sl].astype(BF16), vh, TN_DIMS, preferred_element_type=F32)
        o = inner + cross
        oc = o - jnp.mean(o, axis=-1, keepdims=True)
        y_ref[:, sl] = oc * lax.rsqrt(jnp.mean(oc * oc, axis=-1, keepdims=True) + NORM_EPS)
    g = g_ref[0]
    o_ref[0] = (g * _sigmoid(g) * (y_ref[...] * gn_ref[...])).astype(BF16)


def _retention(ret3, tabs, consts, gn):
    B, S, _ = ret3.shape
    C = consts["din"].shape[1]
    blk = lambda j: pl.BlockSpec((1, C, GROUP_W), lambda b, c, j=j: (b, c, j))
    tab = pl.BlockSpec((1, C, GROUP_W), lambda b, c: (b, c, 0))
    const2 = lambda shape: pl.BlockSpec(shape, lambda b, c: (0,) * len(shape))
    return pl.pallas_call(
        _ret_kernel,
        grid=(B, S // C),
        in_specs=[blk(0), blk(1), blk(2), blk(3), tab, tab,
                  const2((N_HEADS_G, C, C)), const2((C, GROUP_W)), const2((C, GROUP_W)),
                  const2((1, GROUP_W)), const2((1, GROUP_W))],
        out_specs=pl.BlockSpec((1, C, GROUP_W), lambda b, c: (b, c, 0)),
        out_shape=jax.ShapeDtypeStruct((B, S, GROUP_W), BF16),
        scratch_shapes=[pltpu.VMEM((N_HEADS_G, HEAD_DIM, HEAD_DIM), F32), pltpu.VMEM((C, GROUP_W), F32)],
        compiler_params=_params(("parallel", "arbitrary")),
        name="retention",
    )(ret3, ret3, ret3, ret3, tabs[0], tabs[1],
      consts["din"], consts["qd"], consts["kd"], consts["cd"], gn)


def _ret_consts(C):
    H = N_HEADS_G
    lg = jnp.log(1.0 - 2.0 ** (-5.0 - jnp.arange(H, dtype=F32)))
    idx = jnp.arange(C, dtype=F32)
    diff = idx[:, None] - idx[None, :]
    din = jnp.where(diff >= 0, jnp.exp(lg[:, None, None] * jnp.maximum(diff, 0.0)), 0.0)
    q_dec = jnp.exp(lg[:, None] * (idx + 1.0))
    k_dec = jnp.exp(lg[:, None] * (C - 1.0 - idx))
    chunk_dec = jnp.exp(lg * C)
    wide = lambda t: jnp.repeat(t.T, HEAD_DIM, axis=1)
    return {"din": din, "qd": wide(q_dec), "kd": wide(k_dec),
            "cd": jnp.repeat(chunk_dec, HEAD_DIM)[None, :]}


def _rwkv_prep_kernel(per_b, z_ref, zp_ref, mu_ref, w0_ref, a0_ref, w2_ref, a2_ref, g2_ref, kk_ref, ka_ref,
                      rk_ref, hm_ref, r_o, w_o, k_o, v_o, nkk_o, b_o, gate_o, bonus_o):
    z = z_ref[...]
    first = (pl.program_id(0) % per_b) == 0
    prow = jnp.where(first, 0.0, zp_ref[7:8, :])
    rid = lax.broadcasted_iota(I32, z.shape, 0)
    prev = jnp.where(rid == 0, prow, pltpu.roll(z, 1, 0))
    f = z + (prev - z) * mu_ref[...]
    G = GROUP_W
    r, k, v, lo = f[:, 0:G], f[:, G:2 * G], f[:, 2 * G:3 * G], f[:, 3 * G:3 * G + 128]
    w_log = -_softplus(-(w0_ref[...] + _bdot(jnp.tanh(lo), w2_ref[...]))) - 0.5
    decay = jnp.exp(-jnp.exp(w_log))
    a = _sigmoid(a0_ref[...] + _bdot(lo, a2_ref[...]))
    gate = _bdot(_sigmoid(lo), g2_ref[...])
    hm = hm_ref[...]
    kk = k * kk_ref[...]
    kk = kk / jnp.maximum(jnp.sqrt(_split_dot(kk * kk, hm)), 1e-12)
    k2 = k * (1.0 + (a - 1.0) * ka_ref[...])
    r_o[...] = r
    w_o[...] = decay
    k_o[...] = k2
    v_o[...] = v
    nkk_o[...] = -kk
    b_o[...] = kk * a
    gate_o[...] = gate
    bonus_o[...] = _split_dot(r * k2 * rk_ref[...], hm) * v


def _rwkv_prep(z2, S, mu, w0, a0, w2p, a2p, g2p, kk, ka, rk, hm):
    N, W = z2.shape
    tm = ROW_TILE
    per_b = S // tm
    row = lambda i: (i, 0)
    c = lambda shape: pl.BlockSpec(shape, lambda i: (0, 0))
    G = GROUP_W
    return pl.pallas_call(
        functools.partial(_rwkv_prep_kernel, per_b),
        grid=(N // tm,),
        in_specs=[pl.BlockSpec((tm, W), row),
                  pl.BlockSpec((SUBLANES, W), lambda i: (jnp.maximum(i * (tm // SUBLANES) - 1, 0), 0)),
                  c((1, W)), c((1, G)), c((1, G)), c((128, G)), c((128, G)), c((128, G)),
                  c((1, G)), c((1, G)), c((1, G)), c((G, G))],
        out_specs=[pl.BlockSpec((tm, G), row)] * 8,
        out_shape=[jax.ShapeDtypeStruct((N, G), F32)] * 8,
        compiler_params=_params(("parallel",)),
        name="rwkv_prep",
    )(z2, z2, mu, w0, a0, w2p, a2p, g2p, kk, ka, rk, hm)


def _rwkv_scan_kernel(T, nkk_ref, w_ref, b_ref, k_ref, r_ref, v_ref, y_ref, s_ref, rep_ref):
    @pl.when(pl.program_id(0) == 0)
    def _():
        s_ref[...] = jnp.zeros_like(s_ref)

    ch = v_ref.shape[1]
    groups = HEAD_DIM // ch
    width = LANES // groups
    tree = lambda parts: _tree(parts, jnp.add)

    grp = lax.broadcasted_iota(I32, (T * ch, LANES), 1) // width
    for o, ref in enumerate((nkk_ref, w_ref, b_ref, k_ref, r_ref)):
        x = ref[...].reshape(T * ch, LANES)
        turned = [x] + [pltpu.roll(x, m * width, 1) for m in range(1, groups)]
        for j in range(groups):
            y = turned[0]
            for m in range(1, groups):
                y = jnp.where(grp == (j + m) % groups, turned[m], y)
            rep_ref[o, :, j] = y.reshape(T, ch, LANES)

    keys = [(j, kh) for j in range(groups) for kh in range(ch)]
    row = lambda o, t, j, kh: rep_ref[o, t, j, kh:kh + 1, :]

    def step(t, carry):
        vt = v_ref[t]
        sa = tree([s_ref[j * ch + kh] * row(0, t, j, kh) for j, kh in keys])
        ys = []
        for j, kh in keys:
            s_new = (s_ref[j * ch + kh] * row(1, t, j, kh) + sa * row(2, t, j, kh) + vt * row(3, t, j, kh))
            s_ref[j * ch + kh] = s_new
            ys.append(s_new * row(4, t, j, kh))
        y_ref[t] = tree(ys)
        return carry

    lax.fori_loop(0, T, step, 0)


def _rwkv_scan(nkk, w, b, k, r, v):
    S, ch, _ = v.shape
    T = 32
    spec = pl.BlockSpec((T, ch, LANES), lambda i: (i, 0, 0))
    return pl.pallas_call(
        functools.partial(_rwkv_scan_kernel, T),
        grid=(S // T,),
        in_specs=[spec] * 6,
        out_specs=spec,
        out_shape=jax.ShapeDtypeStruct((S, ch, LANES), F32),
        scratch_shapes=[pltpu.VMEM((HEAD_DIM, ch, LANES), F32),
                        pltpu.VMEM((5, T, HEAD_DIM // ch, ch, LANES), F32)],
        compiler_params=_params(("arbitrary",)),
        name="rwkv_scan",
    )(nkk, w, b, k, r, v)


def _to_v_layout(x, B, S):
    P = B * N_HEADS_G
    rep = LANES // P
    t = x.reshape(B, S, N_HEADS_G, HEAD_DIM // rep, rep).transpose(1, 3, 4, 0, 2)
    return t.reshape(S, HEAD_DIM // rep, LANES)


def _from_v_layout(y, B, S):
    P = B * N_HEADS_G
    rep = LANES // P
    t = y.reshape(S, HEAD_DIM // rep, rep, B, N_HEADS_G).transpose(3, 0, 4, 1, 2)
    return t.reshape(B * S, GROUP_W)


def _dsa_prep_kernel(f_ref, qn_ref, wq_ref, wqi_ref, cq_ref, sq_ref, ci_ref, si_ref,
                     qt_o, qit_o, wt_o, k_o, v_o, ki_o):
    f = f_ref[...]
    cq = _rms(f[:, 0:DSA_Q_LORA], qn_ref[...]).astype(BF16)
    cos_q, sin_q = cq_ref[...], sq_ref[...]
    cos_i, sin_i = ci_ref[...], si_ref[...]
    q = _rope(jnp.dot(cq, wq_ref[...], preferred_element_type=F32), cos_q, sin_q, HEAD_DIM, HEAD_DIM // 8)
    tm = f.shape[0]
    qt_o[...] = (q * HEAD_DIM ** -0.5).T.reshape(N_HEADS_G, HEAD_DIM, tm).astype(BF16)
    qi = _rope(jnp.dot(cq, wqi_ref[...], preferred_element_type=F32), cos_i, sin_i, IDX_DIM, IDX_DIM // 8)
    qit_o[...] = qi.T.reshape(IDX_HEADS, IDX_DIM, tm).astype(BF16)
    kv = f[:, 128:256]
    kv_r = _rope(kv, cos_q[:, 0:128], sin_q[:, 0:128], HEAD_DIM, HEAD_DIM // 8)
    k_o[...] = kv_r[:, 0:HEAD_DIM].astype(BF16)
    v_o[...] = kv[:, HEAD_DIM:128].astype(BF16)
    tail = f[:, 256:384]
    tail_r = _rope(tail, cos_i[:, 0:128], sin_i[:, 0:128], IDX_DIM, IDX_DIM // 8)
    ki_o[...] = tail_r[:, 0:IDX_DIM].astype(BF16)
    wt_o[...] = tail.T[IDX_DIM:IDX_DIM + IDX_HEADS, :] * (IDX_HEADS ** -0.5 * IDX_DIM ** -0.5)


def _dsa_prep(f2, qn, wq, wqi, tq, ti):
    N, W = f2.shape
    tm = ROW_TILE
    G = GROUP_W
    row = lambda w: pl.BlockSpec((tm, w), lambda i: (i, 0))
    col = lambda h: pl.BlockSpec((h, tm), lambda i: (0, i))
    slab = lambda h, d: pl.BlockSpec((h, d, tm), lambda i: (0, 0, i))
    c = lambda shape: pl.BlockSpec(shape, lambda i: (0, 0))
    return pl.pallas_call(
        _dsa_prep_kernel,
        grid=(N // tm,),
        in_specs=[row(W), c((1, DSA_Q_LORA)), c((DSA_Q_LORA, G)), c((DSA_Q_LORA, G))] + [row(G)] * 4,
        out_specs=[slab(N_HEADS_G, HEAD_DIM), slab(IDX_HEADS, IDX_DIM), col(IDX_HEADS),
                   row(HEAD_DIM), row(HEAD_DIM), row(IDX_DIM)],
        out_shape=[jax.ShapeDtypeStruct((N_HEADS_G, HEAD_DIM, N), BF16),
                   jax.ShapeDtypeStruct((IDX_HEADS, IDX_DIM, N), BF16),
                   jax.ShapeDtypeStruct((IDX_HEADS, N), F32),
                   jax.ShapeDtypeStruct((N, HEAD_DIM), BF16), jax.ShapeDtypeStruct((N, HEAD_DIM), BF16),
                   jax.ShapeDtypeStruct((N, IDX_DIM), BF16)],
        compiler_params=_params(("parallel",)),
        name="dsa_prep",
    )(f2, qn, wq, wqi, *tq, *ti)


def _float_key(bits):
    return bits ^ ((bits >> 31) & 0x7FFFFFFF)


def _dsa_kernel(TK, topk, qt_ref, qit_ref, wt_ref, k_ref, v_ref, ki_ref, on_ref, o_ref,
                skey_ref, lg_ref, out_ref):
    i = pl.program_id(1)
    TQ = Q_BLOCK
    n_kt = (i * TQ + TQ + TK - 1) // TK
    q_all = jnp.concatenate([qt_ref[h] for h in range(N_HEADS_G)], axis=1)
    qi_all = jnp.concatenate([qit_ref[h] for h in range(IDX_HEADS)], axis=1)
    wt = wt_ref[...]
    kpos = lax.broadcasted_iota(I32, (TK, TQ), 0)
    qpos = i * TQ + lax.broadcasted_iota(I32, (TK, TQ), 1)
    head = lambda x, h: x[:, h * TQ:(h + 1) * TQ]

    def fold(x, op):
        return _tree([x[r * SUBLANES:(r + 1) * SUBLANES] for r in range(TK // SUBLANES)], op)

    def score_tile(kt, tops):
        off = pl.multiple_of(kt * TK, TK)
        rel = jnp.dot(ki_ref[0, pl.ds(off, TK), :], qi_all, preferred_element_type=F32)
        sc = jnp.zeros((TK, TQ), F32)
        for h in range(IDX_HEADS):
            sc = sc + jnp.maximum(head(rel, h), 0.0) * wt[h:h + 1, :]
        key = _float_key(pltpu.bitcast(sc, I32))
        key = jnp.where(sc == 0.0, -(kt * TK + kpos), jnp.where(key < 0, key - skey_ref.shape[0] * TK, key))
        valid = kt * TK + kpos <= qpos
        skey_ref[kt] = jnp.where(valid, key, INT_MIN)
        lg = jnp.dot(k_ref[0, pl.ds(off, TK), :], q_all, preferred_element_type=F32)
        lg_ref[kt] = lg
        return tuple(jnp.maximum(tops[h], fold(jnp.where(valid, head(lg, h), -1e30), jnp.maximum))
                     for h in range(N_HEADS_G))

    lowest = tuple(jnp.full((SUBLANES, TQ), -1e30, F32) for _ in range(N_HEADS_G))
    tops = lax.fori_loop(0, n_kt, score_tile, lowest)

    def count(pred):
        def count_tile(kt, acc):
            return acc + fold(jnp.where(pred(kt, skey_ref[kt]), 1.0, 0.0), jnp.add)

        acc = lax.fori_loop(0, n_kt, count_tile, jnp.zeros((SUBLANES, TQ), F32))
        return jnp.sum(acc, axis=0, keepdims=True)

    def bit_step(j, carry):
        thr, n_thr = carry
        cand = thr + lax.shift_left(jnp.int32(1), 31 - j)
        n = count(lambda kt, keys: keys >= cand)
        return jnp.where(n >= topk, cand, thr), jnp.where(n >= topk, n, n_thr)

    thr, n_thr = lax.fori_loop(0, 32, bit_step,
                               (jnp.full((1, TQ), INT_MIN, I32), jnp.full((1, TQ), float(topk), F32)))
    thr = jnp.maximum(thr, INT_MIN + 1)

    @pl.when(jnp.max(n_thr) > topk)
    def _():
        keep = topk - count(lambda kt, keys: keys > thr)

        pos_bits = (skey_ref.shape[0] * TK - 1).bit_length()

        def pos_step(j, last):
            cand = last + lax.shift_left(jnp.int32(1), pos_bits - 1 - j)
            n = count(lambda kt, keys: jnp.logical_and(keys == thr, kt * TK + kpos < cand))
            return jnp.where(n < keep, cand, last)

        last = lax.fori_loop(0, pos_bits, pos_step, jnp.zeros((1, TQ), I32))

        def retire(kt, carry):
            keys = skey_ref[kt]
            skey_ref[kt] = jnp.where(jnp.logical_and(keys == thr, kt * TK + kpos > last), INT_MIN, keys)
            return carry

        lax.fori_loop(0, n_kt, retire, 0)

    def sweep(shifts):
        def acc_tile(kt, carry):
            ls, acc = carry
            sel = skey_ref[kt] >= thr
            ps = [jnp.where(sel, jnp.exp(lg_ref[kt, :, h * TQ:(h + 1) * TQ] - shifts[h]), 0.0)
                  for h in range(N_HEADS_G)]
            p_all = jnp.concatenate([p.astype(BF16) for p in ps], axis=1)
            vb = v_ref[0, pl.ds(pl.multiple_of(kt * TK, TK), TK), :]
            return (tuple(ls[h] + fold(ps[h], jnp.add) for h in range(N_HEADS_G)),
                    acc + lax.dot_general(vb, p_all, TN_DIMS, preferred_element_type=F32))

        ls, acc = lax.fori_loop(0, n_kt, acc_tile,
                                (tuple(jnp.zeros((SUBLANES, TQ), F32) for _ in range(N_HEADS_G)),
                                 jnp.zeros((HEAD_DIM, N_HEADS_G * TQ), F32)))
        ls = [jnp.sum(l, axis=0, keepdims=True) for l in ls]
        for h in range(N_HEADS_G):
            out_ref[h * HEAD_DIM:(h + 1) * HEAD_DIM, :] = head(acc, h) / ls[h]
        return jnp.min(jnp.minimum(jnp.minimum(ls[0], ls[1]), jnp.minimum(ls[2], ls[3])))

    smallest = sweep([jnp.max(t, axis=0, keepdims=True) for t in tops])

    @pl.when(smallest < SOFTMAX_FLOOR)
    def _():
        def max_tile(kt, ms):
            sel = skey_ref[kt] >= thr
            return tuple(jnp.maximum(ms[h], fold(jnp.where(sel, lg_ref[kt, :, h * TQ:(h + 1) * TQ], -1e30),
                                                 jnp.maximum))
                         for h in range(N_HEADS_G))

        ms = lax.fori_loop(0, n_kt, max_tile, lowest)
        sweep([jnp.max(m, axis=0, keepdims=True) for m in ms])

    o_ref[0] = _rms(out_ref[...].T, on_ref[...]).astype(BF16)


def _dsa_attention(qt, qit, wt, k, v, ki, on):
    B, S, _ = k.shape
    G = GROUP_W
    nq = S // Q_BLOCK
    TK = min(512, S)
    topk = min(DSA_TOPK_MAX, S // 4)
    qcol = lambda h: pl.BlockSpec((h, Q_BLOCK), lambda b, i: (0, b * nq + i))
    qslab = lambda h, d: pl.BlockSpec((h, d, Q_BLOCK), lambda b, i: (0, 0, b * nq + i))
    full = lambda wd: pl.BlockSpec((1, S, wd), lambda b, i: (b, 0, 0))
    return pl.pallas_call(
        functools.partial(_dsa_kernel, TK, topk),
        grid=(B, nq),
        in_specs=[qslab(N_HEADS_G, HEAD_DIM), qslab(IDX_HEADS, IDX_DIM), qcol(IDX_HEADS),
                  full(HEAD_DIM), full(HEAD_DIM), full(IDX_DIM), pl.BlockSpec((1, G), lambda b, i: (0, 0))],
        out_specs=pl.BlockSpec((1, Q_BLOCK, G), lambda b, i: (b, i, 0)),
        out_shape=jax.ShapeDtypeStruct((B, S, G), BF16),
        scratch_shapes=[pltpu.VMEM((S // TK, TK, Q_BLOCK), I32),
                        pltpu.VMEM((S // TK, TK, N_HEADS_G * Q_BLOCK), F32),
                        pltpu.VMEM((G, Q_BLOCK), F32)],
        compiler_params=_params(("parallel", "arbitrary")),
        name="dsa_attention",
    )(qt, qit, wt, k, v, ki, on)


def _sb_kernel(q_ref, k_ref, v_ref, u_ref, on_ref, o_ref, z_ref, lm_ref):
    i = pl.program_id(1)
    T = Q_BLOCK
    H, G = N_HEADS_G, GROUP_W
    u = u_ref[...]
    qt = q_ref[0].astype(F32).T.astype(BF16)
    q_rows = lax.broadcasted_iota(I32, (G, H * T), 0) // HEAD_DIM
    q_cols = lax.broadcasted_iota(I32, (G, H * T), 1) // T
    qbd = jnp.where(q_rows == q_cols, jnp.concatenate([qt] * H, axis=1), jnp.zeros((), BF16))
    kid = lax.broadcasted_iota(I32, (T, H * T), 0)
    qid = lax.broadcasted_iota(I32, (T, H * T), 1) & (T - 1)
    strict = kid < qid

    def cond(carry):
        kt, c, _ = carry
        return jnp.logical_and(kt >= 0, jnp.max(c) > SB_UNDERFLOW)

    def logits(kt):
        z = jnp.dot(k_ref[0, pl.ds(pl.multiple_of(kt * T, T), T), :], qbd, preferred_element_type=F32)
        z = z * HEAD_DIM ** -0.5
        return z, -_softplus(z)

    z0, lm0 = logits(i)
    z_ref[...] = z0
    lm_ref[...] = jnp.where(strict, lm0, 0.0)

    def body(carry):
        kt, c, acc = carry
        off = pl.multiple_of(kt * T, T)
        mask = jnp.logical_or(strict, kt < i)
        z, lm = z_ref[...], lm_ref[...]
        z_ref[...], lm_ref[...] = logits(jnp.maximum(kt - 1, 0))
        hi, lo = _split(lm)
        later = jnp.dot(u, jnp.concatenate([hi, lo], axis=1), preferred_element_type=F32)
        later = later[:, :H * T] + later[:, H * T:]
        a = jnp.where(mask, jnp.exp(z + lm + (c + later)), 0.0)
        av = lax.dot_general(v_ref[0, pl.ds(off, T), :], a.astype(BF16), TN_DIMS, preferred_element_type=F32)
        acc = acc + jnp.concatenate(
            [av[h * HEAD_DIM:(h + 1) * HEAD_DIM, h * T:(h + 1) * T] for h in range(H)], axis=0)
        return kt - 1, c + jnp.sum(lm, axis=0, keepdims=True), acc

    _, _, acc = lax.while_loop(cond, body, (i, jnp.zeros((1, H * T), F32), jnp.zeros((G, T), F32)))
    o_ref[0] = _rms(acc.T, on_ref[...]).astype(BF16)


def _stick_breaking(sb3, u, on):
    B, S, _ = sb3.shape
    G = GROUP_W
    return pl.pallas_call(
        _sb_kernel,
        grid=(B, S // Q_BLOCK),
        in_specs=[pl.BlockSpec((1, Q_BLOCK, G), lambda b, i: (b, i, 0)),
                  pl.BlockSpec((1, S, G), lambda b, i: (b, 0, 1)),
                  pl.BlockSpec((1, S, G), lambda b, i: (b, 0, 2)),
                  pl.BlockSpec((Q_BLOCK, Q_BLOCK), lambda b, i: (0, 0)),
                  pl.BlockSpec((1, G), lambda b, i: (0, 0))],
        out_specs=pl.BlockSpec((1, Q_BLOCK, G), lambda b, i: (b, i, 0)),
        out_shape=jax.ShapeDtypeStruct((B, S, G), BF16),
        scratch_shapes=[pltpu.VMEM((Q_BLOCK, N_HEADS_G * Q_BLOCK), F32)] * 2,
        compiler_params=_params(("parallel", "arbitrary")),
        name="stick_breaking",
    )(sb3, sb3, sb3, u, on)


def _outproj_kernel(yr_ref, yd_ref, ys_ref, yw_ref, bonus_ref, gate_ref, ln_ref, hm_ref, x_ref, g1_ref, sc_ref,
                    sh_ref, gn_ref, wo_ref, rw_ref, rb_ref, x_o, h_o, lg_o):
    G = GROUP_W
    y = yw_ref[...]
    hm = hm_ref[...]
    yc = y - _split_dot(y, hm) * (1.0 / HEAD_DIM)
    var = _split_dot(yc * yc, hm) * (1.0 / HEAD_DIM)
    yw = ((yc * lax.rsqrt(var + RWKV_LN_EPS) * ln_ref[...] + bonus_ref[...]) * gate_ref[...]).astype(BF16)
    dot = lambda yv, g: jnp.dot(yv, wo_ref[g * G:(g + 1) * G, :], preferred_element_type=F32)
    mixed = dot(yr_ref[...], 0) + dot(yw, 1) + dot(yd_ref[...], 2) + dot(ys_ref[...], 3)
    x1 = x_ref[...] + g1_ref[0] * mixed
    x_o[...] = x1
    h = _rms(x1, gn_ref[...]) * (1.0 + sc_ref[0]) + sh_ref[0]
    h_o[...] = h
    hi, lo = _split(h)
    work = (jnp.dot(hi, rw_ref[0], preferred_element_type=F32) + jnp.dot(lo, rw_ref[0], preferred_element_type=F32)
            + jnp.dot(hi, rw_ref[1], preferred_element_type=F32) + rb_ref[...])
    col = lax.broadcasted_iota(I32, work.shape, 1).astype(F32)
    out = jnp.zeros_like(work)
    vals = []
    for j in range(TOP_K):
        m = jnp.max(work, axis=-1, keepdims=True)
        idx = jnp.min(jnp.where(work == m, col, float(LANES)), axis=-1, keepdims=True)
        out = jnp.where(col == float(j), idx, out)
        vals.append(m)
        work = jnp.where(col == idx, -jnp.inf, work)
    es = [jnp.exp(v - vals[0]) for v in vals]
    den = es[0] + es[1] + es[2] + es[3]
    for j in range(TOP_K):
        out = jnp.where(col == float(TOP_K + j), es[j] / den, out)
    lg_o[...] = out


def _outproj(ys, rwkv, x2, mod6, gain, w_out, router_w, router_b, S):
    N, D = x2.shape
    tm = ROW_TILE
    per_b = S // tm
    G = GROUP_W
    row = lambda w: pl.BlockSpec((tm, w), lambda i: (i, 0))
    c = lambda shape: pl.BlockSpec(shape, lambda i: (0, 0))
    return pl.pallas_call(
        _outproj_kernel,
        grid=(N // tm,),
        in_specs=[row(G)] * 6 + [c((1, G)), c((G, G)),
                                 row(D), _mod_row(per_b, 2), _mod_row(per_b, 4), _mod_row(per_b, 3),
                                 c((1, D)), c((D, D)), pl.BlockSpec((2, D, LANES), lambda i: (0, 0, 0)),
                                 c((1, LANES))],
        out_specs=[row(D), row(D), row(LANES)],
        out_shape=[jax.ShapeDtypeStruct((N, D), F32), jax.ShapeDtypeStruct((N, D), F32),
                   jax.ShapeDtypeStruct((N, LANES), F32)],
        compiler_params=_params(("parallel",)),
        name="out_proj_router",
    )(*ys, *rwkv, x2, mod6, mod6, mod6, gain, w_out,
      jnp.pad(jnp.stack(_split(router_w)), ((0, 0), (0, 0), (0, LANES - N_EXPERTS))),
      jnp.pad(router_b, ((0, 0), (0, LANES - N_EXPERTS)), constant_values=-1e30))


def _moe_kernel(be_ref, nu_ref, x_ref, w1_ref, b1_ref, w2_ref, b2_ref, o_ref):
    i = pl.program_id(0)

    @pl.when(i < nu_ref[0])
    def _():
        xb = x_ref[...].astype(BF16)
        w1 = w1_ref.at[0, 0]
        glu = jnp.dot(xb, w1[:, 0:D_FF].astype(BF16), preferred_element_type=F32) + b1_ref[0, :, 0:D_FF]
        lin = jnp.dot(xb, w1[:, D_FF:].astype(BF16), preferred_element_type=F32) + b1_ref[0, :, D_FF:]
        glu = jnp.minimum(glu, SWIGLU_LIMIT)
        lin = jnp.clip(lin, -SWIGLU_LIMIT, SWIGLU_LIMIT)
        act = glu * _sigmoid(SWIGLU_ALPHA * glu) * (lin + 1.0)
        y = jnp.dot(act.astype(BF16), w2_ref[0, 0].astype(BF16), preferred_element_type=F32) + b2_ref[0]
        o_ref[...] = y.astype(BF16)

    @pl.when(i >= nu_ref[0])
    def _():
        o_ref[...] = jnp.zeros_like(o_ref)


def _moe_ffn(block_expert, n_used, xs, layer, w1, b1, w2, b2):
    rows, D = xs.shape
    nb = rows // MOE_BLOCK
    L, E = w1.shape[:2]
    live = lambda i, nu: jnp.minimum(i, nu[0] - 1)
    grid_spec = pltpu.PrefetchScalarGridSpec(
        num_scalar_prefetch=2,
        grid=(nb,),
        in_specs=[pl.BlockSpec((MOE_BLOCK, D), lambda i, be, nu: (live(i, nu), 0)),
                  pl.BlockSpec((1, 1, D, 2 * D_FF), lambda i, be, nu: (layer, be[i], 0, 0)),
                  pl.BlockSpec((1, 1, 2 * D_FF), lambda i, be, nu: (layer * E + be[i], 0, 0)),
                  pl.BlockSpec((1, 1, D_FF, D), lambda i, be, nu: (layer, be[i], 0, 0)),
                  pl.BlockSpec((1, 1, D), lambda i, be, nu: (layer * E + be[i], 0, 0))],
        out_specs=pl.BlockSpec((MOE_BLOCK, D), lambda i, be, nu: (i, 0)),
    )
    return pl.pallas_call(
        _moe_kernel,
        grid_spec=grid_spec,
        out_shape=jax.ShapeDtypeStruct((rows, D), BF16),
        compiler_params=_params(("arbitrary",), 56 * 1024 * 1024),
        name="moe_ffn",
    )(block_expert, n_used, xs, w1, b1.reshape(L * E, 1, -1), w2, b2.reshape(L * E, 1, -1))


def _route(routed):
    n = routed.shape[0]
    expert = routed[:, 0:TOP_K].reshape(-1).astype(I32)
    n_assign = n * TOP_K
    ids = jnp.arange(n_assign, dtype=I32)
    _, order = lax.sort((expert, ids), num_keys=1, is_stable=True)
    _, inv = lax.sort((order, ids), num_keys=1)
    counts = jnp.sum((expert[:, None] == jnp.arange(N_EXPERTS, dtype=I32)[None, :]).astype(I32), axis=0)
    starts = jnp.cumsum(counts) - counts
    padded = (counts + MOE_BLOCK - 1) // MOE_BLOCK * MOE_BLOCK
    pad_end = jnp.cumsum(padded)
    pad_start = pad_end - padded
    dest = pad_start[expert] + inv - starts[expert]
    n_blocks = -(-n_assign // MOE_BLOCK) + N_EXPERTS
    block_start = jnp.arange(n_blocks, dtype=I32) * MOE_BLOCK
    block_expert = jnp.minimum(jnp.sum(pad_end[None, :] <= block_start[:, None], axis=1), N_EXPERTS - 1)
    block_expert = block_expert.astype(I32)
    per_row = lambda t: jnp.repeat(t[block_expert], MOE_BLOCK)
    off = jnp.arange(n_blocks * MOE_BLOCK, dtype=I32) - per_row(pad_start)
    valid = off < per_row(counts)
    src = order[jnp.clip(per_row(starts) + off, 0, n_assign - 1)]
    row_token = jnp.where(valid, src // TOP_K, 0)
    n_used = (pad_end[-1] // MOE_BLOCK).astype(I32).reshape(1)
    return dest.reshape(n, TOP_K), row_token, block_expert, n_used


def _rows(t, idx):
    return t.at[idx].get(mode="promise_in_bounds")


def _final_kernel(x_ref, g2_ref, routed_ref, y0, y1, y2, y3, g_ref, o_ref):
    o_ref[...] = _rms(_combine(x_ref, g2_ref, routed_ref, (y0, y1, y2, y3)), g_ref[...])


def _final_norm(x2, mod6, routed, ys, g, S):
    N, D = x2.shape
    tm = ROW_TILE
    row = pl.BlockSpec((tm, D), lambda i: (i, 0))
    return pl.pallas_call(
        _final_kernel, grid=(N // tm,),
        in_specs=[row, _mod_row(S // tm, 5), pl.BlockSpec((tm, LANES), lambda i: (i, 0))] + [row] * 4
        + [pl.BlockSpec((1, D), lambda i: (0, 0))],
        out_specs=row,
        out_shape=jax.ShapeDtypeStruct((N, D), F32),
        compiler_params=_params(("parallel",)), name="final_norm",
    )(x2, mod6, routed, *ys, g)


def _rope_tables(pos, group, rot_dim, theta, width):
    half = rot_dim // 2
    inv = theta ** (-jnp.arange(half, dtype=F32) / half)
    ang = pos.astype(F32)[..., None] * inv
    cos, sin = jnp.cos(ang), jnp.sin(ang)
    rest = group - rot_dim
    ones = jnp.ones(ang.shape[:2] + (rest,), F32)
    zeros = jnp.zeros(ang.shape[:2] + (rest,), F32)
    rep = width // group
    tile = lambda t: jnp.tile(t, (1, 1, rep)).reshape(-1, width)
    return (tile(jnp.concatenate([cos, cos, ones], -1)), tile(jnp.concatenate([-sin, sin, zeros], -1)))


def _pad_rows(w, start, total=128):
    return jnp.zeros((total, w.shape[1]), w.dtype).at[start:start + w.shape[0]].set(w)


def kernel(x, c, positions, ada_w, ada_b, norm_mix, norm_ffn, w_in, ret_gn, rwkv_mu, rwkv_w0, rwkv_w2, rwkv_a0,
           rwkv_a2, rwkv_g2, rwkv_kk, rwkv_ka, rwkv_rk, rwkv_ln, dsa_qnorm, dsa_wq_up, dsa_wqi_up, dsa_onorm,
           sb_onorm, w_out, router_w, router_b, moe_w1, moe_b1, moe_w2, moe_b2, norm_final):
    B, S, D = x.shape
    N = B * S
    L = ada_w.shape[0]
    G = GROUP_W
    row2 = lambda t: t.reshape(1, -1)

    mod = _ada(c, ada_w, ada_b)
    ret_tabs = tuple(t.reshape(B, S, G) for t in _rope_tables(positions, HEAD_DIM, HEAD_DIM, RET_THETA, G))
    dq_tabs = _rope_tables(positions, HEAD_DIM, HEAD_DIM // 4, ROPE_THETA, G)
    di_tabs = _rope_tables(positions, IDX_DIM, IDX_DIM // 4, ROPE_THETA, G)
    ret_c = _ret_consts(min(256, S))
    hm = (jnp.arange(G)[:, None] // HEAD_DIM == jnp.arange(G)[None, :] // HEAD_DIM).astype(BF16)
    u_later = (jnp.arange(Q_BLOCK)[None, :] > jnp.arange(Q_BLOCK)[:, None]).astype(BF16)
    o1, o2 = RET_COLS + RWKV_COLS, RET_COLS + RWKV_COLS + DSA_COLS

    x2 = x.reshape(N, D)
    comb = None
    for l in range(L):
        mod6 = mod[l].reshape(B * 6, 1, D)
        w_l = w_in[l]
        w_cat = jnp.concatenate(
            [w_l[:, :o1], w_l[:, o1:o2], jnp.zeros((D, DSA_PAD - DSA_COLS), F32), w_l[:, o2:]], axis=1).astype(BF16)
        outs = _inproj(x2, row2(norm_mix[l]), mod6, w_cat, S, comb)
        if comb is not None:
            x2, outs = outs[0], outs[1:]
        ret, rwkv, dsa, sb = outs

        prep = _rwkv_prep(rwkv, S, row2(rwkv_mu[l]), row2(rwkv_w0[l]), row2(rwkv_a0[l]),
                          _pad_rows(rwkv_w2[l], 0).astype(BF16), _pad_rows(rwkv_a2[l], 32).astype(BF16),
                          _pad_rows(rwkv_g2[l], 64).astype(BF16), row2(rwkv_kk[l]), row2(rwkv_ka[l]),
                          row2(rwkv_rk[l]), hm)
        r_, w_, k_, v_, nkk_, b_, gate_, bonus_ = prep
        scan_in = tuple(_to_v_layout(t, B, S) for t in (nkk_, w_, b_, k_, r_, v_))

        y_ret = _retention(ret.reshape(B, S, RET_COLS), ret_tabs, ret_c, row2(ret_gn[l])).reshape(N, G)

        dqt, dqit, dwt, dk, dv, dki = _dsa_prep(dsa, row2(dsa_qnorm[l]), dsa_wq_up[l].astype(BF16),
                                                dsa_wqi_up[l].astype(BF16), dq_tabs, di_tabs)
        y_dsa = _dsa_attention(dqt, dqit, dwt, dk.reshape(B, S, HEAD_DIM), dv.reshape(B, S, HEAD_DIM),
                               dki.reshape(B, S, IDX_DIM), row2(dsa_onorm[l])).reshape(N, G)

        y_sb = _stick_breaking(sb.reshape(B, S, SB_COLS), u_later, row2(sb_onorm[l])).reshape(N, G)

        y_scan = _rwkv_scan(*scan_in)
        rwkv_tail = (_from_v_layout(y_scan, B, S), bonus_, gate_, row2(rwkv_ln[l]), hm)

        x1, h2, routed = _outproj((y_ret, y_dsa, y_sb), rwkv_tail, x2, mod6, row2(norm_ffn[l]),
                                  w_out[l].astype(BF16), router_w[l], row2(router_b[l]), S)

        dest, row_token, block_expert, n_used = _route(routed)
        y = _moe_ffn(block_expert, n_used, _rows(h2, row_token), l, moe_w1, moe_b1, moe_w2, moe_b2)
        x2 = x1
        comb = (mod6, routed, tuple(_rows(y, dest[:, j]) for j in range(TOP_K)))
    return _final_norm(x2, *comb, row2(norm_final), S).reshape(B, S, D)
```

```python
import functools

import jax
import jax.numpy as jnp
from jax import lax
from jax.experimental import pallas as pl
from jax.experimental.pallas import tpu as pltpu

F32 = jnp.float32
BF16 = jnp.bfloat16
I32 = jnp.int32

D_MODEL = 1024
GROUP_W = 256
HEAD_DIM = 64
N_HEADS_G = 4
NORM_EPS = 1e-5
Q_BLOCK = 128
RET_THETA = 10000.0
RWKV_LN_EPS = 64e-5
DSA_Q_LORA = 128
IDX_HEADS = 8
IDX_DIM = 32
DSA_TOPK_MAX = 256
ROPE_THETA = 500000.0
N_EXPERTS = 32
TOP_K = 4
D_FF = D_MODEL
SWIGLU_ALPHA = 1.702
SWIGLU_LIMIT = 7.0
MOE_BLOCK = 512

RET_COLS = 4 * GROUP_W
RWKV_COLS = 3 * GROUP_W + 128
DSA_COLS = 296
DSA_PAD = 384
SB_COLS = 3 * GROUP_W

LANES = 128
SUBLANES = 8
ROW_TILE = 512
VMEM_LIMIT = 48 * 1024 * 1024
INT_MIN = -2 ** 31
SB_UNDERFLOW = -104.0
SOFTMAX_FLOOR = 1e-20

HIGHEST = lax.Precision.HIGHEST
NT_DIMS = (((1,), (1,)), ((), ()))
TN_DIMS = (((0,), (0,)), ((), ()))


def _params(sem, vmem=VMEM_LIMIT):
    return pltpu.CompilerParams(dimension_semantics=sem, vmem_limit_bytes=vmem)


def _bdot(a, b):
    return jnp.dot(a.astype(BF16), b.astype(BF16), preferred_element_type=F32)


def _split(x):
    hi = x.astype(BF16)
    return hi, (x - hi.astype(F32)).astype(BF16)


def _split_dot(x, m):
    hi, lo = _split(x)
    return jnp.dot(hi, m, preferred_element_type=F32) + jnp.dot(lo, m, preferred_element_type=F32)


def _tree(parts, op):
    while len(parts) > 1:
        parts = [op(parts[i], parts[i + 1]) for i in range(0, len(parts), 2)]
    return parts[0]


def _sigmoid(x):
    return 1.0 / (1.0 + jnp.exp(-x))


def _softplus(x):
    return jnp.maximum(x, 0.0) + jnp.log(1.0 + jnp.exp(-jnp.abs(x)))


def _rms(x, g, eps=NORM_EPS):
    return x * lax.rsqrt(jnp.mean(x * x, axis=-1, keepdims=True) + eps) * g


def _rope(x, cos, sin, group, half):
    w = x.shape[-1]
    first = (lax.broadcasted_iota(I32, (1, w), 1) & (group - 1)) < half
    partner = jnp.where(first, pltpu.roll(x, w - half, 1), pltpu.roll(x, half, 1))
    return x * cos + partner * sin


def _ada_kernel(c_ref, w_ref, b_ref, o_ref):
    c = c_ref[...]
    cond = c * _sigmoid(c)
    o_ref[0] = jnp.dot(cond, w_ref[0], preferred_element_type=F32, precision=HIGHEST) + b_ref[0]


def _ada(c, ada_w, ada_b):
    L, D, W = ada_w.shape
    B = c.shape[0]
    tn = 1024
    return pl.pallas_call(
        _ada_kernel,
        grid=(L, W // tn),
        in_specs=[pl.BlockSpec((B, D), lambda l, j: (0, 0)),
                  pl.BlockSpec((1, D, tn), lambda l, j: (l, 0, j)),
                  pl.BlockSpec((1, 1, tn), lambda l, j: (l, 0, j))],
        out_specs=pl.BlockSpec((1, B, tn), lambda l, j: (l, 0, j)),
        out_shape=jax.ShapeDtypeStruct((L, B, W), F32),
        compiler_params=_params(("parallel", "parallel")),
        name="ada_mod",
    )(c, ada_w, ada_b.reshape(L, 1, W))


def _combine(x_ref, g2_ref, routed_ref, y_refs):
    gates = routed_ref[...]
    moe = gates[:, TOP_K:TOP_K + 1] * y_refs[0][...].astype(F32)
    for j in range(1, len(y_refs)):
        moe = moe + gates[:, TOP_K + j:TOP_K + j + 1] * y_refs[j][...].astype(F32)
    return x_ref[...] + g2_ref[0] * moe


def _inproj_kernel(n_comb, x_ref, *refs):
    if n_comb:
        g2_ref, routed_ref, y_refs, refs = refs[0], refs[1], refs[2:2 + n_comb], refs[2 + n_comb:]
        g_ref, sc_ref, sh_ref, w_ref, x_o, ret_ref, rwkv_ref, dsa_ref, sb_ref = refs
        x = _combine(x_ref, g2_ref, routed_ref, y_refs)
        x_o[...] = x
    else:
        g_ref, sc_ref, sh_ref, w_ref, ret_ref, rwkv_ref, dsa_ref, sb_ref = refs
        x = x_ref[...]
    h = _rms(x, g_ref[...]) * (1.0 + sc_ref[0]) + sh_ref[0]
    hb = h.astype(BF16)
    o0, o1, o2 = RET_COLS, RET_COLS + RWKV_COLS, RET_COLS + RWKV_COLS + DSA_PAD
    ret_ref[...] = jnp.dot(hb, w_ref[:, 0:o0], preferred_element_type=F32)
    rwkv_ref[...] = jnp.dot(hb, w_ref[:, o0:o1], preferred_element_type=F32)
    dsa_ref[...] = jnp.dot(hb, w_ref[:, o1:o2], preferred_element_type=F32)
    sb_ref[...] = jnp.dot(hb, w_ref[:, o2:], preferred_element_type=F32).astype(BF16)


def _mod_row(per_b, j):
    return pl.BlockSpec((1, 1, D_MODEL), lambda i: ((i // per_b) * 6 + j, 0, 0))


def _inproj(x2, gain, mod6, w_cat, S, comb=None):
    N, D = x2.shape
    tm = ROW_TILE
    per_b = S // tm
    wt = w_cat.shape[1]
    row = lambda i: (i, 0)
    in_specs = [pl.BlockSpec((tm, D), row)]
    args = [x2]
    out_specs, out_shape = [], []
    if comb is not None:
        in_specs += ([_mod_row(per_b, 5), pl.BlockSpec((tm, LANES), row)]
                     + [pl.BlockSpec((tm, D), row)] * len(comb[2]))
        args += [comb[0], comb[1]] + list(comb[2])
        out_specs.append(pl.BlockSpec((tm, D), row))
        out_shape.append(jax.ShapeDtypeStruct((N, D), F32))
    in_specs += [pl.BlockSpec((1, D), lambda i: (0, 0)), _mod_row(per_b, 1), _mod_row(per_b, 0),
                 pl.BlockSpec((D, wt), lambda i: (0, 0))]
    args += [gain, mod6, mod6, w_cat]
    out_specs += [pl.BlockSpec((tm, RET_COLS), row), pl.BlockSpec((tm, RWKV_COLS), row),
                  pl.BlockSpec((tm, DSA_PAD), row), pl.BlockSpec((tm, SB_COLS), row)]
    out_shape += [jax.ShapeDtypeStruct((N, RET_COLS), F32), jax.ShapeDtypeStruct((N, RWKV_COLS), F32),
                  jax.ShapeDtypeStruct((N, DSA_PAD), F32), jax.ShapeDtypeStruct((N, SB_COLS), BF16)]
    return pl.pallas_call(
        functools.partial(_inproj_kernel, 0 if comb is None else len(comb[2])),
        grid=(N // tm,),
        in_specs=in_specs, out_specs=out_specs, out_shape=out_shape,
        compiler_params=_params(("parallel",)),
        name="in_proj",
    )(*args)


def _ret_kernel(q_ref, k_ref, v_ref, g_ref, cos_ref, sin_ref, din_ref, qd_ref, kd_ref, cd_ref, gn_ref,
                o_ref, state_ref, y_ref):
    @pl.when(pl.program_id(1) == 0)
    def _():
        state_ref[...] = jnp.zeros_like(state_ref)

    cos, sin = cos_ref[0], sin_ref[0]
    q = _rope(q_ref[0], cos, sin, HEAD_DIM, HEAD_DIM // 2)
    k = _rope(k_ref[0], cos, sin, HEAD_DIM, HEAD_DIM // 2) * HEAD_DIM ** -0.5
    v = v_ref[0]
    qd = q * qd_ref[...]
    kd = k * kd_ref[...]
    for h in range(N_HEADS_G):
        sl = slice(h * HEAD_DIM, (h + 1) * HEAD_DIM)
        qh, kh, vh = q[:, sl].astype(BF16), k[:, sl].astype(BF16), v[:, sl].astype(BF16)
        s = lax.dot_general(qh, kh, NT_DIMS, preferred_element_type=F32) * din_ref[h]
        inner = jnp.dot(s.astype(BF16), vh, preferred_element_type=F32)
        st = state_ref[h]
        cross = _bdot(qd[:, sl], st)
        state_ref[h] = st * cd_ref[:, sl] + lax.dot_general(
            kd[:, sl].astype(BF16), vh, TN_DIMS, preferred_element_type=F32)
        o = inner + cross
        oc = o - jnp.mean(o, axis=-1, keepdims=True)
        y_ref[:, sl] = oc * lax.rsqrt(jnp.mean(oc * oc, axis=-1, keepdims=True) + NORM_EPS)
    g = g_ref[0]
    o_ref[0] = (g * _sigmoid(g) * (y_ref[...] * gn_ref[...])).astype(BF16)


def _retention(ret3, tabs, consts, gn):
    B, S, _ = ret3.shape
    C = consts["din"].shape[1]
    blk = lambda j: pl.BlockSpec((1, C, GROUP_W), lambda b, c, j=j: (b, c, j))
    tab = pl.BlockSpec((1, C, GROUP_W), lambda b, c: (b, c, 0))
    const2 = lambda shape: pl.BlockSpec(shape, lambda b, c: (0,) * len(shape))
    return pl.pallas_call(
        _ret_kernel,
        grid=(B, S // C),
        in_specs=[blk(0), blk(1), blk(2), blk(3), tab, tab,
                  const2((N_HEADS_G, C, C)), const2((C, GROUP_W)), const2((C, GROUP_W)),
                  const2((1, GROUP_W)), const2((1, GROUP_W))],
        out_specs=pl.BlockSpec((1, C, GROUP_W), lambda b, c: (b, c, 0)),
        out_shape=jax.ShapeDtypeStruct((B, S, GROUP_W), BF16),
        scratch_shapes=[pltpu.VMEM((N_HEADS_G, HEAD_DIM, HEAD_DIM), F32), pltpu.VMEM((C, GROUP_W), F32)],
        compiler_params=_params(("parallel", "arbitrary")),
        name="retention",
    )(ret3, ret3, ret3, ret3, tabs[0], tabs[1],
      consts["din"], consts["qd"], consts["kd"], consts["cd"], gn)


def _ret_consts(C):
    H = N_HEADS_G
    lg = jnp.log(1.0 - 2.0 ** (-5.0 - jnp.arange(H, dtype=F32)))
    idx = jnp.arange(C, dtype=F32)
    diff = idx[:, None] - idx[None, :]
    din = jnp.where(diff >= 0, jnp.exp(lg[:, None, None] * jnp.maximum(diff, 0.0)), 0.0)
    q_dec = jnp.exp(lg[:, None] * (idx + 1.0))
    k_dec = jnp.exp(lg[:, None] * (C - 1.0 - idx))
    chunk_dec = jnp.exp(lg * C)
    wide = lambda t: jnp.repeat(t.T, HEAD_DIM, axis=1)
    return {"din": din, "qd": wide(q_dec), "kd": wide(k_dec),
            "cd": jnp.repeat(chunk_dec, HEAD_DIM)[None, :]}


def _rwkv_prep_kernel(per_b, z_ref, zp_ref, mu_ref, w0_ref, a0_ref, w2_ref, a2_ref, g2_ref, kk_ref, ka_ref,
                      rk_ref, hm_ref, r_o, w_o, k_o, v_o, nkk_o, b_o, gate_o, bonus_o):
    z = z_ref[...]
    first = (pl.program_id(0) % per_b) == 0
    prow = jnp.where(first, 0.0, zp_ref[7:8, :])
    rid = lax.broadcasted_iota(I32, z.shape, 0)
    prev = jnp.where(rid == 0, prow, pltpu.roll(z, 1, 0))
    f = z + (prev - z) * mu_ref[...]
    G = GROUP_W
    r, k, v, lo = f[:, 0:G], f[:, G:2 * G], f[:, 2 * G:3 * G], f[:, 3 * G:3 * G + 128]
    w_log = -_softplus(-(w0_ref[...] + _bdot(jnp.tanh(lo), w2_ref[...]))) - 0.5
    decay = jnp.exp(-jnp.exp(w_log))
    a = _sigmoid(a0_ref[...] + _bdot(lo, a2_ref[...]))
    gate = _bdot(_sigmoid(lo), g2_ref[...])
    hm = hm_ref[...]
    kk = k * kk_ref[...]
    kk = kk / jnp.maximum(jnp.sqrt(_split_dot(kk * kk, hm)), 1e-12)
    k2 = k * (1.0 + (a - 1.0) * ka_ref[...])
    r_o[...] = r.astype(BF16)
    w_o[...] = decay
    k_o[...] = k2.astype(BF16)
    v_o[...] = v.astype(BF16)
    nkk_o[...] = -kk
    b_o[...] = kk * a
    gate_o[...] = gate
    bonus_o[...] = _split_dot(r * k2 * rk_ref[...], hm) * v


def _rwkv_prep(z2, S, mu, w0, a0, w2p, a2p, g2p, kk, ka, rk, hm):
    N, W = z2.shape
    tm = ROW_TILE
    per_b = S // tm
    row = lambda i: (i, 0)
    c = lambda shape: pl.BlockSpec(shape, lambda i: (0, 0))
    G = GROUP_W
    return pl.pallas_call(
        functools.partial(_rwkv_prep_kernel, per_b),
        grid=(N // tm,),
        in_specs=[pl.BlockSpec((tm, W), row),
                  pl.BlockSpec((SUBLANES, W), lambda i: (jnp.maximum(i * (tm // SUBLANES) - 1, 0), 0)),
                  c((1, W)), c((1, G)), c((1, G)), c((128, G)), c((128, G)), c((128, G)),
                  c((1, G)), c((1, G)), c((1, G)), c((G, G))],
        out_specs=[pl.BlockSpec((tm, G), row)] * 8,
        out_shape=[jax.ShapeDtypeStruct((N, G), dt) for dt in (BF16, F32, BF16, BF16, F32, F32, F32, F32)],
        compiler_params=_params(("parallel",)),
        name="rwkv_prep",
    )(z2, z2, mu, w0, a0, w2p, a2p, g2p, kk, ka, rk, hm)


def _rwkv_scan_kernel(T, nkk_ref, w_ref, b_ref, k_ref, r_ref, v_ref, y_ref, s_ref, rep_ref):
    @pl.when(pl.program_id(0) == 0)
    def _():
        s_ref[...] = jnp.zeros_like(s_ref)

    ch = v_ref.shape[1]
    groups = HEAD_DIM // ch
    width = LANES // groups
    tree = lambda parts: _tree(parts, jnp.add)

    grp = lax.broadcasted_iota(I32, (T * ch, LANES), 1) // width
    for o, ref in enumerate((nkk_ref, w_ref, b_ref, k_ref, r_ref)):
        x = ref[...].astype(F32).reshape(T * ch, LANES)
        turned = [x] + [pltpu.roll(x, m * width, 1) for m in range(1, groups)]
        for j in range(groups):
            y = turned[0]
            for m in range(1, groups):
                y = jnp.where(grp == (j + m) % groups, turned[m], y)
            rep_ref[o, :, j] = y.reshape(T, ch, LANES)

    keys = [(j, kh) for j in range(groups) for kh in range(ch)]
    row = lambda o, t, j, kh: rep_ref[o, t, j, kh:kh + 1, :]

    def step(t, carry):
        vt = v_ref[t].astype(F32)
        sa = tree([s_ref[j * ch + kh] * row(0, t, j, kh) for j, kh in keys])
        ys = []
        for j, kh in keys:
            s_new = (s_ref[j * ch + kh] * row(1, t, j, kh) + sa * row(2, t, j, kh) + vt * row(3, t, j, kh))
            s_ref[j * ch + kh] = s_new
            ys.append(s_new * row(4, t, j, kh))
        y_ref[t] = tree(ys)
        return carry

    lax.fori_loop(0, T, step, 0)


def _rwkv_scan(nkk, w, b, k, r, v):
    S, ch, _ = v.shape
    T = 32
    spec = pl.BlockSpec((T, ch, LANES), lambda i: (i, 0, 0))
    return pl.pallas_call(
        functools.partial(_rwkv_scan_kernel, T),
        grid=(S // T,),
        in_specs=[spec] * 6,
        out_specs=spec,
        out_shape=jax.ShapeDtypeStruct((S, ch, LANES), F32),
        scratch_shapes=[pltpu.VMEM((HEAD_DIM, ch, LANES), F32),
                        pltpu.VMEM((5, T, HEAD_DIM // ch, ch, LANES), F32)],
        compiler_params=_params(("arbitrary",)),
        name="rwkv_scan",
    )(nkk, w, b, k, r, v)


def _to_v_layout(x, B, S):
    P = B * N_HEADS_G
    rep = LANES // P
    t = x.reshape(B, S, N_HEADS_G, HEAD_DIM // rep, rep).transpose(1, 3, 4, 0, 2)
    return t.reshape(S, HEAD_DIM // rep, LANES)


def _from_v_layout(y, B, S):
    P = B * N_HEADS_G
    rep = LANES // P
    t = y.reshape(S, HEAD_DIM // rep, rep, B, N_HEADS_G).transpose(3, 0, 4, 1, 2)
    return t.reshape(B * S, GROUP_W)


def _dsa_prep_kernel(f_ref, qn_ref, wq_ref, wqi_ref, cq_ref, sq_ref, ci_ref, si_ref,
                     qt_o, qit_o, wt_o, k_o, v_o, ki_o):
    f = f_ref[...]
    cq = _rms(f[:, 0:DSA_Q_LORA], qn_ref[...]).astype(BF16)
    cos_q, sin_q = cq_ref[...], sq_ref[...]
    cos_i, sin_i = ci_ref[...], si_ref[...]
    q = _rope(jnp.dot(cq, wq_ref[...], preferred_element_type=F32), cos_q, sin_q, HEAD_DIM, HEAD_DIM // 8)
    tm = f.shape[0]
    qt_o[...] = (q * HEAD_DIM ** -0.5).T.reshape(N_HEADS_G, HEAD_DIM, tm).astype(BF16)
    qi = _rope(jnp.dot(cq, wqi_ref[...], preferred_element_type=F32), cos_i, sin_i, IDX_DIM, IDX_DIM // 8)
    qit_o[...] = qi.T.reshape(IDX_HEADS, IDX_DIM, tm).astype(BF16)
    kv = f[:, 128:256]
    kv_r = _rope(kv, cos_q[:, 0:128], sin_q[:, 0:128], HEAD_DIM, HEAD_DIM // 8)
    k_o[...] = kv_r[:, 0:HEAD_DIM].astype(BF16)
    v_o[...] = kv[:, HEAD_DIM:128].astype(BF16)
    tail = f[:, 256:384]
    tail_r = _rope(tail, cos_i[:, 0:128], sin_i[:, 0:128], IDX_DIM, IDX_DIM // 8)
    ki_o[...] = tail_r[:, 0:IDX_DIM].astype(BF16)
    wt_o[...] = tail.T[IDX_DIM:IDX_DIM + IDX_HEADS, :] * (IDX_HEADS ** -0.5 * IDX_DIM ** -0.5)


def _dsa_prep(f2, qn, wq, wqi, tq, ti):
    N, W = f2.shape
    tm = ROW_TILE
    G = GROUP_W
    row = lambda w: pl.BlockSpec((tm, w), lambda i: (i, 0))
    col = lambda h: pl.BlockSpec((h, tm), lambda i: (0, i))
    slab = lambda h, d: pl.BlockSpec((h, d, tm), lambda i: (0, 0, i))
    c = lambda shape: pl.BlockSpec(shape, lambda i: (0, 0))
    return pl.pallas_call(
        _dsa_prep_kernel,
        grid=(N // tm,),
        in_specs=[row(W), c((1, DSA_Q_LORA)), c((DSA_Q_LORA, G)), c((DSA_Q_LORA, G))] + [row(G)] * 4,
        out_specs=[slab(N_HEADS_G, HEAD_DIM), slab(IDX_HEADS, IDX_DIM), col(IDX_HEADS),
                   row(HEAD_DIM), row(HEAD_DIM), row(IDX_DIM)],
        out_shape=[jax.ShapeDtypeStruct((N_HEADS_G, HEAD_DIM, N), BF16),
                   jax.ShapeDtypeStruct((IDX_HEADS, IDX_DIM, N), BF16),
                   jax.ShapeDtypeStruct((IDX_HEADS, N), F32),
                   jax.ShapeDtypeStruct((N, HEAD_DIM), BF16), jax.ShapeDtypeStruct((N, HEAD_DIM), BF16),
                   jax.ShapeDtypeStruct((N, IDX_DIM), BF16)],
        compiler_params=_params(("parallel",)),
        name="dsa_prep",
    )(f2, qn, wq, wqi, *tq, *ti)


def _float_key(bits):
    return bits ^ ((bits >> 31) & 0x7FFFFFFF)


def _dsa_kernel(TK, topk, qt_ref, qit_ref, wt_ref, k_ref, v_ref, ki_ref, on_ref, o_ref,
                skey_ref, lg_ref, out_ref):
    i = pl.program_id(1)
    TQ = Q_BLOCK
    n_kt = (i * TQ + TQ + TK - 1) // TK
    q_all = jnp.concatenate([qt_ref[h] for h in range(N_HEADS_G)], axis=1)
    qi_all = jnp.concatenate([qit_ref[h] for h in range(IDX_HEADS)], axis=1)
    wt = wt_ref[...]
    kpos = lax.broadcasted_iota(I32, (TK, TQ), 0)
    qpos = i * TQ + lax.broadcasted_iota(I32, (TK, TQ), 1)
    head = lambda x, h: x[:, h * TQ:(h + 1) * TQ]

    def fold(x, op):
        return _tree([x[r * SUBLANES:(r + 1) * SUBLANES] for r in range(TK // SUBLANES)], op)

    def score_tile(kt, tops):
        off = pl.multiple_of(kt * TK, TK)
        rel = jnp.dot(ki_ref[0, pl.ds(off, TK), :], qi_all, preferred_element_type=F32)
        sc = jnp.zeros((TK, TQ), F32)
        for h in range(IDX_HEADS):
            sc = sc + jnp.maximum(head(rel, h), 0.0) * wt[h:h + 1, :]
        key = _float_key(pltpu.bitcast(sc, I32))
        key = jnp.where(sc == 0.0, -(kt * TK + kpos), jnp.where(key < 0, key - skey_ref.shape[0] * TK, key))
        valid = kt * TK + kpos <= qpos
        skey_ref[kt] = jnp.where(valid, key, INT_MIN)
        lg = jnp.dot(k_ref[0, pl.ds(off, TK), :], q_all, preferred_element_type=F32)
        lg_ref[kt] = lg
        return tuple(jnp.maximum(tops[h], fold(jnp.where(valid, head(lg, h), -1e30), jnp.maximum))
                     for h in range(N_HEADS_G))

    lowest = tuple(jnp.full((SUBLANES, TQ), -1e30, F32) for _ in range(N_HEADS_G))
    tops = lax.fori_loop(0, n_kt, score_tile, lowest)

    def count(pred):
        def count_tile(kt, acc):
            return acc + fold(jnp.where(pred(kt, skey_ref[kt]), 1.0, 0.0), jnp.add)

        acc = lax.fori_loop(0, n_kt, count_tile, jnp.zeros((SUBLANES, TQ), F32))
        return jnp.sum(acc, axis=0, keepdims=True)

    def bit_step(j, carry):
        thr, n_thr = carry
        cand = thr + lax.shift_left(jnp.int32(1), 31 - j)
        n = count(lambda kt, keys: keys >= cand)
        return jnp.where(n >= topk, cand, thr), jnp.where(n >= topk, n, n_thr)

    thr, n_thr = lax.fori_loop(0, 32, bit_step,
                               (jnp.full((1, TQ), INT_MIN, I32), jnp.full((1, TQ), float(topk), F32)))
    thr = jnp.maximum(thr, INT_MIN + 1)

    @pl.when(jnp.max(n_thr) > topk)
    def _():
        keep = topk - count(lambda kt, keys: keys > thr)

        pos_bits = (skey_ref.shape[0] * TK - 1).bit_length()

        def pos_step(j, last):
            cand = last + lax.shift_left(jnp.int32(1), pos_bits - 1 - j)
            n = count(lambda kt, keys: jnp.logical_and(keys == thr, kt * TK + kpos < cand))
            return jnp.where(n < keep, cand, last)

        last = lax.fori_loop(0, pos_bits, pos_step, jnp.zeros((1, TQ), I32))

        def retire(kt, carry):
            keys = skey_ref[kt]
            skey_ref[kt] = jnp.where(jnp.logical_and(keys == thr, kt * TK + kpos > last), INT_MIN, keys)
            return carry

        lax.fori_loop(0, n_kt, retire, 0)

    def sweep(shifts):
        def acc_tile(kt, carry):
            ls, acc = carry
            sel = skey_ref[kt] >= thr
            ps = [jnp.where(sel, jnp.exp(lg_ref[kt, :, h * TQ:(h + 1) * TQ] - shifts[h]), 0.0)
                  for h in range(N_HEADS_G)]
            p_all = jnp.concatenate([p.astype(BF16) for p in ps], axis=1)
            vb = v_ref[0, pl.ds(pl.multiple_of(kt * TK, TK), TK), :]
            return (tuple(ls[h] + fold(ps[h], jnp.add) for h in range(N_HEADS_G)),
                    acc + lax.dot_general(vb, p_all, TN_DIMS, preferred_element_type=F32))

        ls, acc = lax.fori_loop(0, n_kt, acc_tile,
                                (tuple(jnp.zeros((SUBLANES, TQ), F32) for _ in range(N_HEADS_G)),
                                 jnp.zeros((HEAD_DIM, N_HEADS_G * TQ), F32)))
        ls = [jnp.sum(l, axis=0, keepdims=True) for l in ls]
        for h in range(N_HEADS_G):
            out_ref[h * HEAD_DIM:(h + 1) * HEAD_DIM, :] = head(acc, h) / ls[h]
        return jnp.min(jnp.minimum(jnp.minimum(ls[0], ls[1]), jnp.minimum(ls[2], ls[3])))

    smallest = sweep([jnp.max(t, axis=0, keepdims=True) for t in tops])

    @pl.when(smallest < SOFTMAX_FLOOR)
    def _():
        def max_tile(kt, ms):
            sel = skey_ref[kt] >= thr
            return tuple(jnp.maximum(ms[h], fold(jnp.where(sel, lg_ref[kt, :, h * TQ:(h + 1) * TQ], -1e30),
                                                 jnp.maximum))
                         for h in range(N_HEADS_G))

        ms = lax.fori_loop(0, n_kt, max_tile, lowest)
        sweep([jnp.max(m, axis=0, keepdims=True) for m in ms])

    o_ref[0] = _rms(out_ref[...].T, on_ref[...]).astype(BF16)


def _dsa_attention(qt, qit, wt, k, v, ki, on):
    B, S, _ = k.shape
    G = GROUP_W
    nq = S // Q_BLOCK
    TK = min(512, S)
    topk = min(DSA_TOPK_MAX, S // 4)
    qcol = lambda h: pl.BlockSpec((h, Q_BLOCK), lambda b, i: (0, b * nq + i))
    qslab = lambda h, d: pl.BlockSpec((h, d, Q_BLOCK), lambda b, i: (0, 0, b * nq + i))
    full = lambda wd: pl.BlockSpec((1, S, wd), lambda b, i: (b, 0, 0))
    return pl.pallas_call(
        functools.partial(_dsa_kernel, TK, topk),
        grid=(B, nq),
        in_specs=[qslab(N_HEADS_G, HEAD_DIM), qslab(IDX_HEADS, IDX_DIM), qcol(IDX_HEADS),
                  full(HEAD_DIM), full(HEAD_DIM), full(IDX_DIM), pl.BlockSpec((1, G), lambda b, i: (0, 0))],
        out_specs=pl.BlockSpec((1, Q_BLOCK, G), lambda b, i: (b, i, 0)),
        out_shape=jax.ShapeDtypeStruct((B, S, G), BF16),
        scratch_shapes=[pltpu.VMEM((S // TK, TK, Q_BLOCK), I32),
                        pltpu.VMEM((S // TK, TK, N_HEADS_G * Q_BLOCK), F32),
                        pltpu.VMEM((G, Q_BLOCK), F32)],
        compiler_params=_params(("parallel", "arbitrary")),
        name="dsa_attention",
    )(qt, qit, wt, k, v, ki, on)


def _sb_kernel(q_ref, k_ref, v_ref, u_ref, on_ref, o_ref, z_ref, lm_ref):
    i = pl.program_id(1)
    T = Q_BLOCK
    H, G = N_HEADS_G, GROUP_W
    u = u_ref[...]
    qt = q_ref[0].astype(F32).T.astype(BF16)
    q_rows = lax.broadcasted_iota(I32, (G, H * T), 0) // HEAD_DIM
    q_cols = lax.broadcasted_iota(I32, (G, H * T), 1) // T
    qbd = jnp.where(q_rows == q_cols, jnp.concatenate([qt] * H, axis=1), jnp.zeros((), BF16))
    kid = lax.broadcasted_iota(I32, (T, H * T), 0)
    qid = lax.broadcasted_iota(I32, (T, H * T), 1) & (T - 1)
    strict = kid < qid

    def cond(carry):
        kt, c, _ = carry
        return jnp.logical_and(kt >= 0, jnp.max(c) > SB_UNDERFLOW)

    def logits(kt):
        z = jnp.dot(k_ref[0, pl.ds(pl.multiple_of(kt * T, T), T), :], qbd, preferred_element_type=F32)
        z = z * HEAD_DIM ** -0.5
        return z, -_softplus(z)

    z0, lm0 = logits(i)
    z_ref[...] = z0
    lm_ref[...] = jnp.where(strict, lm0, 0.0)

    def body(carry):
        kt, c, acc = carry
        off = pl.multiple_of(kt * T, T)
        mask = jnp.logical_or(strict, kt < i)
        z, lm = z_ref[...], lm_ref[...]
        z_ref[...], lm_ref[...] = logits(jnp.maximum(kt - 1, 0))
        hi, lo = _split(lm)
        later = jnp.dot(u, jnp.concatenate([hi, lo], axis=1), preferred_element_type=F32)
        later = later[:, :H * T] + later[:, H * T:]
        a = jnp.where(mask, jnp.exp(z + lm + (c + later)), 0.0)
        av = lax.dot_general(v_ref[0, pl.ds(off, T), :], a.astype(BF16), TN_DIMS, preferred_element_type=F32)
        acc = acc + jnp.concatenate(
            [av[h * HEAD_DIM:(h + 1) * HEAD_DIM, h * T:(h + 1) * T] for h in range(H)], axis=0)
        return kt - 1, c + jnp.sum(lm, axis=0, keepdims=True), acc

    _, _, acc = lax.while_loop(cond, body, (i, jnp.zeros((1, H * T), F32), jnp.zeros((G, T), F32)))
    o_ref[0] = _rms(acc.T, on_ref[...]).astype(BF16)


def _stick_breaking(sb3, u, on):
    B, S, _ = sb3.shape
    G = GROUP_W
    return pl.pallas_call(
        _sb_kernel,
        grid=(B, S // Q_BLOCK),
        in_specs=[pl.BlockSpec((1, Q_BLOCK, G), lambda b, i: (b, i, 0)),
                  pl.BlockSpec((1, S, G), lambda b, i: (b, 0, 1)),
                  pl.BlockSpec((1, S, G), lambda b, i: (b, 0, 2)),
                  pl.BlockSpec((Q_BLOCK, Q_BLOCK), lambda b, i: (0, 0)),
                  pl.BlockSpec((1, G), lambda b, i: (0, 0))],
        out_specs=pl.BlockSpec((1, Q_BLOCK, G), lambda b, i: (b, i, 0)),
        out_shape=jax.ShapeDtypeStruct((B, S, G), BF16),
        scratch_shapes=[pltpu.VMEM((Q_BLOCK, N_HEADS_G * Q_BLOCK), F32)] * 2,
        compiler_params=_params(("parallel", "arbitrary")),
        name="stick_breaking",
    )(sb3, sb3, sb3, u, on)


def _outproj_kernel(yr_ref, yd_ref, ys_ref, yw_ref, bonus_ref, gate_ref, ln_ref, hm_ref, x_ref, g1_ref, sc_ref,
                    sh_ref, gn_ref, wo_ref, rw_ref, rb_ref, x_o, h_o, lg_o):
    G = GROUP_W
    y = yw_ref[...]
    hm = hm_ref[...]
    yc = y - _split_dot(y, hm) * (1.0 / HEAD_DIM)
    var = _split_dot(yc * yc, hm) * (1.0 / HEAD_DIM)
    yw = ((yc * lax.rsqrt(var + RWKV_LN_EPS) * ln_ref[...] + bonus_ref[...]) * gate_ref[...]).astype(BF16)
    dot = lambda yv, g: jnp.dot(yv, wo_ref[g * G:(g + 1) * G, :], preferred_element_type=F32)
    mixed = dot(yr_ref[...], 0) + dot(yw, 1) + dot(yd_ref[...], 2) + dot(ys_ref[...], 3)
    x1 = x_ref[...] + g1_ref[0] * mixed
    x_o[...] = x1
    h = _rms(x1, gn_ref[...]) * (1.0 + sc_ref[0]) + sh_ref[0]
    h_o[...] = h
    hi, lo = _split(h)
    work = (jnp.dot(hi, rw_ref[0], preferred_element_type=F32) + jnp.dot(lo, rw_ref[0], preferred_element_type=F32)
            + jnp.dot(hi, rw_ref[1], preferred_element_type=F32) + rb_ref[...])
    col = lax.broadcasted_iota(I32, work.shape, 1).astype(F32)
    out = jnp.zeros_like(work)
    vals = []
    for j in range(TOP_K):
        m = jnp.max(work, axis=-1, keepdims=True)
        idx = jnp.min(jnp.where(work == m, col, float(LANES)), axis=-1, keepdims=True)
        out = jnp.where(col == float(j), idx, out)
        vals.append(m)
        work = jnp.where(col == idx, -jnp.inf, work)
    es = [jnp.exp(v - vals[0]) for v in vals]
    den = es[0] + es[1] + es[2] + es[3]
    for j in range(TOP_K):
        out = jnp.where(col == float(TOP_K + j), es[j] / den, out)
    lg_o[...] = out


def _outproj(ys, rwkv, x2, mod6, gain, w_out, router_w, router_b, S):
    N, D = x2.shape
    tm = ROW_TILE
    per_b = S // tm
    G = GROUP_W
    row = lambda w: pl.BlockSpec((tm, w), lambda i: (i, 0))
    c = lambda shape: pl.BlockSpec(shape, lambda i: (0, 0))
    return pl.pallas_call(
        _outproj_kernel,
        grid=(N // tm,),
        in_specs=[row(G)] * 6 + [c((1, G)), c((G, G)),
                                 row(D), _mod_row(per_b, 2), _mod_row(per_b, 4), _mod_row(per_b, 3),
                                 c((1, D)), c((D, D)), pl.BlockSpec((2, D, LANES), lambda i: (0, 0, 0)),
                                 c((1, LANES))],
        out_specs=[row(D), row(D), row(LANES)],
        out_shape=[jax.ShapeDtypeStruct((N, D), F32), jax.ShapeDtypeStruct((N, D), F32),
                   jax.ShapeDtypeStruct((N, LANES), F32)],
        compiler_params=_params(("parallel",)),
        name="out_proj_router",
    )(*ys, *rwkv, x2, mod6, mod6, mod6, gain, w_out,
      jnp.pad(jnp.stack(_split(router_w)), ((0, 0), (0, 0), (0, LANES - N_EXPERTS))),
      jnp.pad(router_b, ((0, 0), (0, LANES - N_EXPERTS)), constant_values=-1e30))


def _moe_kernel(be_ref, nu_ref, x_ref, w1_ref, b1_ref, w2_ref, b2_ref, o_ref):
    i = pl.program_id(0)

    @pl.when(i < nu_ref[0])
    def _():
        xb = x_ref[...].astype(BF16)
        w1 = w1_ref.at[0, 0]
        glu = jnp.dot(xb, w1[:, 0:D_FF].astype(BF16), preferred_element_type=F32) + b1_ref[0, :, 0:D_FF]
        lin = jnp.dot(xb, w1[:, D_FF:].astype(BF16), preferred_element_type=F32) + b1_ref[0, :, D_FF:]
        glu = jnp.minimum(glu, SWIGLU_LIMIT)
        lin = jnp.clip(lin, -SWIGLU_LIMIT, SWIGLU_LIMIT)
        act = glu * _sigmoid(SWIGLU_ALPHA * glu) * (lin + 1.0)
        y = jnp.dot(act.astype(BF16), w2_ref[0, 0].astype(BF16), preferred_element_type=F32) + b2_ref[0]
        o_ref[...] = y.astype(BF16)

    @pl.when(i >= nu_ref[0])
    def _():
        o_ref[...] = jnp.zeros_like(o_ref)


def _moe_ffn(block_expert, n_used, xs, layer, w1, b1, w2, b2):
    rows, D = xs.shape
    nb = rows // MOE_BLOCK
    L, E = w1.shape[:2]
    live = lambda i, nu: jnp.minimum(i, nu[0] - 1)
    grid_spec = pltpu.PrefetchScalarGridSpec(
        num_scalar_prefetch=2,
        grid=(nb,),
        in_specs=[pl.BlockSpec((MOE_BLOCK, D), lambda i, be, nu: (live(i, nu), 0)),
                  pl.BlockSpec((1, 1, D, 2 * D_FF), lambda i, be, nu: (layer, be[i], 0, 0)),
                  pl.BlockSpec((1, 1, 2 * D_FF), lambda i, be, nu: (layer * E + be[i], 0, 0)),
                  pl.BlockSpec((1, 1, D_FF, D), lambda i, be, nu: (layer, be[i], 0, 0)),
                  pl.BlockSpec((1, 1, D), lambda i, be, nu: (layer * E + be[i], 0, 0))],
        out_specs=pl.BlockSpec((MOE_BLOCK, D), lambda i, be, nu: (i, 0)),
    )
    return pl.pallas_call(
        _moe_kernel,
        grid_spec=grid_spec,
        out_shape=jax.ShapeDtypeStruct((rows, D), BF16),
        compiler_params=_params(("arbitrary",), 56 * 1024 * 1024),
        name="moe_ffn",
    )(block_expert, n_used, xs, w1, b1.reshape(L * E, 1, -1), w2, b2.reshape(L * E, 1, -1))


def _route(routed):
    n = routed.shape[0]
    expert = routed[:, 0:TOP_K].reshape(-1).astype(I32)
    n_assign = n * TOP_K
    ids = jnp.arange(n_assign, dtype=I32)
    _, order = lax.sort((expert, ids), num_keys=1, is_stable=True)
    _, inv = lax.sort((order, ids), num_keys=1)
    counts = jnp.sum((expert[:, None] == jnp.arange(N_EXPERTS, dtype=I32)[None, :]).astype(I32), axis=0)
    starts = jnp.cumsum(counts) - counts
    padded = (counts + MOE_BLOCK - 1) // MOE_BLOCK * MOE_BLOCK
    pad_end = jnp.cumsum(padded)
    pad_start = pad_end - padded
    dest = pad_start[expert] + inv - starts[expert]
    n_blocks = -(-n_assign // MOE_BLOCK) + N_EXPERTS
    block_start = jnp.arange(n_blocks, dtype=I32) * MOE_BLOCK
    block_expert = jnp.minimum(jnp.sum(pad_end[None, :] <= block_start[:, None], axis=1), N_EXPERTS - 1)
    block_expert = block_expert.astype(I32)
    per_row = lambda t: jnp.repeat(t[block_expert], MOE_BLOCK)
    off = jnp.arange(n_blocks * MOE_BLOCK, dtype=I32) - per_row(pad_start)
    valid = off < per_row(counts)
    src = order[jnp.clip(per_row(starts) + off, 0, n_assign - 1)]
    row_token = jnp.where(valid, src // TOP_K, 0)
    n_used = (pad_end[-1] // MOE_BLOCK).astype(I32).reshape(1)
    return dest.reshape(n, TOP_K), row_token, block_expert, n_used


def _rows(t, idx):
    return t.at[idx].get(mode="promise_in_bounds")


def _final_kernel(x_ref, g2_ref, routed_ref, y0, y1, y2, y3, g_ref, o_ref):
    o_ref[...] = _rms(_combine(x_ref, g2_ref, routed_ref, (y0, y1, y2, y3)), g_ref[...])


def _final_norm(x2, mod6, routed, ys, g, S):
    N, D = x2.shape
    tm = ROW_TILE
    row = pl.BlockSpec((tm, D), lambda i: (i, 0))
    return pl.pallas_call(
        _final_kernel, grid=(N // tm,),
        in_specs=[row, _mod_row(S // tm, 5), pl.BlockSpec((tm, LANES), lambda i: (i, 0))] + [row] * 4
        + [pl.BlockSpec((1, D), lambda i: (0, 0))],
        out_specs=row,
        out_shape=jax.ShapeDtypeStruct((N, D), F32),
        compiler_params=_params(("parallel",)), name="final_norm",
    )(x2, mod6, routed, *ys, g)


def _rope_tables(pos, group, rot_dim, theta, width):
    half = rot_dim // 2
    inv = theta ** (-jnp.arange(half, dtype=F32) / half)
    ang = pos.astype(F32)[..., None] * inv
    cos, sin = jnp.cos(ang), jnp.sin(ang)
    rest = group - rot_dim
    ones = jnp.ones(ang.shape[:2] + (rest,), F32)
    zeros = jnp.zeros(ang.shape[:2] + (rest,), F32)
    rep = width // group
    tile = lambda t: jnp.tile(t, (1, 1, rep)).reshape(-1, width)
    return (tile(jnp.concatenate([cos, cos, ones], -1)), tile(jnp.concatenate([-sin, sin, zeros], -1)))


def _pad_rows(w, start, total=128):
    return jnp.zeros((total, w.shape[1]), w.dtype).at[start:start + w.shape[0]].set(w)


def kernel(x, c, positions, ada_w, ada_b, norm_mix, norm_ffn, w_in, ret_gn, rwkv_mu, rwkv_w0, rwkv_w2, rwkv_a0,
           rwkv_a2, rwkv_g2, rwkv_kk, rwkv_ka, rwkv_rk, rwkv_ln, dsa_qnorm, dsa_wq_up, dsa_wqi_up, dsa_onorm,
           sb_onorm, w_out, router_w, router_b, moe_w1, moe_b1, moe_w2, moe_b2, norm_final):
    B, S, D = x.shape
    N = B * S
    L = ada_w.shape[0]
    G = GROUP_W
    row2 = lambda t: t.reshape(1, -1)

    mod = _ada(c, ada_w, ada_b)
    ret_tabs = tuple(t.reshape(B, S, G) for t in _rope_tables(positions, HEAD_DIM, HEAD_DIM, RET_THETA, G))
    dq_tabs = _rope_tables(positions, HEAD_DIM, HEAD_DIM // 4, ROPE_THETA, G)
    di_tabs = _rope_tables(positions, IDX_DIM, IDX_DIM // 4, ROPE_THETA, G)
    ret_c = _ret_consts(min(256, S))
    hm = (jnp.arange(G)[:, None] // HEAD_DIM == jnp.arange(G)[None, :] // HEAD_DIM).astype(BF16)
    u_later = (jnp.arange(Q_BLOCK)[None, :] > jnp.arange(Q_BLOCK)[:, None]).astype(BF16)
    o1, o2 = RET_COLS + RWKV_COLS, RET_COLS + RWKV_COLS + DSA_COLS

    x2 = x.reshape(N, D)
    comb = None
    for l in range(L):
        mod6 = mod[l].reshape(B * 6, 1, D)
        w_l = w_in[l]
        w_cat = jnp.concatenate(
            [w_l[:, :o1], w_l[:, o1:o2], jnp.zeros((D, DSA_PAD - DSA_COLS), F32), w_l[:, o2:]], axis=1).astype(BF16)
        outs = _inproj(x2, row2(norm_mix[l]), mod6, w_cat, S, comb)
        if comb is not None:
            x2, outs = outs[0], outs[1:]
        ret, rwkv, dsa, sb = outs

        prep = _rwkv_prep(rwkv, S, row2(rwkv_mu[l]), row2(rwkv_w0[l]), row2(rwkv_a0[l]),
                          _pad_rows(rwkv_w2[l], 0).astype(BF16), _pad_rows(rwkv_a2[l], 32).astype(BF16),
                          _pad_rows(rwkv_g2[l], 64).astype(BF16), row2(rwkv_kk[l]), row2(rwkv_ka[l]),
                          row2(rwkv_rk[l]), hm)
        r_, w_, k_, v_, nkk_, b_, gate_, bonus_ = prep
        scan_in = tuple(_to_v_layout(t, B, S) for t in (nkk_, w_, b_, k_, r_, v_))

        y_ret = _retention(ret.reshape(B, S, RET_COLS), ret_tabs, ret_c, row2(ret_gn[l])).reshape(N, G)

        dqt, dqit, dwt, dk, dv, dki = _dsa_prep(dsa, row2(dsa_qnorm[l]), dsa_wq_up[l].astype(BF16),
                                                dsa_wqi_up[l].astype(BF16), dq_tabs, di_tabs)
        y_dsa = _dsa_attention(dqt, dqit, dwt, dk.reshape(B, S, HEAD_DIM), dv.reshape(B, S, HEAD_DIM),
                               dki.reshape(B, S, IDX_DIM), row2(dsa_onorm[l])).reshape(N, G)

        y_sb = _stick_breaking(sb.reshape(B, S, SB_COLS), u_later, row2(sb_onorm[l])).reshape(N, G)

        y_scan = _rwkv_scan(*scan_in)
        rwkv_tail = (_from_v_layout(y_scan, B, S), bonus_, gate_, row2(rwkv_ln[l]), hm)

        x1, h2, routed = _outproj((y_ret, y_dsa, y_sb), rwkv_tail, x2, mod6, row2(norm_ffn[l]),
                                  w_out[l].astype(BF16), router_w[l], row2(router_b[l]), S)

        dest, row_token, block_expert, n_used = _route(routed)
        y = _moe_ffn(block_expert, n_used, _rows(h2, row_token), l, moe_w1, moe_b1, moe_w2, moe_b2)
        x2 = x1
        comb = (mod6, routed, tuple(_rows(y, dest[:, j]) for j in range(TOP_K)))
    return _final_norm(x2, *comb, row2(norm_final), S).reshape(B, S, D)
```

```python
import functools

import jax
import jax.numpy as jnp
from jax import lax
from jax.experimental import pallas as pl
from jax.experimental.pallas import tpu as pltpu

F32 = jnp.float32
BF16 = jnp.bfloat16
I32 = jnp.int32

D_MODEL = 1024
GROUP_W = 256
HEAD_DIM = 64
N_HEADS_G = 4
NORM_EPS = 1e-5
Q_BLOCK = 128
RET_THETA = 10000.0
RWKV_LN_EPS = 64e-5
DSA_Q_LORA = 128
IDX_HEADS = 8
IDX_DIM = 32
DSA_TOPK_MAX = 256
ROPE_THETA = 500000.0
N_EXPERTS = 32
TOP_K = 4
D_FF = D_MODEL
SWIGLU_ALPHA = 1.702
SWIGLU_LIMIT = 7.0
MOE_BLOCK = 512

RET_COLS = 4 * GROUP_W
RWKV_COLS = 3 * GROUP_W + 128
DSA_COLS = 296
DSA_PAD = 384
SB_COLS = 3 * GROUP_W

LANES = 128
SUBLANES = 8
ROW_TILE = 512
VMEM_LIMIT = 48 * 1024 * 1024
INT_MIN = -2 ** 31
SB_UNDERFLOW = -104.0
SOFTMAX_FLOOR = 1e-20

HIGHEST = lax.Precision.HIGHEST
NT_DIMS = (((1,), (1,)), ((), ()))
TN_DIMS = (((0,), (0,)), ((), ()))


def _params(sem, vmem=VMEM_LIMIT):
    return pltpu.CompilerParams(dimension_semantics=sem, vmem_limit_bytes=vmem)


def _bdot(a, b):
    return jnp.dot(a.astype(BF16), b.astype(BF16), preferred_element_type=F32)


def _split(x):
    hi = x.astype(BF16)
    return hi, (x - hi.astype(F32)).astype(BF16)


def _split_dot(x, m):
    hi, lo = _split(x)
    return jnp.dot(hi, m, preferred_element_type=F32) + jnp.dot(lo, m, preferred_element_type=F32)


def _tree(parts, op):
    while len(parts) > 1:
        parts = [op(parts[i], parts[i + 1]) for i in range(0, len(parts), 2)]
    return parts[0]


def _sigmoid(x):
    return 1.0 / (1.0 + jnp.exp(-x))


def _softplus(x):
    return jnp.maximum(x, 0.0) + jnp.log(1.0 + jnp.exp(-jnp.abs(x)))


def _rms(x, g, eps=NORM_EPS):
    return x * lax.rsqrt(jnp.mean(x * x, axis=-1, keepdims=True) + eps) * g


def _rope(x, cos, sin, group, half):
    w = x.shape[-1]
    first = (lax.broadcasted_iota(I32, (1, w), 1) & (group - 1)) < half
    partner = jnp.where(first, pltpu.roll(x, w - half, 1), pltpu.roll(x, half, 1))
    return x * cos + partner * sin


def _ada_kernel(c_ref, w_ref, b_ref, o_ref):
    c = c_ref[...]
    cond = c * _sigmoid(c)
    o_ref[0] = jnp.dot(cond, w_ref[0], preferred_element_type=F32, precision=HIGHEST) + b_ref[0]


def _ada(c, ada_w, ada_b):
    L, D, W = ada_w.shape
    B = c.shape[0]
    tn = 1024
    return pl.pallas_call(
        _ada_kernel,
        grid=(L, W // tn),
        in_specs=[pl.BlockSpec((B, D), lambda l, j: (0, 0)),
                  pl.BlockSpec((1, D, tn), lambda l, j: (l, 0, j)),
                  pl.BlockSpec((1, 1, tn), lambda l, j: (l, 0, j))],
        out_specs=pl.BlockSpec((1, B, tn), lambda l, j: (l, 0, j)),
        out_shape=jax.ShapeDtypeStruct((L, B, W), F32),
        compiler_params=_params(("parallel", "parallel")),
        name="ada_mod",
    )(c, ada_w, ada_b.reshape(L, 1, W))


def _combine(x_ref, g2_ref, routed_ref, y_refs):
    gates = routed_ref[...]
    moe = gates[:, TOP_K:TOP_K + 1] * y_refs[0][...].astype(F32)
    for j in range(1, len(y_refs)):
        moe = moe + gates[:, TOP_K + j:TOP_K + j + 1] * y_refs[j][...].astype(F32)
    return x_ref[...] + g2_ref[0] * moe


def _inproj_kernel(n_comb, x_ref, *refs):
    if n_comb:
        g2_ref, routed_ref, y_refs, refs = refs[0], refs[1], refs[2:2 + n_comb], refs[2 + n_comb:]
        g_ref, sc_ref, sh_ref, w_ref, x_o, ret_ref, rwkv_ref, dsa_ref, sb_ref = refs
        x = _combine(x_ref, g2_ref, routed_ref, y_refs)
        x_o[...] = x
    else:
        g_ref, sc_ref, sh_ref, w_ref, ret_ref, rwkv_ref, dsa_ref, sb_ref = refs
        x = x_ref[...]
    h = _rms(x, g_ref[...]) * (1.0 + sc_ref[0]) + sh_ref[0]
    hb = h.astype(BF16)
    o0, o1, o2 = RET_COLS, RET_COLS + RWKV_COLS, RET_COLS + RWKV_COLS + DSA_PAD
    ret_ref[...] = jnp.dot(hb, w_ref[:, 0:o0], preferred_element_type=F32)
    rwkv_ref[...] = jnp.dot(hb, w_ref[:, o0:o1], preferred_element_type=F32)
    dsa_ref[...] = jnp.dot(hb, w_ref[:, o1:o2], preferred_element_type=F32)
    sb_ref[...] = jnp.dot(hb, w_ref[:, o2:], preferred_element_type=F32).astype(BF16)


def _mod_row(per_b, j):
    return pl.BlockSpec((1, 1, D_MODEL), lambda i: ((i // per_b) * 6 + j, 0, 0))


def _inproj(x2, gain, mod6, w_cat, S, comb=None):
    N, D = x2.shape
    tm = ROW_TILE
    per_b = S // tm
    wt = w_cat.shape[1]
    row = lambda i: (i, 0)
    in_specs = [pl.BlockSpec((tm, D), row)]
    args = [x2]
    out_specs, out_shape = [], []
    if comb is not None:
        in_specs += ([_mod_row(per_b, 5), pl.BlockSpec((tm, LANES), row)]
                     + [pl.BlockSpec((tm, D), row)] * len(comb[2]))
        args += [comb[0], comb[1]] + list(comb[2])
        out_specs.append(pl.BlockSpec((tm, D), row))
        out_shape.append(jax.ShapeDtypeStruct((N, D), F32))
    in_specs += [pl.BlockSpec((1, D), lambda i: (0, 0)), _mod_row(per_b, 1), _mod_row(per_b, 0),
                 pl.BlockSpec((D, wt), lambda i: (0, 0))]
    args += [gain, mod6, mod6, w_cat]
    out_specs += [pl.BlockSpec((tm, RET_COLS), row), pl.BlockSpec((tm, RWKV_COLS), row),
                  pl.BlockSpec((tm, DSA_PAD), row), pl.BlockSpec((tm, SB_COLS), row)]
    out_shape += [jax.ShapeDtypeStruct((N, RET_COLS), F32), jax.ShapeDtypeStruct((N, RWKV_COLS), F32),
                  jax.ShapeDtypeStruct((N, DSA_PAD), F32), jax.ShapeDtypeStruct((N, SB_COLS), BF16)]
    return pl.pallas_call(
        functools.partial(_inproj_kernel, 0 if comb is None else len(comb[2])),
        grid=(N // tm,),
        in_specs=in_specs, out_specs=out_specs, out_shape=out_shape,
        compiler_params=_params(("parallel",)),
        name="in_proj",
    )(*args)


def _ret_kernel(q_ref, k_ref, v_ref, g_ref, cos_ref, sin_ref, din_ref, qd_ref, kd_ref, cd_ref, gn_ref,
                o_ref, state_ref, y_ref):
    @pl.when(pl.program_id(1) == 0)
    def _():
        state_ref[...] = jnp.zeros_like(state_ref)

    cos, sin = cos_ref[0], sin_ref[0]
    q = _rope(q_ref[0], cos, sin, HEAD_DIM, HEAD_DIM // 2)
    k = _rope(k_ref[0], cos, sin, HEAD_DIM, HEAD_DIM // 2) * HEAD_DIM ** -0.5
    v = v_ref[0]
    qd = q * qd_ref[...]
    kd = k * kd_ref[...]
    for h in range(N_HEADS_G):
        sl = slice(h * HEAD_DIM, (h + 1) * HEAD_DIM)
        qh, kh, vh = q[:, sl].astype(BF16), k[:, sl].astype(BF16), v[:, sl].astype(BF16)
        s = lax.dot_general(qh, kh, NT_DIMS, preferred_element_type=F32) * din_ref[h]
        inner = jnp.dot(s.astype(BF16), vh, preferred_element_type=F32)
        st = state_ref[h]
        cross = _bdot(qd[:, sl], st)
        state_ref[h] = st * cd_ref[:, sl] + lax.dot_general(
            kd[:, sl].astype(BF16), vh, TN_DIMS, preferred_element_type=F32)
        o = inner + cross
        oc = o - jnp.mean(o, axis=-1, keepdims=True)
        y_ref[:, sl] = oc * lax.rsqrt(jnp.mean(oc * oc, axis=-1, keepdims=True) + NORM_EPS)
    g = g_ref[0]
    o_ref[0] = (g * _sigmoid(g) * (y_ref[...] * gn_ref[...])).astype(BF16)


def _retention(ret3, tabs, consts, gn):
    B, S, _ = ret3.shape
    C = consts["din"].shape[1]
    blk = lambda j: pl.BlockSpec((1, C, GROUP_W), lambda b, c, j=j: (b, c, j))
    tab = pl.BlockSpec((1, C, GROUP_W), lambda b, c: (b, c, 0))
    const2 = lambda shape: pl.BlockSpec(shape, lambda b, c: (0,) * len(shape))
    return pl.pallas_call(
        _ret_kernel,
        grid=(B, S // C),
        in_specs=[blk(0), blk(1), blk(2), blk(3), tab, tab,
                  const2((N_HEADS_G, C, C)), const2((C, GROUP_W)), const2((C, GROUP_W)),
                  const2((1, GROUP_W)), const2((1, GROUP_W))],
        out_specs=pl.BlockSpec((1, C, GROUP_W), lambda b, c: (b, c, 0)),
        out_shape=jax.ShapeDtypeStruct((B, S, GROUP_W), BF16),
        scratch_shapes=[pltpu.VMEM((N_HEADS_G, HEAD_DIM, HEAD_DIM), F32), pltpu.VMEM((C, GROUP_W), F32)],
        compiler_params=_params(("parallel", "arbitrary")),
        name="retention",
    )(ret3, ret3, ret3, ret3, tabs[0], tabs[1],
      consts["din"], consts["qd"], consts["kd"], consts["cd"], gn)


def _ret_consts(C):
    H = N_HEADS_G
    lg = jnp.log(1.0 - 2.0 ** (-5.0 - jnp.arange(H, dtype=F32)))
    idx = jnp.arange(C, dtype=F32)
    diff = idx[:, None] - idx[None, :]
    din = jnp.where(diff >= 0, jnp.exp(lg[:, None, None] * jnp.maximum(diff, 0.0)), 0.0)
    q_dec = jnp.exp(lg[:, None] * (idx + 1.0))
    k_dec = jnp.exp(lg[:, None] * (C - 1.0 - idx))
    chunk_dec = jnp.exp(lg * C)
    wide = lambda t: jnp.repeat(t.T, HEAD_DIM, axis=1)
    return {"din": din, "qd": wide(q_dec), "kd": wide(k_dec),
            "cd": jnp.repeat(chunk_dec, HEAD_DIM)[None, :]}


def _rwkv_prep_kernel(per_b, z_ref, zp_ref, mu_ref, w0_ref, a0_ref, w2_ref, a2_ref, g2_ref, kk_ref, ka_ref,
                      rk_ref, hm_ref, r_o, w_o, k_o, v_o, nkk_o, b_o, gate_o, bonus_o):
    z = z_ref[...]
    first = (pl.program_id(0) % per_b) == 0
    prow = jnp.where(first, 0.0, zp_ref[7:8, :])
    rid = lax.broadcasted_iota(I32, z.shape, 0)
    prev = jnp.where(rid == 0, prow, pltpu.roll(z, 1, 0))
    f = z + (prev - z) * mu_ref[...]
    G = GROUP_W
    r, k, v, lo = f[:, 0:G], f[:, G:2 * G], f[:, 2 * G:3 * G], f[:, 3 * G:3 * G + 128]
    w_log = -_softplus(-(w0_ref[...] + _bdot(jnp.tanh(lo), w2_ref[...]))) - 0.5
    decay = jnp.exp(-jnp.exp(w_log))
    a = _sigmoid(a0_ref[...] + _bdot(lo, a2_ref[...]))
    gate = _bdot(_sigmoid(lo), g2_ref[...])
    hm = hm_ref[...]
    kk = k * kk_ref[...]
    kk = kk / jnp.maximum(jnp.sqrt(_split_dot(kk * kk, hm)), 1e-12)
    k2 = k * (1.0 + (a - 1.0) * ka_ref[...])
    r_o[...] = r.astype(BF16)
    w_o[...] = decay
    k_o[...] = k2.astype(BF16)
    v_o[...] = v.astype(BF16)
    nkk_o[...] = -kk
    b_o[...] = kk * a
    gate_o[...] = gate
    bonus_o[...] = _split_dot(r * k2 * rk_ref[...], hm) * v


def _rwkv_prep(z2, S, mu, w0, a0, w2p, a2p, g2p, kk, ka, rk, hm):
    N, W = z2.shape
    tm = ROW_TILE
    per_b = S // tm
    row = lambda i: (i, 0)
    c = lambda shape: pl.BlockSpec(shape, lambda i: (0, 0))
    G = GROUP_W
    return pl.pallas_call(
        functools.partial(_rwkv_prep_kernel, per_b),
        grid=(N // tm,),
        in_specs=[pl.BlockSpec((tm, W), row),
                  pl.BlockSpec((SUBLANES, W), lambda i: (jnp.maximum(i * (tm // SUBLANES) - 1, 0), 0)),
                  c((1, W)), c((1, G)), c((1, G)), c((128, G)), c((128, G)), c((128, G)),
                  c((1, G)), c((1, G)), c((1, G)), c((G, G))],
        out_specs=[pl.BlockSpec((tm, G), row)] * 8,
        out_shape=[jax.ShapeDtypeStruct((N, G), dt) for dt in (BF16, F32, BF16, BF16, F32, F32, F32, F32)],
        compiler_params=_params(("parallel",)),
        name="rwkv_prep",
    )(z2, z2, mu, w0, a0, w2p, a2p, g2p, kk, ka, rk, hm)


def _rwkv_scan_kernel(T, nkk_ref, w_ref, b_ref, k_ref, r_ref, v_ref, nkk_nx, w_nx, b_nx, k_nx, r_nx,
                      y_ref, s_ref, rep_ref):
    i = pl.program_id(0)
    ch = v_ref.shape[1]
    groups = HEAD_DIM // ch
    width = LANES // groups
    tree = lambda parts: _tree(parts, jnp.add)

    def spread(x):
        grp = lax.broadcasted_iota(I32, x.shape, 1) // width
        turned = [x] + [pltpu.roll(x, m * width, 1) for m in range(1, groups)]
        outs = []
        for j in range(groups):
            y = turned[0]
            for m in range(1, groups):
                y = jnp.where(grp == (j + m) % groups, turned[m], y)
            outs.append(y)
        return outs

    @pl.when(i == 0)
    def _():
        s_ref[...] = jnp.zeros_like(s_ref)
        for o, ref in enumerate((nkk_ref, w_ref, b_ref, k_ref, r_ref)):
            for j, y in enumerate(spread(ref[...].astype(F32).reshape(T * ch, LANES))):
                rep_ref[0, o, :, j] = y.reshape(T, ch, LANES)

    cur = i % 2
    keys = [(j, kh) for j in range(groups) for kh in range(ch)]
    row = lambda o, t, j, kh: rep_ref[cur, o, t, j, kh:kh + 1, :]

    def step(t, carry):
        for o, ref in enumerate((nkk_nx, w_nx, b_nx, k_nx, r_nx)):
            for j, y in enumerate(spread(ref[t].astype(F32))):
                rep_ref[1 - cur, o, t, j] = y
        vt = v_ref[t].astype(F32)
        sa = tree([s_ref[j * ch + kh] * row(0, t, j, kh) for j, kh in keys])
        ys = []
        for j, kh in keys:
            s_new = (s_ref[j * ch + kh] * row(1, t, j, kh) + sa * row(2, t, j, kh) + vt * row(3, t, j, kh))
            s_ref[j * ch + kh] = s_new
            ys.append(s_new * row(4, t, j, kh))
        y_ref[t] = tree(ys)
        return carry

    lax.fori_loop(0, T, step, 0)


def _rwkv_scan(nkk, w, b, k, r, v):
    S, ch, _ = v.shape
    T = 32
    n = S // T
    spec = pl.BlockSpec((T, ch, LANES), lambda i: (i, 0, 0))
    nxt = pl.BlockSpec((T, ch, LANES), lambda i: (jnp.minimum(i + 1, n - 1), 0, 0))
    return pl.pallas_call(
        functools.partial(_rwkv_scan_kernel, T),
        grid=(n,),
        in_specs=[spec] * 6 + [nxt] * 5,
        out_specs=spec,
        out_shape=jax.ShapeDtypeStruct((S, ch, LANES), F32),
        scratch_shapes=[pltpu.VMEM((HEAD_DIM, ch, LANES), F32),
                        pltpu.VMEM((2, 5, T, HEAD_DIM // ch, ch, LANES), F32)],
        compiler_params=_params(("arbitrary",)),
        name="rwkv_scan",
    )(nkk, w, b, k, r, v, nkk, w, b, k, r)


def _to_v_layout(x, B, S):
    P = B * N_HEADS_G
    rep = LANES // P
    t = x.reshape(B, S, N_HEADS_G, HEAD_DIM // rep, rep).transpose(1, 3, 4, 0, 2)
    return t.reshape(S, HEAD_DIM // rep, LANES)


def _from_v_layout(y, B, S):
    P = B * N_HEADS_G
    rep = LANES // P
    t = y.reshape(S, HEAD_DIM // rep, rep, B, N_HEADS_G).transpose(3, 0, 4, 1, 2)
    return t.reshape(B * S, GROUP_W)


def _dsa_prep_kernel(f_ref, qn_ref, wq_ref, wqi_ref, cq_ref, sq_ref, ci_ref, si_ref,
                     qt_o, qit_o, wt_o, k_o, v_o, ki_o):
    f = f_ref[...]
    cq = _rms(f[:, 0:DSA_Q_LORA], qn_ref[...]).astype(BF16)
    cos_q, sin_q = cq_ref[...], sq_ref[...]
    cos_i, sin_i = ci_ref[...], si_ref[...]
    q = _rope(jnp.dot(cq, wq_ref[...], preferred_element_type=F32), cos_q, sin_q, HEAD_DIM, HEAD_DIM // 8)
    tm = f.shape[0]
    qt_o[...] = (q * HEAD_DIM ** -0.5).T.reshape(N_HEADS_G, HEAD_DIM, tm).astype(BF16)
    qi = _rope(jnp.dot(cq, wqi_ref[...], preferred_element_type=F32), cos_i, sin_i, IDX_DIM, IDX_DIM // 8)
    qit_o[...] = qi.T.reshape(IDX_HEADS, IDX_DIM, tm).astype(BF16)
    kv = f[:, 128:256]
    kv_r = _rope(kv, cos_q[:, 0:128], sin_q[:, 0:128], HEAD_DIM, HEAD_DIM // 8)
    k_o[...] = kv_r[:, 0:HEAD_DIM].astype(BF16)
    v_o[...] = kv[:, HEAD_DIM:128].astype(BF16)
    tail = f[:, 256:384]
    tail_r = _rope(tail, cos_i[:, 0:128], sin_i[:, 0:128], IDX_DIM, IDX_DIM // 8)
    ki_o[...] = tail_r[:, 0:IDX_DIM].astype(BF16)
    wt_o[...] = tail.T[IDX_DIM:IDX_DIM + IDX_HEADS, :] * (IDX_HEADS ** -0.5 * IDX_DIM ** -0.5)


def _dsa_prep(f2, qn, wq, wqi, tq, ti):
    N, W = f2.shape
    tm = ROW_TILE
    G = GROUP_W
    row = lambda w: pl.BlockSpec((tm, w), lambda i: (i, 0))
    col = lambda h: pl.BlockSpec((h, tm), lambda i: (0, i))
    slab = lambda h, d: pl.BlockSpec((h, d, tm), lambda i: (0, 0, i))
    c = lambda shape: pl.BlockSpec(shape, lambda i: (0, 0))
    return pl.pallas_call(
        _dsa_prep_kernel,
        grid=(N // tm,),
        in_specs=[row(W), c((1, DSA_Q_LORA)), c((DSA_Q_LORA, G)), c((DSA_Q_LORA, G))] + [row(G)] * 4,
        out_specs=[slab(N_HEADS_G, HEAD_DIM), slab(IDX_HEADS, IDX_DIM), col(IDX_HEADS),
                   row(HEAD_DIM), row(HEAD_DIM), row(IDX_DIM)],
        out_shape=[jax.ShapeDtypeStruct((N_HEADS_G, HEAD_DIM, N), BF16),
                   jax.ShapeDtypeStruct((IDX_HEADS, IDX_DIM, N), BF16),
                   jax.ShapeDtypeStruct((IDX_HEADS, N), F32),
                   jax.ShapeDtypeStruct((N, HEAD_DIM), BF16), jax.ShapeDtypeStruct((N, HEAD_DIM), BF16),
                   jax.ShapeDtypeStruct((N, IDX_DIM), BF16)],
        compiler_params=_params(("parallel",)),
        name="dsa_prep",
    )(f2, qn, wq, wqi, *tq, *ti)


def _float_key(bits):
    return bits ^ ((bits >> 31) & 0x7FFFFFFF)


def _dsa_kernel(TK, topk, qt_ref, qit_ref, wt_ref, k_ref, v_ref, ki_ref, on_ref, o_ref,
                skey_ref, lg_ref, out_ref):
    i = pl.program_id(1)
    TQ = Q_BLOCK
    n_kt = (i * TQ + TQ + TK - 1) // TK
    q_all = jnp.concatenate([qt_ref[h] for h in range(N_HEADS_G)], axis=1)
    qi_all = jnp.concatenate([qit_ref[h] for h in range(IDX_HEADS)], axis=1)
    wt = wt_ref[...]
    kpos = lax.broadcasted_iota(I32, (TK, TQ), 0)
    qpos = i * TQ + lax.broadcasted_iota(I32, (TK, TQ), 1)
    head = lambda x, h: x[:, h * TQ:(h + 1) * TQ]

    def fold(x, op):
        return _tree([x[r * SUBLANES:(r + 1) * SUBLANES] for r in range(TK // SUBLANES)], op)

    def score_tile(kt, tops):
        off = pl.multiple_of(kt * TK, TK)
        rel = jnp.dot(ki_ref[0, pl.ds(off, TK), :], qi_all, preferred_element_type=F32)
        sc = jnp.zeros((TK, TQ), F32)
        for h in range(IDX_HEADS):
            sc = sc + jnp.maximum(head(rel, h), 0.0) * wt[h:h + 1, :]
        key = _float_key(pltpu.bitcast(sc, I32))
        key = jnp.where(sc == 0.0, -(kt * TK + kpos), jnp.where(key < 0, key - skey_ref.shape[0] * TK, key))
        valid = kt * TK + kpos <= qpos
        skey_ref[kt] = jnp.where(valid, key, INT_MIN)
        lg = jnp.dot(k_ref[0, pl.ds(off, TK), :], q_all, preferred_element_type=F32)
        lg_ref[kt] = lg
        return tuple(jnp.maximum(tops[h], fold(jnp.where(valid, head(lg, h), -1e30), jnp.maximum))
                     for h in range(N_HEADS_G))

    lowest = tuple(jnp.full((SUBLANES, TQ), -1e30, F32) for _ in range(N_HEADS_G))
    tops = lax.fori_loop(0, n_kt, score_tile, lowest)

    def count(pred):
        def count_tile(kt, acc):
            return acc + fold(jnp.where(pred(kt, skey_ref[kt]), 1.0, 0.0), jnp.add)

        acc = lax.fori_loop(0, n_kt, count_tile, jnp.zeros((SUBLANES, TQ), F32))
        return jnp.sum(acc, axis=0, keepdims=True)

    def bit_step(j, carry):
        thr, n_thr = carry
        cand = thr + lax.shift_left(jnp.int32(1), 31 - j)
        n = count(lambda kt, keys: keys >= cand)
        return jnp.where(n >= topk, cand, thr), jnp.where(n >= topk, n, n_thr)

    thr, n_thr = lax.fori_loop(0, 32, bit_step,
                               (jnp.full((1, TQ), INT_MIN, I32), jnp.full((1, TQ), float(topk), F32)))
    thr = jnp.maximum(thr, INT_MIN + 1)

    @pl.when(jnp.max(n_thr) > topk)
    def _():
        keep = topk - count(lambda kt, keys: keys > thr)

        pos_bits = (skey_ref.shape[0] * TK - 1).bit_length()

        def pos_step(j, last):
            cand = last + lax.shift_left(jnp.int32(1), pos_bits - 1 - j)
            n = count(lambda kt, keys: jnp.logical_and(keys == thr, kt * TK + kpos < cand))
            return jnp.where(n < keep, cand, last)

        last = lax.fori_loop(0, pos_bits, pos_step, jnp.zeros((1, TQ), I32))

        def retire(kt, carry):
            keys = skey_ref[kt]
            skey_ref[kt] = jnp.where(jnp.logical_and(keys == thr, kt * TK + kpos > last), INT_MIN, keys)
            return carry

        lax.fori_loop(0, n_kt, retire, 0)

    def sweep(shifts):
        def acc_tile(kt, carry):
            ls, acc = carry
            sel = skey_ref[kt] >= thr
            ps = [jnp.where(sel, jnp.exp(lg_ref[kt, :, h * TQ:(h + 1) * TQ] - shifts[h]), 0.0)
                  for h in range(N_HEADS_G)]
            p_all = jnp.concatenate([p.astype(BF16) for p in ps], axis=1)
            vb = v_ref[0, pl.ds(pl.multiple_of(kt * TK, TK), TK), :]
            return (tuple(ls[h] + fold(ps[h], jnp.add) for h in range(N_HEADS_G)),
                    acc + lax.dot_general(vb, p_all, TN_DIMS, preferred_element_type=F32))

        ls, acc = lax.fori_loop(0, n_kt, acc_tile,
                                (tuple(jnp.zeros((SUBLANES, TQ), F32) for _ in range(N_HEADS_G)),
                                 jnp.zeros((HEAD_DIM, N_HEADS_G * TQ), F32)))
        ls = [jnp.sum(l, axis=0, keepdims=True) for l in ls]
        for h in range(N_HEADS_G):
            out_ref[h * HEAD_DIM:(h + 1) * HEAD_DIM, :] = head(acc, h) / ls[h]
        return jnp.min(jnp.minimum(jnp.minimum(ls[0], ls[1]), jnp.minimum(ls[2], ls[3])))

    smallest = sweep([jnp.max(t, axis=0, keepdims=True) for t in tops])

    @pl.when(smallest < SOFTMAX_FLOOR)
    def _():
        def max_tile(kt, ms):
            sel = skey_ref[kt] >= thr
            return tuple(jnp.maximum(ms[h], fold(jnp.where(sel, lg_ref[kt, :, h * TQ:(h + 1) * TQ], -1e30),
                                                 jnp.maximum))
                         for h in range(N_HEADS_G))

        ms = lax.fori_loop(0, n_kt, max_tile, lowest)
        sweep([jnp.max(m, axis=0, keepdims=True) for m in ms])

    o_ref[0] = _rms(out_ref[...].T, on_ref[...]).astype(BF16)


def _dsa_attention(qt, qit, wt, k, v, ki, on):
    B, S, _ = k.shape
    G = GROUP_W
    nq = S // Q_BLOCK
    TK = min(512, S)
    topk = min(DSA_TOPK_MAX, S // 4)
    qcol = lambda h: pl.BlockSpec((h, Q_BLOCK), lambda b, i: (0, b * nq + i))
    qslab = lambda h, d: pl.BlockSpec((h, d, Q_BLOCK), lambda b, i: (0, 0, b * nq + i))
    full = lambda wd: pl.BlockSpec((1, S, wd), lambda b, i: (b, 0, 0))
    return pl.pallas_call(
        functools.partial(_dsa_kernel, TK, topk),
        grid=(B, nq),
        in_specs=[qslab(N_HEADS_G, HEAD_DIM), qslab(IDX_HEADS, IDX_DIM), qcol(IDX_HEADS),
                  full(HEAD_DIM), full(HEAD_DIM), full(IDX_DIM), pl.BlockSpec((1, G), lambda b, i: (0, 0))],
        out_specs=pl.BlockSpec((1, Q_BLOCK, G), lambda b, i: (b, i, 0)),
        out_shape=jax.ShapeDtypeStruct((B, S, G), BF16),
        scratch_shapes=[pltpu.VMEM((S // TK, TK, Q_BLOCK), I32),
                        pltpu.VMEM((S // TK, TK, N_HEADS_G * Q_BLOCK), F32),
                        pltpu.VMEM((G, Q_BLOCK), F32)],
        compiler_params=_params(("parallel", "arbitrary")),
        name="dsa_attention",
    )(qt, qit, wt, k, v, ki, on)


def _sb_kernel(q_ref, k_ref, v_ref, u_ref, on_ref, o_ref, z_ref, lm_ref):
    i = pl.program_id(1)
    T = Q_BLOCK
    H, G = N_HEADS_G, GROUP_W
    u = u_ref[...]
    qt = q_ref[0].astype(F32).T.astype(BF16)
    q_rows = lax.broadcasted_iota(I32, (G, H * T), 0) // HEAD_DIM
    q_cols = lax.broadcasted_iota(I32, (G, H * T), 1) // T
    qbd = jnp.where(q_rows == q_cols, jnp.concatenate([qt] * H, axis=1), jnp.zeros((), BF16))
    kid = lax.broadcasted_iota(I32, (T, H * T), 0)
    qid = lax.broadcasted_iota(I32, (T, H * T), 1) & (T - 1)
    strict = kid < qid

    def cond(carry):
        kt, c, _ = carry
        return jnp.logical_and(kt >= 0, jnp.max(c) > SB_UNDERFLOW)

    def logits(kt):
        z = jnp.dot(k_ref[0, pl.ds(pl.multiple_of(kt * T, T), T), :], qbd, preferred_element_type=F32)
        z = z * HEAD_DIM ** -0.5
        return z, -_softplus(z)

    z0, lm0 = logits(i)
    z_ref[...] = z0
    lm_ref[...] = jnp.where(strict, lm0, 0.0)

    def body(carry):
        kt, c, acc = carry
        off = pl.multiple_of(kt * T, T)
        mask = jnp.logical_or(strict, kt < i)
        z, lm = z_ref[...], lm_ref[...]
        z_ref[...], lm_ref[...] = logits(jnp.maximum(kt - 1, 0))
        hi, lo = _split(lm)
        later = jnp.dot(u, jnp.concatenate([hi, lo], axis=1), preferred_element_type=F32)
        later = later[:, :H * T] + later[:, H * T:]
        a = jnp.where(mask, jnp.exp(z + lm + (c + later)), 0.0)
        av = lax.dot_general(v_ref[0, pl.ds(off, T), :], a.astype(BF16), TN_DIMS, preferred_element_type=F32)
        acc = acc + jnp.concatenate(
            [av[h * HEAD_DIM:(h + 1) * HEAD_DIM, h * T:(h + 1) * T] for h in range(H)], axis=0)
        return kt - 1, c + jnp.sum(lm, axis=0, keepdims=True), acc

    _, _, acc = lax.while_loop(cond, body, (i, jnp.zeros((1, H * T), F32), jnp.zeros((G, T), F32)))
    o_ref[0] = _rms(acc.T, on_ref[...]).astype(BF16)


def _stick_breaking(sb3, u, on):
    B, S, _ = sb3.shape
    G = GROUP_W
    return pl.pallas_call(
        _sb_kernel,
        grid=(B, S // Q_BLOCK),
        in_specs=[pl.BlockSpec((1, Q_BLOCK, G), lambda b, i: (b, i, 0)),
                  pl.BlockSpec((1, S, G), lambda b, i: (b, 0, 1)),
                  pl.BlockSpec((1, S, G), lambda b, i: (b, 0, 2)),
                  pl.BlockSpec((Q_BLOCK, Q_BLOCK), lambda b, i: (0, 0)),
                  pl.BlockSpec((1, G), lambda b, i: (0, 0))],
        out_specs=pl.BlockSpec((1, Q_BLOCK, G), lambda b, i: (b, i, 0)),
        out_shape=jax.ShapeDtypeStruct((B, S, G), BF16),
        scratch_shapes=[pltpu.VMEM((Q_BLOCK, N_HEADS_G * Q_BLOCK), F32)] * 2,
        compiler_params=_params(("parallel", "arbitrary")),
        name="stick_breaking",
    )(sb3, sb3, sb3, u, on)


def _outproj_kernel(yr_ref, yd_ref, ys_ref, yw_ref, bonus_ref, gate_ref, ln_ref, hm_ref, x_ref, g1_ref, sc_ref,
                    sh_ref, gn_ref, wo_ref, rw_ref, rb_ref, x_o, h_o, lg_o):
    G = GROUP_W
    y = yw_ref[...]
    hm = hm_ref[...]
    yc = y - _split_dot(y, hm) * (1.0 / HEAD_DIM)
    var = _split_dot(yc * yc, hm) * (1.0 / HEAD_DIM)
    yw = ((yc * lax.rsqrt(var + RWKV_LN_EPS) * ln_ref[...] + bonus_ref[...]) * gate_ref[...]).astype(BF16)
    dot = lambda yv, g: jnp.dot(yv, wo_ref[g * G:(g + 1) * G, :], preferred_element_type=F32)
    mixed = dot(yr_ref[...], 0) + dot(yw, 1) + dot(yd_ref[...], 2) + dot(ys_ref[...], 3)
    x1 = x_ref[...] + g1_ref[0] * mixed
    x_o[...] = x1
    h = _rms(x1, gn_ref[...]) * (1.0 + sc_ref[0]) + sh_ref[0]
    h_o[...] = h
    hi, lo = _split(h)
    work = (jnp.dot(hi, rw_ref[0], preferred_element_type=F32) + jnp.dot(lo, rw_ref[0], preferred_element_type=F32)
            + jnp.dot(hi, rw_ref[1], preferred_element_type=F32) + rb_ref[...])
    col = lax.broadcasted_iota(I32, work.shape, 1).astype(F32)
    out = jnp.zeros_like(work)
    vals = []
    for j in range(TOP_K):
        m = jnp.max(work, axis=-1, keepdims=True)
        idx = jnp.min(jnp.where(work == m, col, float(LANES)), axis=-1, keepdims=True)
        out = jnp.where(col == float(j), idx, out)
        vals.append(m)
        work = jnp.where(col == idx, -jnp.inf, work)
    es = [jnp.exp(v - vals[0]) for v in vals]
    den = es[0] + es[1] + es[2] + es[3]
    for j in range(TOP_K):
        out = jnp.where(col == float(TOP_K + j), es[j] / den, out)
    lg_o[...] = out


def _outproj(ys, rwkv, x2, mod6, gain, w_out, router_w, router_b, S):
    N, D = x2.shape
    tm = ROW_TILE
    per_b = S // tm
    G = GROUP_W
    row = lambda w: pl.BlockSpec((tm, w), lambda i: (i, 0))
    c = lambda shape: pl.BlockSpec(shape, lambda i: (0, 0))
    return pl.pallas_call(
        _outproj_kernel,
        grid=(N // tm,),
        in_specs=[row(G)] * 6 + [c((1, G)), c((G, G)),
                                 row(D), _mod_row(per_b, 2), _mod_row(per_b, 4), _mod_row(per_b, 3),
                                 c((1, D)), c((D, D)), pl.BlockSpec((2, D, LANES), lambda i: (0, 0, 0)),
                                 c((1, LANES))],
        out_specs=[row(D), row(D), row(LANES)],
        out_shape=[jax.ShapeDtypeStruct((N, D), F32), jax.ShapeDtypeStruct((N, D), F32),
                   jax.ShapeDtypeStruct((N, LANES), F32)],
        compiler_params=_params(("parallel",)),
        name="out_proj_router",
    )(*ys, *rwkv, x2, mod6, mod6, mod6, gain, w_out,
      jnp.pad(jnp.stack(_split(router_w)), ((0, 0), (0, 0), (0, LANES - N_EXPERTS))),
      jnp.pad(router_b, ((0, 0), (0, LANES - N_EXPERTS)), constant_values=-1e30))


def _moe_kernel(be_ref, nu_ref, x_ref, w1_ref, b1_ref, w2_ref, b2_ref, o_ref):
    i = pl.program_id(0)

    @pl.when(i < nu_ref[0])
    def _():
        xb = x_ref[...].astype(BF16)
        w1 = w1_ref.at[0, 0]
        glu = jnp.dot(xb, w1[:, 0:D_FF].astype(BF16), preferred_element_type=F32) + b1_ref[0, :, 0:D_FF]
        lin = jnp.dot(xb, w1[:, D_FF:].astype(BF16), preferred_element_type=F32) + b1_ref[0, :, D_FF:]
        glu = jnp.minimum(glu, SWIGLU_LIMIT)
        lin = jnp.clip(lin, -SWIGLU_LIMIT, SWIGLU_LIMIT)
        act = glu * _sigmoid(SWIGLU_ALPHA * glu) * (lin + 1.0)
        y = jnp.dot(act.astype(BF16), w2_ref[0, 0].astype(BF16), preferred_element_type=F32) + b2_ref[0]
        o_ref[...] = y.astype(BF16)

    @pl.when(i >= nu_ref[0])
    def _():
        o_ref[...] = jnp.zeros_like(o_ref)


def _moe_ffn(block_expert, n_used, xs, layer, w1, b1, w2, b2):
    rows, D = xs.shape
    nb = rows // MOE_BLOCK
    L, E = w1.shape[:2]
    live = lambda i, nu: jnp.minimum(i, nu[0] - 1)
    grid_spec = pltpu.PrefetchScalarGridSpec(
        num_scalar_prefetch=2,
        grid=(nb,),
        in_specs=[pl.BlockSpec((MOE_BLOCK, D), lambda i, be, nu: (live(i, nu), 0)),
                  pl.BlockSpec((1, 1, D, 2 * D_FF), lambda i, be, nu: (layer, be[i], 0, 0)),
                  pl.BlockSpec((1, 1, 2 * D_FF), lambda i, be, nu: (layer * E + be[i], 0, 0)),
                  pl.BlockSpec((1, 1, D_FF, D), lambda i, be, nu: (layer, be[i], 0, 0)),
                  pl.BlockSpec((1, 1, D), lambda i, be, nu: (layer * E + be[i], 0, 0))],
        out_specs=pl.BlockSpec((MOE_BLOCK, D), lambda i, be, nu: (i, 0)),
    )
    return pl.pallas_call(
        _moe_kernel,
        grid_spec=grid_spec,
        out_shape=jax.ShapeDtypeStruct((rows, D), BF16),
        compiler_params=_params(("arbitrary",), 56 * 1024 * 1024),
        name="moe_ffn",
    )(block_expert, n_used, xs, w1, b1.reshape(L * E, 1, -1), w2, b2.reshape(L * E, 1, -1))


def _route(routed):
    n = routed.shape[0]
    expert = routed[:, 0:TOP_K].reshape(-1).astype(I32)
    n_assign = n * TOP_K
    ids = jnp.arange(n_assign, dtype=I32)
    _, order = lax.sort((expert, ids), num_keys=1, is_stable=True)
    _, inv = lax.sort((order, ids), num_keys=1)
    counts = jnp.sum((expert[:, None] == jnp.arange(N_EXPERTS, dtype=I32)[None, :]).astype(I32), axis=0)
    starts = jnp.cumsum(counts) - counts
    padded = (counts + MOE_BLOCK - 1) // MOE_BLOCK * MOE_BLOCK
    pad_end = jnp.cumsum(padded)
    pad_start = pad_end - padded
    dest = pad_start[expert] + inv - starts[expert]
    n_blocks = -(-n_assign // MOE_BLOCK) + N_EXPERTS
    block_start = jnp.arange(n_blocks, dtype=I32) * MOE_BLOCK
    block_expert = jnp.minimum(jnp.sum(pad_end[None, :] <= block_start[:, None], axis=1), N_EXPERTS - 1)
    block_expert = block_expert.astype(I32)
    per_row = lambda t: jnp.repeat(t[block_expert], MOE_BLOCK)
    off = jnp.arange(n_blocks * MOE_BLOCK, dtype=I32) - per_row(pad_start)
    valid = off < per_row(counts)
    src = order[jnp.clip(per_row(starts) + off, 0, n_assign - 1)]
    row_token = jnp.where(valid, src // TOP_K, 0)
    n_used = (pad_end[-1] // MOE_BLOCK).astype(I32).reshape(1)
    return dest.reshape(n, TOP_K), row_token, block_expert, n_used


def _rows(t, idx):
    return t.at[idx].get(mode="promise_in_bounds")


def _final_kernel(x_ref, g2_ref, routed_ref, y0, y1, y2, y3, g_ref, o_ref):
    o_ref[...] = _rms(_combine(x_ref, g2_ref, routed_ref, (y0, y1, y2, y3)), g_ref[...])


def _final_norm(x2, mod6, routed, ys, g, S):
    N, D = x2.shape
    tm = ROW_TILE
    row = pl.BlockSpec((tm, D), lambda i: (i, 0))
    return pl.pallas_call(
        _final_kernel, grid=(N // tm,),
        in_specs=[row, _mod_row(S // tm, 5), pl.BlockSpec((tm, LANES), lambda i: (i, 0))] + [row] * 4
        + [pl.BlockSpec((1, D), lambda i: (0, 0))],
        out_specs=row,
        out_shape=jax.ShapeDtypeStruct((N, D), F32),
        compiler_params=_params(("parallel",)), name="final_norm",
    )(x2, mod6, routed, *ys, g)


def _rope_tables(pos, group, rot_dim, theta, width):
    half = rot_dim // 2
    inv = theta ** (-jnp.arange(half, dtype=F32) / half)
    ang = pos.astype(F32)[..., None] * inv
    cos, sin = jnp.cos(ang), jnp.sin(ang)
    rest = group - rot_dim
    ones = jnp.ones(ang.shape[:2] + (rest,), F32)
    zeros = jnp.zeros(ang.shape[:2] + (rest,), F32)
    rep = width // group
    tile = lambda t: jnp.tile(t, (1, 1, rep)).reshape(-1, width)
    return (tile(jnp.concatenate([cos, cos, ones], -1)), tile(jnp.concatenate([-sin, sin, zeros], -1)))


def _pad_rows(w, start, total=128):
    return jnp.zeros((total, w.shape[1]), w.dtype).at[start:start + w.shape[0]].set(w)


def kernel(x, c, positions, ada_w, ada_b, norm_mix, norm_ffn, w_in, ret_gn, rwkv_mu, rwkv_w0, rwkv_w2, rwkv_a0,
           rwkv_a2, rwkv_g2, rwkv_kk, rwkv_ka, rwkv_rk, rwkv_ln, dsa_qnorm, dsa_wq_up, dsa_wqi_up, dsa_onorm,
           sb_onorm, w_out, router_w, router_b, moe_w1, moe_b1, moe_w2, moe_b2, norm_final):
    B, S, D = x.shape
    N = B * S
    L = ada_w.shape[0]
    G = GROUP_W
    row2 = lambda t: t.reshape(1, -1)

    mod = _ada(c, ada_w, ada_b)
    ret_tabs = tuple(t.reshape(B, S, G) for t in _rope_tables(positions, HEAD_DIM, HEAD_DIM, RET_THETA, G))
    dq_tabs = _rope_tables(positions, HEAD_DIM, HEAD_DIM // 4, ROPE_THETA, G)
    di_tabs = _rope_tables(positions, IDX_DIM, IDX_DIM // 4, ROPE_THETA, G)
    ret_c = _ret_consts(min(256, S))
    hm = (jnp.arange(G)[:, None] // HEAD_DIM == jnp.arange(G)[None, :] // HEAD_DIM).astype(BF16)
    u_later = (jnp.arange(Q_BLOCK)[None, :] > jnp.arange(Q_BLOCK)[:, None]).astype(BF16)
    o1, o2 = RET_COLS + RWKV_COLS, RET_COLS + RWKV_COLS + DSA_COLS

    x2 = x.reshape(N, D)
    comb = None
    for l in range(L):
        mod6 = mod[l].reshape(B * 6, 1, D)
        w_l = w_in[l]
        w_cat = jnp.concatenate(
            [w_l[:, :o1], w_l[:, o1:o2], jnp.zeros((D, DSA_PAD - DSA_COLS), F32), w_l[:, o2:]], axis=1).astype(BF16)
        outs = _inproj(x2, row2(norm_mix[l]), mod6, w_cat, S, comb)
        if comb is not None:
            x2, outs = outs[0], outs[1:]
        ret, rwkv, dsa, sb = outs

        prep = _rwkv_prep(rwkv, S, row2(rwkv_mu[l]), row2(rwkv_w0[l]), row2(rwkv_a0[l]),
                          _pad_rows(rwkv_w2[l], 0).astype(BF16), _pad_rows(rwkv_a2[l], 32).astype(BF16),
                          _pad_rows(rwkv_g2[l], 64).astype(BF16), row2(rwkv_kk[l]), row2(rwkv_ka[l]),
                          row2(rwkv_rk[l]), hm)
        r_, w_, k_, v_, nkk_, b_, gate_, bonus_ = prep
        scan_in = tuple(_to_v_layout(t, B, S) for t in (nkk_, w_, b_, k_, r_, v_))

        y_ret = _retention(ret.reshape(B, S, RET_COLS), ret_tabs, ret_c, row2(ret_gn[l])).reshape(N, G)

        dqt, dqit, dwt, dk, dv, dki = _dsa_prep(dsa, row2(dsa_qnorm[l]), dsa_wq_up[l].astype(BF16),
                                                dsa_wqi_up[l].astype(BF16), dq_tabs, di_tabs)
        y_dsa = _dsa_attention(dqt, dqit, dwt, dk.reshape(B, S, HEAD_DIM), dv.reshape(B, S, HEAD_DIM),
                               dki.reshape(B, S, IDX_DIM), row2(dsa_onorm[l])).reshape(N, G)

        y_sb = _stick_breaking(sb.reshape(B, S, SB_COLS), u_later, row2(sb_onorm[l])).reshape(N, G)

        y_scan = _rwkv_scan(*scan_in)
        rwkv_tail = (_from_v_layout(y_scan, B, S), bonus_, gate_, row2(rwkv_ln[l]), hm)

        x1, h2, routed = _outproj((y_ret, y_dsa, y_sb), rwkv_tail, x2, mod6, row2(norm_ffn[l]),
                                  w_out[l].astype(BF16), router_w[l], row2(router_b[l]), S)

        dest, row_token, block_expert, n_used = _route(routed)
        y = _moe_ffn(block_expert, n_used, _rows(h2, row_token), l, moe_w1, moe_b1, moe_w2, moe_b2)
        x2 = x1
        comb = (mod6, routed, tuple(_rows(y, dest[:, j]) for j in range(TOP_K)))
    return _final_norm(x2, *comb, row2(norm_final), S).reshape(B, S, D)
```

```python
import functools

import jax
import jax.numpy as jnp
from jax import lax
from jax.experimental import pallas as pl
from jax.experimental.pallas import tpu as pltpu

F32 = jnp.float32
BF16 = jnp.bfloat16
I32 = jnp.int32

D_MODEL = 1024
GROUP_W = 256
HEAD_DIM = 64
N_HEADS_G = 4
NORM_EPS = 1e-5
Q_BLOCK = 128
RET_THETA = 10000.0
RWKV_LN_EPS = 64e-5
DSA_Q_LORA = 128
IDX_HEADS = 8
IDX_DIM = 32
DSA_TOPK_MAX = 256
ROPE_THETA = 500000.0
N_EXPERTS = 32
TOP_K = 4
D_FF = D_MODEL
SWIGLU_ALPHA = 1.702
SWIGLU_LIMIT = 7.0
MOE_BLOCK = 512

RET_COLS = 4 * GROUP_W
RWKV_COLS = 3 * GROUP_W + 128
DSA_COLS = 296
DSA_PAD = 384
SB_COLS = 3 * GROUP_W

LANES = 128
SUBLANES = 8
ROW_TILE = 512
VMEM_LIMIT = 48 * 1024 * 1024
INT_MIN = -2 ** 31
SB_UNDERFLOW = -104.0
SOFTMAX_FLOOR = 1e-20

HIGHEST = lax.Precision.HIGHEST
NT_DIMS = (((1,), (1,)), ((), ()))
TN_DIMS = (((0,), (0,)), ((), ()))


def _params(sem, vmem=VMEM_LIMIT):
    return pltpu.CompilerParams(dimension_semantics=sem, vmem_limit_bytes=vmem)


def _bdot(a, b):
    return jnp.dot(a.astype(BF16), b.astype(BF16), preferred_element_type=F32)


def _split(x):
    hi = x.astype(BF16)
    return hi, (x - hi.astype(F32)).astype(BF16)


def _split_dot(x, m):
    hi, lo = _split(x)
    return jnp.dot(hi, m, preferred_element_type=F32) + jnp.dot(lo, m, preferred_element_type=F32)


def _tree(parts, op):
    while len(parts) > 1:
        parts = [op(parts[i], parts[i + 1]) for i in range(0, len(parts), 2)]
    return parts[0]


def _sigmoid(x):
    return 1.0 / (1.0 + jnp.exp(-x))


def _softplus(x):
    return jnp.maximum(x, 0.0) + jnp.log(1.0 + jnp.exp(-jnp.abs(x)))


def _rms(x, g, eps=NORM_EPS):
    return x * lax.rsqrt(jnp.mean(x * x, axis=-1, keepdims=True) + eps) * g


def _rope(x, cos, sin, group, half):
    w = x.shape[-1]
    first = (lax.broadcasted_iota(I32, (1, w), 1) & (group - 1)) < half
    partner = jnp.where(first, pltpu.roll(x, w - half, 1), pltpu.roll(x, half, 1))
    return x * cos + partner * sin


def _ada_kernel(c_ref, w_ref, b_ref, o_ref):
    c = c_ref[...]
    cond = c * _sigmoid(c)
    o_ref[0] = jnp.dot(cond, w_ref[0], preferred_element_type=F32, precision=HIGHEST) + b_ref[0]


def _ada(c, ada_w, ada_b):
    L, D, W = ada_w.shape
    B = c.shape[0]
    tn = 1024
    return pl.pallas_call(
        _ada_kernel,
        grid=(L, W // tn),
        in_specs=[pl.BlockSpec((B, D), lambda l, j: (0, 0)),
                  pl.BlockSpec((1, D, tn), lambda l, j: (l, 0, j)),
                  pl.BlockSpec((1, 1, tn), lambda l, j: (l, 0, j))],
        out_specs=pl.BlockSpec((1, B, tn), lambda l, j: (l, 0, j)),
        out_shape=jax.ShapeDtypeStruct((L, B, W), F32),
        compiler_params=_params(("parallel", "parallel")),
        name="ada_mod",
    )(c, ada_w, ada_b.reshape(L, 1, W))


def _combine(x_ref, g2_ref, routed_ref, y_refs):
    gates = routed_ref[...]
    moe = gates[:, TOP_K:TOP_K + 1] * y_refs[0][...].astype(F32)
    for j in range(1, len(y_refs)):
        moe = moe + gates[:, TOP_K + j:TOP_K + j + 1] * y_refs[j][...].astype(F32)
    return x_ref[...] + g2_ref[0] * moe


def _inproj_kernel(n_comb, x_ref, *refs):
    if n_comb:
        g2_ref, routed_ref, y_refs, refs = refs[0], refs[1], refs[2:2 + n_comb], refs[2 + n_comb:]
        g_ref, sc_ref, sh_ref, w_ref, x_o, ret_ref, rwkv_ref, dsa_ref, sb_ref = refs
        x = _combine(x_ref, g2_ref, routed_ref, y_refs)
        x_o[...] = x
    else:
        g_ref, sc_ref, sh_ref, w_ref, ret_ref, rwkv_ref, dsa_ref, sb_ref = refs
        x = x_ref[...]
    h = _rms(x, g_ref[...]) * (1.0 + sc_ref[0]) + sh_ref[0]
    hb = h.astype(BF16)
    o0, o1, o2 = RET_COLS, RET_COLS + RWKV_COLS, RET_COLS + RWKV_COLS + DSA_PAD
    ret_ref[...] = jnp.dot(hb, w_ref[:, 0:o0], preferred_element_type=F32)
    rwkv_ref[...] = jnp.dot(hb, w_ref[:, o0:o1], preferred_element_type=F32)
    dsa_ref[...] = jnp.dot(hb, w_ref[:, o1:o2], preferred_element_type=F32)
    sb_ref[...] = jnp.dot(hb, w_ref[:, o2:], preferred_element_type=F32).astype(BF16)


def _mod_row(per_b, j):
    return pl.BlockSpec((1, 1, D_MODEL), lambda i: ((i // per_b) * 6 + j, 0, 0))


def _inproj(x2, gain, mod6, w_cat, S, comb=None):
    N, D = x2.shape
    tm = ROW_TILE
    per_b = S // tm
    wt = w_cat.shape[1]
    row = lambda i: (i, 0)
    in_specs = [pl.BlockSpec((tm, D), row)]
    args = [x2]
    out_specs, out_shape = [], []
    if comb is not None:
        in_specs += ([_mod_row(per_b, 5), pl.BlockSpec((tm, LANES), row)]
                     + [pl.BlockSpec((tm, D), row)] * len(comb[2]))
        args += [comb[0], comb[1]] + list(comb[2])
        out_specs.append(pl.BlockSpec((tm, D), row))
        out_shape.append(jax.ShapeDtypeStruct((N, D), F32))
    in_specs += [pl.BlockSpec((1, D), lambda i: (0, 0)), _mod_row(per_b, 1), _mod_row(per_b, 0),
                 pl.BlockSpec((D, wt), lambda i: (0, 0))]
    args += [gain, mod6, mod6, w_cat]
    out_specs += [pl.BlockSpec((tm, RET_COLS), row), pl.BlockSpec((tm, RWKV_COLS), row),
                  pl.BlockSpec((tm, DSA_PAD), row), pl.BlockSpec((tm, SB_COLS), row)]
    out_shape += [jax.ShapeDtypeStruct((N, RET_COLS), F32), jax.ShapeDtypeStruct((N, RWKV_COLS), F32),
                  jax.ShapeDtypeStruct((N, DSA_PAD), F32), jax.ShapeDtypeStruct((N, SB_COLS), BF16)]
    return pl.pallas_call(
        functools.partial(_inproj_kernel, 0 if comb is None else len(comb[2])),
        grid=(N // tm,),
        in_specs=in_specs, out_specs=out_specs, out_shape=out_shape,
        compiler_params=_params(("parallel",)),
        name="in_proj",
    )(*args)


def _ret_kernel(q_ref, k_ref, v_ref, g_ref, cos_ref, sin_ref, din_ref, qd_ref, kd_ref, cd_ref, gn_ref,
                o_ref, state_ref, y_ref):
    @pl.when(pl.program_id(1) == 0)
    def _():
        state_ref[...] = jnp.zeros_like(state_ref)

    cos, sin = cos_ref[0], sin_ref[0]
    q = _rope(q_ref[0], cos, sin, HEAD_DIM, HEAD_DIM // 2)
    k = _rope(k_ref[0], cos, sin, HEAD_DIM, HEAD_DIM // 2) * HEAD_DIM ** -0.5
    v = v_ref[0]
    qd = q * qd_ref[...]
    kd = k * kd_ref[...]
    for h in range(N_HEADS_G):
        sl = slice(h * HEAD_DIM, (h + 1) * HEAD_DIM)
        qh, kh, vh = q[:, sl].astype(BF16), k[:, sl].astype(BF16), v[:, sl].astype(BF16)
        s = lax.dot_general(qh, kh, NT_DIMS, preferred_element_type=F32) * din_ref[h]
        inner = jnp.dot(s.astype(BF16), vh, preferred_element_type=F32)
        st = state_ref[h]
        cross = _bdot(qd[:, sl], st)
        state_ref[h] = st * cd_ref[:, sl] + lax.dot_general(
            kd[:, sl].astype(BF16), vh, TN_DIMS, preferred_element_type=F32)
        o = inner + cross
        oc = o - jnp.mean(o, axis=-1, keepdims=True)
        y_ref[:, sl] = oc * lax.rsqrt(jnp.mean(oc * oc, axis=-1, keepdims=True) + NORM_EPS)
    g = g_ref[0]
    o_ref[0] = (g * _sigmoid(g) * (y_ref[...] * gn_ref[...])).astype(BF16)


def _retention(ret3, tabs, consts, gn):
    B, S, _ = ret3.shape
    C = consts["din"].shape[1]
    blk = lambda j: pl.BlockSpec((1, C, GROUP_W), lambda b, c, j=j: (b, c, j))
    tab = pl.BlockSpec((1, C, GROUP_W), lambda b, c: (b, c, 0))
    const2 = lambda shape: pl.BlockSpec(shape, lambda b, c: (0,) * len(shape))
    return pl.pallas_call(
        _ret_kernel,
        grid=(B, S // C),
        in_specs=[blk(0), blk(1), blk(2), blk(3), tab, tab,
                  const2((N_HEADS_G, C, C)), const2((C, GROUP_W)), const2((C, GROUP_W)),
                  const2((1, GROUP_W)), const2((1, GROUP_W))],
        out_specs=pl.BlockSpec((1, C, GROUP_W), lambda b, c: (b, c, 0)),
        out_shape=jax.ShapeDtypeStruct((B, S, GROUP_W), BF16),
        scratch_shapes=[pltpu.VMEM((N_HEADS_G, HEAD_DIM, HEAD_DIM), F32), pltpu.VMEM((C, GROUP_W), F32)],
        compiler_params=_params(("parallel", "arbitrary")),
        name="retention",
    )(ret3, ret3, ret3, ret3, tabs[0], tabs[1],
      consts["din"], consts["qd"], consts["kd"], consts["cd"], gn)


def _ret_consts(C):
    H = N_HEADS_G
    lg = jnp.log(1.0 - 2.0 ** (-5.0 - jnp.arange(H, dtype=F32)))
    idx = jnp.arange(C, dtype=F32)
    diff = idx[:, None] - idx[None, :]
    din = jnp.where(diff >= 0, jnp.exp(lg[:, None, None] * jnp.maximum(diff, 0.0)), 0.0)
    q_dec = jnp.exp(lg[:, None] * (idx + 1.0))
    k_dec = jnp.exp(lg[:, None] * (C - 1.0 - idx))
    chunk_dec = jnp.exp(lg * C)
    wide = lambda t: jnp.repeat(t.T, HEAD_DIM, axis=1)
    return {"din": din, "qd": wide(q_dec), "kd": wide(k_dec),
            "cd": jnp.repeat(chunk_dec, HEAD_DIM)[None, :]}


def _rwkv_prep_kernel(per_b, z_ref, zp_ref, mu_ref, w0_ref, a0_ref, w2_ref, a2_ref, g2_ref, kk_ref, ka_ref,
                      rk_ref, hm_ref, r_o, w_o, k_o, v_o, nkk_o, b_o, gate_o, bonus_o):
    z = z_ref[...]
    first = (pl.program_id(0) % per_b) == 0
    prow = jnp.where(first, 0.0, zp_ref[7:8, :])
    rid = lax.broadcasted_iota(I32, z.shape, 0)
    prev = jnp.where(rid == 0, prow, pltpu.roll(z, 1, 0))
    f = z + (prev - z) * mu_ref[...]
    G = GROUP_W
    r, k, v, lo = f[:, 0:G], f[:, G:2 * G], f[:, 2 * G:3 * G], f[:, 3 * G:3 * G + 128]
    w_log = -_softplus(-(w0_ref[...] + _bdot(jnp.tanh(lo), w2_ref[...]))) - 0.5
    decay = jnp.exp(-jnp.exp(w_log))
    a = _sigmoid(a0_ref[...] + _bdot(lo, a2_ref[...]))
    gate = _bdot(_sigmoid(lo), g2_ref[...])
    hm = hm_ref[...]
    kk = k * kk_ref[...]
    kk = kk / jnp.maximum(jnp.sqrt(_split_dot(kk * kk, hm)), 1e-12)
    k2 = k * (1.0 + (a - 1.0) * ka_ref[...])
    r_o[...] = r.astype(BF16)
    w_o[...] = decay
    k_o[...] = k2.astype(BF16)
    v_o[...] = v.astype(BF16)
    nkk_o[...] = (-kk).astype(BF16)
    b_o[...] = (kk * a).astype(BF16)
    gate_o[...] = gate
    bonus_o[...] = _split_dot(r * k2 * rk_ref[...], hm) * v


def _rwkv_prep(z2, S, mu, w0, a0, w2p, a2p, g2p, kk, ka, rk, hm):
    N, W = z2.shape
    tm = ROW_TILE
    per_b = S // tm
    row = lambda i: (i, 0)
    c = lambda shape: pl.BlockSpec(shape, lambda i: (0, 0))
    G = GROUP_W
    return pl.pallas_call(
        functools.partial(_rwkv_prep_kernel, per_b),
        grid=(N // tm,),
        in_specs=[pl.BlockSpec((tm, W), row),
                  pl.BlockSpec((SUBLANES, W), lambda i: (jnp.maximum(i * (tm // SUBLANES) - 1, 0), 0)),
                  c((1, W)), c((1, G)), c((1, G)), c((128, G)), c((128, G)), c((128, G)),
                  c((1, G)), c((1, G)), c((1, G)), c((G, G))],
        out_specs=[pl.BlockSpec((tm, G), row)] * 8,
        out_shape=[jax.ShapeDtypeStruct((N, G), dt) for dt in (BF16, F32, BF16, BF16, BF16, BF16, F32, F32)],
        compiler_params=_params(("parallel",)),
        name="rwkv_prep",
    )(z2, z2, mu, w0, a0, w2p, a2p, g2p, kk, ka, rk, hm)


def _rwkv_scan_kernel(T, nkk_ref, w_ref, b_ref, k_ref, r_ref, v_ref, y_ref, s_ref, rep_ref):
    @pl.when(pl.program_id(0) == 0)
    def _():
        s_ref[...] = jnp.zeros_like(s_ref)

    ch = v_ref.shape[1]
    groups = HEAD_DIM // ch
    width = LANES // groups
    tree = lambda parts: _tree(parts, jnp.add)

    grp = lax.broadcasted_iota(I32, (T * ch, LANES), 1) // width
    for o, ref in enumerate((nkk_ref, w_ref, b_ref, k_ref, r_ref)):
        x = ref[...].astype(F32).reshape(T * ch, LANES)
        turned = [x] + [pltpu.roll(x, m * width, 1) for m in range(1, groups)]
        for j in range(groups):
            y = turned[0]
            for m in range(1, groups):
                y = jnp.where(grp == (j + m) % groups, turned[m], y)
            rep_ref[o, :, j] = y.reshape(T, ch, LANES)

    keys = [(j, kh) for j in range(groups) for kh in range(ch)]
    row = lambda o, t, j, kh: rep_ref[o, t, j, kh:kh + 1, :]

    def step(t, carry):
        vt = v_ref[t].astype(F32)
        sa = tree([s_ref[j * ch + kh] * row(0, t, j, kh) for j, kh in keys])
        ys = []
        for j, kh in keys:
            s_new = (s_ref[j * ch + kh] * row(1, t, j, kh) + sa * row(2, t, j, kh) + vt * row(3, t, j, kh))
            s_ref[j * ch + kh] = s_new
            ys.append(s_new * row(4, t, j, kh))
        y_ref[t] = tree(ys)
        return carry

    lax.fori_loop(0, T, step, 0)


def _rwkv_scan(nkk, w, b, k, r, v):
    S, ch, _ = v.shape
    T = 32
    spec = pl.BlockSpec((T, ch, LANES), lambda i: (i, 0, 0))
    return pl.pallas_call(
        functools.partial(_rwkv_scan_kernel, T),
        grid=(S // T,),
        in_specs=[spec] * 6,
        out_specs=spec,
        out_shape=jax.ShapeDtypeStruct((S, ch, LANES), F32),
        scratch_shapes=[pltpu.VMEM((HEAD_DIM, ch, LANES), F32),
                        pltpu.VMEM((5, T, HEAD_DIM // ch, ch, LANES), F32)],
        compiler_params=_params(("arbitrary",)),
        name="rwkv_scan",
    )(nkk, w, b, k, r, v)


def _to_v_layout(x, B, S):
    P = B * N_HEADS_G
    rep = LANES // P
    t = x.reshape(B, S, N_HEADS_G, HEAD_DIM // rep, rep).transpose(1, 3, 4, 0, 2)
    return t.reshape(S, HEAD_DIM // rep, LANES)


def _from_v_layout(y, B, S):
    P = B * N_HEADS_G
    rep = LANES // P
    t = y.reshape(S, HEAD_DIM // rep, rep, B, N_HEADS_G).transpose(3, 0, 4, 1, 2)
    return t.reshape(B * S, GROUP_W)


def _dsa_prep_kernel(f_ref, qn_ref, wq_ref, wqi_ref, cq_ref, sq_ref, ci_ref, si_ref,
                     qt_o, qit_o, wt_o, k_o, v_o, ki_o):
    f = f_ref[...]
    cq = _rms(f[:, 0:DSA_Q_LORA], qn_ref[...]).astype(BF16)
    cos_q, sin_q = cq_ref[...], sq_ref[...]
    cos_i, sin_i = ci_ref[...], si_ref[...]
    q = _rope(jnp.dot(cq, wq_ref[...], preferred_element_type=F32), cos_q, sin_q, HEAD_DIM, HEAD_DIM // 8)
    tm = f.shape[0]
    qt_o[...] = (q * HEAD_DIM ** -0.5).T.reshape(N_HEADS_G, HEAD_DIM, tm).astype(BF16)
    qi = _rope(jnp.dot(cq, wqi_ref[...], preferred_element_type=F32), cos_i, sin_i, IDX_DIM, IDX_DIM // 8)
    qit_o[...] = qi.T.reshape(IDX_HEADS, IDX_DIM, tm).astype(BF16)
    kv = f[:, 128:256]
    kv_r = _rope(kv, cos_q[:, 0:128], sin_q[:, 0:128], HEAD_DIM, HEAD_DIM // 8)
    k_o[...] = kv_r[:, 0:HEAD_DIM].astype(BF16)
    v_o[...] = kv[:, HEAD_DIM:128].astype(BF16)
    tail = f[:, 256:384]
    tail_r = _rope(tail, cos_i[:, 0:128], sin_i[:, 0:128], IDX_DIM, IDX_DIM // 8)
    ki_o[...] = tail_r[:, 0:IDX_DIM].astype(BF16)
    wt_o[...] = tail.T[IDX_DIM:IDX_DIM + IDX_HEADS, :] * (IDX_HEADS ** -0.5 * IDX_DIM ** -0.5)


def _dsa_prep(f2, qn, wq, wqi, tq, ti):
    N, W = f2.shape
    tm = ROW_TILE
    G = GROUP_W
    row = lambda w: pl.BlockSpec((tm, w), lambda i: (i, 0))
    col = lambda h: pl.BlockSpec((h, tm), lambda i: (0, i))
    slab = lambda h, d: pl.BlockSpec((h, d, tm), lambda i: (0, 0, i))
    c = lambda shape: pl.BlockSpec(shape, lambda i: (0, 0))
    return pl.pallas_call(
        _dsa_prep_kernel,
        grid=(N // tm,),
        in_specs=[row(W), c((1, DSA_Q_LORA)), c((DSA_Q_LORA, G)), c((DSA_Q_LORA, G))] + [row(G)] * 4,
        out_specs=[slab(N_HEADS_G, HEAD_DIM), slab(IDX_HEADS, IDX_DIM), col(IDX_HEADS),
                   row(HEAD_DIM), row(HEAD_DIM), row(IDX_DIM)],
        out_shape=[jax.ShapeDtypeStruct((N_HEADS_G, HEAD_DIM, N), BF16),
                   jax.ShapeDtypeStruct((IDX_HEADS, IDX_DIM, N), BF16),
                   jax.ShapeDtypeStruct((IDX_HEADS, N), F32),
                   jax.ShapeDtypeStruct((N, HEAD_DIM), BF16), jax.ShapeDtypeStruct((N, HEAD_DIM), BF16),
                   jax.ShapeDtypeStruct((N, IDX_DIM), BF16)],
        compiler_params=_params(("parallel",)),
        name="dsa_prep",
    )(f2, qn, wq, wqi, *tq, *ti)


def _float_key(bits):
    return bits ^ ((bits >> 31) & 0x7FFFFFFF)


def _dsa_kernel(TK, topk, qt_ref, qit_ref, wt_ref, k_ref, v_ref, ki_ref, on_ref, o_ref,
                skey_ref, lg_ref, out_ref):
    i = pl.program_id(1)
    TQ = Q_BLOCK
    n_kt = (i * TQ + TQ + TK - 1) // TK
    q_all = jnp.concatenate([qt_ref[h] for h in range(N_HEADS_G)], axis=1)
    qi_all = jnp.concatenate([qit_ref[h] for h in range(IDX_HEADS)], axis=1)
    wt = wt_ref[...]
    kpos = lax.broadcasted_iota(I32, (TK, TQ), 0)
    qpos = i * TQ + lax.broadcasted_iota(I32, (TK, TQ), 1)
    head = lambda x, h: x[:, h * TQ:(h + 1) * TQ]

    def fold(x, op):
        return _tree([x[r * SUBLANES:(r + 1) * SUBLANES] for r in range(TK // SUBLANES)], op)

    def score_tile(kt, tops):
        off = pl.multiple_of(kt * TK, TK)
        rel = jnp.dot(ki_ref[0, pl.ds(off, TK), :], qi_all, preferred_element_type=F32)
        sc = jnp.zeros((TK, TQ), F32)
        for h in range(IDX_HEADS):
            sc = sc + jnp.maximum(head(rel, h), 0.0) * wt[h:h + 1, :]
        key = _float_key(pltpu.bitcast(sc, I32))
        key = jnp.where(sc == 0.0, -(kt * TK + kpos), jnp.where(key < 0, key - skey_ref.shape[0] * TK, key))
        valid = kt * TK + kpos <= qpos
        skey_ref[kt] = jnp.where(valid, key, INT_MIN)
        lg = jnp.dot(k_ref[0, pl.ds(off, TK), :], q_all, preferred_element_type=F32)
        lg_ref[kt] = lg
        return tuple(jnp.maximum(tops[h], fold(jnp.where(valid, head(lg, h), -1e30), jnp.maximum))
                     for h in range(N_HEADS_G))

    lowest = tuple(jnp.full((SUBLANES, TQ), -1e30, F32) for _ in range(N_HEADS_G))
    tops = lax.fori_loop(0, n_kt, score_tile, lowest)

    def count(pred):
        def count_tile(kt, acc):
            return acc + fold(jnp.where(pred(kt, skey_ref[kt]), 1.0, 0.0), jnp.add)

        acc = lax.fori_loop(0, n_kt, count_tile, jnp.zeros((SUBLANES, TQ), F32))
        return jnp.sum(acc, axis=0, keepdims=True)

    def bit_step(j, carry):
        thr, n_thr = carry
        cand = thr + lax.shift_left(jnp.int32(1), 31 - j)
        n = count(lambda kt, keys: keys >= cand)
        return jnp.where(n >= topk, cand, thr), jnp.where(n >= topk, n, n_thr)

    thr, n_thr = lax.fori_loop(0, 32, bit_step,
                               (jnp.full((1, TQ), INT_MIN, I32), jnp.full((1, TQ), float(topk), F32)))
    thr = jnp.maximum(thr, INT_MIN + 1)

    @pl.when(jnp.max(n_thr) > topk)
    def _():
        keep = topk - count(lambda kt, keys: keys > thr)

        pos_bits = (skey_ref.shape[0] * TK - 1).bit_length()

        def pos_step(j, last):
            cand = last + lax.shift_left(jnp.int32(1), pos_bits - 1 - j)
            n = count(lambda kt, keys: jnp.logical_and(keys == thr, kt * TK + kpos < cand))
            return jnp.where(n < keep, cand, last)

        last = lax.fori_loop(0, pos_bits, pos_step, jnp.zeros((1, TQ), I32))

        def retire(kt, carry):
            keys = skey_ref[kt]
            skey_ref[kt] = jnp.where(jnp.logical_and(keys == thr, kt * TK + kpos > last), INT_MIN, keys)
            return carry

        lax.fori_loop(0, n_kt, retire, 0)

    def sweep(shifts):
        def acc_tile(kt, carry):
            ls, acc = carry
            sel = skey_ref[kt] >= thr
            ps = [jnp.where(sel, jnp.exp(lg_ref[kt, :, h * TQ:(h + 1) * TQ] - shifts[h]), 0.0)
                  for h in range(N_HEADS_G)]
            p_all = jnp.concatenate([p.astype(BF16) for p in ps], axis=1)
            vb = v_ref[0, pl.ds(pl.multiple_of(kt * TK, TK), TK), :]
            return (tuple(ls[h] + fold(ps[h], jnp.add) for h in range(N_HEADS_G)),
                    acc + lax.dot_general(vb, p_all, TN_DIMS, preferred_element_type=F32))

        ls, acc = lax.fori_loop(0, n_kt, acc_tile,
                                (tuple(jnp.zeros((SUBLANES, TQ), F32) for _ in range(N_HEADS_G)),
                                 jnp.zeros((HEAD_DIM, N_HEADS_G * TQ), F32)))
        ls = [jnp.sum(l, axis=0, keepdims=True) for l in ls]
        for h in range(N_HEADS_G):
            out_ref[h * HEAD_DIM:(h + 1) * HEAD_DIM, :] = head(acc, h) / ls[h]
        return jnp.min(jnp.minimum(jnp.minimum(ls[0], ls[1]), jnp.minimum(ls[2], ls[3])))

    smallest = sweep([jnp.max(t, axis=0, keepdims=True) for t in tops])

    @pl.when(smallest < SOFTMAX_FLOOR)
    def _():
        def max_tile(kt, ms):
            sel = skey_ref[kt] >= thr
            return tuple(jnp.maximum(ms[h], fold(jnp.where(sel, lg_ref[kt, :, h * TQ:(h + 1) * TQ], -1e30),
                                                 jnp.maximum))
                         for h in range(N_HEADS_G))

        ms = lax.fori_loop(0, n_kt, max_tile, lowest)
        sweep([jnp.max(m, axis=0, keepdims=True) for m in ms])

    o_ref[0] = _rms(out_ref[...].T, on_ref[...]).astype(BF16)


def _dsa_attention(qt, qit, wt, k, v, ki, on):
    B, S, _ = k.shape
    G = GROUP_W
    nq = S // Q_BLOCK
    TK = min(512, S)
    topk = min(DSA_TOPK_MAX, S // 4)
    qcol = lambda h: pl.BlockSpec((h, Q_BLOCK), lambda b, i: (0, b * nq + i))
    qslab = lambda h, d: pl.BlockSpec((h, d, Q_BLOCK), lambda b, i: (0, 0, b * nq + i))
    full = lambda wd: pl.BlockSpec((1, S, wd), lambda b, i: (b, 0, 0))
    return pl.pallas_call(
        functools.partial(_dsa_kernel, TK, topk),
        grid=(B, nq),
        in_specs=[qslab(N_HEADS_G, HEAD_DIM), qslab(IDX_HEADS, IDX_DIM), qcol(IDX_HEADS),
                  full(HEAD_DIM), full(HEAD_DIM), full(IDX_DIM), pl.BlockSpec((1, G), lambda b, i: (0, 0))],
        out_specs=pl.BlockSpec((1, Q_BLOCK, G), lambda b, i: (b, i, 0)),
        out_shape=jax.ShapeDtypeStruct((B, S, G), BF16),
        scratch_shapes=[pltpu.VMEM((S // TK, TK, Q_BLOCK), I32),
                        pltpu.VMEM((S // TK, TK, N_HEADS_G * Q_BLOCK), F32),
                        pltpu.VMEM((G, Q_BLOCK), F32)],
        compiler_params=_params(("parallel", "arbitrary")),
        name="dsa_attention",
    )(qt, qit, wt, k, v, ki, on)


def _sb_kernel(q_ref, k_ref, v_ref, u_ref, on_ref, o_ref, z_ref, lm_ref):
    i = pl.program_id(1)
    T = Q_BLOCK
    H, G = N_HEADS_G, GROUP_W
    u = u_ref[...]
    qt = q_ref[0].astype(F32).T.astype(BF16)
    q_rows = lax.broadcasted_iota(I32, (G, H * T), 0) // HEAD_DIM
    q_cols = lax.broadcasted_iota(I32, (G, H * T), 1) // T
    qbd = jnp.where(q_rows == q_cols, jnp.concatenate([qt] * H, axis=1), jnp.zeros((), BF16))
    kid = lax.broadcasted_iota(I32, (T, H * T), 0)
    qid = lax.broadcasted_iota(I32, (T, H * T), 1) & (T - 1)
    strict = kid < qid

    def cond(carry):
        kt, c, _ = carry
        return jnp.logical_and(kt >= 0, jnp.max(c) > SB_UNDERFLOW)

    def logits(kt):
        z = jnp.dot(k_ref[0, pl.ds(pl.multiple_of(kt * T, T), T), :], qbd, preferred_element_type=F32)
        z = z * HEAD_DIM ** -0.5
        return z, -_softplus(z)

    z0, lm0 = logits(i)
    z_ref[...] = z0
    lm_ref[...] = jnp.where(strict, lm0, 0.0)

    def body(carry):
        kt, c, acc = carry
        off = pl.multiple_of(kt * T, T)
        mask = jnp.logical_or(strict, kt < i)
        z, lm = z_ref[...], lm_ref[...]
        z_ref[...], lm_ref[...] = logits(jnp.maximum(kt - 1, 0))
        hi, lo = _split(lm)
        later = jnp.dot(u, jnp.concatenate([hi, lo], axis=1), preferred_element_type=F32)
        later = later[:, :H * T] + later[:, H * T:]
        a = jnp.where(mask, jnp.exp(z + lm + (c + later)), 0.0)
        av = lax.dot_general(v_ref[0, pl.ds(off, T), :], a.astype(BF16), TN_DIMS, preferred_element_type=F32)
        acc = acc + jnp.concatenate(
            [av[h * HEAD_DIM:(h + 1) * HEAD_DIM, h * T:(h + 1) * T] for h in range(H)], axis=0)
        return kt - 1, c + jnp.sum(lm, axis=0, keepdims=True), acc

    _, _, acc = lax.while_loop(cond, body, (i, jnp.zeros((1, H * T), F32), jnp.zeros((G, T), F32)))
    o_ref[0] = _rms(acc.T, on_ref[...]).astype(BF16)


def _stick_breaking(sb3, u, on):
    B, S, _ = sb3.shape
    G = GROUP_W
    return pl.pallas_call(
        _sb_kernel,
        grid=(B, S // Q_BLOCK),
        in_specs=[pl.BlockSpec((1, Q_BLOCK, G), lambda b, i: (b, i, 0)),
                  pl.BlockSpec((1, S, G), lambda b, i: (b, 0, 1)),
                  pl.BlockSpec((1, S, G), lambda b, i: (b, 0, 2)),
                  pl.BlockSpec((Q_BLOCK, Q_BLOCK), lambda b, i: (0, 0)),
                  pl.BlockSpec((1, G), lambda b, i: (0, 0))],
        out_specs=pl.BlockSpec((1, Q_BLOCK, G), lambda b, i: (b, i, 0)),
        out_shape=jax.ShapeDtypeStruct((B, S, G), BF16),
        scratch_shapes=[pltpu.VMEM((Q_BLOCK, N_HEADS_G * Q_BLOCK), F32)] * 2,
        compiler_params=_params(("parallel", "arbitrary")),
        name="stick_breaking",
    )(sb3, sb3, sb3, u, on)


def _outproj_kernel(yr_ref, yd_ref, ys_ref, yw_ref, bonus_ref, gate_ref, ln_ref, hm_ref, x_ref, g1_ref, sc_ref,
                    sh_ref, gn_ref, wo_ref, rw_ref, rb_ref, x_o, h_o, lg_o):
    G = GROUP_W
    y = yw_ref[...]
    hm = hm_ref[...]
    yc = y - _split_dot(y, hm) * (1.0 / HEAD_DIM)
    var = _split_dot(yc * yc, hm) * (1.0 / HEAD_DIM)
    yw = ((yc * lax.rsqrt(var + RWKV_LN_EPS) * ln_ref[...] + bonus_ref[...]) * gate_ref[...]).astype(BF16)
    dot = lambda yv, g: jnp.dot(yv, wo_ref[g * G:(g + 1) * G, :], preferred_element_type=F32)
    mixed = dot(yr_ref[...], 0) + dot(yw, 1) + dot(yd_ref[...], 2) + dot(ys_ref[...], 3)
    x1 = x_ref[...] + g1_ref[0] * mixed
    x_o[...] = x1
    h = _rms(x1, gn_ref[...]) * (1.0 + sc_ref[0]) + sh_ref[0]
    h_o[...] = h
    hi, lo = _split(h)
    work = (jnp.dot(hi, rw_ref[0], preferred_element_type=F32) + jnp.dot(lo, rw_ref[0], preferred_element_type=F32)
            + jnp.dot(hi, rw_ref[1], preferred_element_type=F32) + rb_ref[...])
    col = lax.broadcasted_iota(I32, work.shape, 1).astype(F32)
    out = jnp.zeros_like(work)
    vals = []
    for j in range(TOP_K):
        m = jnp.max(work, axis=-1, keepdims=True)
        idx = jnp.min(jnp.where(work == m, col, float(LANES)), axis=-1, keepdims=True)
        out = jnp.where(col == float(j), idx, out)
        vals.append(m)
        work = jnp.where(col == idx, -jnp.inf, work)
    es = [jnp.exp(v - vals[0]) for v in vals]
    den = es[0] + es[1] + es[2] + es[3]
    for j in range(TOP_K):
        out = jnp.where(col == float(TOP_K + j), es[j] / den, out)
    lg_o[...] = out


def _outproj(ys, rwkv, x2, mod6, gain, w_out, router_w, router_b, S):
    N, D = x2.shape
    tm = ROW_TILE
    per_b = S // tm
    G = GROUP_W
    row = lambda w: pl.BlockSpec((tm, w), lambda i: (i, 0))
    c = lambda shape: pl.BlockSpec(shape, lambda i: (0, 0))
    return pl.pallas_call(
        _outproj_kernel,
        grid=(N // tm,),
        in_specs=[row(G)] * 6 + [c((1, G)), c((G, G)),
                                 row(D), _mod_row(per_b, 2), _mod_row(per_b, 4), _mod_row(per_b, 3),
                                 c((1, D)), c((D, D)), pl.BlockSpec((2, D, LANES), lambda i: (0, 0, 0)),
                                 c((1, LANES))],
        out_specs=[row(D), row(D), row(LANES)],
        out_shape=[jax.ShapeDtypeStruct((N, D), F32), jax.ShapeDtypeStruct((N, D), F32),
                   jax.ShapeDtypeStruct((N, LANES), F32)],
        compiler_params=_params(("parallel",)),
        name="out_proj_router",
    )(*ys, *rwkv, x2, mod6, mod6, mod6, gain, w_out,
      jnp.pad(jnp.stack(_split(router_w)), ((0, 0), (0, 0), (0, LANES - N_EXPERTS))),
      jnp.pad(router_b, ((0, 0), (0, LANES - N_EXPERTS)), constant_values=-1e30))


def _moe_kernel(be_ref, nu_ref, x_ref, w1_ref, b1_ref, w2_ref, b2_ref, o_ref):
    i = pl.program_id(0)

    @pl.when(i < nu_ref[0])
    def _():
        xb = x_ref[...].astype(BF16)
        w1 = w1_ref.at[0, 0]
        glu = jnp.dot(xb, w1[:, 0:D_FF].astype(BF16), preferred_element_type=F32) + b1_ref[0, :, 0:D_FF]
        lin = jnp.dot(xb, w1[:, D_FF:].astype(BF16), preferred_element_type=F32) + b1_ref[0, :, D_FF:]
        glu = jnp.minimum(glu, SWIGLU_LIMIT)
        lin = jnp.clip(lin, -SWIGLU_LIMIT, SWIGLU_LIMIT)
        act = glu * _sigmoid(SWIGLU_ALPHA * glu) * (lin + 1.0)
        y = jnp.dot(act.astype(BF16), w2_ref[0, 0].astype(BF16), preferred_element_type=F32) + b2_ref[0]
        o_ref[...] = y.astype(BF16)

    @pl.when(i >= nu_ref[0])
    def _():
        o_ref[...] = jnp.zeros_like(o_ref)


def _moe_ffn(block_expert, n_used, xs, layer, w1, b1, w2, b2):
    rows, D = xs.shape
    nb = rows // MOE_BLOCK
    L, E = w1.shape[:2]
    live = lambda i, nu: jnp.minimum(i, nu[0] - 1)
    grid_spec = pltpu.PrefetchScalarGridSpec(
        num_scalar_prefetch=2,
        grid=(nb,),
        in_specs=[pl.BlockSpec((MOE_BLOCK, D), lambda i, be, nu: (live(i, nu), 0)),
                  pl.BlockSpec((1, 1, D, 2 * D_FF), lambda i, be, nu: (layer, be[i], 0, 0)),
                  pl.BlockSpec((1, 1, 2 * D_FF), lambda i, be, nu: (layer * E + be[i], 0, 0)),
                  pl.BlockSpec((1, 1, D_FF, D), lambda i, be, nu: (layer, be[i], 0, 0)),
                  pl.BlockSpec((1, 1, D), lambda i, be, nu: (layer * E + be[i], 0, 0))],
        out_specs=pl.BlockSpec((MOE_BLOCK, D), lambda i, be, nu: (i, 0)),
    )
    return pl.pallas_call(
        _moe_kernel,
        grid_spec=grid_spec,
        out_shape=jax.ShapeDtypeStruct((rows, D), BF16),
        compiler_params=_params(("arbitrary",), 56 * 1024 * 1024),
        name="moe_ffn",
    )(block_expert, n_used, xs, w1, b1.reshape(L * E, 1, -1), w2, b2.reshape(L * E, 1, -1))


def _route(routed):
    n = routed.shape[0]
    expert = routed[:, 0:TOP_K].reshape(-1).astype(I32)
    n_assign = n * TOP_K
    ids = jnp.arange(n_assign, dtype=I32)
    _, order = lax.sort((expert, ids), num_keys=1, is_stable=True)
    _, inv = lax.sort((order, ids), num_keys=1)
    counts = jnp.sum((expert[:, None] == jnp.arange(N_EXPERTS, dtype=I32)[None, :]).astype(I32), axis=0)
    starts = jnp.cumsum(counts) - counts
    padded = (counts + MOE_BLOCK - 1) // MOE_BLOCK * MOE_BLOCK
    pad_end = jnp.cumsum(padded)
    pad_start = pad_end - padded
    dest = pad_start[expert] + inv - starts[expert]
    n_blocks = -(-n_assign // MOE_BLOCK) + N_EXPERTS
    block_start = jnp.arange(n_blocks, dtype=I32) * MOE_BLOCK
    block_expert = jnp.minimum(jnp.sum(pad_end[None, :] <= block_start[:, None], axis=1), N_EXPERTS - 1)
    block_expert = block_expert.astype(I32)
    per_row = lambda t: jnp.repeat(t[block_expert], MOE_BLOCK)
    off = jnp.arange(n_blocks * MOE_BLOCK, dtype=I32) - per_row(pad_start)
    valid = off < per_row(counts)
    src = order[jnp.clip(per_row(starts) + off, 0, n_assign - 1)]
    row_token = jnp.where(valid, src // TOP_K, 0)
    n_used = (pad_end[-1] // MOE_BLOCK).astype(I32).reshape(1)
    return dest.reshape(n, TOP_K), row_token, block_expert, n_used


def _rows(t, idx):
    return t.at[idx].get(mode="promise_in_bounds")


def _final_kernel(x_ref, g2_ref, routed_ref, y0, y1, y2, y3, g_ref, o_ref):
    o_ref[...] = _rms(_combine(x_ref, g2_ref, routed_ref, (y0, y1, y2, y3)), g_ref[...])


def _final_norm(x2, mod6, routed, ys, g, S):
    N, D = x2.shape
    tm = ROW_TILE
    row = pl.BlockSpec((tm, D), lambda i: (i, 0))
    return pl.pallas_call(
        _final_kernel, grid=(N // tm,),
        in_specs=[row, _mod_row(S // tm, 5), pl.BlockSpec((tm, LANES), lambda i: (i, 0))] + [row] * 4
        + [pl.BlockSpec((1, D), lambda i: (0, 0))],
        out_specs=row,
        out_shape=jax.ShapeDtypeStruct((N, D), F32),
        compiler_params=_params(("parallel",)), name="final_norm",
    )(x2, mod6, routed, *ys, g)


def _rope_tables(pos, group, rot_dim, theta, width):
    half = rot_dim // 2
    inv = theta ** (-jnp.arange(half, dtype=F32) / half)
    ang = pos.astype(F32)[..., None] * inv
    cos, sin = jnp.cos(ang), jnp.sin(ang)
    rest = group - rot_dim
    ones = jnp.ones(ang.shape[:2] + (rest,), F32)
    zeros = jnp.zeros(ang.shape[:2] + (rest,), F32)
    rep = width // group
    tile = lambda t: jnp.tile(t, (1, 1, rep)).reshape(-1, width)
    return (tile(jnp.concatenate([cos, cos, ones], -1)), tile(jnp.concatenate([-sin, sin, zeros], -1)))


def _pad_rows(w, start, total=128):
    return jnp.zeros((total, w.shape[1]), w.dtype).at[start:start + w.shape[0]].set(w)


def kernel(x, c, positions, ada_w, ada_b, norm_mix, norm_ffn, w_in, ret_gn, rwkv_mu, rwkv_w0, rwkv_w2, rwkv_a0,
           rwkv_a2, rwkv_g2, rwkv_kk, rwkv_ka, rwkv_rk, rwkv_ln, dsa_qnorm, dsa_wq_up, dsa_wqi_up, dsa_onorm,
           sb_onorm, w_out, router_w, router_b, moe_w1, moe_b1, moe_w2, moe_b2, norm_final):
    B, S, D = x.shape
    N = B * S
    L = ada_w.shape[0]
    G = GROUP_W
    row2 = lambda t: t.reshape(1, -1)

    mod = _ada(c, ada_w, ada_b)
    ret_tabs = tuple(t.reshape(B, S, G) for t in _rope_tables(positions, HEAD_DIM, HEAD_DIM, RET_THETA, G))
    dq_tabs = _rope_tables(positions, HEAD_DIM, HEAD_DIM // 4, ROPE_THETA, G)
    di_tabs = _rope_tables(positions, IDX_DIM, IDX_DIM // 4, ROPE_THETA, G)
    ret_c = _ret_consts(min(256, S))
    hm = (jnp.arange(G)[:, None] // HEAD_DIM == jnp.arange(G)[None, :] // HEAD_DIM).astype(BF16)
    u_later = (jnp.arange(Q_BLOCK)[None, :] > jnp.arange(Q_BLOCK)[:, None]).astype(BF16)
    o1, o2 = RET_COLS + RWKV_COLS, RET_COLS + RWKV_COLS + DSA_COLS

    x2 = x.reshape(N, D)
    comb = None
    for l in range(L):
        mod6 = mod[l].reshape(B * 6, 1, D)
        w_l = w_in[l]
        w_cat = jnp.concatenate(
            [w_l[:, :o1], w_l[:, o1:o2], jnp.zeros((D, DSA_PAD - DSA_COLS), F32), w_l[:, o2:]], axis=1).astype(BF16)
        outs = _inproj(x2, row2(norm_mix[l]), mod6, w_cat, S, comb)
        if comb is not None:
            x2, outs = outs[0], outs[1:]
        ret, rwkv, dsa, sb = outs

        prep = _rwkv_prep(rwkv, S, row2(rwkv_mu[l]), row2(rwkv_w0[l]), row2(rwkv_a0[l]),
                          _pad_rows(rwkv_w2[l], 0).astype(BF16), _pad_rows(rwkv_a2[l], 32).astype(BF16),
                          _pad_rows(rwkv_g2[l], 64).astype(BF16), row2(rwkv_kk[l]), row2(rwkv_ka[l]),
                          row2(rwkv_rk[l]), hm)
        r_, w_, k_, v_, nkk_, b_, gate_, bonus_ = prep
        scan_in = tuple(_to_v_layout(t, B, S) for t in (nkk_, w_, b_, k_, r_, v_))

        y_ret = _retention(ret.reshape(B, S, RET_COLS), ret_tabs, ret_c, row2(ret_gn[l])).reshape(N, G)

        dqt, dqit, dwt, dk, dv, dki = _dsa_prep(dsa, row2(dsa_qnorm[l]), dsa_wq_up[l].astype(BF16),
                                                dsa_wqi_up[l].astype(BF16), dq_tabs, di_tabs)
        y_dsa = _dsa_attention(dqt, dqit, dwt, dk.reshape(B, S, HEAD_DIM), dv.reshape(B, S, HEAD_DIM),
                               dki.reshape(B, S, IDX_DIM), row2(dsa_onorm[l])).reshape(N, G)

        y_sb = _stick_breaking(sb.reshape(B, S, SB_COLS), u_later, row2(sb_onorm[l])).reshape(N, G)

        y_scan = _rwkv_scan(*scan_in)
        rwkv_tail = (_from_v_layout(y_scan, B, S), bonus_, gate_, row2(rwkv_ln[l]), hm)

        x1, h2, routed = _outproj((y_ret, y_dsa, y_sb), rwkv_tail, x2, mod6, row2(norm_ffn[l]),
                                  w_out[l].astype(BF16), router_w[l], row2(router_b[l]), S)

        dest, row_token, block_expert, n_used = _route(routed)
        y = _moe_ffn(block_expert, n_used, _rows(h2, row_token), l, moe_w1, moe_b1, moe_w2, moe_b2)
        x2 = x1
        comb = (mod6, routed, tuple(_rows(y, dest[:, j]) for j in range(TOP_K)))
    return _final_norm(x2, *comb, row2(norm_final), S).reshape(B, S, D)
```

```python
import functools

import jax
import jax.numpy as jnp
from jax import lax
from jax.experimental import pallas as pl
from jax.experimental.pallas import tpu as pltpu

F32 = jnp.float32
BF16 = jnp.bfloat16
I32 = jnp.int32

D_MODEL = 1024
GROUP_W = 256
HEAD_DIM = 64
N_HEADS_G = 4
NORM_EPS = 1e-5
Q_BLOCK = 128
RET_THETA = 10000.0
RWKV_LN_EPS = 64e-5
DSA_Q_LORA = 128
IDX_HEADS = 8
IDX_DIM = 32
DSA_TOPK_MAX = 256
ROPE_THETA = 500000.0
N_EXPERTS = 32
TOP_K = 4
D_FF = D_MODEL
SWIGLU_ALPHA = 1.702
SWIGLU_LIMIT = 7.0
MOE_BLOCK = 512

RET_COLS = 4 * GROUP_W
RWKV_COLS = 3 * GROUP_W + 128
DSA_COLS = 296
DSA_PAD = 384
SB_COLS = 3 * GROUP_W

LANES = 128
SUBLANES = 8
ROW_TILE = 512
VMEM_LIMIT = 48 * 1024 * 1024
INT_MIN = -2 ** 31
SB_UNDERFLOW = -104.0
SOFTMAX_FLOOR = 1e-20

HIGHEST = lax.Precision.HIGHEST
NT_DIMS = (((1,), (1,)), ((), ()))
TN_DIMS = (((0,), (0,)), ((), ()))


def _params(sem, vmem=VMEM_LIMIT):
    return pltpu.CompilerParams(dimension_semantics=sem, vmem_limit_bytes=vmem)


def _bdot(a, b):
    return jnp.dot(a.astype(BF16), b.astype(BF16), preferred_element_type=F32)


def _split(x):
    hi = x.astype(BF16)
    return hi, (x - hi.astype(F32)).astype(BF16)


def _split_dot(x, m):
    hi, lo = _split(x)
    return jnp.dot(hi, m, preferred_element_type=F32) + jnp.dot(lo, m, preferred_element_type=F32)


def _tree(parts, op):
    while len(parts) > 1:
        parts = [op(parts[i], parts[i + 1]) for i in range(0, len(parts), 2)]
    return parts[0]


def _sigmoid(x):
    return 1.0 / (1.0 + jnp.exp(-x))


def _softplus(x):
    return jnp.maximum(x, 0.0) + jnp.log(1.0 + jnp.exp(-jnp.abs(x)))


def _rms(x, g, eps=NORM_EPS):
    return x * lax.rsqrt(jnp.mean(x * x, axis=-1, keepdims=True) + eps) * g


def _rope(x, cos, sin, group, half):
    w = x.shape[-1]
    first = (lax.broadcasted_iota(I32, (1, w), 1) & (group - 1)) < half
    partner = jnp.where(first, pltpu.roll(x, w - half, 1), pltpu.roll(x, half, 1))
    return x * cos + partner * sin


def _ada_kernel(c_ref, w_ref, b_ref, o_ref):
    c = c_ref[...]
    cond = c * _sigmoid(c)
    o_ref[0] = jnp.dot(cond, w_ref[0], preferred_element_type=F32, precision=HIGHEST) + b_ref[0]


def _ada(c, ada_w, ada_b):
    L, D, W = ada_w.shape
    B = c.shape[0]
    tn = 1024
    return pl.pallas_call(
        _ada_kernel,
        grid=(L, W // tn),
        in_specs=[pl.BlockSpec((B, D), lambda l, j: (0, 0)),
                  pl.BlockSpec((1, D, tn), lambda l, j: (l, 0, j)),
                  pl.BlockSpec((1, 1, tn), lambda l, j: (l, 0, j))],
        out_specs=pl.BlockSpec((1, B, tn), lambda l, j: (l, 0, j)),
        out_shape=jax.ShapeDtypeStruct((L, B, W), F32),
        compiler_params=_params(("parallel", "parallel")),
        name="ada_mod",
    )(c, ada_w, ada_b.reshape(L, 1, W))


def _combine(x_ref, g2_ref, routed_ref, y_refs):
    gates = routed_ref[...]
    moe = gates[:, TOP_K:TOP_K + 1] * y_refs[0][...].astype(F32)
    for j in range(1, len(y_refs)):
        moe = moe + gates[:, TOP_K + j:TOP_K + j + 1] * y_refs[j][...].astype(F32)
    return x_ref[...] + g2_ref[0] * moe


def _inproj_kernel(n_comb, x_ref, *refs):
    if n_comb:
        g2_ref, routed_ref, y_refs, refs = refs[0], refs[1], refs[2:2 + n_comb], refs[2 + n_comb:]
        g_ref, sc_ref, sh_ref, w_ref, x_o, ret_ref, rwkv_ref, dsa_ref, sb_ref = refs
        x = _combine(x_ref, g2_ref, routed_ref, y_refs)
        x_o[...] = x
    else:
        g_ref, sc_ref, sh_ref, w_ref, ret_ref, rwkv_ref, dsa_ref, sb_ref = refs
        x = x_ref[...]
    h = _rms(x, g_ref[...]) * (1.0 + sc_ref[0]) + sh_ref[0]
    hb = h.astype(BF16)
    o0, o1, o2 = RET_COLS, RET_COLS + RWKV_COLS, RET_COLS + RWKV_COLS + DSA_PAD
    ret_ref[...] = jnp.dot(hb, w_ref[:, 0:o0], preferred_element_type=F32)
    rwkv_ref[...] = jnp.dot(hb, w_ref[:, o0:o1], preferred_element_type=F32)
    dsa_ref[...] = jnp.dot(hb, w_ref[:, o1:o2], preferred_element_type=F32)
    sb_ref[...] = jnp.dot(hb, w_ref[:, o2:], preferred_element_type=F32).astype(BF16)


def _mod_row(per_b, j):
    return pl.BlockSpec((1, 1, D_MODEL), lambda i: ((i // per_b) * 6 + j, 0, 0))


def _inproj(x2, gain, mod6, w_cat, S, comb=None):
    N, D = x2.shape
    tm = ROW_TILE
    per_b = S // tm
    wt = w_cat.shape[1]
    row = lambda i: (i, 0)
    in_specs = [pl.BlockSpec((tm, D), row)]
    args = [x2]
    out_specs, out_shape = [], []
    if comb is not None:
        in_specs += ([_mod_row(per_b, 5), pl.BlockSpec((tm, LANES), row)]
                     + [pl.BlockSpec((tm, D), row)] * len(comb[2]))
        args += [comb[0], comb[1]] + list(comb[2])
        out_specs.append(pl.BlockSpec((tm, D), row))
        out_shape.append(jax.ShapeDtypeStruct((N, D), F32))
    in_specs += [pl.BlockSpec((1, D), lambda i: (0, 0)), _mod_row(per_b, 1), _mod_row(per_b, 0),
                 pl.BlockSpec((D, wt), lambda i: (0, 0))]
    args += [gain, mod6, mod6, w_cat]
    out_specs += [pl.BlockSpec((tm, RET_COLS), row), pl.BlockSpec((tm, RWKV_COLS), row),
                  pl.BlockSpec((tm, DSA_PAD), row), pl.BlockSpec((tm, SB_COLS), row)]
    out_shape += [jax.ShapeDtypeStruct((N, RET_COLS), F32), jax.ShapeDtypeStruct((N, RWKV_COLS), F32),
                  jax.ShapeDtypeStruct((N, DSA_PAD), F32), jax.ShapeDtypeStruct((N, SB_COLS), BF16)]
    return pl.pallas_call(
        functools.partial(_inproj_kernel, 0 if comb is None else len(comb[2])),
        grid=(N // tm,),
        in_specs=in_specs, out_specs=out_specs, out_shape=out_shape,
        compiler_params=_params(("parallel",)),
        name="in_proj",
    )(*args)


def _ret_kernel(q_ref, k_ref, v_ref, g_ref, cos_ref, sin_ref, din_ref, qd_ref, kd_ref, cd_ref, gn_ref,
                o_ref, state_ref, y_ref):
    @pl.when(pl.program_id(1) == 0)
    def _():
        state_ref[...] = jnp.zeros_like(state_ref)

    cos, sin = cos_ref[0], sin_ref[0]
    q = _rope(q_ref[0], cos, sin, HEAD_DIM, HEAD_DIM // 2)
    k = _rope(k_ref[0], cos, sin, HEAD_DIM, HEAD_DIM // 2) * HEAD_DIM ** -0.5
    v = v_ref[0]
    qd = q * qd_ref[...]
    kd = k * kd_ref[...]
    for h in range(N_HEADS_G):
        sl = slice(h * HEAD_DIM, (h + 1) * HEAD_DIM)
        qh, kh, vh = q[:, sl].astype(BF16), k[:, sl].astype(BF16), v[:, sl].astype(BF16)
        s = lax.dot_general(qh, kh, NT_DIMS, preferred_element_type=F32) * din_ref[h]
        inner = jnp.dot(s.astype(BF16), vh, preferred_element_type=F32)
        st = state_ref[h]
        cross = _bdot(qd[:, sl], st)
        state_ref[h] = st * cd_ref[:, sl] + lax.dot_general(
            kd[:, sl].astype(BF16), vh, TN_DIMS, preferred_element_type=F32)
        o = inner + cross
        oc = o - jnp.mean(o, axis=-1, keepdims=True)
        y_ref[:, sl] = oc * lax.rsqrt(jnp.mean(oc * oc, axis=-1, keepdims=True) + NORM_EPS)
    g = g_ref[0]
    o_ref[0] = (g * _sigmoid(g) * (y_ref[...] * gn_ref[...])).astype(BF16)


def _retention(ret3, tabs, consts, gn):
    B, S, _ = ret3.shape
    C = consts["din"].shape[1]
    blk = lambda j: pl.BlockSpec((1, C, GROUP_W), lambda b, c, j=j: (b, c, j))
    tab = pl.BlockSpec((1, C, GROUP_W), lambda b, c: (b, c, 0))
    const2 = lambda shape: pl.BlockSpec(shape, lambda b, c: (0,) * len(shape))
    return pl.pallas_call(
        _ret_kernel,
        grid=(B, S // C),
        in_specs=[blk(0), blk(1), blk(2), blk(3), tab, tab,
                  const2((N_HEADS_G, C, C)), const2((C, GROUP_W)), const2((C, GROUP_W)),
                  const2((1, GROUP_W)), const2((1, GROUP_W))],
        out_specs=pl.BlockSpec((1, C, GROUP_W), lambda b, c: (b, c, 0)),
        out_shape=jax.ShapeDtypeStruct((B, S, GROUP_W), BF16),
        scratch_shapes=[pltpu.VMEM((N_HEADS_G, HEAD_DIM, HEAD_DIM), F32), pltpu.VMEM((C, GROUP_W), F32)],
        compiler_params=_params(("parallel", "arbitrary")),
        name="retention",
    )(ret3, ret3, ret3, ret3, tabs[0], tabs[1],
      consts["din"], consts["qd"], consts["kd"], consts["cd"], gn)


def _ret_consts(C):
    H = N_HEADS_G
    lg = jnp.log(1.0 - 2.0 ** (-5.0 - jnp.arange(H, dtype=F32)))
    idx = jnp.arange(C, dtype=F32)
    diff = idx[:, None] - idx[None, :]
    din = jnp.where(diff >= 0, jnp.exp(lg[:, None, None] * jnp.maximum(diff, 0.0)), 0.0)
    q_dec = jnp.exp(lg[:, None] * (idx + 1.0))
    k_dec = jnp.exp(lg[:, None] * (C - 1.0 - idx))
    chunk_dec = jnp.exp(lg * C)
    wide = lambda t: jnp.repeat(t.T, HEAD_DIM, axis=1)
    return {"din": din, "qd": wide(q_dec), "kd": wide(k_dec),
            "cd": jnp.repeat(chunk_dec, HEAD_DIM)[None, :]}


def _rwkv_prep_kernel(per_b, z_ref, zp_ref, mu_ref, w0_ref, a0_ref, w2_ref, a2_ref, g2_ref, kk_ref, ka_ref,
                      rk_ref, hm_ref, r_o, w_o, k_o, v_o, nkk_o, b_o, gate_o, bonus_o):
    z = z_ref[...]
    first = (pl.program_id(0) % per_b) == 0
    prow = jnp.where(first, 0.0, zp_ref[7:8, :])
    rid = lax.broadcasted_iota(I32, z.shape, 0)
    prev = jnp.where(rid == 0, prow, pltpu.roll(z, 1, 0))
    f = z + (prev - z) * mu_ref[...]
    G = GROUP_W
    r, k, v, lo = f[:, 0:G], f[:, G:2 * G], f[:, 2 * G:3 * G], f[:, 3 * G:3 * G + 128]
    w_log = -_softplus(-(w0_ref[...] + _bdot(jnp.tanh(lo), w2_ref[...]))) - 0.5
    decay = jnp.exp(-jnp.exp(w_log))
    a = _sigmoid(a0_ref[...] + _bdot(lo, a2_ref[...]))
    gate = _bdot(_sigmoid(lo), g2_ref[...])
    hm = hm_ref[...]
    kk = k * kk_ref[...]
    kk = kk / jnp.maximum(jnp.sqrt(_split_dot(kk * kk, hm)), 1e-12)
    k2 = k * (1.0 + (a - 1.0) * ka_ref[...])
    r_o[...] = r.astype(BF16)
    w_o[...] = decay
    k_o[...] = k2.astype(BF16)
    v_o[...] = v.astype(BF16)
    nkk_o[...] = (-kk).astype(BF16)
    b_o[...] = (kk * a).astype(BF16)
    gate_o[...] = gate.astype(BF16)
    bonus_o[...] = (_split_dot(r * k2 * rk_ref[...], hm) * v).astype(BF16)


def _rwkv_prep(z2, S, mu, w0, a0, w2p, a2p, g2p, kk, ka, rk, hm):
    N, W = z2.shape
    tm = ROW_TILE
    per_b = S // tm
    row = lambda i: (i, 0)
    c = lambda shape: pl.BlockSpec(shape, lambda i: (0, 0))
    G = GROUP_W
    return pl.pallas_call(
        functools.partial(_rwkv_prep_kernel, per_b),
        grid=(N // tm,),
        in_specs=[pl.BlockSpec((tm, W), row),
                  pl.BlockSpec((SUBLANES, W), lambda i: (jnp.maximum(i * (tm // SUBLANES) - 1, 0), 0)),
                  c((1, W)), c((1, G)), c((1, G)), c((128, G)), c((128, G)), c((128, G)),
                  c((1, G)), c((1, G)), c((1, G)), c((G, G))],
        out_specs=[pl.BlockSpec((tm, G), row)] * 8,
        out_shape=[jax.ShapeDtypeStruct((N, G), dt) for dt in (BF16, F32, BF16, BF16, BF16, BF16, BF16, BF16)],
        compiler_params=_params(("parallel",)),
        name="rwkv_prep",
    )(z2, z2, mu, w0, a0, w2p, a2p, g2p, kk, ka, rk, hm)


def _rwkv_scan_kernel(T, nkk_ref, w_ref, b_ref, k_ref, r_ref, v_ref, y_ref, s_ref, rep_ref):
    @pl.when(pl.program_id(0) == 0)
    def _():
        s_ref[...] = jnp.zeros_like(s_ref)

    ch = v_ref.shape[1]
    groups = HEAD_DIM // ch
    width = LANES // groups
    tree = lambda parts: _tree(parts, jnp.add)

    grp = lax.broadcasted_iota(I32, (T * ch, LANES), 1) // width
    for o, ref in enumerate((nkk_ref, w_ref, b_ref, k_ref, r_ref)):
        x = ref[...].astype(F32).reshape(T * ch, LANES)
        turned = [x] + [pltpu.roll(x, m * width, 1) for m in range(1, groups)]
        for j in range(groups):
            y = turned[0]
            for m in range(1, groups):
                y = jnp.where(grp == (j + m) % groups, turned[m], y)
            rep_ref[o, :, j] = y.reshape(T, ch, LANES)

    keys = [(j, kh) for j in range(groups) for kh in range(ch)]
    row = lambda o, t, j, kh: rep_ref[o, t, j, kh:kh + 1, :]

    def step(t, carry):
        vt = v_ref[t].astype(F32)
        sa = tree([s_ref[j * ch + kh] * row(0, t, j, kh) for j, kh in keys])
        ys = []
        for j, kh in keys:
            s_new = (s_ref[j * ch + kh] * row(1, t, j, kh) + sa * row(2, t, j, kh) + vt * row(3, t, j, kh))
            s_ref[j * ch + kh] = s_new
            ys.append(s_new * row(4, t, j, kh))
        y_ref[t] = tree(ys).astype(BF16)
        return carry

    lax.fori_loop(0, T, step, 0)


def _rwkv_scan(nkk, w, b, k, r, v):
    S, ch, _ = v.shape
    T = 32
    spec = pl.BlockSpec((T, ch, LANES), lambda i: (i, 0, 0))
    return pl.pallas_call(
        functools.partial(_rwkv_scan_kernel, T),
        grid=(S // T,),
        in_specs=[spec] * 6,
        out_specs=spec,
        out_shape=jax.ShapeDtypeStruct((S, ch, LANES), BF16),
        scratch_shapes=[pltpu.VMEM((HEAD_DIM, ch, LANES), F32),
                        pltpu.VMEM((5, T, HEAD_DIM // ch, ch, LANES), F32)],
        compiler_params=_params(("arbitrary",)),
        name="rwkv_scan",
    )(nkk, w, b, k, r, v)


def _to_v_layout(x, B, S):
    P = B * N_HEADS_G
    rep = LANES // P
    t = x.reshape(B, S, N_HEADS_G, HEAD_DIM // rep, rep).transpose(1, 3, 4, 0, 2)
    return t.reshape(S, HEAD_DIM // rep, LANES)


def _from_v_layout(y, B, S):
    P = B * N_HEADS_G
    rep = LANES // P
    t = y.reshape(S, HEAD_DIM // rep, rep, B, N_HEADS_G).transpose(3, 0, 4, 1, 2)
    return t.reshape(B * S, GROUP_W)


def _dsa_prep_kernel(f_ref, qn_ref, wq_ref, wqi_ref, cq_ref, sq_ref, ci_ref, si_ref,
                     qt_o, qit_o, wt_o, k_o, v_o, ki_o):
    f = f_ref[...]
    cq = _rms(f[:, 0:DSA_Q_LORA], qn_ref[...]).astype(BF16)
    cos_q, sin_q = cq_ref[...], sq_ref[...]
    cos_i, sin_i = ci_ref[...], si_ref[...]
    q = _rope(jnp.dot(cq, wq_ref[...], preferred_element_type=F32), cos_q, sin_q, HEAD_DIM, HEAD_DIM // 8)
    tm = f.shape[0]
    qt_o[...] = (q * HEAD_DIM ** -0.5).T.reshape(N_HEADS_G, HEAD_DIM, tm).astype(BF16)
    qi = _rope(jnp.dot(cq, wqi_ref[...], preferred_element_type=F32), cos_i, sin_i, IDX_DIM, IDX_DIM // 8)
    qit_o[...] = qi.T.reshape(IDX_HEADS, IDX_DIM, tm).astype(BF16)
    kv = f[:, 128:256]
    kv_r = _rope(kv, cos_q[:, 0:128], sin_q[:, 0:128], HEAD_DIM, HEAD_DIM // 8)
    k_o[...] = kv_r[:, 0:HEAD_DIM].astype(BF16)
    v_o[...] = kv[:, HEAD_DIM:128].astype(BF16)
    tail = f[:, 256:384]
    tail_r = _rope(tail, cos_i[:, 0:128], sin_i[:, 0:128], IDX_DIM, IDX_DIM // 8)
    ki_o[...] = tail_r[:, 0:IDX_DIM].astype(BF16)
    wt_o[...] = tail.T[IDX_DIM:IDX_DIM + IDX_HEADS, :] * (IDX_HEADS ** -0.5 * IDX_DIM ** -0.5)


def _dsa_prep(f2, qn, wq, wqi, tq, ti):
    N, W = f2.shape
    tm = ROW_TILE
    G = GROUP_W
    row = lambda w: pl.BlockSpec((tm, w), lambda i: (i, 0))
    col = lambda h: pl.BlockSpec((h, tm), lambda i: (0, i))
    slab = lambda h, d: pl.BlockSpec((h, d, tm), lambda i: (0, 0, i))
    c = lambda shape: pl.BlockSpec(shape, lambda i: (0, 0))
    return pl.pallas_call(
        _dsa_prep_kernel,
        grid=(N // tm,),
        in_specs=[row(W), c((1, DSA_Q_LORA)), c((DSA_Q_LORA, G)), c((DSA_Q_LORA, G))] + [row(G)] * 4,
        out_specs=[slab(N_HEADS_G, HEAD_DIM), slab(IDX_HEADS, IDX_DIM), col(IDX_HEADS),
                   row(HEAD_DIM), row(HEAD_DIM), row(IDX_DIM)],
        out_shape=[jax.ShapeDtypeStruct((N_HEADS_G, HEAD_DIM, N), BF16),
                   jax.ShapeDtypeStruct((IDX_HEADS, IDX_DIM, N), BF16),
                   jax.ShapeDtypeStruct((IDX_HEADS, N), F32),
                   jax.ShapeDtypeStruct((N, HEAD_DIM), BF16), jax.ShapeDtypeStruct((N, HEAD_DIM), BF16),
                   jax.ShapeDtypeStruct((N, IDX_DIM), BF16)],
        compiler_params=_params(("parallel",)),
        name="dsa_prep",
    )(f2, qn, wq, wqi, *tq, *ti)


def _float_key(bits):
    return bits ^ ((bits >> 31) & 0x7FFFFFFF)


def _dsa_kernel(TK, topk, qt_ref, qit_ref, wt_ref, k_ref, v_ref, ki_ref, on_ref, o_ref,
                skey_ref, lg_ref, out_ref):
    i = pl.program_id(1)
    TQ = Q_BLOCK
    n_kt = (i * TQ + TQ + TK - 1) // TK
    q_all = jnp.concatenate([qt_ref[h] for h in range(N_HEADS_G)], axis=1)
    qi_all = jnp.concatenate([qit_ref[h] for h in range(IDX_HEADS)], axis=1)
    wt = wt_ref[...]
    kpos = lax.broadcasted_iota(I32, (TK, TQ), 0)
    qpos = i * TQ + lax.broadcasted_iota(I32, (TK, TQ), 1)
    head = lambda x, h: x[:, h * TQ:(h + 1) * TQ]

    def fold(x, op):
        return _tree([x[r * SUBLANES:(r + 1) * SUBLANES] for r in range(TK // SUBLANES)], op)

    def score_tile(kt, tops):
        off = pl.multiple_of(kt * TK, TK)
        rel = jnp.dot(ki_ref[0, pl.ds(off, TK), :], qi_all, preferred_element_type=F32)
        sc = jnp.zeros((TK, TQ), F32)
        for h in range(IDX_HEADS):
            sc = sc + jnp.maximum(head(rel, h), 0.0) * wt[h:h + 1, :]
        key = _float_key(pltpu.bitcast(sc, I32))
        key = jnp.where(sc == 0.0, -(kt * TK + kpos), jnp.where(key < 0, key - skey_ref.shape[0] * TK, key))
        valid = kt * TK + kpos <= qpos
        skey_ref[kt] = jnp.where(valid, key, INT_MIN)
        lg = jnp.dot(k_ref[0, pl.ds(off, TK), :], q_all, preferred_element_type=F32)
        lg_ref[kt] = lg
        return tuple(jnp.maximum(tops[h], fold(jnp.where(valid, head(lg, h), -1e30), jnp.maximum))
                     for h in range(N_HEADS_G))

    lowest = tuple(jnp.full((SUBLANES, TQ), -1e30, F32) for _ in range(N_HEADS_G))
    tops = lax.fori_loop(0, n_kt, score_tile, lowest)

    def count(pred):
        def count_tile(kt, acc):
            return acc + fold(jnp.where(pred(kt, skey_ref[kt]), 1.0, 0.0), jnp.add)

        acc = lax.fori_loop(0, n_kt, count_tile, jnp.zeros((SUBLANES, TQ), F32))
        return jnp.sum(acc, axis=0, keepdims=True)

    def bit_step(j, carry):
        thr, n_thr = carry
        cand = thr + lax.shift_left(jnp.int32(1), 31 - j)
        n = count(lambda kt, keys: keys >= cand)
        return jnp.where(n >= topk, cand, thr), jnp.where(n >= topk, n, n_thr)

    thr, n_thr = lax.fori_loop(0, 32, bit_step,
                               (jnp.full((1, TQ), INT_MIN, I32), jnp.full((1, TQ), float(topk), F32)))
    thr = jnp.maximum(thr, INT_MIN + 1)

    @pl.when(jnp.max(n_thr) > topk)
    def _():
        keep = topk - count(lambda kt, keys: keys > thr)

        pos_bits = (skey_ref.shape[0] * TK - 1).bit_length()

        def pos_step(j, last):
            cand = last + lax.shift_left(jnp.int32(1), pos_bits - 1 - j)
            n = count(lambda kt, keys: jnp.logical_and(keys == thr, kt * TK + kpos < cand))
            return jnp.where(n < keep, cand, last)

        last = lax.fori_loop(0, pos_bits, pos_step, jnp.zeros((1, TQ), I32))

        def retire(kt, carry):
            keys = skey_ref[kt]
            skey_ref[kt] = jnp.where(jnp.logical_and(keys == thr, kt * TK + kpos > last), INT_MIN, keys)
            return carry

        lax.fori_loop(0, n_kt, retire, 0)

    def sweep(shifts):
        def acc_tile(kt, carry):
            ls, acc = carry
            sel = skey_ref[kt] >= thr
            ps = [jnp.where(sel, jnp.exp(lg_ref[kt, :, h * TQ:(h + 1) * TQ] - shifts[h]), 0.0)
                  for h in range(N_HEADS_G)]
            p_all = jnp.concatenate([p.astype(BF16) for p in ps], axis=1)
            vb = v_ref[0, pl.ds(pl.multiple_of(kt * TK, TK), TK), :]
            return (tuple(ls[h] + fold(ps[h], jnp.add) for h in range(N_HEADS_G)),
                    acc + lax.dot_general(vb, p_all, TN_DIMS, preferred_element_type=F32))

        ls, acc = lax.fori_loop(0, n_kt, acc_tile,
                                (tuple(jnp.zeros((SUBLANES, TQ), F32) for _ in range(N_HEADS_G)),
                                 jnp.zeros((HEAD_DIM, N_HEADS_G * TQ), F32)))
        ls = [jnp.sum(l, axis=0, keepdims=True) for l in ls]
        for h in range(N_HEADS_G):
            out_ref[h * HEAD_DIM:(h + 1) * HEAD_DIM, :] = head(acc, h) / ls[h]
        return jnp.min(jnp.minimum(jnp.minimum(ls[0], ls[1]), jnp.minimum(ls[2], ls[3])))

    smallest = sweep([jnp.max(t, axis=0, keepdims=True) for t in tops])

    @pl.when(smallest < SOFTMAX_FLOOR)
    def _():
        def max_tile(kt, ms):
            sel = skey_ref[kt] >= thr
            return tuple(jnp.maximum(ms[h], fold(jnp.where(sel, lg_ref[kt, :, h * TQ:(h + 1) * TQ], -1e30),
                                                 jnp.maximum))
                         for h in range(N_HEADS_G))

        ms = lax.fori_loop(0, n_kt, max_tile, lowest)
        sweep([jnp.max(m, axis=0, keepdims=True) for m in ms])

    o_ref[0] = _rms(out_ref[...].T, on_ref[...]).astype(BF16)


def _dsa_attention(qt, qit, wt, k, v, ki, on):
    B, S, _ = k.shape
    G = GROUP_W
    nq = S // Q_BLOCK
    TK = min(512, S)
    topk = min(DSA_TOPK_MAX, S // 4)
    qcol = lambda h: pl.BlockSpec((h, Q_BLOCK), lambda b, i: (0, b * nq + i))
    qslab = lambda h, d: pl.BlockSpec((h, d, Q_BLOCK), lambda b, i: (0, 0, b * nq + i))
    full = lambda wd: pl.BlockSpec((1, S, wd), lambda b, i: (b, 0, 0))
    return pl.pallas_call(
        functools.partial(_dsa_kernel, TK, topk),
        grid=(B, nq),
        in_specs=[qslab(N_HEADS_G, HEAD_DIM), qslab(IDX_HEADS, IDX_DIM), qcol(IDX_HEADS),
                  full(HEAD_DIM), full(HEAD_DIM), full(IDX_DIM), pl.BlockSpec((1, G), lambda b, i: (0, 0))],
        out_specs=pl.BlockSpec((1, Q_BLOCK, G), lambda b, i: (b, i, 0)),
        out_shape=jax.ShapeDtypeStruct((B, S, G), BF16),
        scratch_shapes=[pltpu.VMEM((S // TK, TK, Q_BLOCK), I32),
                        pltpu.VMEM((S // TK, TK, N_HEADS_G * Q_BLOCK), F32),
                        pltpu.VMEM((G, Q_BLOCK), F32)],
        compiler_params=_params(("parallel", "arbitrary")),
        name="dsa_attention",
    )(qt, qit, wt, k, v, ki, on)


def _sb_kernel(q_ref, k_ref, v_ref, u_ref, on_ref, o_ref, z_ref, lm_ref):
    i = pl.program_id(1)
    T = Q_BLOCK
    H, G = N_HEADS_G, GROUP_W
    u = u_ref[...]
    qt = q_ref[0].astype(F32).T.astype(BF16)
    q_rows = lax.broadcasted_iota(I32, (G, H * T), 0) // HEAD_DIM
    q_cols = lax.broadcasted_iota(I32, (G, H * T), 1) // T
    qbd = jnp.where(q_rows == q_cols, jnp.concatenate([qt] * H, axis=1), jnp.zeros((), BF16))
    kid = lax.broadcasted_iota(I32, (T, H * T), 0)
    qid = lax.broadcasted_iota(I32, (T, H * T), 1) & (T - 1)
    strict = kid < qid

    def cond(carry):
        kt, c, _ = carry
        return jnp.logical_and(kt >= 0, jnp.max(c) > SB_UNDERFLOW)

    def logits(kt):
        z = jnp.dot(k_ref[0, pl.ds(pl.multiple_of(kt * T, T), T), :], qbd, preferred_element_type=F32)
        z = z * HEAD_DIM ** -0.5
        return z, -_softplus(z)

    z0, lm0 = logits(i)
    z_ref[...] = z0
    lm_ref[...] = jnp.where(strict, lm0, 0.0)

    def body(carry):
        kt, c, acc = carry
        off = pl.multiple_of(kt * T, T)
        mask = jnp.logical_or(strict, kt < i)
        z, lm = z_ref[...], lm_ref[...]
        z_ref[...], lm_ref[...] = logits(jnp.maximum(kt - 1, 0))
        hi, lo = _split(lm)
        later = jnp.dot(u, jnp.concatenate([hi, lo], axis=1), preferred_element_type=F32)
        later = later[:, :H * T] + later[:, H * T:]
        a = jnp.where(mask, jnp.exp(z + lm + (c + later)), 0.0)
        av = lax.dot_general(v_ref[0, pl.ds(off, T), :], a.astype(BF16), TN_DIMS, preferred_element_type=F32)
        acc = acc + jnp.concatenate(
            [av[h * HEAD_DIM:(h + 1) * HEAD_DIM, h * T:(h + 1) * T] for h in range(H)], axis=0)
        return kt - 1, c + jnp.sum(lm, axis=0, keepdims=True), acc

    _, _, acc = lax.while_loop(cond, body, (i, jnp.zeros((1, H * T), F32), jnp.zeros((G, T), F32)))
    o_ref[0] = _rms(acc.T, on_ref[...]).astype(BF16)


def _stick_breaking(sb3, u, on):
    B, S, _ = sb3.shape
    G = GROUP_W
    return pl.pallas_call(
        _sb_kernel,
        grid=(B, S // Q_BLOCK),
        in_specs=[pl.BlockSpec((1, Q_BLOCK, G), lambda b, i: (b, i, 0)),
                  pl.BlockSpec((1, S, G), lambda b, i: (b, 0, 1)),
                  pl.BlockSpec((1, S, G), lambda b, i: (b, 0, 2)),
                  pl.BlockSpec((Q_BLOCK, Q_BLOCK), lambda b, i: (0, 0)),
                  pl.BlockSpec((1, G), lambda b, i: (0, 0))],
        out_specs=pl.BlockSpec((1, Q_BLOCK, G), lambda b, i: (b, i, 0)),
        out_shape=jax.ShapeDtypeStruct((B, S, G), BF16),
        scratch_shapes=[pltpu.VMEM((Q_BLOCK, N_HEADS_G * Q_BLOCK), F32)] * 2,
        compiler_params=_params(("parallel", "arbitrary")),
        name="stick_breaking",
    )(sb3, sb3, sb3, u, on)


def _outproj_kernel(yr_ref, yd_ref, ys_ref, yw_ref, bonus_ref, gate_ref, ln_ref, hm_ref, x_ref, g1_ref, sc_ref,
                    sh_ref, gn_ref, wo_ref, rw_ref, rb_ref, x_o, h_o, lg_o):
    G = GROUP_W
    y = yw_ref[...].astype(F32)
    hm = hm_ref[...]
    yc = y - _split_dot(y, hm) * (1.0 / HEAD_DIM)
    var = _split_dot(yc * yc, hm) * (1.0 / HEAD_DIM)
    yw = ((yc * lax.rsqrt(var + RWKV_LN_EPS) * ln_ref[...] + bonus_ref[...].astype(F32))
          * gate_ref[...].astype(F32)).astype(BF16)
    dot = lambda yv, g: jnp.dot(yv, wo_ref[g * G:(g + 1) * G, :], preferred_element_type=F32)
    mixed = dot(yr_ref[...], 0) + dot(yw, 1) + dot(yd_ref[...], 2) + dot(ys_ref[...], 3)
    x1 = x_ref[...] + g1_ref[0] * mixed
    x_o[...] = x1
    h = _rms(x1, gn_ref[...]) * (1.0 + sc_ref[0]) + sh_ref[0]
    h_o[...] = h
    hi, lo = _split(h)
    work = (jnp.dot(hi, rw_ref[0], preferred_element_type=F32) + jnp.dot(lo, rw_ref[0], preferred_element_type=F32)
            + jnp.dot(hi, rw_ref[1], preferred_element_type=F32) + rb_ref[...])
    col = lax.broadcasted_iota(I32, work.shape, 1).astype(F32)
    out = jnp.zeros_like(work)
    vals = []
    for j in range(TOP_K):
        m = jnp.max(work, axis=-1, keepdims=True)
        idx = jnp.min(jnp.where(work == m, col, float(LANES)), axis=-1, keepdims=True)
        out = jnp.where(col == float(j), idx, out)
        vals.append(m)
        work = jnp.where(col == idx, -jnp.inf, work)
    es = [jnp.exp(v - vals[0]) for v in vals]
    den = es[0] + es[1] + es[2] + es[3]
    for j in range(TOP_K):
        out = jnp.where(col == float(TOP_K + j), es[j] / den, out)
    lg_o[...] = out


def _outproj(ys, rwkv, x2, mod6, gain, w_out, router_w, router_b, S):
    N, D = x2.shape
    tm = ROW_TILE
    per_b = S // tm
    G = GROUP_W
    row = lambda w: pl.BlockSpec((tm, w), lambda i: (i, 0))
    c = lambda shape: pl.BlockSpec(shape, lambda i: (0, 0))
    return pl.pallas_call(
        _outproj_kernel,
        grid=(N // tm,),
        in_specs=[row(G)] * 6 + [c((1, G)), c((G, G)),
                                 row(D), _mod_row(per_b, 2), _mod_row(per_b, 4), _mod_row(per_b, 3),
                                 c((1, D)), c((D, D)), pl.BlockSpec((2, D, LANES), lambda i: (0, 0, 0)),
                                 c((1, LANES))],
        out_specs=[row(D), row(D), row(LANES)],
        out_shape=[jax.ShapeDtypeStruct((N, D), F32), jax.ShapeDtypeStruct((N, D), F32),
                   jax.ShapeDtypeStruct((N, LANES), F32)],
        compiler_params=_params(("parallel",)),
        name="out_proj_router",
    )(*ys, *rwkv, x2, mod6, mod6, mod6, gain, w_out,
      jnp.pad(jnp.stack(_split(router_w)), ((0, 0), (0, 0), (0, LANES - N_EXPERTS))),
      jnp.pad(router_b, ((0, 0), (0, LANES - N_EXPERTS)), constant_values=-1e30))


def _moe_kernel(be_ref, nu_ref, x_ref, w1_ref, b1_ref, w2_ref, b2_ref, o_ref):
    i = pl.program_id(0)

    @pl.when(i < nu_ref[0])
    def _():
        xb = x_ref[...].astype(BF16)
        w1 = w1_ref.at[0, 0]
        glu = jnp.dot(xb, w1[:, 0:D_FF].astype(BF16), preferred_element_type=F32) + b1_ref[0, :, 0:D_FF]
        lin = jnp.dot(xb, w1[:, D_FF:].astype(BF16), preferred_element_type=F32) + b1_ref[0, :, D_FF:]
        glu = jnp.minimum(glu, SWIGLU_LIMIT)
        lin = jnp.clip(lin, -SWIGLU_LIMIT, SWIGLU_LIMIT)
        act = glu * _sigmoid(SWIGLU_ALPHA * glu) * (lin + 1.0)
        y = jnp.dot(act.astype(BF16), w2_ref[0, 0].astype(BF16), preferred_element_type=F32) + b2_ref[0]
        o_ref[...] = y.astype(BF16)

    @pl.when(i >= nu_ref[0])
    def _():
        o_ref[...] = jnp.zeros_like(o_ref)


def _moe_ffn(block_expert, n_used, xs, layer, w1, b1, w2, b2):
    rows, D = xs.shape
    nb = rows // MOE_BLOCK
    L, E = w1.shape[:2]
    live = lambda i, nu: jnp.minimum(i, nu[0] - 1)
    grid_spec = pltpu.PrefetchScalarGridSpec(
        num_scalar_prefetch=2,
        grid=(nb,),
        in_specs=[pl.BlockSpec((MOE_BLOCK, D), lambda i, be, nu: (live(i, nu), 0)),
                  pl.BlockSpec((1, 1, D, 2 * D_FF), lambda i, be, nu: (layer, be[i], 0, 0)),
                  pl.BlockSpec((1, 1, 2 * D_FF), lambda i, be, nu: (layer * E + be[i], 0, 0)),
                  pl.BlockSpec((1, 1, D_FF, D), lambda i, be, nu: (layer, be[i], 0, 0)),
                  pl.BlockSpec((1, 1, D), lambda i, be, nu: (layer * E + be[i], 0, 0))],
        out_specs=pl.BlockSpec((MOE_BLOCK, D), lambda i, be, nu: (i, 0)),
    )
    return pl.pallas_call(
        _moe_kernel,
        grid_spec=grid_spec,
        out_shape=jax.ShapeDtypeStruct((rows, D), BF16),
        compiler_params=_params(("arbitrary",), 56 * 1024 * 1024),
        name="moe_ffn",
    )(block_expert, n_used, xs, w1, b1.reshape(L * E, 1, -1), w2, b2.reshape(L * E, 1, -1))


def _route(routed):
    n = routed.shape[0]
    expert = routed[:, 0:TOP_K].reshape(-1).astype(I32)
    n_assign = n * TOP_K
    ids = jnp.arange(n_assign, dtype=I32)
    _, order = lax.sort((expert, ids), num_keys=1, is_stable=True)
    _, inv = lax.sort((order, ids), num_keys=1)
    counts = jnp.sum((expert[:, None] == jnp.arange(N_EXPERTS, dtype=I32)[None, :]).astype(I32), axis=0)
    starts = jnp.cumsum(counts) - counts
    padded = (counts + MOE_BLOCK - 1) // MOE_BLOCK * MOE_BLOCK
    pad_end = jnp.cumsum(padded)
    pad_start = pad_end - padded
    dest = pad_start[expert] + inv - starts[expert]
    n_blocks = -(-n_assign // MOE_BLOCK) + N_EXPERTS
    block_start = jnp.arange(n_blocks, dtype=I32) * MOE_BLOCK
    block_expert = jnp.minimum(jnp.sum(pad_end[None, :] <= block_start[:, None], axis=1), N_EXPERTS - 1)
    block_expert = block_expert.astype(I32)
    per_row = lambda t: jnp.repeat(t[block_expert], MOE_BLOCK)
    off = jnp.arange(n_blocks * MOE_BLOCK, dtype=I32) - per_row(pad_start)
    valid = off < per_row(counts)
    src = order[jnp.clip(per_row(starts) + off, 0, n_assign - 1)]
    row_token = jnp.where(valid, src // TOP_K, 0)
    n_used = (pad_end[-1] // MOE_BLOCK).astype(I32).reshape(1)
    return dest.reshape(n, TOP_K), row_token, block_expert, n_used


def _rows(t, idx):
    return t.at[idx].get(mode="promise_in_bounds")


def _final_kernel(x_ref, g2_ref, routed_ref, y0, y1, y2, y3, g_ref, o_ref):
    o_ref[...] = _rms(_combine(x_ref, g2_ref, routed_ref, (y0, y1, y2, y3)), g_ref[...])


def _final_norm(x2, mod6, routed, ys, g, S):
    N, D = x2.shape
    tm = ROW_TILE
    row = pl.BlockSpec((tm, D), lambda i: (i, 0))
    return pl.pallas_call(
        _final_kernel, grid=(N // tm,),
        in_specs=[row, _mod_row(S // tm, 5), pl.BlockSpec((tm, LANES), lambda i: (i, 0))] + [row] * 4
        + [pl.BlockSpec((1, D), lambda i: (0, 0))],
        out_specs=row,
        out_shape=jax.ShapeDtypeStruct((N, D), F32),
        compiler_params=_params(("parallel",)), name="final_norm",
    )(x2, mod6, routed, *ys, g)


def _rope_tables(pos, group, rot_dim, theta, width):
    half = rot_dim // 2
    inv = theta ** (-jnp.arange(half, dtype=F32) / half)
    ang = pos.astype(F32)[..., None] * inv
    cos, sin = jnp.cos(ang), jnp.sin(ang)
    rest = group - rot_dim
    ones = jnp.ones(ang.shape[:2] + (rest,), F32)
    zeros = jnp.zeros(ang.shape[:2] + (rest,), F32)
    rep = width // group
    tile = lambda t: jnp.tile(t, (1, 1, rep)).reshape(-1, width)
    return (tile(jnp.concatenate([cos, cos, ones], -1)), tile(jnp.concatenate([-sin, sin, zeros], -1)))


def _pad_rows(w, start, total=128):
    return jnp.zeros((total, w.shape[1]), w.dtype).at[start:start + w.shape[0]].set(w)


def kernel(x, c, positions, ada_w, ada_b, norm_mix, norm_ffn, w_in, ret_gn, rwkv_mu, rwkv_w0, rwkv_w2, rwkv_a0,
           rwkv_a2, rwkv_g2, rwkv_kk, rwkv_ka, rwkv_rk, rwkv_ln, dsa_qnorm, dsa_wq_up, dsa_wqi_up, dsa_onorm,
           sb_onorm, w_out, router_w, router_b, moe_w1, moe_b1, moe_w2, moe_b2, norm_final):
    B, S, D = x.shape
    N = B * S
    L = ada_w.shape[0]
    G = GROUP_W
    row2 = lambda t: t.reshape(1, -1)

    mod = _ada(c, ada_w, ada_b)
    ret_tabs = tuple(t.reshape(B, S, G) for t in _rope_tables(positions, HEAD_DIM, HEAD_DIM, RET_THETA, G))
    dq_tabs = _rope_tables(positions, HEAD_DIM, HEAD_DIM // 4, ROPE_THETA, G)
    di_tabs = _rope_tables(positions, IDX_DIM, IDX_DIM // 4, ROPE_THETA, G)
    ret_c = _ret_consts(min(256, S))
    hm = (jnp.arange(G)[:, None] // HEAD_DIM == jnp.arange(G)[None, :] // HEAD_DIM).astype(BF16)
    u_later = (jnp.arange(Q_BLOCK)[None, :] > jnp.arange(Q_BLOCK)[:, None]).astype(BF16)
    o1, o2 = RET_COLS + RWKV_COLS, RET_COLS + RWKV_COLS + DSA_COLS

    x2 = x.reshape(N, D)
    comb = None
    for l in range(L):
        mod6 = mod[l].reshape(B * 6, 1, D)
        w_l = w_in[l]
        w_cat = jnp.concatenate(
            [w_l[:, :o1], w_l[:, o1:o2], jnp.zeros((D, DSA_PAD - DSA_COLS), F32), w_l[:, o2:]], axis=1).astype(BF16)
        outs = _inproj(x2, row2(norm_mix[l]), mod6, w_cat, S, comb)
        if comb is not None:
            x2, outs = outs[0], outs[1:]
        ret, rwkv, dsa, sb = outs

        prep = _rwkv_prep(rwkv, S, row2(rwkv_mu[l]), row2(rwkv_w0[l]), row2(rwkv_a0[l]),
                          _pad_rows(rwkv_w2[l], 0).astype(BF16), _pad_rows(rwkv_a2[l], 32).astype(BF16),
                          _pad_rows(rwkv_g2[l], 64).astype(BF16), row2(rwkv_kk[l]), row2(rwkv_ka[l]),
                          row2(rwkv_rk[l]), hm)
        r_, w_, k_, v_, nkk_, b_, gate_, bonus_ = prep
        scan_in = tuple(_to_v_layout(t, B, S) for t in (nkk_, w_, b_, k_, r_, v_))

        y_ret = _retention(ret.reshape(B, S, RET_COLS), ret_tabs, ret_c, row2(ret_gn[l])).reshape(N, G)

        dqt, dqit, dwt, dk, dv, dki = _dsa_prep(dsa, row2(dsa_qnorm[l]), dsa_wq_up[l].astype(BF16),
                                                dsa_wqi_up[l].astype(BF16), dq_tabs, di_tabs)
        y_dsa = _dsa_attention(dqt, dqit, dwt, dk.reshape(B, S, HEAD_DIM), dv.reshape(B, S, HEAD_DIM),
                               dki.reshape(B, S, IDX_DIM), row2(dsa_onorm[l])).reshape(N, G)

        y_sb = _stick_breaking(sb.reshape(B, S, SB_COLS), u_later, row2(sb_onorm[l])).reshape(N, G)

        y_scan = _rwkv_scan(*scan_in)
        rwkv_tail = (_from_v_layout(y_scan, B, S), bonus_, gate_, row2(rwkv_ln[l]), hm)

        x1, h2, routed = _outproj((y_ret, y_dsa, y_sb), rwkv_tail, x2, mod6, row2(norm_ffn[l]),
                                  w_out[l].astype(BF16), router_w[l], row2(router_b[l]), S)

        dest, row_token, block_expert, n_used = _route(routed)
        y = _moe_ffn(block_expert, n_used, _rows(h2, row_token), l, moe_w1, moe_b1, moe_w2, moe_b2)
        x2 = x1
        comb = (mod6, routed, tuple(_rows(y, dest[:, j]) for j in range(TOP_K)))
    return _final_norm(x2, *comb, row2(norm_final), S).reshape(B, S, D)
```

```python
import functools

import jax
import jax.numpy as jnp
from jax import lax
from jax.experimental import pallas as pl
from jax.experimental.pallas import tpu as pltpu

F32 = jnp.float32
BF16 = jnp.bfloat16
I32 = jnp.int32

D_MODEL = 1024
GROUP_W = 256
HEAD_DIM = 64
N_HEADS_G = 4
NORM_EPS = 1e-5
Q_BLOCK = 128
RET_THETA = 10000.0
RWKV_LN_EPS = 64e-5
DSA_Q_LORA = 128
IDX_HEADS = 8
IDX_DIM = 32
DSA_TOPK_MAX = 256
ROPE_THETA = 500000.0
N_EXPERTS = 32
TOP_K = 4
D_FF = D_MODEL
SWIGLU_ALPHA = 1.702
SWIGLU_LIMIT = 7.0
MOE_BLOCK = 512

RET_COLS = 4 * GROUP_W
RWKV_COLS = 3 * GROUP_W + 128
DSA_COLS = 296
DSA_PAD = 384
SB_COLS = 3 * GROUP_W

LANES = 128
SUBLANES = 8
ROW_TILE = 512
VMEM_LIMIT = 48 * 1024 * 1024
INT_MIN = -2 ** 31
SB_UNDERFLOW = -104.0
SOFTMAX_FLOOR = 1e-20

HIGHEST = lax.Precision.HIGHEST
NT_DIMS = (((1,), (1,)), ((), ()))
TN_DIMS = (((0,), (0,)), ((), ()))


def _params(sem, vmem=VMEM_LIMIT):
    return pltpu.CompilerParams(dimension_semantics=sem, vmem_limit_bytes=vmem)


def _bdot(a, b):
    return jnp.dot(a.astype(BF16), b.astype(BF16), preferred_element_type=F32)


def _split(x):
    hi = x.astype(BF16)
    return hi, (x - hi.astype(F32)).astype(BF16)


def _split_dot(x, m):
    hi, lo = _split(x)
    return jnp.dot(hi, m, preferred_element_type=F32) + jnp.dot(lo, m, preferred_element_type=F32)


def _tree(parts, op):
    while len(parts) > 1:
        parts = [op(parts[i], parts[i + 1]) for i in range(0, len(parts), 2)]
    return parts[0]


def _sigmoid(x):
    return 1.0 / (1.0 + jnp.exp(-x))


def _softplus(x):
    return jnp.maximum(x, 0.0) + jnp.log(1.0 + jnp.exp(-jnp.abs(x)))


def _rms(x, g, eps=NORM_EPS):
    return x * lax.rsqrt(jnp.mean(x * x, axis=-1, keepdims=True) + eps) * g


def _rope(x, cos, sin, group, half):
    w = x.shape[-1]
    first = (lax.broadcasted_iota(I32, (1, w), 1) & (group - 1)) < half
    partner = jnp.where(first, pltpu.roll(x, w - half, 1), pltpu.roll(x, half, 1))
    return x * cos + partner * sin


def _ada_kernel(c_ref, w_ref, b_ref, o_ref):
    c = c_ref[...]
    cond = c * _sigmoid(c)
    o_ref[0] = jnp.dot(cond, w_ref[0], preferred_element_type=F32, precision=HIGHEST) + b_ref[0]


def _ada(c, ada_w, ada_b):
    L, D, W = ada_w.shape
    B = c.shape[0]
    tn = 1024
    return pl.pallas_call(
        _ada_kernel,
        grid=(L, W // tn),
        in_specs=[pl.BlockSpec((B, D), lambda l, j: (0, 0)),
                  pl.BlockSpec((1, D, tn), lambda l, j: (l, 0, j)),
                  pl.BlockSpec((1, 1, tn), lambda l, j: (l, 0, j))],
        out_specs=pl.BlockSpec((1, B, tn), lambda l, j: (l, 0, j)),
        out_shape=jax.ShapeDtypeStruct((L, B, W), F32),
        compiler_params=_params(("parallel", "parallel")),
        name="ada_mod",
    )(c, ada_w, ada_b.reshape(L, 1, W))


def _combine(x_ref, g2_ref, routed_ref, y_refs):
    gates = routed_ref[...]
    moe = gates[:, TOP_K:TOP_K + 1] * y_refs[0][...].astype(F32)
    for j in range(1, len(y_refs)):
        moe = moe + gates[:, TOP_K + j:TOP_K + j + 1] * y_refs[j][...].astype(F32)
    return x_ref[...] + g2_ref[0] * moe


def _inproj_kernel(n_comb, x_ref, *refs):
    if n_comb:
        g2_ref, routed_ref, y_refs, refs = refs[0], refs[1], refs[2:2 + n_comb], refs[2 + n_comb:]
        g_ref, sc_ref, sh_ref, w_ref, x_o, ret_ref, rwkv_ref, dsa_ref, sb_ref = refs
        x = _combine(x_ref, g2_ref, routed_ref, y_refs)
        x_o[...] = x
    else:
        g_ref, sc_ref, sh_ref, w_ref, ret_ref, rwkv_ref, dsa_ref, sb_ref = refs
        x = x_ref[...]
    h = _rms(x, g_ref[...]) * (1.0 + sc_ref[0]) + sh_ref[0]
    hb = h.astype(BF16)
    o0, o1, o2 = RET_COLS, RET_COLS + RWKV_COLS, RET_COLS + RWKV_COLS + DSA_PAD
    ret_ref[...] = jnp.dot(hb, w_ref[:, 0:o0], preferred_element_type=F32)
    rwkv_ref[...] = jnp.dot(hb, w_ref[:, o0:o1], preferred_element_type=F32)
    dsa_ref[...] = jnp.dot(hb, w_ref[:, o1:o2], preferred_element_type=F32)
    sb_ref[...] = jnp.dot(hb, w_ref[:, o2:], preferred_element_type=F32).astype(BF16)


def _mod_row(per_b, j):
    return pl.BlockSpec((1, 1, D_MODEL), lambda i: ((i // per_b) * 6 + j, 0, 0))


def _inproj(x2, gain, mod6, w_cat, S, comb=None):
    N, D = x2.shape
    tm = ROW_TILE
    per_b = S // tm
    wt = w_cat.shape[1]
    row = lambda i: (i, 0)
    in_specs = [pl.BlockSpec((tm, D), row)]
    args = [x2]
    out_specs, out_shape = [], []
    if comb is not None:
        in_specs += ([_mod_row(per_b, 5), pl.BlockSpec((tm, LANES), row)]
                     + [pl.BlockSpec((tm, D), row)] * len(comb[2]))
        args += [comb[0], comb[1]] + list(comb[2])
        out_specs.append(pl.BlockSpec((tm, D), row))
        out_shape.append(jax.ShapeDtypeStruct((N, D), F32))
    in_specs += [pl.BlockSpec((1, D), lambda i: (0, 0)), _mod_row(per_b, 1), _mod_row(per_b, 0),
                 pl.BlockSpec((D, wt), lambda i: (0, 0))]
    args += [gain, mod6, mod6, w_cat]
    out_specs += [pl.BlockSpec((tm, RET_COLS), row), pl.BlockSpec((tm, RWKV_COLS), row),
                  pl.BlockSpec((tm, DSA_PAD), row), pl.BlockSpec((tm, SB_COLS), row)]
    out_shape += [jax.ShapeDtypeStruct((N, RET_COLS), F32), jax.ShapeDtypeStruct((N, RWKV_COLS), F32),
                  jax.ShapeDtypeStruct((N, DSA_PAD), F32), jax.ShapeDtypeStruct((N, SB_COLS), BF16)]
    return pl.pallas_call(
        functools.partial(_inproj_kernel, 0 if comb is None else len(comb[2])),
        grid=(N // tm,),
        in_specs=in_specs, out_specs=out_specs, out_shape=out_shape,
        compiler_params=_params(("parallel",)),
        name="in_proj",
    )(*args)


def _ret_kernel(q_ref, k_ref, v_ref, g_ref, cos_ref, sin_ref, din_ref, qd_ref, kd_ref, cd_ref, gn_ref,
                o_ref, state_ref, y_ref):
    @pl.when(pl.program_id(1) == 0)
    def _():
        state_ref[...] = jnp.zeros_like(state_ref)

    cos, sin = cos_ref[0], sin_ref[0]
    q = _rope(q_ref[0], cos, sin, HEAD_DIM, HEAD_DIM // 2)
    k = _rope(k_ref[0], cos, sin, HEAD_DIM, HEAD_DIM // 2) * HEAD_DIM ** -0.5
    v = v_ref[0]
    qd = q * qd_ref[...]
    kd = k * kd_ref[...]
    for h in range(N_HEADS_G):
        sl = slice(h * HEAD_DIM, (h + 1) * HEAD_DIM)
        qh, kh, vh = q[:, sl].astype(BF16), k[:, sl].astype(BF16), v[:, sl].astype(BF16)
        s = lax.dot_general(qh, kh, NT_DIMS, preferred_element_type=F32) * din_ref[h]
        inner = jnp.dot(s.astype(BF16), vh, preferred_element_type=F32)
        st = state_ref[h]
        cross = _bdot(qd[:, sl], st)
        state_ref[h] = st * cd_ref[:, sl] + lax.dot_general(
            kd[:, sl].astype(BF16), vh, TN_DIMS, preferred_element_type=F32)
        o = inner + cross
        oc = o - jnp.mean(o, axis=-1, keepdims=True)
        y_ref[:, sl] = oc * lax.rsqrt(jnp.mean(oc * oc, axis=-1, keepdims=True) + NORM_EPS)
    g = g_ref[0]
    o_ref[0] = (g * _sigmoid(g) * (y_ref[...] * gn_ref[...])).astype(BF16)


def _retention(ret3, tabs, consts, gn):
    B, S, _ = ret3.shape
    C = consts["din"].shape[1]
    blk = lambda j: pl.BlockSpec((1, C, GROUP_W), lambda b, c, j=j: (b, c, j))
    tab = pl.BlockSpec((1, C, GROUP_W), lambda b, c: (b, c, 0))
    const2 = lambda shape: pl.BlockSpec(shape, lambda b, c: (0,) * len(shape))
    return pl.pallas_call(
        _ret_kernel,
        grid=(B, S // C),
        in_specs=[blk(0), blk(1), blk(2), blk(3), tab, tab,
                  const2((N_HEADS_G, C, C)), const2((C, GROUP_W)), const2((C, GROUP_W)),
                  const2((1, GROUP_W)), const2((1, GROUP_W))],
        out_specs=pl.BlockSpec((1, C, GROUP_W), lambda b, c: (b, c, 0)),
        out_shape=jax.ShapeDtypeStruct((B, S, GROUP_W), BF16),
        scratch_shapes=[pltpu.VMEM((N_HEADS_G, HEAD_DIM, HEAD_DIM), F32), pltpu.VMEM((C, GROUP_W), F32)],
        compiler_params=_params(("parallel", "arbitrary")),
        name="retention",
    )(ret3, ret3, ret3, ret3, tabs[0], tabs[1],
      consts["din"], consts["qd"], consts["kd"], consts["cd"], gn)


def _ret_consts(C):
    H = N_HEADS_G
    lg = jnp.log(1.0 - 2.0 ** (-5.0 - jnp.arange(H, dtype=F32)))
    idx = jnp.arange(C, dtype=F32)
    diff = idx[:, None] - idx[None, :]
    din = jnp.where(diff >= 0, jnp.exp(lg[:, None, None] * jnp.maximum(diff, 0.0)), 0.0)
    q_dec = jnp.exp(lg[:, None] * (idx + 1.0))
    k_dec = jnp.exp(lg[:, None] * (C - 1.0 - idx))
    chunk_dec = jnp.exp(lg * C)
    wide = lambda t: jnp.repeat(t.T, HEAD_DIM, axis=1)
    return {"din": din, "qd": wide(q_dec), "kd": wide(k_dec),
            "cd": jnp.repeat(chunk_dec, HEAD_DIM)[None, :]}


def _rwkv_prep_kernel(per_b, z_ref, zp_ref, mu_ref, w0_ref, a0_ref, w2_ref, a2_ref, g2_ref, kk_ref, ka_ref,
                      rk_ref, hm_ref, r_o, w_o, k_o, v_o, nkk_o, b_o, gate_o, bonus_o):
    z = z_ref[...]
    first = (pl.program_id(0) % per_b) == 0
    prow = jnp.where(first, 0.0, zp_ref[7:8, :])
    rid = lax.broadcasted_iota(I32, z.shape, 0)
    prev = jnp.where(rid == 0, prow, pltpu.roll(z, 1, 0))
    f = z + (prev - z) * mu_ref[...]
    G = GROUP_W
    r, k, v, lo = f[:, 0:G], f[:, G:2 * G], f[:, 2 * G:3 * G], f[:, 3 * G:3 * G + 128]
    w_log = -_softplus(-(w0_ref[...] + _bdot(jnp.tanh(lo), w2_ref[...]))) - 0.5
    decay = jnp.exp(-jnp.exp(w_log))
    a = _sigmoid(a0_ref[...] + _bdot(lo, a2_ref[...]))
    gate = _bdot(_sigmoid(lo), g2_ref[...])
    hm = hm_ref[...]
    kk = k * kk_ref[...]
    kk = kk / jnp.maximum(jnp.sqrt(_split_dot(kk * kk, hm)), 1e-12)
    k2 = k * (1.0 + (a - 1.0) * ka_ref[...])
    r_o[...] = r.astype(BF16)
    w_o[...] = decay
    k_o[...] = k2.astype(BF16)
    v_o[...] = v.astype(BF16)
    nkk_o[...] = (-kk).astype(BF16)
    b_o[...] = (kk * a).astype(BF16)
    gate_o[...] = gate.astype(BF16)
    bonus_o[...] = (_split_dot(r * k2 * rk_ref[...], hm) * v).astype(BF16)


def _rwkv_prep(z2, S, mu, w0, a0, w2p, a2p, g2p, kk, ka, rk, hm):
    N, W = z2.shape
    tm = ROW_TILE
    per_b = S // tm
    row = lambda i: (i, 0)
    c = lambda shape: pl.BlockSpec(shape, lambda i: (0, 0))
    G = GROUP_W
    return pl.pallas_call(
        functools.partial(_rwkv_prep_kernel, per_b),
        grid=(N // tm,),
        in_specs=[pl.BlockSpec((tm, W), row),
                  pl.BlockSpec((SUBLANES, W), lambda i: (jnp.maximum(i * (tm // SUBLANES) - 1, 0), 0)),
                  c((1, W)), c((1, G)), c((1, G)), c((128, G)), c((128, G)), c((128, G)),
                  c((1, G)), c((1, G)), c((1, G)), c((G, G))],
        out_specs=[pl.BlockSpec((tm, G), row)] * 8,
        out_shape=[jax.ShapeDtypeStruct((N, G), dt) for dt in (BF16, F32, BF16, BF16, BF16, BF16, BF16, BF16)],
        compiler_params=_params(("parallel",)),
        name="rwkv_prep",
    )(z2, z2, mu, w0, a0, w2p, a2p, g2p, kk, ka, rk, hm)


def _rwkv_scan_kernel(T, nkk_ref, w_ref, b_ref, k_ref, r_ref, v_ref, y_ref, s_ref, rep_ref):
    @pl.when(pl.program_id(0) == 0)
    def _():
        s_ref[...] = jnp.zeros_like(s_ref)

    ch = v_ref.shape[1]
    groups = HEAD_DIM // ch
    width = LANES // groups
    tree = lambda parts: _tree(parts, jnp.add)

    piece = 4
    grp = lax.broadcasted_iota(I32, (piece * ch, LANES), 1) // width
    for o, ref in enumerate((nkk_ref, w_ref, b_ref, k_ref, r_ref)):
        for t0 in range(0, T, piece):
            x = ref[t0:t0 + piece].astype(F32).reshape(piece * ch, LANES)
            turned = [x] + [pltpu.roll(x, m * width, 1) for m in range(1, groups)]
            for j in range(groups):
                y = turned[0]
                for m in range(1, groups):
                    y = jnp.where(grp == (j + m) % groups, turned[m], y)
                rep_ref[o, t0:t0 + piece, j] = y.reshape(piece, ch, LANES)

    keys = [(j, kh) for j in range(groups) for kh in range(ch)]
    row = lambda o, t, j, kh: rep_ref[o, t, j, kh:kh + 1, :]

    def step(t, carry):
        vt = v_ref[t].astype(F32)
        sa = tree([s_ref[j * ch + kh] * row(0, t, j, kh) for j, kh in keys])
        ys = []
        for j, kh in keys:
            s_new = (s_ref[j * ch + kh] * row(1, t, j, kh) + sa * row(2, t, j, kh) + vt * row(3, t, j, kh))
            s_ref[j * ch + kh] = s_new
            ys.append(s_new * row(4, t, j, kh))
        y_ref[t] = tree(ys).astype(BF16)
        return carry

    lax.fori_loop(0, T, step, 0)


def _rwkv_scan(nkk, w, b, k, r, v):
    S, ch, _ = v.shape
    T = 32
    spec = pl.BlockSpec((T, ch, LANES), lambda i: (i, 0, 0))
    return pl.pallas_call(
        functools.partial(_rwkv_scan_kernel, T),
        grid=(S // T,),
        in_specs=[spec] * 6,
        out_specs=spec,
        out_shape=jax.ShapeDtypeStruct((S, ch, LANES), BF16),
        scratch_shapes=[pltpu.VMEM((HEAD_DIM, ch, LANES), F32),
                        pltpu.VMEM((5, T, HEAD_DIM // ch, ch, LANES), F32)],
        compiler_params=_params(("arbitrary",)),
        name="rwkv_scan",
    )(nkk, w, b, k, r, v)


def _to_v_layout(x, B, S):
    P = B * N_HEADS_G
    rep = LANES // P
    t = x.reshape(B, S, N_HEADS_G, HEAD_DIM // rep, rep).transpose(1, 3, 4, 0, 2)
    return t.reshape(S, HEAD_DIM // rep, LANES)


def _from_v_layout(y, B, S):
    P = B * N_HEADS_G
    rep = LANES // P
    t = y.reshape(S, HEAD_DIM // rep, rep, B, N_HEADS_G).transpose(3, 0, 4, 1, 2)
    return t.reshape(B * S, GROUP_W)


def _dsa_prep_kernel(f_ref, qn_ref, wq_ref, wqi_ref, cq_ref, sq_ref, ci_ref, si_ref,
                     qt_o, qit_o, wt_o, k_o, v_o, ki_o):
    f = f_ref[...]
    cq = _rms(f[:, 0:DSA_Q_LORA], qn_ref[...]).astype(BF16)
    cos_q, sin_q = cq_ref[...], sq_ref[...]
    cos_i, sin_i = ci_ref[...], si_ref[...]
    q = _rope(jnp.dot(cq, wq_ref[...], preferred_element_type=F32), cos_q, sin_q, HEAD_DIM, HEAD_DIM // 8)
    tm = f.shape[0]
    qt_o[...] = (q * HEAD_DIM ** -0.5).T.reshape(N_HEADS_G, HEAD_DIM, tm).astype(BF16)
    qi = _rope(jnp.dot(cq, wqi_ref[...], preferred_element_type=F32), cos_i, sin_i, IDX_DIM, IDX_DIM // 8)
    qit_o[...] = qi.T.reshape(IDX_HEADS, IDX_DIM, tm).astype(BF16)
    kv = f[:, 128:256]
    kv_r = _rope(kv, cos_q[:, 0:128], sin_q[:, 0:128], HEAD_DIM, HEAD_DIM // 8)
    k_o[...] = kv_r[:, 0:HEAD_DIM].astype(BF16)
    v_o[...] = kv[:, HEAD_DIM:128].astype(BF16)
    tail = f[:, 256:384]
    tail_r = _rope(tail, cos_i[:, 0:128], sin_i[:, 0:128], IDX_DIM, IDX_DIM // 8)
    ki_o[...] = tail_r[:, 0:IDX_DIM].astype(BF16)
    wt_o[...] = tail.T[IDX_DIM:IDX_DIM + IDX_HEADS, :] * (IDX_HEADS ** -0.5 * IDX_DIM ** -0.5)


def _dsa_prep(f2, qn, wq, wqi, tq, ti):
    N, W = f2.shape
    tm = ROW_TILE
    G = GROUP_W
    row = lambda w: pl.BlockSpec((tm, w), lambda i: (i, 0))
    col = lambda h: pl.BlockSpec((h, tm), lambda i: (0, i))
    slab = lambda h, d: pl.BlockSpec((h, d, tm), lambda i: (0, 0, i))
    c = lambda shape: pl.BlockSpec(shape, lambda i: (0, 0))
    return pl.pallas_call(
        _dsa_prep_kernel,
        grid=(N // tm,),
        in_specs=[row(W), c((1, DSA_Q_LORA)), c((DSA_Q_LORA, G)), c((DSA_Q_LORA, G))] + [row(G)] * 4,
        out_specs=[slab(N_HEADS_G, HEAD_DIM), slab(IDX_HEADS, IDX_DIM), col(IDX_HEADS),
                   row(HEAD_DIM), row(HEAD_DIM), row(IDX_DIM)],
        out_shape=[jax.ShapeDtypeStruct((N_HEADS_G, HEAD_DIM, N), BF16),
                   jax.ShapeDtypeStruct((IDX_HEADS, IDX_DIM, N), BF16),
                   jax.ShapeDtypeStruct((IDX_HEADS, N), F32),
                   jax.ShapeDtypeStruct((N, HEAD_DIM), BF16), jax.ShapeDtypeStruct((N, HEAD_DIM), BF16),
                   jax.ShapeDtypeStruct((N, IDX_DIM), BF16)],
        compiler_params=_params(("parallel",)),
        name="dsa_prep",
    )(f2, qn, wq, wqi, *tq, *ti)


def _float_key(bits):
    return bits ^ ((bits >> 31) & 0x7FFFFFFF)


def _dsa_kernel(TK, topk, qt_ref, qit_ref, wt_ref, k_ref, v_ref, ki_ref, on_ref, o_ref,
                skey_ref, lg_ref, out_ref):
    i = pl.program_id(1)
    TQ = Q_BLOCK
    n_kt = (i * TQ + TQ + TK - 1) // TK
    q_all = jnp.concatenate([qt_ref[h] for h in range(N_HEADS_G)], axis=1)
    qi_all = jnp.concatenate([qit_ref[h] for h in range(IDX_HEADS)], axis=1)
    wt = wt_ref[...]
    kpos = lax.broadcasted_iota(I32, (TK, TQ), 0)
    qpos = i * TQ + lax.broadcasted_iota(I32, (TK, TQ), 1)
    head = lambda x, h: x[:, h * TQ:(h + 1) * TQ]

    def fold(x, op):
        return _tree([x[r * SUBLANES:(r + 1) * SUBLANES] for r in range(TK // SUBLANES)], op)

    def score_tile(kt, tops):
        off = pl.multiple_of(kt * TK, TK)
        rel = jnp.dot(ki_ref[0, pl.ds(off, TK), :], qi_all, preferred_element_type=F32)
        sc = jnp.zeros((TK, TQ), F32)
        for h in range(IDX_HEADS):
            sc = sc + jnp.maximum(head(rel, h), 0.0) * wt[h:h + 1, :]
        key = _float_key(pltpu.bitcast(sc, I32))
        key = jnp.where(sc == 0.0, -(kt * TK + kpos), jnp.where(key < 0, key - skey_ref.shape[0] * TK, key))
        valid = kt * TK + kpos <= qpos
        skey_ref[kt] = jnp.where(valid, key, INT_MIN)
        lg = jnp.dot(k_ref[0, pl.ds(off, TK), :], q_all, preferred_element_type=F32)
        lg_ref[kt] = lg
        return tuple(jnp.maximum(tops[h], fold(jnp.where(valid, head(lg, h), -1e30), jnp.maximum))
                     for h in range(N_HEADS_G))

    lowest = tuple(jnp.full((SUBLANES, TQ), -1e30, F32) for _ in range(N_HEADS_G))
    tops = lax.fori_loop(0, n_kt, score_tile, lowest)

    def count(pred):
        def count_tile(kt, acc):
            return acc + fold(jnp.where(pred(kt, skey_ref[kt]), 1.0, 0.0), jnp.add)

        acc = lax.fori_loop(0, n_kt, count_tile, jnp.zeros((SUBLANES, TQ), F32))
        return jnp.sum(acc, axis=0, keepdims=True)

    def bit_step(j, carry):
        thr, n_thr = carry
        cand = thr + lax.shift_left(jnp.int32(1), 31 - j)
        n = count(lambda kt, keys: keys >= cand)
        return jnp.where(n >= topk, cand, thr), jnp.where(n >= topk, n, n_thr)

    thr, n_thr = lax.fori_loop(0, 32, bit_step,
                               (jnp.full((1, TQ), INT_MIN, I32), jnp.full((1, TQ), float(topk), F32)))
    thr = jnp.maximum(thr, INT_MIN + 1)

    @pl.when(jnp.max(n_thr) > topk)
    def _():
        keep = topk - count(lambda kt, keys: keys > thr)

        pos_bits = (skey_ref.shape[0] * TK - 1).bit_length()

        def pos_step(j, last):
            cand = last + lax.shift_left(jnp.int32(1), pos_bits - 1 - j)
            n = count(lambda kt, keys: jnp.logical_and(keys == thr, kt * TK + kpos < cand))
            return jnp.where(n < keep, cand, last)

        last = lax.fori_loop(0, pos_bits, pos_step, jnp.zeros((1, TQ), I32))

        def retire(kt, carry):
            keys = skey_ref[kt]
            skey_ref[kt] = jnp.where(jnp.logical_and(keys == thr, kt * TK + kpos > last), INT_MIN, keys)
            return carry

        lax.fori_loop(0, n_kt, retire, 0)

    def sweep(shifts):
        def acc_tile(kt, carry):
            ls, acc = carry
            sel = skey_ref[kt] >= thr
            ps = [jnp.where(sel, jnp.exp(lg_ref[kt, :, h * TQ:(h + 1) * TQ] - shifts[h]), 0.0)
                  for h in range(N_HEADS_G)]
            p_all = jnp.concatenate([p.astype(BF16) for p in ps], axis=1)
            vb = v_ref[0, pl.ds(pl.multiple_of(kt * TK, TK), TK), :]
            return (tuple(ls[h] + fold(ps[h], jnp.add) for h in range(N_HEADS_G)),
                    acc + lax.dot_general(vb, p_all, TN_DIMS, preferred_element_type=F32))

        ls, acc = lax.fori_loop(0, n_kt, acc_tile,
                                (tuple(jnp.zeros((SUBLANES, TQ), F32) for _ in range(N_HEADS_G)),
                                 jnp.zeros((HEAD_DIM, N_HEADS_G * TQ), F32)))
        ls = [jnp.sum(l, axis=0, keepdims=True) for l in ls]
        for h in range(N_HEADS_G):
            out_ref[h * HEAD_DIM:(h + 1) * HEAD_DIM, :] = head(acc, h) / ls[h]
        return jnp.min(jnp.minimum(jnp.minimum(ls[0], ls[1]), jnp.minimum(ls[2], ls[3])))

    smallest = sweep([jnp.max(t, axis=0, keepdims=True) for t in tops])

    @pl.when(smallest < SOFTMAX_FLOOR)
    def _():
        def max_tile(kt, ms):
            sel = skey_ref[kt] >= thr
            return tuple(jnp.maximum(ms[h], fold(jnp.where(sel, lg_ref[kt, :, h * TQ:(h + 1) * TQ], -1e30),
                                                 jnp.maximum))
                         for h in range(N_HEADS_G))

        ms = lax.fori_loop(0, n_kt, max_tile, lowest)
        sweep([jnp.max(m, axis=0, keepdims=True) for m in ms])

    o_ref[0] = _rms(out_ref[...].T, on_ref[...]).astype(BF16)


def _dsa_attention(qt, qit, wt, k, v, ki, on):
    B, S, _ = k.shape
    G = GROUP_W
    nq = S // Q_BLOCK
    TK = min(512, S)
    topk = min(DSA_TOPK_MAX, S // 4)
    qcol = lambda h: pl.BlockSpec((h, Q_BLOCK), lambda b, i: (0, b * nq + i))
    qslab = lambda h, d: pl.BlockSpec((h, d, Q_BLOCK), lambda b, i: (0, 0, b * nq + i))
    full = lambda wd: pl.BlockSpec((1, S, wd), lambda b, i: (b, 0, 0))
    return pl.pallas_call(
        functools.partial(_dsa_kernel, TK, topk),
        grid=(B, nq),
        in_specs=[qslab(N_HEADS_G, HEAD_DIM), qslab(IDX_HEADS, IDX_DIM), qcol(IDX_HEADS),
                  full(HEAD_DIM), full(HEAD_DIM), full(IDX_DIM), pl.BlockSpec((1, G), lambda b, i: (0, 0))],
        out_specs=pl.BlockSpec((1, Q_BLOCK, G), lambda b, i: (b, i, 0)),
        out_shape=jax.ShapeDtypeStruct((B, S, G), BF16),
        scratch_shapes=[pltpu.VMEM((S // TK, TK, Q_BLOCK), I32),
                        pltpu.VMEM((S // TK, TK, N_HEADS_G * Q_BLOCK), F32),
                        pltpu.VMEM((G, Q_BLOCK), F32)],
        compiler_params=_params(("parallel", "arbitrary")),
        name="dsa_attention",
    )(qt, qit, wt, k, v, ki, on)


def _sb_kernel(q_ref, k_ref, v_ref, u_ref, on_ref, o_ref, z_ref, lm_ref):
    i = pl.program_id(1)
    T = Q_BLOCK
    H, G = N_HEADS_G, GROUP_W
    u = u_ref[...]
    qt = q_ref[0].astype(F32).T.astype(BF16)
    q_rows = lax.broadcasted_iota(I32, (G, H * T), 0) // HEAD_DIM
    q_cols = lax.broadcasted_iota(I32, (G, H * T), 1) // T
    qbd = jnp.where(q_rows == q_cols, jnp.concatenate([qt] * H, axis=1), jnp.zeros((), BF16))
    kid = lax.broadcasted_iota(I32, (T, H * T), 0)
    qid = lax.broadcasted_iota(I32, (T, H * T), 1) & (T - 1)
    strict = kid < qid

    def cond(carry):
        kt, c, _ = carry
        return jnp.logical_and(kt >= 0, jnp.max(c) > SB_UNDERFLOW)

    def logits(kt):
        z = jnp.dot(k_ref[0, pl.ds(pl.multiple_of(kt * T, T), T), :], qbd, preferred_element_type=F32)
        z = z * HEAD_DIM ** -0.5
        return z, -_softplus(z)

    z0, lm0 = logits(i)
    z_ref[...] = z0
    lm_ref[...] = jnp.where(strict, lm0, 0.0)

    def body(carry):
        kt, c, acc = carry
        off = pl.multiple_of(kt * T, T)
        mask = jnp.logical_or(strict, kt < i)
        z, lm = z_ref[...], lm_ref[...]
        z_ref[...], lm_ref[...] = logits(jnp.maximum(kt - 1, 0))
        hi, lo = _split(lm)
        later = jnp.dot(u, jnp.concatenate([hi, lo], axis=1), preferred_element_type=F32)
        later = later[:, :H * T] + later[:, H * T:]
        a = jnp.where(mask, jnp.exp(z + lm + (c + later)), 0.0)
        av = lax.dot_general(v_ref[0, pl.ds(off, T), :], a.astype(BF16), TN_DIMS, preferred_element_type=F32)
        acc = acc + jnp.concatenate(
            [av[h * HEAD_DIM:(h + 1) * HEAD_DIM, h * T:(h + 1) * T] for h in range(H)], axis=0)
        return kt - 1, c + jnp.sum(lm, axis=0, keepdims=True), acc

    _, _, acc = lax.while_loop(cond, body, (i, jnp.zeros((1, H * T), F32), jnp.zeros((G, T), F32)))
    o_ref[0] = _rms(acc.T, on_ref[...]).astype(BF16)


def _stick_breaking(sb3, u, on):
    B, S, _ = sb3.shape
    G = GROUP_W
    return pl.pallas_call(
        _sb_kernel,
        grid=(B, S // Q_BLOCK),
        in_specs=[pl.BlockSpec((1, Q_BLOCK, G), lambda b, i: (b, i, 0)),
                  pl.BlockSpec((1, S, G), lambda b, i: (b, 0, 1)),
                  pl.BlockSpec((1, S, G), lambda b, i: (b, 0, 2)),
                  pl.BlockSpec((Q_BLOCK, Q_BLOCK), lambda b, i: (0, 0)),
                  pl.BlockSpec((1, G), lambda b, i: (0, 0))],
        out_specs=pl.BlockSpec((1, Q_BLOCK, G), lambda b, i: (b, i, 0)),
        out_shape=jax.ShapeDtypeStruct((B, S, G), BF16),
        scratch_shapes=[pltpu.VMEM((Q_BLOCK, N_HEADS_G * Q_BLOCK), F32)] * 2,
        compiler_params=_params(("parallel", "arbitrary")),
        name="stick_breaking",
    )(sb3, sb3, sb3, u, on)


def _outproj_kernel(yr_ref, yd_ref, ys_ref, yw_ref, bonus_ref, gate_ref, ln_ref, hm_ref, x_ref, g1_ref, sc_ref,
                    sh_ref, gn_ref, wo_ref, rw_ref, rb_ref, x_o, h_o, lg_o):
    G = GROUP_W
    y = yw_ref[...].astype(F32)
    hm = hm_ref[...]
    yc = y - _split_dot(y, hm) * (1.0 / HEAD_DIM)
    var = _split_dot(yc * yc, hm) * (1.0 / HEAD_DIM)
    yw = ((yc * lax.rsqrt(var + RWKV_LN_EPS) * ln_ref[...] + bonus_ref[...].astype(F32))
          * gate_ref[...].astype(F32)).astype(BF16)
    dot = lambda yv, g: jnp.dot(yv, wo_ref[g * G:(g + 1) * G, :], preferred_element_type=F32)
    mixed = dot(yr_ref[...], 0) + dot(yw, 1) + dot(yd_ref[...], 2) + dot(ys_ref[...], 3)
    x1 = x_ref[...] + g1_ref[0] * mixed
    x_o[...] = x1
    h = _rms(x1, gn_ref[...]) * (1.0 + sc_ref[0]) + sh_ref[0]
    h_o[...] = h
    hi, lo = _split(h)
    work = (jnp.dot(hi, rw_ref[0], preferred_element_type=F32) + jnp.dot(lo, rw_ref[0], preferred_element_type=F32)
            + jnp.dot(hi, rw_ref[1], preferred_element_type=F32) + rb_ref[...])
    col = lax.broadcasted_iota(I32, work.shape, 1).astype(F32)
    out = jnp.zeros_like(work)
    vals = []
    for j in range(TOP_K):
        m = jnp.max(work, axis=-1, keepdims=True)
        idx = jnp.min(jnp.where(work == m, col, float(LANES)), axis=-1, keepdims=True)
        out = jnp.where(col == float(j), idx, out)
        vals.append(m)
        work = jnp.where(col == idx, -jnp.inf, work)
    es = [jnp.exp(v - vals[0]) for v in vals]
    den = es[0] + es[1] + es[2] + es[3]
    for j in range(TOP_K):
        out = jnp.where(col == float(TOP_K + j), es[j] / den, out)
    lg_o[...] = out


def _outproj(ys, rwkv, x2, mod6, gain, w_out, router_w, router_b, S):
    N, D = x2.shape
    tm = ROW_TILE
    per_b = S // tm
    G = GROUP_W
    row = lambda w: pl.BlockSpec((tm, w), lambda i: (i, 0))
    c = lambda shape: pl.BlockSpec(shape, lambda i: (0, 0))
    return pl.pallas_call(
        _outproj_kernel,
        grid=(N // tm,),
        in_specs=[row(G)] * 6 + [c((1, G)), c((G, G)),
                                 row(D), _mod_row(per_b, 2), _mod_row(per_b, 4), _mod_row(per_b, 3),
                                 c((1, D)), c((D, D)), pl.BlockSpec((2, D, LANES), lambda i: (0, 0, 0)),
                                 c((1, LANES))],
        out_specs=[row(D), row(D), row(LANES)],
        out_shape=[jax.ShapeDtypeStruct((N, D), F32), jax.ShapeDtypeStruct((N, D), F32),
                   jax.ShapeDtypeStruct((N, LANES), F32)],
        compiler_params=_params(("parallel",)),
        name="out_proj_router",
    )(*ys, *rwkv, x2, mod6, mod6, mod6, gain, w_out,
      jnp.pad(jnp.stack(_split(router_w)), ((0, 0), (0, 0), (0, LANES - N_EXPERTS))),
      jnp.pad(router_b, ((0, 0), (0, LANES - N_EXPERTS)), constant_values=-1e30))


def _moe_kernel(be_ref, nu_ref, x_ref, w1_ref, b1_ref, w2_ref, b2_ref, o_ref):
    i = pl.program_id(0)

    @pl.when(i < nu_ref[0])
    def _():
        xb = x_ref[...].astype(BF16)
        w1 = w1_ref.at[0, 0]
        glu = jnp.dot(xb, w1[:, 0:D_FF].astype(BF16), preferred_element_type=F32) + b1_ref[0, :, 0:D_FF]
        lin = jnp.dot(xb, w1[:, D_FF:].astype(BF16), preferred_element_type=F32) + b1_ref[0, :, D_FF:]
        glu = jnp.minimum(glu, SWIGLU_LIMIT)
        lin = jnp.clip(lin, -SWIGLU_LIMIT, SWIGLU_LIMIT)
        act = glu * _sigmoid(SWIGLU_ALPHA * glu) * (lin + 1.0)
        y = jnp.dot(act.astype(BF16), w2_ref[0, 0].astype(BF16), preferred_element_type=F32) + b2_ref[0]
        o_ref[...] = y.astype(BF16)

    @pl.when(i >= nu_ref[0])
    def _():
        o_ref[...] = jnp.zeros_like(o_ref)


def _moe_ffn(block_expert, n_used, xs, layer, w1, b1, w2, b2):
    rows, D = xs.shape
    nb = rows // MOE_BLOCK
    L, E = w1.shape[:2]
    live = lambda i, nu: jnp.minimum(i, nu[0] - 1)
    grid_spec = pltpu.PrefetchScalarGridSpec(
        num_scalar_prefetch=2,
        grid=(nb,),
        in_specs=[pl.BlockSpec((MOE_BLOCK, D), lambda i, be, nu: (live(i, nu), 0)),
                  pl.BlockSpec((1, 1, D, 2 * D_FF), lambda i, be, nu: (layer, be[i], 0, 0)),
                  pl.BlockSpec((1, 1, 2 * D_FF), lambda i, be, nu: (layer * E + be[i], 0, 0)),
                  pl.BlockSpec((1, 1, D_FF, D), lambda i, be, nu: (layer, be[i], 0, 0)),
                  pl.BlockSpec((1, 1, D), lambda i, be, nu: (layer * E + be[i], 0, 0))],
        out_specs=pl.BlockSpec((MOE_BLOCK, D), lambda i, be, nu: (i, 0)),
    )
    return pl.pallas_call(
        _moe_kernel,
        grid_spec=grid_spec,
        out_shape=jax.ShapeDtypeStruct((rows, D), BF16),
        compiler_params=_params(("arbitrary",), 56 * 1024 * 1024),
        name="moe_ffn",
    )(block_expert, n_used, xs, w1, b1.reshape(L * E, 1, -1), w2, b2.reshape(L * E, 1, -1))


def _route(routed):
    n = routed.shape[0]
    expert = routed[:, 0:TOP_K].reshape(-1).astype(I32)
    n_assign = n * TOP_K
    ids = jnp.arange(n_assign, dtype=I32)
    _, order = lax.sort((expert, ids), num_keys=1, is_stable=True)
    _, inv = lax.sort((order, ids), num_keys=1)
    counts = jnp.sum((expert[:, None] == jnp.arange(N_EXPERTS, dtype=I32)[None, :]).astype(I32), axis=0)
    starts = jnp.cumsum(counts) - counts
    padded = (counts + MOE_BLOCK - 1) // MOE_BLOCK * MOE_BLOCK
    pad_end = jnp.cumsum(padded)
    pad_start = pad_end - padded
    dest = pad_start[expert] + inv - starts[expert]
    n_blocks = -(-n_assign // MOE_BLOCK) + N_EXPERTS
    block_start = jnp.arange(n_blocks, dtype=I32) * MOE_BLOCK
    block_expert = jnp.minimum(jnp.sum(pad_end[None, :] <= block_start[:, None], axis=1), N_EXPERTS - 1)
    block_expert = block_expert.astype(I32)
    per_row = lambda t: jnp.repeat(t[block_expert], MOE_BLOCK)
    off = jnp.arange(n_blocks * MOE_BLOCK, dtype=I32) - per_row(pad_start)
    valid = off < per_row(counts)
    src = order[jnp.clip(per_row(starts) + off, 0, n_assign - 1)]
    row_token = jnp.where(valid, src // TOP_K, 0)
    n_used = (pad_end[-1] // MOE_BLOCK).astype(I32).reshape(1)
    return dest.reshape(n, TOP_K), row_token, block_expert, n_used


def _rows(t, idx):
    return t.at[idx].get(mode="promise_in_bounds")


def _final_kernel(x_ref, g2_ref, routed_ref, y0, y1, y2, y3, g_ref, o_ref):
    o_ref[...] = _rms(_combine(x_ref, g2_ref, routed_ref, (y0, y1, y2, y3)), g_ref[...])


def _final_norm(x2, mod6, routed, ys, g, S):
    N, D = x2.shape
    tm = ROW_TILE
    row = pl.BlockSpec((tm, D), lambda i: (i, 0))
    return pl.pallas_call(
        _final_kernel, grid=(N // tm,),
        in_specs=[row, _mod_row(S // tm, 5), pl.BlockSpec((tm, LANES), lambda i: (i, 0))] + [row] * 4
        + [pl.BlockSpec((1, D), lambda i: (0, 0))],
        out_specs=row,
        out_shape=jax.ShapeDtypeStruct((N, D), F32),
        compiler_params=_params(("parallel",)), name="final_norm",
    )(x2, mod6, routed, *ys, g)


def _rope_tables(pos, group, rot_dim, theta, width):
    half = rot_dim // 2
    inv = theta ** (-jnp.arange(half, dtype=F32) / half)
    ang = pos.astype(F32)[..., None] * inv
    cos, sin = jnp.cos(ang), jnp.sin(ang)
    rest = group - rot_dim
    ones = jnp.ones(ang.shape[:2] + (rest,), F32)
    zeros = jnp.zeros(ang.shape[:2] + (rest,), F32)
    rep = width // group
    tile = lambda t: jnp.tile(t, (1, 1, rep)).reshape(-1, width)
    return (tile(jnp.concatenate([cos, cos, ones], -1)), tile(jnp.concatenate([-sin, sin, zeros], -1)))


def _pad_rows(w, start, total=128):
    return jnp.zeros((total, w.shape[1]), w.dtype).at[start:start + w.shape[0]].set(w)


def kernel(x, c, positions, ada_w, ada_b, norm_mix, norm_ffn, w_in, ret_gn, rwkv_mu, rwkv_w0, rwkv_w2, rwkv_a0,
           rwkv_a2, rwkv_g2, rwkv_kk, rwkv_ka, rwkv_rk, rwkv_ln, dsa_qnorm, dsa_wq_up, dsa_wqi_up, dsa_onorm,
           sb_onorm, w_out, router_w, router_b, moe_w1, moe_b1, moe_w2, moe_b2, norm_final):
    B, S, D = x.shape
    N = B * S
    L = ada_w.shape[0]
    G = GROUP_W
    row2 = lambda t: t.reshape(1, -1)

    mod = _ada(c, ada_w, ada_b)
    ret_tabs = tuple(t.reshape(B, S, G) for t in _rope_tables(positions, HEAD_DIM, HEAD_DIM, RET_THETA, G))
    dq_tabs = _rope_tables(positions, HEAD_DIM, HEAD_DIM // 4, ROPE_THETA, G)
    di_tabs = _rope_tables(positions, IDX_DIM, IDX_DIM // 4, ROPE_THETA, G)
    ret_c = _ret_consts(min(256, S))
    hm = (jnp.arange(G)[:, None] // HEAD_DIM == jnp.arange(G)[None, :] // HEAD_DIM).astype(BF16)
    u_later = (jnp.arange(Q_BLOCK)[None, :] > jnp.arange(Q_BLOCK)[:, None]).astype(BF16)
    o1, o2 = RET_COLS + RWKV_COLS, RET_COLS + RWKV_COLS + DSA_COLS

    x2 = x.reshape(N, D)
    comb = None
    for l in range(L):
        mod6 = mod[l].reshape(B * 6, 1, D)
        w_l = w_in[l]
        w_cat = jnp.concatenate(
            [w_l[:, :o1], w_l[:, o1:o2], jnp.zeros((D, DSA_PAD - DSA_COLS), F32), w_l[:, o2:]], axis=1).astype(BF16)
        outs = _inproj(x2, row2(norm_mix[l]), mod6, w_cat, S, comb)
        if comb is not None:
            x2, outs = outs[0], outs[1:]
        ret, rwkv, dsa, sb = outs

        prep = _rwkv_prep(rwkv, S, row2(rwkv_mu[l]), row2(rwkv_w0[l]), row2(rwkv_a0[l]),
                          _pad_rows(rwkv_w2[l], 0).astype(BF16), _pad_rows(rwkv_a2[l], 32).astype(BF16),
                          _pad_rows(rwkv_g2[l], 64).astype(BF16), row2(rwkv_kk[l]), row2(rwkv_ka[l]),
                          row2(rwkv_rk[l]), hm)
        r_, w_, k_, v_, nkk_, b_, gate_, bonus_ = prep
        scan_in = tuple(_to_v_layout(t, B, S) for t in (nkk_, w_, b_, k_, r_, v_))

        y_ret = _retention(ret.reshape(B, S, RET_COLS), ret_tabs, ret_c, row2(ret_gn[l])).reshape(N, G)

        dqt, dqit, dwt, dk, dv, dki = _dsa_prep(dsa, row2(dsa_qnorm[l]), dsa_wq_up[l].astype(BF16),
                                                dsa_wqi_up[l].astype(BF16), dq_tabs, di_tabs)
        y_dsa = _dsa_attention(dqt, dqit, dwt, dk.reshape(B, S, HEAD_DIM), dv.reshape(B, S, HEAD_DIM),
                               dki.reshape(B, S, IDX_DIM), row2(dsa_onorm[l])).reshape(N, G)

        y_sb = _stick_breaking(sb.reshape(B, S, SB_COLS), u_later, row2(sb_onorm[l])).reshape(N, G)

        y_scan = _rwkv_scan(*scan_in)
        rwkv_tail = (_from_v_layout(y_scan, B, S), bonus_, gate_, row2(rwkv_ln[l]), hm)

        x1, h2, routed = _outproj((y_ret, y_dsa, y_sb), rwkv_tail, x2, mod6, row2(norm_ffn[l]),
                                  w_out[l].astype(BF16), router_w[l], row2(router_b[l]), S)

        dest, row_token, block_expert, n_used = _route(routed)
        y = _moe_ffn(block_expert, n_used, _rows(h2, row_token), l, moe_w1, moe_b1, moe_w2, moe_b2)
        x2 = x1
        comb = (mod6, routed, tuple(_rows(y, dest[:, j]) for j in range(TOP_K)))
    return _final_norm(x2, *comb, row2(norm_final), S).reshape(B, S, D)
```
